```python
import jax, jax.numpy as jnp
from jax import lax
import numpy as np

D_MODEL = 2048
BATCH = 2
SEQ = 4096
DEPTH = 1

MIX_WIDTH = D_MODEL
MOBA_WIDTH = D_MODEL // 2
MOBA_HEADS = 8
MOBA_HEAD_DIM = MOBA_WIDTH // MOBA_HEADS
MOBA_BLOCK = 256
MOBA_TOPK = 3
MOBA_QCHUNK = 64
ROPE_THETA = 500000.0
ROPE_DIMS = MOBA_HEAD_DIM // 4
RET_WIDTH = MIX_WIDTH - MOBA_WIDTH
RET_HEADS = 4
RET_HEAD_DIM = RET_WIDTH // RET_HEADS
RET_CHUNK = 128
RET_ROPE_BASE = 10000.0
IN_SPLITS = [MOBA_WIDTH] * 3 + [RET_WIDTH] * 4
IN_PROJ_DIM = sum(IN_SPLITS)
N_EXPERTS = 64
TOP_K = 8
N_GROUPS = 8
TOPK_GROUPS = 4
EXPERT_DIM = D_MODEL // 4
SHARED_DIM = EXPERT_DIM
ROUTE_SCALE = 2.5
MOE_BLOCK = 128
NORM_EPS = 1e-6
NEG = -1e30

kernel_name = "hymba_moba_retnet_moe_adaln"


def rms_norm(x, g):
    xf = x.astype(jnp.float32)
    y = xf * lax.rsqrt(jnp.mean(xf * xf, axis=-1, keepdims=True) + NORM_EPS)
    return (y * g.astype(jnp.float32)).astype(x.dtype)


def modulate(h, shift, scale):
    return h * (1.0 + scale[:, None, :]) + shift[:, None, :]


def rotary(x, positions, rot_dims, inv_freq):
    half = rot_dims // 2
    ang = positions[:, None, :, None].astype(jnp.float32) * inv_freq
    cos, sin = jnp.cos(ang), jnp.sin(ang)
    xr = x[..., :rot_dims].astype(jnp.float32)
    x1, x2 = xr[..., :half], xr[..., half:]
    rot = jnp.concatenate([x1 * cos - x2 * sin, x2 * cos + x1 * sin], axis=-1)
    return jnp.concatenate([rot.astype(x.dtype), x[..., rot_dims:]], axis=-1)


def moba_attention(q, k, v):
    B, H, T, hd = q.shape
    nb = -(-T // MOBA_BLOCK)
    Tp = nb * MOBA_BLOCK
    pad = ((0, 0), (0, 0), (0, Tp - T), (0, 0))
    q, k, v = jnp.pad(q, pad), jnp.pad(k, pad), jnp.pad(v, pad)
    kb = k.reshape(B, H, nb, MOBA_BLOCK, hd)
    vb = v.reshape(B, H, nb, MOBA_BLOCK, hd)
    kmean = jnp.mean(kb.astype(jnp.float32), axis=3)
    gate = jnp.einsum('bhtd,bhnd->bhtn', q.astype(jnp.float32), kmean)
    qblk = jnp.arange(Tp) // MOBA_BLOCK
    past = jnp.arange(nb)[None, :] < qblk[:, None]
    gate = jnp.where(past, gate, NEG)
    ksel = min(MOBA_TOPK, max(nb - 1, 1))
    gval, gidx = lax.top_k(gate, ksel)
    gvalid = gval > 0.5 * NEG
    nq = Tp // MOBA_QCHUNK
    def chunked(t):
        return jnp.moveaxis(t.reshape(B, H, nq, MOBA_QCHUNK, *t.shape[3:]), 2, 0)
    q_c, idx_c, val_c = chunked(q), chunked(gidx), chunked(gvalid)
    scale = hd ** -0.5
    bi = jnp.arange(B)[:, None, None, None]
    hi = jnp.arange(H)[None, :, None, None]

    def attend(args):
        c, qc, idx, valid = args
        kg = kb[bi, hi, idx].astype(jnp.float32)
        vg = vb[bi, hi, idx].astype(jnp.float32)
        blk = (c * MOBA_QCHUNK) // MOBA_BLOCK
        ko = lax.dynamic_index_in_dim(kb, blk, axis=2, keepdims=False).astype(jnp.float32)
        vo = lax.dynamic_index_in_dim(vb, blk, axis=2, keepdims=False).astype(jnp.float32)
        qf = qc.astype(jnp.float32) * scale
        s_sel = jnp.einsum('bhqd,bhqskd->bhqsk', qf, kg)
        s_sel = jnp.where(valid[..., None], s_sel, NEG).reshape(B, H, MOBA_QCHUNK, ksel * MOBA_BLOCK)
        s_own = jnp.einsum('bhqd,bhkd->bhqk', qf, ko)
        qpos = c * MOBA_QCHUNK + jnp.arange(MOBA_QCHUNK)
        kpos = blk * MOBA_BLOCK + jnp.arange(MOBA_BLOCK)
        s_own = jnp.where(kpos[None, :] <= qpos[:, None], s_own, NEG)
        p = jax.nn.softmax(jnp.concatenate([s_sel, s_own], axis=-1), axis=-1)
        p_sel = p[..., :ksel * MOBA_BLOCK].reshape(B, H, MOBA_QCHUNK, ksel, MOBA_BLOCK)
        p_own = p[..., ksel * MOBA_BLOCK:]
        o = jnp.einsum('bhqsk,bhqskd->bhqd', p_sel, vg) + jnp.einsum('bhqk,bhkd->bhqd', p_own, vo)
        return o.astype(q.dtype)

    out = lax.map(attend, (jnp.arange(nq), q_c, idx_c, val_c))
    out = jnp.moveaxis(out, 0, 2).reshape(B, H, Tp, hd)
    return out[:, :, :T]


def retention(q, k, v):
    B, H, T, d = q.shape
    C = RET_CHUNK
    n = T // C
    q = q.astype(jnp.float32)
    k = k.astype(jnp.float32) * (d ** -0.5)
    v = v.astype(jnp.float32)
    gamma = 1.0 - jnp.exp2(-5.0 - jnp.arange(H, dtype=jnp.float32))
    log_g = jnp.log(gamma)
    pos = jnp.arange(C, dtype=jnp.float32)
    diff = pos[:, None] - pos[None, :]
    dmask = jnp.where(diff >= 0, jnp.exp(jnp.maximum(diff, 0.0) * log_g[:, None, None]), 0.0)
    xi = jnp.exp((pos + 1.0) * log_g[:, None])
    zeta = jnp.exp((C - 1.0 - pos) * log_g[:, None])
    chunk_decay = jnp.exp(C * log_g)
    def chunked(t):
        return jnp.moveaxis(t.reshape(B, H, n, C, d), 2, 0)

    def step(S, inp):
        qi, ki, vi = inp
        inner = jnp.einsum('bhnd,bhmd->bhnm', qi, ki) * dmask
        o = jnp.einsum('bhnm,bhme->bhne', inner, vi) + jnp.einsum('bhnd,bhde->bhne', qi, S) * xi[None, :, :, None]
        S = S * chunk_decay[None, :, None, None] + jnp.einsum('bhmd,bhme->bhde', ki * zeta[None, :, :, None], vi)
        return S, o

    S0 = jnp.zeros((B, H, d, d), jnp.float32)
    _, o = lax.scan(step, S0, (chunked(q), chunked(k), chunked(v)))
    return jnp.moveaxis(o, 0, 2).reshape(B, H, T, d)


def moe_ffn(h, w_router, router_bias, w_gate, w_up, w_down, ws_gate, ws_up, ws_down):
    N, D = h.shape
    E, K, M = N_EXPERTS, TOP_K, MOE_BLOCK
    scores = jax.nn.sigmoid(h.astype(jnp.float32) @ w_router.astype(jnp.float32))
    biased = scores + router_bias.astype(jnp.float32)
    grp = biased.reshape(N, N_GROUPS, E // N_GROUPS)
    grp_score = jnp.sum(lax.top_k(grp, 2)[0], axis=-1)
    _, gsel = lax.top_k(grp_score, TOPK_GROUPS)
    gmask = jnp.sum(jax.nn.one_hot(gsel, N_GROUPS, dtype=jnp.float32), axis=1) > 0
    emask = jnp.repeat(gmask, E // N_GROUPS, axis=1)
    _, eidx = lax.top_k(jnp.where(emask, biased, NEG), K)
    w = jnp.take_along_axis(scores, eidx, axis=1)
    w = w / jnp.sum(w, axis=-1, keepdims=True) * ROUTE_SCALE
    e_flat = eidx.reshape(-1)
    tok_flat = jnp.repeat(jnp.arange(N, dtype=jnp.int32), K)
    w_flat = w.reshape(-1)
    order = jnp.argsort(e_flat)
    e_s, tok_s, w_s = e_flat[order], tok_flat[order], w_flat[order]
    counts = jnp.bincount(e_flat, length=E)
    pcounts = (counts + M - 1) // M * M
    start = jnp.cumsum(counts) - counts
    pend = jnp.cumsum(pcounts)
    pstart = pend - pcounts
    dest = pstart[e_s] + jnp.arange(N * K) - start[e_s]
    R = N * K + E * M
    nblk = R // M
    row_tok = jnp.zeros((R,), jnp.int32).at[dest].set(tok_s)
    row_w = jnp.zeros((R,), jnp.float32).at[dest].set(w_s)
    blk_expert = jnp.clip(jnp.searchsorted(pend, jnp.arange(nblk) * M, side='right'), 0, E - 1)

    def expert_block(args):
        tok, wt, e = args
        xb = h[tok]
        a = xb @ w_gate[e]
        u = xb @ w_up[e]
        return ((jax.nn.silu(a) * u) @ w_down[e]) * wt[:, None].astype(h.dtype)

    y = lax.map(expert_block, (row_tok.reshape(nblk, M), row_w.reshape(nblk, M), blk_expert))
    routed = jax.ops.segment_sum(y.reshape(R, D), row_tok, num_segments=N)
    shared = (jax.nn.silu(h @ ws_gate) * (h @ ws_up)) @ ws_down
    return routed + shared


def setup_inputs(seed: int = 0) -> dict:
    key = jax.random.key(seed)
    ks = jax.random.split(key, 17)
    f32 = jnp.float32
    D = D_MODEL
    def nrm(k, shape, fan_in, s=1.0):
        return jax.random.normal(k, shape, f32) * (s * fan_in ** -0.5)
    x = jax.random.normal(ks[0], (BATCH, SEQ, D), f32)
    c = jax.random.normal(ks[1], (BATCH, D), f32)
    positions = jnp.broadcast_to(jnp.arange(SEQ, dtype=jnp.int32)[None, :], (BATCH, SEQ))
    w_ada = nrm(ks[2], (DEPTH, D, 6 * D), D, 0.5)
    b_ada = 0.02 * jax.random.normal(ks[3], (DEPTH, 6 * D), f32)
    norm_mix = 1.0 + 0.02 * jax.random.normal(ks[4], (DEPTH, D), f32)
    norm_ffn = 1.0 + 0.02 * jax.random.normal(ks[5], (DEPTH, D), f32)
    norm_out = 1.0 + 0.02 * jax.random.normal(ks[6], (D,), f32)
    w_in = nrm(ks[7], (DEPTH, D, IN_PROJ_DIM), D)
    w_out = nrm(ks[8], (DEPTH, MIX_WIDTH, D), MIX_WIDTH)
    w_router = nrm(ks[9], (DEPTH, D, N_EXPERTS), D)
    router_bias = 0.01 * jax.random.normal(ks[10], (DEPTH, N_EXPERTS), f32)
    w_gate = nrm(ks[11], (DEPTH, N_EXPERTS, D, EXPERT_DIM), D)
    w_up = nrm(ks[12], (DEPTH, N_EXPERTS, D, EXPERT_DIM), D)
    w_down = nrm(ks[13], (DEPTH, N_EXPERTS, EXPERT_DIM, D), EXPERT_DIM)
    w_sh_gate = nrm(ks[14], (DEPTH, D, SHARED_DIM), D)
    w_sh_up = nrm(ks[15], (DEPTH, D, SHARED_DIM), D)
    w_sh_down = nrm(ks[16], (DEPTH, SHARED_DIM, D), SHARED_DIM)
    return {"x": x, "c": c, "positions": positions, "w_ada": w_ada, "b_ada": b_ada,
            "norm_mix": norm_mix, "norm_ffn": norm_ffn, "norm_out": norm_out,
            "w_in": w_in, "w_out": w_out, "w_router": w_router, "router_bias": router_bias,
            "w_gate": w_gate, "w_up": w_up, "w_down": w_down,
            "w_sh_gate": w_sh_gate, "w_sh_up": w_sh_up, "w_sh_down": w_sh_down}


def reference(x, c, positions, w_ada, b_ada, norm_mix, norm_ffn, norm_out, w_in, w_out,
              w_router, router_bias, w_gate, w_up, w_down, w_sh_gate, w_sh_up, w_sh_down):
    B, T, D = x.shape
    moba_inv = ROPE_THETA ** (-(jnp.arange(ROPE_DIMS // 2, dtype=jnp.float32) * 2.0 / ROPE_DIMS))
    ret_inv = RET_ROPE_BASE ** (-jnp.linspace(0.0, 1.0, RET_HEAD_DIM // 2, dtype=jnp.float32))
    split_idx = [int(s) for s in np.cumsum(IN_SPLITS)[:-1]]
    def to_heads(t, H):
        return t.reshape(B, T, H, -1).transpose(0, 2, 1, 3)
    def from_heads(t):
        return t.transpose(0, 2, 1, 3).reshape(B, T, -1)
    for l in range(DEPTH):
        mod = jax.nn.silu(c) @ w_ada[l] + b_ada[l]
        shift_a, scale_a, gate_a, shift_f, scale_f, gate_f = jnp.split(mod, 6, axis=-1)
        h = modulate(rms_norm(x, norm_mix[l]), shift_a, scale_a)
        proj = h @ w_in[l]
        qa, ka, va, qr, kr, vr, gr = jnp.split(proj, split_idx, axis=-1)
        qa = rotary(to_heads(qa, MOBA_HEADS), positions, ROPE_DIMS, moba_inv)
        ka = rotary(to_heads(ka, MOBA_HEADS), positions, ROPE_DIMS, moba_inv)
        o_a = from_heads(moba_attention(qa, ka, to_heads(va, MOBA_HEADS)))
        qr = rotary(to_heads(qr, RET_HEADS), positions, RET_HEAD_DIM, ret_inv)
        kr = rotary(to_heads(kr, RET_HEADS), positions, RET_HEAD_DIM, ret_inv)
        o_r = retention(qr, kr, to_heads(vr, RET_HEADS))
        mu = jnp.mean(o_r, axis=-1, keepdims=True)
        var = jnp.mean(jnp.square(o_r - mu), axis=-1, keepdims=True)
        o_r = ((o_r - mu) * lax.rsqrt(var + NORM_EPS)).astype(x.dtype)
        o_r = from_heads(o_r) * jax.nn.silu(gr)
        mix = jnp.concatenate([o_a, o_r], axis=-1) @ w_out[l]
        x = x + gate_a[:, None, :] * mix
        h = modulate(rms_norm(x, norm_ffn[l]), shift_f, scale_f)
        y = moe_ffn(h.reshape(B * T, D), w_router[l], router_bias[l], w_gate[l], w_up[l], w_down[l],
                    w_sh_gate[l], w_sh_up[l], w_sh_down[l]).reshape(B, T, D)
        x = x + gate_f[:, None, :] * y
    return rms_norm(x, norm_out)
```

```python
import functools

import jax
import jax.numpy as jnp
import numpy as np
from jax import lax
from jax.experimental import pallas as pl
from jax.experimental.pallas import tpu as pltpu

MOBA_HEADS = 8
MOBA_HEAD_DIM = 128
MOBA_BLOCK = 256
MOBA_TOPK = 3
ROPE_THETA = 500000.0
ROPE_DIMS = 32
RET_HEADS = 4
RET_HEAD_DIM = 256
RET_ROPE_BASE = 10000.0
N_EXPERTS = 64
TOP_K = 8
N_GROUPS = 8
TOPK_GROUPS = 4
ROUTE_SCALE = 2.5
NORM_EPS = 1e-6
NEG = -1e30

LANES = 128
VMEM_LIMIT = 56 * 1024 * 1024

RET_CHUNK = 256
ROW_TILE = 256
TOK_TILE = 256
FIN_TILE = 128

F32 = jnp.float32
BF16 = jnp.bfloat16


def _cparams(sem):
    return pltpu.CompilerParams(dimension_semantics=sem, vmem_limit_bytes=VMEM_LIMIT)


def _rms_mod(xf, g, shift, scale):
    ms = jnp.mean(xf * xf, axis=-1, keepdims=True)
    y = xf * lax.rsqrt(ms + NORM_EPS) * g
    return y * (1.0 + scale) + shift


def _tables_kernel(pos_ref, invm_ref, invr_ref, mc_ref, ms1_ref, ms2_ref, rc_ref, rs_ref):
    pos = pos_ref[0].astype(F32)
    angm = pos * invm_ref[...]
    lane = lax.broadcasted_iota(jnp.int32, angm.shape, 1)
    half = ROPE_DIMS // 2
    c = jnp.cos(angm)
    s = jnp.sin(angm)
    mc_ref[0] = c
    ms1_ref[0] = jnp.where(lane < half, -s, 0.0)
    ms2_ref[0] = jnp.where((lane >= half) & (lane < ROPE_DIMS), s, 0.0)
    angr = pos * invr_ref[...]
    rc_ref[0] = jnp.cos(angr)
    rs_ref[0] = jnp.sin(angr)


def _rope_tables(positions):
    B, T = positions.shape
    tm = 512
    half = ROPE_DIMS // 2
    moba_inv = ROPE_THETA ** (-(jnp.arange(half, dtype=F32) * 2.0 / ROPE_DIMS))
    invm = jnp.concatenate([moba_inv, moba_inv, jnp.zeros((LANES - ROPE_DIMS,), F32)])[None, :]
    invr = (RET_ROPE_BASE ** (-jnp.linspace(0.0, 1.0, RET_HEAD_DIM // 2, dtype=F32)))[None, :]
    pos3 = positions.reshape(B, T, 1)
    tab = jax.ShapeDtypeStruct((B, T, LANES), F32)
    spec = pl.BlockSpec((1, tm, LANES), lambda b, i: (b, i, 0))
    return pl.pallas_call(
        _tables_kernel,
        grid=(B, T // tm),
        in_specs=[pl.BlockSpec((1, tm, 1), lambda b, i: (b, i, 0)),
                  pl.BlockSpec((1, LANES), lambda b, i: (0, 0)),
                  pl.BlockSpec((1, LANES), lambda b, i: (0, 0))],
        out_specs=[spec] * 5,
        out_shape=[tab] * 5,
        compiler_params=_cparams(("parallel", "parallel")),
        name="rope_tables",
    )(pos3, invm, invr)


def _ada_kernel(c_ref, w_ref, b_ref, o_ref):
    c = c_ref[...]
    sc = c * jax.nn.sigmoid(c)
    o_ref[...] = jnp.dot(sc, w_ref[...], preferred_element_type=F32,
                         precision=lax.Precision.HIGHEST) + b_ref[...]


def _ada(c, w_ada, b_ada):
    B, D = c.shape
    n_out = w_ada.shape[1]
    rows = 8
    tn = 1024
    c8 = jnp.zeros((rows, D), F32).at[:B].set(c)
    mod = pl.pallas_call(
        _ada_kernel,
        grid=(n_out // tn,),
        in_specs=[pl.BlockSpec((rows, D), lambda j: (0, 0)),
                  pl.BlockSpec((D, tn), lambda j: (0, j)),
                  pl.BlockSpec((1, tn), lambda j: (0, j))],
        out_specs=pl.BlockSpec((rows, tn), lambda j: (0, j)),
        out_shape=jax.ShapeDtypeStruct((rows, n_out), F32),
        compiler_params=_cparams(("parallel",)),
        name="adaln_mod",
    )(c8, w_ada, b_ada[None, :])
    return mod[:B]


def _inproj_kernel(x_ref, g_ref, sh_ref, sc_ref, w_ref, mc_ref, ms1_ref, ms2_ref, rc_ref, rs_ref,
                   o_ref, hn_ref, *, tn, moba_tiles, ret_lo, ret_k_lo, ret_hi):
    j = pl.program_id(2)

    @pl.when(j == 0)
    def _():
        h = _rms_mod(x_ref[0], g_ref[...], sh_ref[0], sc_ref[0])
        hn_ref[...] = h.astype(BF16)

    acc = jnp.dot(hn_ref[...], w_ref[...], preferred_element_type=F32)

    @pl.when(j < moba_tiles)
    def _():
        c, s1, s2 = mc_ref[0], ms1_ref[0], ms2_ref[0]
        half = ROPE_DIMS // 2
        for g in range(tn // LANES):
            a = acc[:, g * LANES:(g + 1) * LANES]
            r = a * c + pltpu.roll(a, LANES - half, 1) * s1 + pltpu.roll(a, half, 1) * s2
            o_ref[0, :, g * LANES:(g + 1) * LANES] = r.astype(o_ref.dtype)

    @pl.when((j >= ret_lo) & (j < ret_hi))
    def _():
        c, s = rc_ref[0], rs_ref[0]
        fac = jnp.where(j >= ret_k_lo, RET_HEAD_DIM ** -0.5, 1.0).astype(F32)
        hw = RET_HEAD_DIM // 2
        for g in range(tn // RET_HEAD_DIM):
            x1 = acc[:, g * RET_HEAD_DIM:g * RET_HEAD_DIM + hw]
            x2 = acc[:, g * RET_HEAD_DIM + hw:(g + 1) * RET_HEAD_DIM]
            o_ref[0, :, g * RET_HEAD_DIM:g * RET_HEAD_DIM + hw] = ((x1 * c - x2 * s) * fac).astype(o_ref.dtype)
            o_ref[0, :, g * RET_HEAD_DIM + hw:(g + 1) * RET_HEAD_DIM] = ((x2 * c + x1 * s) * fac).astype(o_ref.dtype)

    @pl.when(((j >= moba_tiles) & (j < ret_lo)) | (j >= ret_hi))
    def _():
        o_ref[0] = acc.astype(o_ref.dtype)


def _in_proj(x, g, shift, scale, w_bf, tabs, moba_w, ret_w):
    B, T, D = x.shape
    NC = w_bf.shape[1]
    tm, tn = 512, 512
    mc, ms1, ms2, rc, rs = tabs
    kern = functools.partial(
        _inproj_kernel, tn=tn,
        moba_tiles=2 * moba_w // tn,
        ret_lo=3 * moba_w // tn,
        ret_k_lo=(3 * moba_w + ret_w) // tn,
        ret_hi=(3 * moba_w + 2 * ret_w) // tn)
    tab_spec = pl.BlockSpec((1, tm, LANES), lambda b, i, j: (b, i, 0))
    vec_spec = pl.BlockSpec((1, 1, D), lambda b, i, j: (b, 0, 0))
    return pl.pallas_call(
        kern,
        grid=(B, T // tm, NC // tn),
        in_specs=[pl.BlockSpec((1, tm, D), lambda b, i, j: (b, i, 0)),
                  pl.BlockSpec((1, D), lambda b, i, j: (0, 0)),
                  vec_spec, vec_spec,
                  pl.BlockSpec((D, tn), lambda b, i, j: (0, j)),
                  tab_spec, tab_spec, tab_spec, tab_spec, tab_spec],
        out_specs=pl.BlockSpec((1, tm, tn), lambda b, i, j: (b, i, j)),
        out_shape=jax.ShapeDtypeStruct((B, T, NC), BF16),
        scratch_shapes=[pltpu.VMEM((tm, D), BF16)],
        compiler_params=_cparams(("parallel", "parallel", "arbitrary")),
        name="in_proj",
    )(x, g[None, :], shift[:, None, :], scale[:, None, :], w_bf, mc, ms1, ms2, rc, rs)


def _moba_kernel(q_ref, k_ref, v_ref, o_ref, km_ref, *, nb):
    qb = pl.program_id(2)
    BS = MOBA_BLOCK

    @pl.when(qb == 0)
    def _():
        km_ref[...] = jnp.zeros_like(km_ref)
        for n in range(nb):
            kb = k_ref[0, n * BS:(n + 1) * BS, :].astype(F32)
            km_ref[n:n + 1, :] = jnp.sum(kb, axis=0, keepdims=True) * (1.0 / BS)

    q = q_ref[0]
    qf = q.astype(F32)
    gate = lax.dot_general(qf, km_ref[...], (((1,), (1,)), ((), ())),
                           preferred_element_type=F32, precision=lax.Precision.HIGHEST)
    lane = lax.broadcasted_iota(jnp.int32, gate.shape, 1)
    lane_f = lane.astype(F32)
    g = jnp.where(lane < qb, gate, NEG)
    sel = jnp.zeros(gate.shape, F32)
    for _ in range(MOBA_TOPK):
        m = jnp.max(g, axis=1, keepdims=True)
        idx = jnp.min(jnp.where(g == m, lane_f, float(LANES)), axis=1, keepdims=True)
        pick = lane_f == idx
        sel = jnp.where(pick & (m > 0.5 * NEG), 1.0, sel)
        g = jnp.where(pick, -jnp.inf, g)

    scale = MOBA_HEAD_DIM ** -0.5
    qs = (qf * scale).astype(BF16)
    nt = (((1,), (1,)), ((), ()))

    own = pl.multiple_of(qb * BS, BS)
    s = lax.dot_general(qs, k_ref[0, pl.ds(own, BS), :], nt, preferred_element_type=F32)
    row = lax.broadcasted_iota(jnp.int32, s.shape, 0)
    col = lax.broadcasted_iota(jnp.int32, s.shape, 1)
    s = jnp.where(col <= row, s, NEG)
    m0 = jnp.max(s, axis=1, keepdims=True)
    p = jnp.exp(s - m0)
    l0 = jnp.sum(p, axis=1, keepdims=True)
    acc0 = jnp.dot(p.astype(BF16), v_ref[0, pl.ds(own, BS), :], preferred_element_type=F32)

    def body(n, carry):
        m, l, acc = carry
        selcol = jnp.max(jnp.where(lane == n, sel, 0.0), axis=1, keepdims=True)
        off = pl.multiple_of(n * BS, BS)
        sn = lax.dot_general(qs, k_ref[0, pl.ds(off, BS), :], nt, preferred_element_type=F32)
        sn = jnp.where(selcol > 0.0, sn, NEG)
        m_new = jnp.maximum(m, jnp.max(sn, axis=1, keepdims=True))
        alpha = jnp.exp(m - m_new)
        pn = jnp.exp(sn - m_new)
        l = alpha * l + jnp.sum(pn, axis=1, keepdims=True)
        acc = alpha * acc + jnp.dot(pn.astype(BF16), v_ref[0, pl.ds(off, BS), :],
                                    preferred_element_type=F32)
        return m_new, l, acc

    _, l, acc = lax.fori_loop(0, qb, body, (m0, l0, acc0))
    o_ref[0] = (acc / l).astype(o_ref.dtype)


def _moba(proj, B, T, moba_w):
    H, hd, BS = MOBA_HEADS, MOBA_HEAD_DIM, MOBA_BLOCK
    nb = T // BS
    hpw = moba_w // hd
    return pl.pallas_call(
        functools.partial(_moba_kernel, nb=nb),
        grid=(B, H, nb),
        in_specs=[pl.BlockSpec((1, BS, hd), lambda b, h, i: (b, i, h)),
                  pl.BlockSpec((1, T, hd), lambda b, h, i: (b, 0, hpw + h)),
                  pl.BlockSpec((1, T, hd), lambda b, h, i: (b, 0, 2 * hpw + h))],
        out_specs=pl.BlockSpec((1, BS, hd), lambda b, h, i: (b, i, h)),
        out_shape=jax.ShapeDtypeStruct((B, T, moba_w), BF16),
        scratch_shapes=[pltpu.VMEM((LANES, hd), F32)],
        compiler_params=_cparams(("parallel", "parallel", "arbitrary")),
        name="moba_attn",
    )(proj, proj, proj)


def _ret_kernel(q_ref, k_ref, v_ref, g_ref, dm_ref, xi_ref, zeta_ref, cd_ref, o_ref, s_ref):
    c = pl.program_id(1)
    d = RET_HEAD_DIM

    @pl.when(c == 0)
    def _():
        s_ref[...] = jnp.zeros_like(s_ref)

    for h in range(RET_HEADS):
        sl = slice(h * d, (h + 1) * d)
        q = q_ref[0, :, sl]
        k = k_ref[0, :, sl]
        v = v_ref[0, :, sl]
        inner = lax.dot_general(q, k, (((1,), (1,)), ((), ())), preferred_element_type=F32) * dm_ref[h]
        S = s_ref[h]
        o = (jnp.dot(inner.astype(BF16), v, preferred_element_type=F32)
             + jnp.dot(q, S.astype(BF16), preferred_element_type=F32) * xi_ref[h])
        kz = (k.astype(F32) * zeta_ref[h]).astype(BF16)
        s_ref[h] = S * cd_ref[h] + lax.dot_general(kz, v, (((0,), (0,)), ((), ())),
                                                   preferred_element_type=F32)
        mu = jnp.mean(o, axis=-1, keepdims=True)
        dlt = o - mu
        var = jnp.mean(dlt * dlt, axis=-1, keepdims=True)
        on = dlt * lax.rsqrt(var + NORM_EPS)
        gg = g_ref[0, :, sl].astype(F32)
        o_ref[0, :, sl] = (on * (gg * jax.nn.sigmoid(gg))).astype(o_ref.dtype)


def _retention(proj, B, T, moba_w, ret_w):
    C, H = RET_CHUNK, RET_HEADS
    gamma = 1.0 - jnp.exp2(-5.0 - jnp.arange(H, dtype=F32))
    log_g = jnp.log(gamma)
    pos = jnp.arange(C, dtype=F32)
    diff = pos[:, None] - pos[None, :]
    dmask = jnp.where(diff >= 0, jnp.exp(jnp.maximum(diff, 0.0) * log_g[:, None, None]), 0.0)
    xi = jnp.exp((pos + 1.0) * log_g[:, None])[:, :, None]
    zeta = jnp.exp((C - 1.0 - pos) * log_g[:, None])[:, :, None]
    cd = jnp.exp(C * log_g)[:, None, None]
    base = 3 * moba_w // ret_w
    col = lambda off: pl.BlockSpec((1, C, ret_w), lambda b, c: (b, c, base + off))
    full = lambda shp: pl.BlockSpec(shp, lambda b, c: (0,) * len(shp))
    return pl.pallas_call(
        _ret_kernel,
        grid=(B, T // C),
        in_specs=[col(0), col(1), col(2), col(3),
                  full((H, C, C)), full((H, C, 1)), full((H, C, 1)), full((H, 1, 1))],
        out_specs=pl.BlockSpec((1, C, ret_w), lambda b, c: (b, c, 0)),
        out_shape=jax.ShapeDtypeStruct((B, T, ret_w), BF16),
        scratch_shapes=[pltpu.VMEM((H, RET_HEAD_DIM, RET_HEAD_DIM), F32)],
        compiler_params=_cparams(("parallel", "arbitrary")),
        name="retention",
    )(proj, proj, proj, proj, dmask, xi, zeta, cd)


def _outproj_kernel(oa_ref, or_ref, w_ref, x_ref, ga_ref, g_ref, sh_ref, sc_ref, wr_ref,
                    x1_ref, h_ref, lg_ref, *, moba_w):
    mix = (jnp.dot(oa_ref[0], w_ref[:moba_w, :], preferred_element_type=F32)
           + jnp.dot(or_ref[0], w_ref[moba_w:, :], preferred_element_type=F32))
    x1 = x_ref[0] + ga_ref[0] * mix
    x1_ref[0] = x1
    h = _rms_mod(x1, g_ref[...], sh_ref[0], sc_ref[0])
    h_ref[0] = h
    lg_ref[0] = jnp.dot(h, wr_ref[...], preferred_element_type=F32, precision=lax.Precision.HIGHEST)


def _out_proj(o_a, o_r, w_bf, x, gate_a, g, shift, scale, w_router):
    B, T, D = x.shape
    moba_w, ret_w = o_a.shape[-1], o_r.shape[-1]
    E = w_router.shape[1]
    tm = 256
    vec = pl.BlockSpec((1, 1, D), lambda b, i: (b, 0, 0))
    row = lambda w: pl.BlockSpec((1, tm, w), lambda b, i: (b, i, 0))
    return pl.pallas_call(
        functools.partial(_outproj_kernel, moba_w=moba_w),
        grid=(B, T // tm),
        in_specs=[row(moba_w), row(ret_w),
                  pl.BlockSpec((moba_w + ret_w, D), lambda b, i: (0, 0)),
                  row(D), vec,
                  pl.BlockSpec((1, D), lambda b, i: (0, 0)),
                  vec, vec,
                  pl.BlockSpec((D, E), lambda b, i: (0, 0))],
        out_specs=[row(D), row(D), row(E)],
        out_shape=[jax.ShapeDtypeStruct((B, T, D), F32),
                   jax.ShapeDtypeStruct((B, T, D), F32),
                   jax.ShapeDtypeStruct((B, T, E), F32)],
        compiler_params=_cparams(("parallel", "parallel")),
        name="out_proj",
    )(o_a, o_r, w_bf, x, gate_a[:, None, :], g[None, :], shift[:, None, :], scale[:, None, :], w_router)


def _first_argmax(v, lane_f, width):
    m = jnp.max(v, axis=1, keepdims=True)
    idx = jnp.min(jnp.where(v == m, lane_f, float(width)), axis=1, keepdims=True)
    return m, idx


def _route_kernel(lg_ref, b_ref, selr_ref, wf_ref, rank_ref, cnt_ref, carry_ref):
    i = pl.program_id(0)

    @pl.when(i == 0)
    def _():
        carry_ref[...] = jnp.zeros_like(carry_ref)

    E = N_EXPERTS
    gsz = E // N_GROUPS
    s = jax.nn.sigmoid(lg_ref[...])
    biased = s + b_ref[...]
    tm = s.shape[0]
    lane = lax.broadcasted_iota(jnp.int32, (tm, E), 1)
    lane_f = lane.astype(F32)
    grp = lax.shift_right_logical(lane, gsz.bit_length() - 1)

    gscore = jnp.full((tm, E), -jnp.inf, F32)
    for gi in range(N_GROUPS):
        v = jnp.where(grp == gi, biased, -jnp.inf)
        m1, i1 = _first_argmax(v, lane_f, E)
        m2 = jnp.max(jnp.where(lane_f == i1, -jnp.inf, v), axis=1, keepdims=True)
        gscore = jnp.where(lane == gi, m1 + m2, gscore)

    emask = jnp.zeros((tm, E), jnp.bool_)
    grp_f = grp.astype(F32)
    for _ in range(TOPK_GROUPS):
        _, gi = _first_argmax(gscore, lane_f, E)
        emask = emask | (grp_f == gi)
        gscore = jnp.where(lane_f == gi, -jnp.inf, gscore)

    cand = jnp.where(emask, biased, NEG)
    selr = jnp.zeros((tm, E), F32)
    for r in range(TOP_K):
        _, ei = _first_argmax(cand, lane_f, E)
        pick = lane_f == ei
        selr = jnp.where(pick, float(r + 1), selr)
        cand = jnp.where(pick, -jnp.inf, cand)

    chosen = selr > 0.0
    w = jnp.where(chosen, s, 0.0)
    wsum = jnp.sum(w, axis=1, keepdims=True)
    selr_ref[...] = selr
    wf_ref[...] = w / wsum * ROUTE_SCALE

    onehot = chosen.astype(BF16)
    r_i = lax.broadcasted_iota(jnp.int32, (tm, tm), 0)
    c_i = lax.broadcasted_iota(jnp.int32, (tm, tm), 1)
    tri = (c_i < r_i).astype(BF16)
    carry = carry_ref[...]
    rank_ref[...] = jnp.dot(tri, onehot, preferred_element_type=F32) + carry
    carry = carry + jnp.sum(chosen.astype(F32), axis=0, keepdims=True)
    carry_ref[...] = carry
    cnt_ref[...] = carry


def _route(logits, bias):
    N, E = logits.shape
    tm = 512
    blk = pl.BlockSpec((tm, E), lambda i: (i, 0))
    one = pl.BlockSpec((1, E), lambda i: (0, 0))
    full = jax.ShapeDtypeStruct((N, E), F32)
    return pl.pallas_call(
        _route_kernel,
        grid=(N // tm,),
        in_specs=[blk, one],
        out_specs=[blk, blk, blk, one],
        out_shape=[full, full, full, jax.ShapeDtypeStruct((1, E), F32)],
        scratch_shapes=[pltpu.VMEM((1, E), F32)],
        compiler_params=_cparams(("arbitrary",)),
        name="route_topk",
    )(logits, bias[None, :])


def _dest_kernel(selr_ref, wf_ref, rank_ref, ps_ref, dest_ref, wk_ref):
    selr = selr_ref[...]
    destfull = rank_ref[...] + ps_ref[...]
    wf = wf_ref[...]
    for r in range(TOP_K):
        hit = selr == float(r + 1)
        dest_ref[:, r:r + 1] = jnp.sum(jnp.where(hit, destfull, 0.0), axis=1, keepdims=True).astype(jnp.int32)
        wk_ref[:, r:r + 1] = jnp.sum(jnp.where(hit, wf, 0.0), axis=1, keepdims=True)


def _dest(selr, wf, rank, pstart_f):
    N, E = selr.shape
    tm = 512
    blk = pl.BlockSpec((tm, E), lambda i: (i, 0))
    outb = pl.BlockSpec((tm, TOP_K), lambda i: (i, 0))
    return pl.pallas_call(
        _dest_kernel,
        grid=(N // tm,),
        in_specs=[blk, blk, blk, pl.BlockSpec((1, E), lambda i: (0, 0))],
        out_specs=[outb, outb],
        out_shape=[jax.ShapeDtypeStruct((N, TOP_K), jnp.int32), jax.ShapeDtypeStruct((N, TOP_K), F32)],
        compiler_params=_cparams(("parallel",)),
        name="route_dest",
    )(selr, wf, rank, pstart_f)


def _row_copy(src, s_row, dst, d_row, n, sem):
    return pltpu.make_async_copy(src.at[pl.ds(s_row, n)], dst.at[pl.ds(d_row, n)], sem)


def _dispatch_kernel(padlo_ref, padn_ref, dest_ref, h_ref, z_ref, xs_ref, sem, zsem, *, tt, n_exp):
    i = pl.program_id(0)

    def issue(t, _):
        for k in range(TOP_K):
            _row_copy(h_ref, t, xs_ref, dest_ref[0, 0, t * TOP_K + k], 1, sem).start()
        return 0

    lax.fori_loop(0, tt, issue, 0)

    def each_pad(fn):
        def per_expert(e, _):
            lo = padlo_ref[e]

            def one(r, _):
                fn(lo + r)
                return 0

            lax.fori_loop(0, padn_ref[e], one, 0)
            return 0

        lax.fori_loop(0, n_exp, per_expert, 0)

    @pl.when(i == 0)
    def _():
        each_pad(lambda r: _row_copy(z_ref, 0, xs_ref, r, 1, zsem).start())

    def drain(t, _):
        for k in range(TOP_K):
            _row_copy(h_ref, 0, xs_ref, 0, 1, sem).wait()
        return 0

    lax.fori_loop(0, tt, drain, 0)

    @pl.when(i == 0)
    def _():
        each_pad(lambda r: _row_copy(z_ref, 0, xs_ref, 0, 1, zsem).wait())


def _dispatch(h2, dest, pad_lo, pad_n, R):
    N, D = h2.shape
    tt = TOK_TILE
    dest3 = dest.reshape(N // tt, 1, tt * TOP_K)
    zeros = jnp.zeros((8, D), h2.dtype)
    grid_spec = pltpu.PrefetchScalarGridSpec(
        num_scalar_prefetch=2,
        grid=(N // tt,),
        in_specs=[pl.BlockSpec((1, 1, tt * TOP_K), lambda i, lo, n: (i, 0, 0), memory_space=pltpu.SMEM),
                  pl.BlockSpec((tt, D), lambda i, lo, n: (i, 0)),
                  pl.BlockSpec((8, D), lambda i, lo, n: (0, 0))],
        out_specs=pl.BlockSpec(memory_space=pl.ANY),
        scratch_shapes=[pltpu.SemaphoreType.DMA(()), pltpu.SemaphoreType.DMA(())],
    )
    return pl.pallas_call(
        functools.partial(_dispatch_kernel, tt=tt, n_exp=N_EXPERTS),
        grid_spec=grid_spec,
        out_shape=jax.ShapeDtypeStruct((R, D), h2.dtype),
        compiler_params=_cparams(("arbitrary",)),
        name="moe_dispatch",
    )(pad_lo, pad_n, dest3, h2, zeros)


def _expert_kernel(te_ref, nu_ref, xs_ref, wg_ref, wu_ref, wd_ref, y_ref):
    i = pl.program_id(0)

    @pl.when(i < nu_ref[0])
    def _():
        xb = xs_ref[...].astype(BF16)
        a = jnp.dot(xb, wg_ref[0], preferred_element_type=F32)
        u = jnp.dot(xb, wu_ref[0], preferred_element_type=F32)
        hmid = (a * jax.nn.sigmoid(a) * u).astype(BF16)
        y_ref[...] = jnp.dot(hmid, wd_ref[0], preferred_element_type=F32)


def _experts(xs, tile_expert, n_used, wg, wu, wd):
    R, D = xs.shape
    M = ROW_TILE
    F = wg.shape[-1]
    row = lambda i, te, nu: (jnp.minimum(i, nu[0] - 1), 0)
    wmap = lambda i, te, nu: (te[i], 0, 0)
    grid_spec = pltpu.PrefetchScalarGridSpec(
        num_scalar_prefetch=2,
        grid=(R // M,),
        in_specs=[pl.BlockSpec((M, D), row),
                  pl.BlockSpec((1, D, F), wmap),
                  pl.BlockSpec((1, D, F), wmap),
                  pl.BlockSpec((1, F, D), wmap)],
        out_specs=pl.BlockSpec((M, D), row),
    )
    return pl.pallas_call(
        _expert_kernel,
        grid_spec=grid_spec,
        out_shape=jax.ShapeDtypeStruct((R, D), F32),
        compiler_params=_cparams(("arbitrary",)),
        name="moe_experts",
    )(tile_expert, n_used, xs, wg, wu, wd)


def _final_kernel(dest_ref, h_ref, wsg_ref, wsu_ref, wsd_ref, x1_ref, gf_ref, g_ref, wk_ref, y_ref,
                  o_ref, ybuf, sem, *, tt):
    def issue(t, _):
        for k in range(TOP_K):
            pltpu.make_async_copy(y_ref.at[pl.ds(dest_ref[0, 0, t * TOP_K + k], 1)],
                                  ybuf.at[k, pl.ds(t, 1)], sem).start()
        return 0

    lax.fori_loop(0, tt, issue, 0)

    hb = h_ref[...].astype(BF16)
    a = jnp.dot(hb, wsg_ref[...], preferred_element_type=F32)
    u = jnp.dot(hb, wsu_ref[...], preferred_element_type=F32)
    hmid = (a * jax.nn.sigmoid(a) * u).astype(BF16)
    total = jnp.dot(hmid, wsd_ref[...], preferred_element_type=F32)

    def drain(t, _):
        for k in range(TOP_K):
            pltpu.make_async_copy(y_ref.at[pl.ds(0, 1)], ybuf.at[k, pl.ds(0, 1)], sem).wait()
        return 0

    lax.fori_loop(0, tt, drain, 0)

    wk = wk_ref[...]
    for k in range(TOP_K):
        total = total + wk[:, k:k + 1] * ybuf[k]
    x2 = x1_ref[...] + gf_ref[0] * total
    ms = jnp.mean(x2 * x2, axis=-1, keepdims=True)
    o_ref[...] = x2 * lax.rsqrt(ms + NORM_EPS) * g_ref[...]


def _final(h2, wsg, wsu, wsd, x1, gate_f, norm_out, dest, wk, y, T):
    N, D = h2.shape
    F = wsg.shape[1]
    tt = FIN_TILE
    per_b = T // tt
    dest3 = dest.reshape(N // tt, 1, tt * TOP_K)
    rowb = pl.BlockSpec((tt, D), lambda i: (i, 0))
    const = lambda shp: pl.BlockSpec(shp, lambda i: (0,) * len(shp))
    return pl.pallas_call(
        functools.partial(_final_kernel, tt=tt),
        grid=(N // tt,),
        in_specs=[pl.BlockSpec((1, 1, tt * TOP_K), lambda i: (i, 0, 0), memory_space=pltpu.SMEM),
                  rowb, const((D, F)), const((D, F)), const((F, D)), rowb,
                  pl.BlockSpec((1, 1, D), lambda i: (i // per_b, 0, 0)),
                  const((1, D)),
                  pl.BlockSpec((tt, TOP_K), lambda i: (i, 0)),
                  pl.BlockSpec(memory_space=pl.ANY)],
        out_specs=rowb,
        out_shape=jax.ShapeDtypeStruct((N, D), F32),
        scratch_shapes=[pltpu.VMEM((TOP_K, tt, D), F32), pltpu.SemaphoreType.DMA(())],
        compiler_params=_cparams(("arbitrary",)),
        name="moe_combine_final",
    )(dest3, h2, wsg, wsu, wsd, x1, gate_f[:, None, :], norm_out[None, :], wk, y)


def kernel(x, c, positions, w_ada, b_ada, norm_mix, norm_ffn, norm_out, w_in, w_out, w_router, router_bias,
           w_gate, w_up, w_down, w_sh_gate, w_sh_up, w_sh_down):
    B, T, D = x.shape
    depth = w_ada.shape[0]
    assert depth == 1, "the final rmsnorm is fused into the layer's last kernel"
    moba_w = MOBA_HEADS * MOBA_HEAD_DIM
    ret_w = RET_HEADS * RET_HEAD_DIM
    N = B * T
    E, M = N_EXPERTS, ROW_TILE
    R = N * TOP_K + E * M
    tabs = _rope_tables(positions)

    for l in range(depth):
        mod = _ada(c, w_ada[l], b_ada[l])
        shift_a, scale_a, gate_a, shift_f, scale_f, gate_f = jnp.split(mod, 6, axis=-1)

        proj = _in_proj(x, norm_mix[l], shift_a, scale_a, w_in[l].astype(BF16), tabs, moba_w, ret_w)
        o_a = _moba(proj, B, T, moba_w)
        o_r = _retention(proj, B, T, moba_w, ret_w)
        x1, h2, logits = _out_proj(o_a, o_r, w_out[l].astype(BF16), x, gate_a, norm_ffn[l],
                                   shift_f, scale_f, w_router[l])

        h2f = h2.reshape(N, D)
        selr, wf, rank, counts = _route(logits.reshape(N, E), router_bias[l])
        cnt = counts[0].astype(jnp.int32)
        pcnt = (cnt + M - 1) // M * M
        pend = jnp.cumsum(pcnt)
        pstart = pend - pcnt
        tile_expert = jnp.minimum(
            jnp.sum(pend[None, :] <= (jnp.arange(R // M, dtype=jnp.int32) * M)[:, None], axis=1), E - 1
        ).astype(jnp.int32)
        n_used = (pend[-1:] // M).astype(jnp.int32)
        dest, wk = _dest(selr, wf, rank, pstart.astype(F32)[None, :])

        xs = _dispatch(h2f, dest, (pstart + cnt).astype(jnp.int32), (pcnt - cnt).astype(jnp.int32), R)
        y = _experts(xs, tile_expert, n_used, w_gate[l].astype(BF16), w_up[l].astype(BF16),
                     w_down[l].astype(BF16))
        out = _final(h2f, w_sh_gate[l].astype(BF16), w_sh_up[l].astype(BF16), w_sh_down[l].astype(BF16),
                     x1.reshape(N, D), gate_f, norm_out, dest, wk, y, T)
        x = out.reshape(B, T, D)
    return x
```

```python
import functools

import jax
import jax.numpy as jnp
from jax import lax
from jax.experimental import pallas as pl
from jax.experimental.pallas import tpu as pltpu

MOBA_HEADS = 8
MOBA_HEAD_DIM = 128
MOBA_BLOCK = 256
MOBA_TOPK = 3
ROPE_THETA = 500000.0
ROPE_DIMS = 32
RET_HEADS = 4
RET_HEAD_DIM = 256
RET_ROPE_BASE = 10000.0
N_EXPERTS = 64
TOP_K = 8
N_GROUPS = 8
TOPK_GROUPS = 4
ROUTE_SCALE = 2.5
NORM_EPS = 1e-6
NEG = -1e30

LANES = 128
SUBLANES = 8
VMEM_LIMIT = 56 * 1024 * 1024

RET_CHUNK = 256
ROW_TILE = 256
TOK_TILE = 256

F32 = jnp.float32
BF16 = jnp.bfloat16
U32 = jnp.uint32


def _cparams(sem):
    return pltpu.CompilerParams(dimension_semantics=sem, vmem_limit_bytes=VMEM_LIMIT)


def _rms_mod(xf, g, shift, scale):
    ms = jnp.mean(xf * xf, axis=-1, keepdims=True)
    y = xf * lax.rsqrt(ms + NORM_EPS) * g
    return y * (1.0 + scale) + shift


def _pack_pair(lo, hi):
    lo_b = lax.bitcast_convert_type(lo.astype(BF16).astype(F32), U32)
    hi_b = lax.bitcast_convert_type(hi.astype(BF16).astype(F32), U32)
    return (lo_b >> 16) | hi_b


def _unpack_pair(p):
    lo = lax.bitcast_convert_type(p << 16, F32)
    hi = lax.bitcast_convert_type(p & jnp.uint32(0xFFFF0000), F32)
    return lo, hi


def _tables_kernel(pos_ref, invm_ref, invr_ref, mc_ref, ms1_ref, ms2_ref, rc_ref, rs_ref):
    pos = pos_ref[0].astype(F32)
    angm = pos * invm_ref[...]
    lane = lax.broadcasted_iota(jnp.int32, angm.shape, 1)
    half = ROPE_DIMS // 2
    c = jnp.cos(angm)
    s = jnp.sin(angm)
    mc_ref[0] = c
    ms1_ref[0] = jnp.where(lane < half, -s, 0.0)
    ms2_ref[0] = jnp.where((lane >= half) & (lane < ROPE_DIMS), s, 0.0)
    angr = pos * invr_ref[...]
    rc_ref[0] = jnp.cos(angr)
    rs_ref[0] = jnp.sin(angr)


def _rope_tables(positions):
    B, T = positions.shape
    tm = 512
    half = ROPE_DIMS // 2
    moba_inv = ROPE_THETA ** (-(jnp.arange(half, dtype=F32) * 2.0 / ROPE_DIMS))
    invm = jnp.concatenate([moba_inv, moba_inv, jnp.zeros((LANES - ROPE_DIMS,), F32)])[None, :]
    invr = (RET_ROPE_BASE ** (-jnp.linspace(0.0, 1.0, RET_HEAD_DIM // 2, dtype=F32)))[None, :]
    pos3 = positions.reshape(B, T, 1)
    tab = jax.ShapeDtypeStruct((B, T, LANES), F32)
    spec = pl.BlockSpec((1, tm, LANES), lambda b, i: (b, i, 0))
    return pl.pallas_call(
        _tables_kernel,
        grid=(B, T // tm),
        in_specs=[pl.BlockSpec((1, tm, 1), lambda b, i: (b, i, 0)),
                  pl.BlockSpec((1, LANES), lambda b, i: (0, 0)),
                  pl.BlockSpec((1, LANES), lambda b, i: (0, 0))],
        out_specs=[spec] * 5,
        out_shape=[tab] * 5,
        compiler_params=_cparams(("parallel", "parallel")),
        name="rope_tables",
    )(pos3, invm, invr)


def _ada_kernel(c_ref, w_ref, b_ref, o_ref):
    c = c_ref[...]
    sc = c * jax.nn.sigmoid(c)
    o_ref[...] = jnp.dot(sc, w_ref[...], preferred_element_type=F32,
                         precision=lax.Precision.HIGHEST) + b_ref[...]


def _ada(c, w_ada, b_ada):
    B, D = c.shape
    n_out = w_ada.shape[1]
    rows = SUBLANES
    tn = 1024
    c8 = jnp.zeros((rows, D), F32).at[:B].set(c)
    mod = pl.pallas_call(
        _ada_kernel,
        grid=(n_out // tn,),
        in_specs=[pl.BlockSpec((rows, D), lambda j: (0, 0)),
                  pl.BlockSpec((D, tn), lambda j: (0, j)),
                  pl.BlockSpec((1, tn), lambda j: (0, j))],
        out_specs=pl.BlockSpec((rows, tn), lambda j: (0, j)),
        out_shape=jax.ShapeDtypeStruct((rows, n_out), F32),
        compiler_params=_cparams(("parallel",)),
        name="adaln_mod",
    )(c8, w_ada, b_ada[None, :])
    return mod[:B]


def _inproj_kernel(x_ref, g_ref, sh_ref, sc_ref, w_ref, mc_ref, ms1_ref, ms2_ref, rc_ref, rs_ref,
                   o_ref, hn_ref, *, tn, moba_tiles, ret_lo, ret_k_lo, ret_hi):
    j = pl.program_id(2)

    @pl.when(j == 0)
    def _():
        h = _rms_mod(x_ref[0], g_ref[...], sh_ref[0], sc_ref[0])
        hn_ref[...] = h.astype(BF16)

    acc = jnp.dot(hn_ref[...], w_ref[...], preferred_element_type=F32)

    @pl.when(j < moba_tiles)
    def _():
        c, s1, s2 = mc_ref[0], ms1_ref[0], ms2_ref[0]
        half = ROPE_DIMS // 2
        for g in range(tn // LANES):
            a = acc[:, g * LANES:(g + 1) * LANES]
            r = a * c + pltpu.roll(a, LANES - half, 1) * s1 + pltpu.roll(a, half, 1) * s2
            o_ref[0, :, g * LANES:(g + 1) * LANES] = r.astype(o_ref.dtype)

    @pl.when((j >= ret_lo) & (j < ret_hi))
    def _():
        c, s = rc_ref[0], rs_ref[0]
        fac = jnp.where(j >= ret_k_lo, RET_HEAD_DIM ** -0.5, 1.0).astype(F32)
        hw = RET_HEAD_DIM // 2
        for g in range(tn // RET_HEAD_DIM):
            x1 = acc[:, g * RET_HEAD_DIM:g * RET_HEAD_DIM + hw]
            x2 = acc[:, g * RET_HEAD_DIM + hw:(g + 1) * RET_HEAD_DIM]
            o_ref[0, :, g * RET_HEAD_DIM:g * RET_HEAD_DIM + hw] = ((x1 * c - x2 * s) * fac).astype(o_ref.dtype)
            o_ref[0, :, g * RET_HEAD_DIM + hw:(g + 1) * RET_HEAD_DIM] = ((x2 * c + x1 * s) * fac).astype(o_ref.dtype)

    @pl.when(((j >= moba_tiles) & (j < ret_lo)) | (j >= ret_hi))
    def _():
        o_ref[0] = acc.astype(o_ref.dtype)


def _in_proj(x, g, shift, scale, w_bf, tabs, moba_w, ret_w):
    B, T, D = x.shape
    NC = w_bf.shape[1]
    tm, tn = 512, 512
    mc, ms1, ms2, rc, rs = tabs
    kern = functools.partial(
        _inproj_kernel, tn=tn,
        moba_tiles=2 * moba_w // tn,
        ret_lo=3 * moba_w // tn,
        ret_k_lo=(3 * moba_w + ret_w) // tn,
        ret_hi=(3 * moba_w + 2 * ret_w) // tn)
    tab_spec = pl.BlockSpec((1, tm, LANES), lambda b, i, j: (b, i, 0))
    vec_spec = pl.BlockSpec((1, 1, D), lambda b, i, j: (b, 0, 0))
    return pl.pallas_call(
        kern,
        grid=(B, T // tm, NC // tn),
        in_specs=[pl.BlockSpec((1, tm, D), lambda b, i, j: (b, i, 0)),
                  pl.BlockSpec((1, D), lambda b, i, j: (0, 0)),
                  vec_spec, vec_spec,
                  pl.BlockSpec((D, tn), lambda b, i, j: (0, j)),
                  tab_spec, tab_spec, tab_spec, tab_spec, tab_spec],
        out_specs=pl.BlockSpec((1, tm, tn), lambda b, i, j: (b, i, j)),
        out_shape=jax.ShapeDtypeStruct((B, T, NC), BF16),
        scratch_shapes=[pltpu.VMEM((tm, D), BF16)],
        compiler_params=_cparams(("parallel", "parallel", "arbitrary")),
        name="in_proj",
    )(x, g[None, :], shift[:, None, :], scale[:, None, :], w_bf, mc, ms1, ms2, rc, rs)


def _moba_kernel(q_ref, k_ref, v_ref, o_ref, km_ref, vt_ref, sel_ref, *, nb, hp):
    qb = pl.program_id(2)
    BS, hd = MOBA_BLOCK, MOBA_HEAD_DIM

    @pl.when(qb == 0)
    def _():
        for h in range(hp):
            hs = slice(h * hd, (h + 1) * hd)
            for n in range(nb):
                kb = k_ref[0, n * BS:(n + 1) * BS, hs].astype(F32)
                km_ref[h, n:n + 1, :] = jnp.sum(kb, axis=0, keepdims=True) * (1.0 / BS)
                vt_ref[h, n] = v_ref[0, n * BS:(n + 1) * BS, hs].astype(F32).T.astype(BF16)

    scale = hd ** -0.5
    own = pl.multiple_of(qb * BS, BS)
    blk = lax.broadcasted_iota(jnp.int32, (nb, BS), 0)
    blk_f = blk.astype(F32)
    krow = lax.broadcasted_iota(jnp.int32, (BS, BS), 0)
    qcol = lax.broadcasted_iota(jnp.int32, (BS, BS), 1)

    qs_all, init = [], []
    for h in range(hp):
        hs = slice(h * hd, (h + 1) * hd)
        qT = q_ref[0, :, hs].astype(F32).T
        gate = jnp.dot(km_ref[h], qT, preferred_element_type=F32,
                       precision=lax.Precision.HIGHEST)
        g = jnp.where(blk < qb, gate, NEG)
        sel = jnp.zeros((nb, BS), F32)
        for _ in range(MOBA_TOPK):
            m = jnp.max(g, axis=0, keepdims=True)
            idx = jnp.min(jnp.where(g == m, blk_f, float(nb)), axis=0, keepdims=True)
            pick = blk_f == idx
            sel = jnp.where(pick & (m > 0.5 * NEG), 1.0, sel)
            g = jnp.where(pick, -jnp.inf, g)
        sel_ref[h] = sel

        qs = (qT * scale).astype(BF16)
        s = jnp.dot(k_ref[0, pl.ds(own, BS), hs], qs, preferred_element_type=F32)
        s = jnp.where(krow <= qcol, s, NEG)
        m0 = jnp.max(s, axis=0, keepdims=True)
        p = jnp.exp(s - m0)
        l0 = jnp.sum(p, axis=0, keepdims=True)
        acc0 = jnp.dot(vt_ref[h, qb], p.astype(BF16), preferred_element_type=F32)
        qs_all.append(qs)
        init.append((m0, l0, acc0))

    def body(n, carry):
        off = pl.multiple_of(n * BS, BS)
        out = []
        for h in range(hp):
            m, l, acc = carry[h]
            selrow = sel_ref[h, pl.ds(n, 1), :]
            s = jnp.dot(k_ref[0, pl.ds(off, BS), h * hd:(h + 1) * hd], qs_all[h],
                        preferred_element_type=F32)
            s = jnp.where(selrow > 0.0, s, NEG)
            m_new = jnp.maximum(m, jnp.max(s, axis=0, keepdims=True))
            alpha = jnp.exp(m - m_new)
            p = jnp.exp(s - m_new)
            l = alpha * l + jnp.sum(p, axis=0, keepdims=True)
            acc = alpha * acc + jnp.dot(vt_ref[h, n], p.astype(BF16), preferred_element_type=F32)
            out.append((m_new, l, acc))
        return tuple(out)

    final = lax.fori_loop(0, qb, body, tuple(init))
    for h in range(hp):
        _, l, acc = final[h]
        o_ref[0, :, h * hd:(h + 1) * hd] = (acc / l).T.astype(o_ref.dtype)


def _moba(proj, B, T, moba_w):
    H, hd, BS = MOBA_HEADS, MOBA_HEAD_DIM, MOBA_BLOCK
    nb = T // BS
    hp = 2
    gw = hp * hd
    gpw = moba_w // gw
    return pl.pallas_call(
        functools.partial(_moba_kernel, nb=nb, hp=hp),
        grid=(B, H // hp, nb),
        in_specs=[pl.BlockSpec((1, BS, gw), lambda b, h, i: (b, i, h)),
                  pl.BlockSpec((1, T, gw), lambda b, h, i: (b, 0, gpw + h)),
                  pl.BlockSpec((1, T, gw), lambda b, h, i: (b, 0, 2 * gpw + h))],
        out_specs=pl.BlockSpec((1, BS, gw), lambda b, h, i: (b, i, h)),
        out_shape=jax.ShapeDtypeStruct((B, T, moba_w), BF16),
        scratch_shapes=[pltpu.VMEM((hp, nb, hd), F32),
                        pltpu.VMEM((hp, nb, hd, BS), BF16),
                        pltpu.VMEM((hp, nb, BS), F32)],
        compiler_params=_cparams(("parallel", "parallel", "arbitrary")),
        name="moba_attn",
    )(proj, proj, proj)


def _ret_kernel(q_ref, k_ref, v_ref, g_ref, dm_ref, xi_ref, zeta_ref, cd_ref, o_ref, s_ref):
    c = pl.program_id(1)
    d = RET_HEAD_DIM

    @pl.when(c == 0)
    def _():
        s_ref[...] = jnp.zeros_like(s_ref)

    for h in range(RET_HEADS):
        sl = slice(h * d, (h + 1) * d)
        q = q_ref[0, :, sl]
        k = k_ref[0, :, sl]
        v = v_ref[0, :, sl]
        inner = lax.dot_general(q, k, (((1,), (1,)), ((), ())), preferred_element_type=F32) * dm_ref[h]
        S = s_ref[h]
        o = (jnp.dot(inner.astype(BF16), v, preferred_element_type=F32)
             + jnp.dot(q, S.astype(BF16), preferred_element_type=F32) * xi_ref[h])
        kz = (k.astype(F32) * zeta_ref[h]).astype(BF16)
        s_ref[h] = S * cd_ref[h] + lax.dot_general(kz, v, (((0,), (0,)), ((), ())),
                                                   preferred_element_type=F32)
        mu = jnp.mean(o, axis=-1, keepdims=True)
        dlt = o - mu
        var = jnp.mean(dlt * dlt, axis=-1, keepdims=True)
        on = dlt * lax.rsqrt(var + NORM_EPS)
        gg = g_ref[0, :, sl].astype(F32)
        o_ref[0, :, sl] = (on * (gg * jax.nn.sigmoid(gg))).astype(o_ref.dtype)


def _retention(proj, B, T, moba_w, ret_w):
    C, H = RET_CHUNK, RET_HEADS
    gamma = 1.0 - jnp.exp2(-5.0 - jnp.arange(H, dtype=F32))
    log_g = jnp.log(gamma)
    pos = jnp.arange(C, dtype=F32)
    diff = pos[:, None] - pos[None, :]
    dmask = jnp.where(diff >= 0, jnp.exp(jnp.maximum(diff, 0.0) * log_g[:, None, None]), 0.0)
    xi = jnp.exp((pos + 1.0) * log_g[:, None])[:, :, None]
    zeta = jnp.exp((C - 1.0 - pos) * log_g[:, None])[:, :, None]
    cd = jnp.exp(C * log_g)[:, None, None]
    base = 3 * moba_w // ret_w
    col = lambda off: pl.BlockSpec((1, C, ret_w), lambda b, c: (b, c, base + off))
    full = lambda shp: pl.BlockSpec(shp, lambda b, c: (0,) * len(shp))
    return pl.pallas_call(
        _ret_kernel,
        grid=(B, T // C),
        in_specs=[col(0), col(1), col(2), col(3),
                  full((H, C, C)), full((H, C, 1)), full((H, C, 1)), full((H, 1, 1))],
        out_specs=pl.BlockSpec((1, C, ret_w), lambda b, c: (b, c, 0)),
        out_shape=jax.ShapeDtypeStruct((B, T, ret_w), BF16),
        scratch_shapes=[pltpu.VMEM((H, RET_HEAD_DIM, RET_HEAD_DIM), F32)],
        compiler_params=_cparams(("parallel", "arbitrary")),
        name="retention",
    )(proj, proj, proj, proj, dmask, xi, zeta, cd)


def _outproj_kernel(oa_ref, or_ref, w_ref, x_ref, ga_ref, g_ref, sh_ref, sc_ref, wr_ref,
                    x1_ref, hp_ref, lg_ref, *, moba_w):
    mix = (jnp.dot(oa_ref[0], w_ref[:moba_w, :], preferred_element_type=F32)
           + jnp.dot(or_ref[0], w_ref[moba_w:, :], preferred_element_type=F32))
    x1 = x_ref[0] + ga_ref[0] * mix
    x1_ref[0] = x1
    h = _rms_mod(x1, g_ref[...], sh_ref[0], sc_ref[0])
    half = h.shape[-1] // 2
    hp_ref[0] = _pack_pair(h[:, :half], h[:, half:])
    lg_ref[0] = jnp.dot(h, wr_ref[...], preferred_element_type=F32, precision=lax.Precision.HIGHEST)


def _out_proj(o_a, o_r, w_bf, x, gate_a, g, shift, scale, w_router):
    B, T, D = x.shape
    moba_w, ret_w = o_a.shape[-1], o_r.shape[-1]
    E = w_router.shape[1]
    tm = 256
    vec = pl.BlockSpec((1, 1, D), lambda b, i: (b, 0, 0))
    row = lambda w: pl.BlockSpec((1, tm, w), lambda b, i: (b, i, 0))
    return pl.pallas_call(
        functools.partial(_outproj_kernel, moba_w=moba_w),
        grid=(B, T // tm),
        in_specs=[row(moba_w), row(ret_w),
                  pl.BlockSpec((moba_w + ret_w, D), lambda b, i: (0, 0)),
                  row(D), vec,
                  pl.BlockSpec((1, D), lambda b, i: (0, 0)),
                  vec, vec,
                  pl.BlockSpec((D, E), lambda b, i: (0, 0))],
        out_specs=[row(D), row(D // 2), row(E)],
        out_shape=[jax.ShapeDtypeStruct((B, T, D), F32),
                   jax.ShapeDtypeStruct((B, T, D // 2), U32),
                   jax.ShapeDtypeStruct((B, T, E), F32)],
        compiler_params=_cparams(("parallel", "parallel")),
        name="out_proj",
    )(o_a, o_r, w_bf, x, gate_a[:, None, :], g[None, :], shift[:, None, :], scale[:, None, :], w_router)


def _first_argmax(v, lane_f, width):
    m = jnp.max(v, axis=1, keepdims=True)
    idx = jnp.min(jnp.where(v == m, lane_f, float(width)), axis=1, keepdims=True)
    return m, idx


def _route_kernel(lg_ref, b_ref, selr_ref, wf_ref, rank_ref, cnt_ref, carry_ref):
    i = pl.program_id(0)

    @pl.when(i == 0)
    def _():
        carry_ref[...] = jnp.zeros_like(carry_ref)

    E = N_EXPERTS
    gsz = E // N_GROUPS
    s = jax.nn.sigmoid(lg_ref[...])
    biased = s + b_ref[...]
    tm = s.shape[0]
    lane = lax.broadcasted_iota(jnp.int32, (tm, E), 1)
    lane_f = lane.astype(F32)
    grp = lax.shift_right_logical(lane, gsz.bit_length() - 1)

    gscore = jnp.full((tm, E), -jnp.inf, F32)
    for gi in range(N_GROUPS):
        v = jnp.where(grp == gi, biased, -jnp.inf)
        m1, i1 = _first_argmax(v, lane_f, E)
        m2 = jnp.max(jnp.where(lane_f == i1, -jnp.inf, v), axis=1, keepdims=True)
        gscore = jnp.where(lane == gi, m1 + m2, gscore)

    emask = jnp.zeros((tm, E), jnp.bool_)
    grp_f = grp.astype(F32)
    for _ in range(TOPK_GROUPS):
        _, gi = _first_argmax(gscore, lane_f, E)
        emask = emask | (grp_f == gi)
        gscore = jnp.where(lane_f == gi, -jnp.inf, gscore)

    cand = jnp.where(emask, biased, NEG)
    selr = jnp.zeros((tm, E), F32)
    for r in range(TOP_K):
        _, ei = _first_argmax(cand, lane_f, E)
        pick = lane_f == ei
        selr = jnp.where(pick, float(r + 1), selr)
        cand = jnp.where(pick, -jnp.inf, cand)

    chosen = selr > 0.0
    w = jnp.where(chosen, s, 0.0)
    wsum = jnp.sum(w, axis=1, keepdims=True)
    selr_ref[...] = selr
    wf_ref[...] = w / wsum * ROUTE_SCALE

    onehot = chosen.astype(BF16)
    r_i = lax.broadcasted_iota(jnp.int32, (tm, tm), 0)
    c_i = lax.broadcasted_iota(jnp.int32, (tm, tm), 1)
    tri = (c_i < r_i).astype(BF16)
    carry = carry_ref[...]
    rank_ref[...] = jnp.dot(tri, onehot, preferred_element_type=F32) + carry
    carry = carry + jnp.sum(chosen.astype(F32), axis=0, keepdims=True)
    carry_ref[...] = carry
    cnt_ref[...] = carry


def _route(logits, bias):
    N, E = logits.shape
    tm = 512
    blk = pl.BlockSpec((tm, E), lambda i: (i, 0))
    one = pl.BlockSpec((1, E), lambda i: (0, 0))
    full = jax.ShapeDtypeStruct((N, E), F32)
    return pl.pallas_call(
        _route_kernel,
        grid=(N // tm,),
        in_specs=[blk, one],
        out_specs=[blk, blk, blk, one],
        out_shape=[full, full, full, jax.ShapeDtypeStruct((1, E), F32)],
        scratch_shapes=[pltpu.VMEM((1, E), F32)],
        compiler_params=_cparams(("arbitrary",)),
        name="route_topk",
    )(logits, bias[None, :])


def _dest_kernel(selr_ref, wf_ref, rank_ref, ps_ref, dest_ref, wk_ref):
    selr = selr_ref[...]
    destfull = rank_ref[...] + ps_ref[...]
    wf = wf_ref[...]
    for r in range(TOP_K):
        hit = selr == float(r + 1)
        dest_ref[:, r:r + 1] = jnp.sum(jnp.where(hit, destfull, 0.0), axis=1, keepdims=True).astype(jnp.int32)
        wk_ref[:, r:r + 1] = jnp.sum(jnp.where(hit, wf, 0.0), axis=1, keepdims=True)


def _dest(selr, wf, rank, pstart_f):
    N, E = selr.shape
    tm = 512
    blk = pl.BlockSpec((tm, E), lambda i: (i, 0))
    outb = pl.BlockSpec((tm, TOP_K), lambda i: (i, 0))
    return pl.pallas_call(
        _dest_kernel,
        grid=(N // tm,),
        in_specs=[blk, blk, blk, pl.BlockSpec((1, E), lambda i: (0, 0))],
        out_specs=[outb, outb],
        out_shape=[jax.ShapeDtypeStruct((N, TOP_K), jnp.int32), jax.ShapeDtypeStruct((N, TOP_K), F32)],
        compiler_params=_cparams(("parallel",)),
        name="route_dest",
    )(selr, wf, rank, pstart_f)


def _row_copy(src, s_row, dst, d_row, n, sem):
    return pltpu.make_async_copy(src.at[pl.ds(s_row, n)], dst.at[pl.ds(d_row, n)], sem)


def _dispatch_kernel(padlo_ref, padn_ref, dest_ref, h_ref, z_ref, xs_ref, sem, zsem, *, tt, n_exp):
    i = pl.program_id(0)

    def issue(t, _):
        for k in range(TOP_K):
            _row_copy(h_ref, t, xs_ref, dest_ref[0, 0, t * TOP_K + k], 1, sem).start()
        return 0

    lax.fori_loop(0, tt, issue, 0)

    def each_pad(fn):
        def per_expert(e, _):
            lo = padlo_ref[e]

            def one(r, _):
                fn(lo + r)
                return 0

            lax.fori_loop(0, padn_ref[e], one, 0)
            return 0

        lax.fori_loop(0, n_exp, per_expert, 0)

    @pl.when(i == 0)
    def _():
        each_pad(lambda r: _row_copy(z_ref, 0, xs_ref, r, 1, zsem).start())

    for k in range(TOP_K):
        _row_copy(h_ref, 0, xs_ref, 0, tt, sem).wait()

    @pl.when(i == 0)
    def _():
        each_pad(lambda r: _row_copy(z_ref, 0, xs_ref, 0, 1, zsem).wait())


def _dispatch(h2p, dest, pad_lo, pad_n, R):
    N, W = h2p.shape
    tt = TOK_TILE
    dest3 = dest.reshape(N // tt, 1, tt * TOP_K)
    zeros = jnp.zeros((SUBLANES, W), h2p.dtype)
    grid_spec = pltpu.PrefetchScalarGridSpec(
        num_scalar_prefetch=2,
        grid=(N // tt,),
        in_specs=[pl.BlockSpec((1, 1, tt * TOP_K), lambda i, lo, n: (i, 0, 0), memory_space=pltpu.SMEM),
                  pl.BlockSpec((tt, W), lambda i, lo, n: (i, 0)),
                  pl.BlockSpec((SUBLANES, W), lambda i, lo, n: (0, 0))],
        out_specs=pl.BlockSpec(memory_space=pl.ANY),
        scratch_shapes=[pltpu.SemaphoreType.DMA(()), pltpu.SemaphoreType.DMA(())],
    )
    return pl.pallas_call(
        functools.partial(_dispatch_kernel, tt=tt, n_exp=N_EXPERTS),
        grid_spec=grid_spec,
        out_shape=jax.ShapeDtypeStruct((R, W), h2p.dtype),
        compiler_params=_cparams(("arbitrary",)),
        name="moe_dispatch",
    )(pad_lo, pad_n, dest3, h2p, zeros)


def _expert_kernel(te_ref, nu_ref, first_ref, nxt_ref, xs_ref, wg_hbm, wu_hbm, wd_hbm, y_ref,
                   stg, stu, std, wgb, wub, wdb, sem):
    i = pl.program_id(0)

    def fetch(e):
        return (pltpu.make_async_copy(wg_hbm.at[e], stg, sem.at[0]),
                pltpu.make_async_copy(wu_hbm.at[e], stu, sem.at[1]),
                pltpu.make_async_copy(wd_hbm.at[e], std, sem.at[2]))

    @pl.when(i == 0)
    def _():
        for cp in fetch(te_ref[0]):
            cp.start()

    active = i < nu_ref[0]

    @pl.when(active & (first_ref[i] == 1))
    def _():
        for cp in fetch(0):
            cp.wait()
        wgb[...] = stg[...].astype(BF16)
        wub[...] = stu[...].astype(BF16)
        wdb[...] = std[...].astype(BF16)

        @pl.when(nxt_ref[i] >= 0)
        def _():
            for cp in fetch(nxt_ref[i]):
                cp.start()

    @pl.when(active)
    def _():
        lo, hi = _unpack_pair(xs_ref[...])
        lo = lo.astype(BF16)
        hi = hi.astype(BF16)
        half = lo.shape[-1]
        a = (jnp.dot(lo, wgb[:half, :], preferred_element_type=F32)
             + jnp.dot(hi, wgb[half:, :], preferred_element_type=F32))
        u = (jnp.dot(lo, wub[:half, :], preferred_element_type=F32)
             + jnp.dot(hi, wub[half:, :], preferred_element_type=F32))
        hmid = (a * jax.nn.sigmoid(a) * u).astype(BF16)
        y = jnp.dot(hmid, wdb[...], preferred_element_type=F32)
        y_ref[...] = _pack_pair(y[:, :half], y[:, half:])


def _experts(xs, tile_expert, n_used, first, nxt, wg, wu, wd):
    R, W = xs.shape
    M = ROW_TILE
    _, D, F = wg.shape
    row = lambda i, te, nu, fi, nx: (jnp.minimum(i, nu[0] - 1), 0)
    grid_spec = pltpu.PrefetchScalarGridSpec(
        num_scalar_prefetch=4,
        grid=(R // M,),
        in_specs=[pl.BlockSpec((M, W), row),
                  pl.BlockSpec(memory_space=pl.ANY),
                  pl.BlockSpec(memory_space=pl.ANY),
                  pl.BlockSpec(memory_space=pl.ANY)],
        out_specs=pl.BlockSpec((M, W), row),
        scratch_shapes=[pltpu.VMEM((D, F), F32), pltpu.VMEM((D, F), F32), pltpu.VMEM((F, D), F32),
                        pltpu.VMEM((D, F), BF16), pltpu.VMEM((D, F), BF16), pltpu.VMEM((F, D), BF16),
                        pltpu.SemaphoreType.DMA((3,))],
    )
    return pl.pallas_call(
        _expert_kernel,
        grid_spec=grid_spec,
        out_shape=jax.ShapeDtypeStruct((R, W), U32),
        compiler_params=_cparams(("arbitrary",)),
        name="moe_experts",
    )(tile_expert, n_used, first, nxt, xs, wg, wu, wd)


def _final_kernel(dcur_ref, dnxt_ref, h_ref, wsg_ref, wsu_ref, wsd_ref, x1_ref, gf_ref, g_ref, wk_ref, y_ref,
                  o_ref, ybuf, sem, *, tt, n_tiles):
    i = pl.program_id(0)
    slot = lax.rem(i, 2)

    def gather(d_ref, s):
        def issue(t, _):
            for k in range(TOP_K):
                pltpu.make_async_copy(y_ref.at[pl.ds(d_ref[0, 0, t * TOP_K + k], 1)],
                                      ybuf.at[s, k, pl.ds(t, 1)], sem.at[s]).start()
            return 0

        lax.fori_loop(0, tt, issue, 0)

    @pl.when(i == 0)
    def _():
        gather(dcur_ref, 0)

    @pl.when(i + 1 < n_tiles)
    def _():
        gather(dnxt_ref, 1 - slot)

    lo, hi = _unpack_pair(h_ref[...])
    lo = lo.astype(BF16)
    hi = hi.astype(BF16)
    half = lo.shape[-1]
    a = (jnp.dot(lo, wsg_ref[:half, :], preferred_element_type=F32)
         + jnp.dot(hi, wsg_ref[half:, :], preferred_element_type=F32))
    u = (jnp.dot(lo, wsu_ref[:half, :], preferred_element_type=F32)
         + jnp.dot(hi, wsu_ref[half:, :], preferred_element_type=F32))
    hmid = (a * jax.nn.sigmoid(a) * u).astype(BF16)
    shared = jnp.dot(hmid, wsd_ref[...], preferred_element_type=F32)

    for k in range(TOP_K):
        pltpu.make_async_copy(y_ref.at[pl.ds(0, tt)], ybuf.at[slot, k], sem.at[slot]).wait()

    wk = wk_ref[...]
    r_lo = jnp.zeros((tt, half), F32)
    r_hi = jnp.zeros((tt, half), F32)
    for k in range(TOP_K):
        ylo, yhi = _unpack_pair(ybuf[slot, k])
        wcol = wk[:, k:k + 1]
        r_lo = r_lo + wcol * ylo
        r_hi = r_hi + wcol * yhi
    total = shared + jnp.concatenate([r_lo, r_hi], axis=1)
    x2 = x1_ref[...] + gf_ref[0] * total
    ms = jnp.mean(x2 * x2, axis=-1, keepdims=True)
    o_ref[...] = x2 * lax.rsqrt(ms + NORM_EPS) * g_ref[...]


def _final(h2p, wsg, wsu, wsd, x1, gate_f, norm_out, dest, wk, y, T):
    N, W = h2p.shape
    D = x1.shape[-1]
    F = wsg.shape[1]
    tt = TOK_TILE
    per_b = T // tt
    n_tiles = N // tt
    dest3 = dest.reshape(n_tiles, 1, tt * TOP_K)
    rowb = lambda w: pl.BlockSpec((tt, w), lambda i: (i, 0))
    const = lambda shp: pl.BlockSpec(shp, lambda i: (0,) * len(shp))
    dspec = lambda f: pl.BlockSpec((1, 1, tt * TOP_K), f, memory_space=pltpu.SMEM)
    return pl.pallas_call(
        functools.partial(_final_kernel, tt=tt, n_tiles=n_tiles),
        grid=(n_tiles,),
        in_specs=[dspec(lambda i: (i, 0, 0)),
                  dspec(lambda i: (jnp.minimum(i + 1, n_tiles - 1), 0, 0)),
                  rowb(W), const((D, F)), const((D, F)), const((F, D)), rowb(D),
                  pl.BlockSpec((1, 1, D), lambda i: (i // per_b, 0, 0)),
                  const((1, D)),
                  pl.BlockSpec((tt, TOP_K), lambda i: (i, 0)),
                  pl.BlockSpec(memory_space=pl.ANY)],
        out_specs=rowb(D),
        out_shape=jax.ShapeDtypeStruct((N, D), F32),
        scratch_shapes=[pltpu.VMEM((2, TOP_K, tt, W), U32), pltpu.SemaphoreType.DMA((2,))],
        compiler_params=_cparams(("arbitrary",)),
        name="moe_combine_final",
    )(dest3, dest3, h2p, wsg, wsu, wsd, x1, gate_f[:, None, :], norm_out[None, :], wk, y)


def kernel(x, c, positions, w_ada, b_ada, norm_mix, norm_ffn, norm_out, w_in, w_out, w_router, router_bias,
           w_gate, w_up, w_down, w_sh_gate, w_sh_up, w_sh_down):
    B, T, D = x.shape
    depth = w_ada.shape[0]
    assert depth == 1, "the final rmsnorm is fused into the layer's last kernel"
    moba_w = MOBA_HEADS * MOBA_HEAD_DIM
    ret_w = RET_HEADS * RET_HEAD_DIM
    N = B * T
    E, M = N_EXPERTS, ROW_TILE
    R = N * TOP_K + E * M
    n_tiles = R // M
    tabs = _rope_tables(positions)

    for l in range(depth):
        mod = _ada(c, w_ada[l], b_ada[l])
        shift_a, scale_a, gate_a, shift_f, scale_f, gate_f = jnp.split(mod, 6, axis=-1)

        proj = _in_proj(x, norm_mix[l], shift_a, scale_a, w_in[l].astype(BF16), tabs, moba_w, ret_w)
        o_a = _moba(proj, B, T, moba_w)
        o_r = _retention(proj, B, T, moba_w, ret_w)
        x1, h2p, logits = _out_proj(o_a, o_r, w_out[l].astype(BF16), x, gate_a, norm_ffn[l],
                                    shift_f, scale_f, w_router[l])

        h2p = h2p.reshape(N, D // 2)
        selr, wf, rank, counts = _route(logits.reshape(N, E), router_bias[l])
        cnt = counts[0].astype(jnp.int32)
        pcnt = (cnt + M - 1) // M * M
        pend = jnp.cumsum(pcnt)
        pstart = pend - pcnt
        tidx = jnp.arange(n_tiles, dtype=jnp.int32)
        tile_expert = jnp.minimum(jnp.sum(pend[None, :] <= (tidx * M)[:, None], axis=1), E - 1).astype(jnp.int32)
        n_used = (pend[-1:] // M).astype(jnp.int32)
        first = (((tidx == 0) | (tile_expert != jnp.roll(tile_expert, 1))) & (tidx < n_used[0])).astype(jnp.int32)
        group_end = pend[tile_expert] // M
        nxt = jnp.where(group_end < n_used[0], tile_expert[jnp.minimum(group_end, n_tiles - 1)], -1).astype(jnp.int32)
        dest, wk = _dest(selr, wf, rank, pstart.astype(F32)[None, :])

        xs = _dispatch(h2p, dest, (pstart + cnt).astype(jnp.int32), (pcnt - cnt).astype(jnp.int32), R)
        y = _experts(xs, tile_expert, n_used, first, nxt, w_gate[l], w_up[l], w_down[l])
        out = _final(h2p, w_sh_gate[l].astype(BF16), w_sh_up[l].astype(BF16), w_sh_down[l].astype(BF16),
                     x1.reshape(N, D), gate_f, norm_out, dest, wk, y, T)
        x = out.reshape(B, T, D)
    return x
```

```python
import functools

import jax
import jax.numpy as jnp
from jax import lax
from jax.experimental import pallas as pl
from jax.experimental.pallas import tpu as pltpu

MOBA_HEADS = 8
MOBA_HEAD_DIM = 128
MOBA_BLOCK = 256
MOBA_TOPK = 3
ROPE_THETA = 500000.0
ROPE_DIMS = 32
RET_HEADS = 4
RET_HEAD_DIM = 256
RET_ROPE_BASE = 10000.0
N_EXPERTS = 64
TOP_K = 8
N_GROUPS = 8
TOPK_GROUPS = 4
ROUTE_SCALE = 2.5
NORM_EPS = 1e-6
NEG = -1e30

LANES = 128
SUBLANES = 8
VMEM_LIMIT = 56 * 1024 * 1024

RET_CHUNK = 256
ROW_TILE = 256
TOK_TILE = 256

F32 = jnp.float32
BF16 = jnp.bfloat16
U32 = jnp.uint32


def _cparams(sem):
    return pltpu.CompilerParams(dimension_semantics=sem, vmem_limit_bytes=VMEM_LIMIT)


def _rms_mod(xf, g, shift, scale):
    ms = jnp.mean(xf * xf, axis=-1, keepdims=True)
    y = xf * lax.rsqrt(ms + NORM_EPS) * g
    return y * (1.0 + scale) + shift


def _pack_pair(lo, hi):
    lo_b = lax.bitcast_convert_type(lo.astype(BF16).astype(F32), U32)
    hi_b = lax.bitcast_convert_type(hi.astype(BF16).astype(F32), U32)
    return (lo_b >> 16) | hi_b


def _unpack_pair(p):
    lo = lax.bitcast_convert_type(p << 16, F32)
    hi = lax.bitcast_convert_type(p & jnp.uint32(0xFFFF0000), F32)
    return lo, hi


def _tables_kernel(pos_ref, invm_ref, invr_ref, mc_ref, ms1_ref, ms2_ref, rc_ref, rs_ref):
    pos = pos_ref[0].astype(F32)
    angm = pos * invm_ref[...]
    lane = lax.broadcasted_iota(jnp.int32, angm.shape, 1)
    half = ROPE_DIMS // 2
    c = jnp.cos(angm)
    s = jnp.sin(angm)
    mc_ref[0] = c
    ms1_ref[0] = jnp.where(lane < half, -s, 0.0)
    ms2_ref[0] = jnp.where((lane >= half) & (lane < ROPE_DIMS), s, 0.0)
    angr = pos * invr_ref[...]
    rc_ref[0] = jnp.cos(angr)
    rs_ref[0] = jnp.sin(angr)


def _rope_tables(positions):
    B, T = positions.shape
    tm = 512
    half = ROPE_DIMS // 2
    moba_inv = ROPE_THETA ** (-(jnp.arange(half, dtype=F32) * 2.0 / ROPE_DIMS))
    invm = jnp.concatenate([moba_inv, moba_inv, jnp.zeros((LANES - ROPE_DIMS,), F32)])[None, :]
    invr = (RET_ROPE_BASE ** (-jnp.linspace(0.0, 1.0, RET_HEAD_DIM // 2, dtype=F32)))[None, :]
    pos3 = positions.reshape(B, T, 1)
    tab = jax.ShapeDtypeStruct((B, T, LANES), F32)
    spec = pl.BlockSpec((1, tm, LANES), lambda b, i: (b, i, 0))
    return pl.pallas_call(
        _tables_kernel,
        grid=(B, T // tm),
        in_specs=[pl.BlockSpec((1, tm, 1), lambda b, i: (b, i, 0)),
                  pl.BlockSpec((1, LANES), lambda b, i: (0, 0)),
                  pl.BlockSpec((1, LANES), lambda b, i: (0, 0))],
        out_specs=[spec] * 5,
        out_shape=[tab] * 5,
        compiler_params=_cparams(("parallel", "parallel")),
        name="rope_tables",
    )(pos3, invm, invr)


def _ada_kernel(c_ref, w_ref, b_ref, o_ref):
    c = c_ref[...]
    sc = c * jax.nn.sigmoid(c)
    o_ref[...] = jnp.dot(sc, w_ref[...], preferred_element_type=F32,
                         precision=lax.Precision.HIGHEST) + b_ref[...]


def _ada(c, w_ada, b_ada):
    B, D = c.shape
    n_out = w_ada.shape[1]
    rows = SUBLANES
    tn = 1024
    c8 = jnp.zeros((rows, D), F32).at[:B].set(c)
    mod = pl.pallas_call(
        _ada_kernel,
        grid=(n_out // tn,),
        in_specs=[pl.BlockSpec((rows, D), lambda j: (0, 0)),
                  pl.BlockSpec((D, tn), lambda j: (0, j)),
                  pl.BlockSpec((1, tn), lambda j: (0, j))],
        out_specs=pl.BlockSpec((rows, tn), lambda j: (0, j)),
        out_shape=jax.ShapeDtypeStruct((rows, n_out), F32),
        compiler_params=_cparams(("parallel",)),
        name="adaln_mod",
    )(c8, w_ada, b_ada[None, :])
    return mod[:B]


def _inproj_kernel(x_ref, g_ref, sh_ref, sc_ref, w_ref, mc_ref, ms1_ref, ms2_ref, rc_ref, rs_ref,
                   o_ref, hn_ref, cf_ref, *, tn, moba_tiles, ret_lo, ret_k_lo, ret_hi):
    j = pl.program_id(2)

    @pl.when(j == 0)
    def _():
        h = _rms_mod(x_ref[0], g_ref[...], sh_ref[0], sc_ref[0])
        hn_ref[...] = h.astype(BF16)

    is_moba = j < moba_tiles
    is_ret = (j >= ret_lo) & (j < ret_hi)
    fac = jnp.where(j >= ret_k_lo, RET_HEAD_DIM ** -0.5, 1.0).astype(F32)
    cf_ref[0] = jnp.where(is_moba, mc_ref[0], jnp.where(is_ret, rc_ref[0] * fac, 1.0))
    cf_ref[1] = jnp.where(is_ret, rs_ref[0] * fac, 0.0)
    cf_ref[2] = jnp.where(is_moba, ms1_ref[0], 0.0)
    cf_ref[3] = jnp.where(is_moba, ms2_ref[0], 0.0)
    half = ROPE_DIMS // 2
    rot = lambda a: pltpu.roll(a, LANES - half, 1) * cf_ref[2] + pltpu.roll(a, half, 1) * cf_ref[3]
    for g in range(tn // (2 * LANES)):
        lo = slice(2 * g * LANES, (2 * g + 1) * LANES)
        hi = slice((2 * g + 1) * LANES, (2 * g + 2) * LANES)
        acc = jnp.dot(hn_ref[...], w_ref[:, 2 * g * LANES:(2 * g + 2) * LANES],
                      preferred_element_type=F32)
        a, b = acc[:, :LANES], acc[:, LANES:]
        o_ref[0, :, lo] = (a * cf_ref[0] - b * cf_ref[1] + rot(a)).astype(o_ref.dtype)
        o_ref[0, :, hi] = (b * cf_ref[0] + a * cf_ref[1] + rot(b)).astype(o_ref.dtype)


def _in_proj(x, g, shift, scale, w_bf, tabs, moba_w, ret_w):
    B, T, D = x.shape
    NC = w_bf.shape[1]
    tm, tn = 512, 512
    mc, ms1, ms2, rc, rs = tabs
    kern = functools.partial(
        _inproj_kernel, tn=tn,
        moba_tiles=2 * moba_w // tn,
        ret_lo=3 * moba_w // tn,
        ret_k_lo=(3 * moba_w + ret_w) // tn,
        ret_hi=(3 * moba_w + 2 * ret_w) // tn)
    tab_spec = pl.BlockSpec((1, tm, LANES), lambda b, i, j: (b, i, 0))
    vec_spec = pl.BlockSpec((1, 1, D), lambda b, i, j: (b, 0, 0))
    return pl.pallas_call(
        kern,
        grid=(B, T // tm, NC // tn),
        in_specs=[pl.BlockSpec((1, tm, D), lambda b, i, j: (b, i, 0)),
                  pl.BlockSpec((1, D), lambda b, i, j: (0, 0)),
                  vec_spec, vec_spec,
                  pl.BlockSpec((D, tn), lambda b, i, j: (0, j)),
                  tab_spec, tab_spec, tab_spec, tab_spec, tab_spec],
        out_specs=pl.BlockSpec((1, tm, tn), lambda b, i, j: (b, i, j)),
        out_shape=jax.ShapeDtypeStruct((B, T, NC), BF16),
        scratch_shapes=[pltpu.VMEM((tm, D), BF16), pltpu.VMEM((4, tm, LANES), F32)],
        compiler_params=_cparams(("parallel", "parallel", "arbitrary")),
        name="in_proj",
    )(x, g[None, :], shift[:, None, :], scale[:, None, :], w_bf, mc, ms1, ms2, rc, rs)


def _moba_kernel(q_ref, k_ref, v_ref, o_ref, km_ref, vt_ref, sel_ref, qs_ref, s_ref, m_ref, l_ref, acc_ref,
                 *, nb, hp):
    qb = pl.program_id(2)
    BS, hd = MOBA_BLOCK, MOBA_HEAD_DIM

    @pl.when(qb == 0)
    def _():
        for h in range(hp):
            hs = slice(h * hd, (h + 1) * hd)
            for n in range(nb):
                kb = k_ref[0, n * BS:(n + 1) * BS, hs].astype(F32)
                km_ref[h, n:n + 1, :] = jnp.sum(kb, axis=0, keepdims=True) * (1.0 / BS)
                vt_ref[h, n] = v_ref[0, n * BS:(n + 1) * BS, hs].astype(F32).T.astype(BF16)

    scale = hd ** -0.5
    own = pl.multiple_of(qb * BS, BS)
    blk = lax.broadcasted_iota(jnp.int32, (nb, BS), 0)
    blk_f = blk.astype(F32)

    for h in range(hp):
        hs = slice(h * hd, (h + 1) * hd)
        qT = q_ref[0, :, hs].astype(F32).T
        gate = jnp.dot(km_ref[h], qT, preferred_element_type=F32,
                       precision=lax.Precision.HIGHEST)
        g = jnp.where(blk < qb, gate, NEG)
        sel = jnp.zeros((nb, BS), F32)
        for _ in range(MOBA_TOPK):
            m = jnp.max(g, axis=0, keepdims=True)
            idx = jnp.min(jnp.where(g == m, blk_f, float(nb)), axis=0, keepdims=True)
            pick = blk_f == idx
            sel = jnp.where(pick & (m > 0.5 * NEG), 1.0, sel)
            g = jnp.where(pick, -jnp.inf, g)
        sel_ref[h] = sel
        qs_ref[h] = (qT * scale).astype(BF16)
        m_ref[h] = jnp.full((1, BS), NEG, F32)

    n_pairs = lax.shift_right_logical(qb + 1, 1)

    def sweep_scores(j, _):
        for h in range(hp):
            mh = m_ref[h]
            for u in range(2):
                n = 2 * j + u
                off = pl.multiple_of(n * BS, BS)
                s = jnp.dot(k_ref[0, pl.ds(off, BS), h * hd:(h + 1) * hd], qs_ref[h],
                            preferred_element_type=F32)
                s = jnp.where(sel_ref[h, pl.ds(n, 1), :] > 0.0, s, NEG)
                s_ref[h, n] = s
                mh = jnp.maximum(mh, jnp.max(s, axis=0, keepdims=True))
            m_ref[h] = mh
        return 0

    lax.fori_loop(0, n_pairs, sweep_scores, 0)

    krow = lax.broadcasted_iota(jnp.int32, (BS, BS), 0)
    qcol = lax.broadcasted_iota(jnp.int32, (BS, BS), 1)
    for h in range(hp):
        hs = slice(h * hd, (h + 1) * hd)
        s = jnp.dot(k_ref[0, pl.ds(own, BS), hs], qs_ref[h], preferred_element_type=F32)
        s = jnp.where(krow <= qcol, s, NEG)
        m = jnp.maximum(m_ref[h], jnp.max(s, axis=0, keepdims=True))
        m_ref[h] = m
        p = jnp.exp(s - m)
        l_ref[h] = jnp.sum(p, axis=0, keepdims=True)
        acc_ref[h] = jnp.dot(vt_ref[h, qb], p.astype(BF16), preferred_element_type=F32)

    def sweep_values(j, _):
        for h in range(hp):
            mh = m_ref[h]
            p0 = jnp.exp(s_ref[h, 2 * j] - mh)
            p1 = jnp.exp(s_ref[h, 2 * j + 1] - mh)
            l_ref[h] = l_ref[h] + (jnp.sum(p0, axis=0, keepdims=True) + jnp.sum(p1, axis=0, keepdims=True))
            acc_ref[h] = acc_ref[h] + (
                jnp.dot(vt_ref[h, 2 * j], p0.astype(BF16), preferred_element_type=F32)
                + jnp.dot(vt_ref[h, 2 * j + 1], p1.astype(BF16), preferred_element_type=F32))
        return 0

    lax.fori_loop(0, n_pairs, sweep_values, 0)

    for h in range(hp):
        o_ref[0, :, h * hd:(h + 1) * hd] = (acc_ref[h] / l_ref[h]).T.astype(o_ref.dtype)


def _moba_kernel_v2(q_ref, k_ref, v_ref, o_ref, km_ref, vt_ref, sel_ref, *, nb, hp):
    qb = pl.program_id(2)
    BS, hd = MOBA_BLOCK, MOBA_HEAD_DIM

    @pl.when(qb == 0)
    def _():
        for h in range(hp):
            hs = slice(h * hd, (h + 1) * hd)
            for n in range(nb):
                kb = k_ref[0, n * BS:(n + 1) * BS, hs].astype(F32)
                km_ref[h, n:n + 1, :] = jnp.sum(kb, axis=0, keepdims=True) * (1.0 / BS)
                vt_ref[h, n] = v_ref[0, n * BS:(n + 1) * BS, hs].astype(F32).T.astype(BF16)

    scale = hd ** -0.5
    own = pl.multiple_of(qb * BS, BS)
    blk = lax.broadcasted_iota(jnp.int32, (nb, BS), 0)
    blk_f = blk.astype(F32)
    krow = lax.broadcasted_iota(jnp.int32, (BS, BS), 0)
    qcol = lax.broadcasted_iota(jnp.int32, (BS, BS), 1)

    qs_all, init = [], []
    for h in range(hp):
        hs = slice(h * hd, (h + 1) * hd)
        qT = q_ref[0, :, hs].astype(F32).T
        gate = jnp.dot(km_ref[h], qT, preferred_element_type=F32,
                       precision=lax.Precision.HIGHEST)
        g = jnp.where(blk < qb, gate, NEG)
        sel = jnp.zeros((nb, BS), F32)
        for _ in range(MOBA_TOPK):
            m = jnp.max(g, axis=0, keepdims=True)
            idx = jnp.min(jnp.where(g == m, blk_f, float(nb)), axis=0, keepdims=True)
            pick = blk_f == idx
            sel = jnp.where(pick & (m > 0.5 * NEG), 1.0, sel)
            g = jnp.where(pick, -jnp.inf, g)
        sel_ref[h] = sel

        qs = (qT * scale).astype(BF16)
        s = jnp.dot(k_ref[0, pl.ds(own, BS), hs], qs, preferred_element_type=F32)
        s = jnp.where(krow <= qcol, s, NEG)
        m0 = jnp.max(s, axis=0, keepdims=True)
        p = jnp.exp(s - m0)
        l0 = jnp.sum(p, axis=0, keepdims=True)
        acc0 = jnp.dot(vt_ref[h, qb], p.astype(BF16), preferred_element_type=F32)
        qs_all.append(qs)
        init.append((m0, l0, acc0))

    def body(n, carry):
        off = pl.multiple_of(n * BS, BS)
        out = []
        for h in range(hp):
            m, l, acc = carry[h]
            selrow = sel_ref[h, pl.ds(n, 1), :]
            s = jnp.dot(k_ref[0, pl.ds(off, BS), h * hd:(h + 1) * hd], qs_all[h],
                        preferred_element_type=F32)
            s = jnp.where(selrow > 0.0, s, NEG)
            m_new = jnp.maximum(m, jnp.max(s, axis=0, keepdims=True))
            alpha = jnp.exp(m - m_new)
            p = jnp.exp(s - m_new)
            l = alpha * l + jnp.sum(p, axis=0, keepdims=True)
            acc = alpha * acc + jnp.dot(vt_ref[h, n], p.astype(BF16), preferred_element_type=F32)
            out.append((m_new, l, acc))
        return tuple(out)

    final = lax.fori_loop(0, qb, body, tuple(init))
    for h in range(hp):
        _, l, acc = final[h]
        o_ref[0, :, h * hd:(h + 1) * hd] = (acc / l).T.astype(o_ref.dtype)


def _moba(proj, B, T, moba_w):
    H, hd, BS = MOBA_HEADS, MOBA_HEAD_DIM, MOBA_BLOCK
    nb = T // BS
    hp = 2
    gw = hp * hd
    gpw = moba_w // gw
    return pl.pallas_call(
        functools.partial(_moba_kernel, nb=nb, hp=hp),
        grid=(B, H // hp, nb),
        in_specs=[pl.BlockSpec((1, BS, gw), lambda b, h, i: (b, i, h)),
                  pl.BlockSpec((1, T, gw), lambda b, h, i: (b, 0, gpw + h)),
                  pl.BlockSpec((1, T, gw), lambda b, h, i: (b, 0, 2 * gpw + h))],
        out_specs=pl.BlockSpec((1, BS, gw), lambda b, h, i: (b, i, h)),
        out_shape=jax.ShapeDtypeStruct((B, T, moba_w), BF16),
        scratch_shapes=[pltpu.VMEM((hp, nb, hd), F32),
                        pltpu.VMEM((hp, nb, hd, BS), BF16),
                        pltpu.VMEM((hp, nb, BS), F32),
                        pltpu.VMEM((hp, hd, BS), BF16),
                        pltpu.VMEM((hp, nb, BS, BS), F32),
                        pltpu.VMEM((hp, 1, BS), F32),
                        pltpu.VMEM((hp, 1, BS), F32),
                        pltpu.VMEM((hp, hd, BS), F32)],
        compiler_params=_cparams(("parallel", "parallel", "arbitrary")),
        name="moba_attn",
    )(proj, proj, proj)


def _ret_kernel(q_ref, k_ref, v_ref, g_ref, dm_ref, xi_ref, zeta_ref, cd_ref, o_ref, s_ref):
    c = pl.program_id(1)
    d = RET_HEAD_DIM

    @pl.when(c == 0)
    def _():
        s_ref[...] = jnp.zeros_like(s_ref)

    for h in range(RET_HEADS):
        sl = slice(h * d, (h + 1) * d)
        q = q_ref[0, :, sl]
        k = k_ref[0, :, sl]
        v = v_ref[0, :, sl]
        inner = lax.dot_general(q, k, (((1,), (1,)), ((), ())), preferred_element_type=F32) * dm_ref[h]
        S = s_ref[h]
        o = (jnp.dot(inner.astype(BF16), v, preferred_element_type=F32)
             + jnp.dot(q, S.astype(BF16), preferred_element_type=F32) * xi_ref[h])
        kz = (k.astype(F32) * zeta_ref[h]).astype(BF16)
        s_ref[h] = S * cd_ref[h] + lax.dot_general(kz, v, (((0,), (0,)), ((), ())),
                                                   preferred_element_type=F32)
        mu = jnp.mean(o, axis=-1, keepdims=True)
        dlt = o - mu
        var = jnp.mean(dlt * dlt, axis=-1, keepdims=True)
        on = dlt * lax.rsqrt(var + NORM_EPS)
        gg = g_ref[0, :, sl].astype(F32)
        o_ref[0, :, sl] = (on * (gg * jax.nn.sigmoid(gg))).astype(o_ref.dtype)


def _retention(proj, B, T, moba_w, ret_w):
    C, H = RET_CHUNK, RET_HEADS
    gamma = 1.0 - jnp.exp2(-5.0 - jnp.arange(H, dtype=F32))
    log_g = jnp.log(gamma)
    pos = jnp.arange(C, dtype=F32)
    diff = pos[:, None] - pos[None, :]
    dmask = jnp.where(diff >= 0, jnp.exp(jnp.maximum(diff, 0.0) * log_g[:, None, None]), 0.0)
    xi = jnp.exp((pos + 1.0) * log_g[:, None])[:, :, None]
    zeta = jnp.exp((C - 1.0 - pos) * log_g[:, None])[:, :, None]
    cd = jnp.exp(C * log_g)[:, None, None]
    base = 3 * moba_w // ret_w
    col = lambda off: pl.BlockSpec((1, C, ret_w), lambda b, c: (b, c, base + off))
    full = lambda shp: pl.BlockSpec(shp, lambda b, c: (0,) * len(shp))
    return pl.pallas_call(
        _ret_kernel,
        grid=(B, T // C),
        in_specs=[col(0), col(1), col(2), col(3),
                  full((H, C, C)), full((H, C, 1)), full((H, C, 1)), full((H, 1, 1))],
        out_specs=pl.BlockSpec((1, C, ret_w), lambda b, c: (b, c, 0)),
        out_shape=jax.ShapeDtypeStruct((B, T, ret_w), BF16),
        scratch_shapes=[pltpu.VMEM((H, RET_HEAD_DIM, RET_HEAD_DIM), F32)],
        compiler_params=_cparams(("parallel", "arbitrary")),
        name="retention",
    )(proj, proj, proj, proj, dmask, xi, zeta, cd)


def _outproj_kernel(oa_ref, or_ref, w_ref, x_ref, ga_ref, g_ref, sh_ref, sc_ref, wrh_ref, wrl_ref,
                    x1_ref, hp_ref, lg_ref, *, moba_w):
    mix = (jnp.dot(oa_ref[0], w_ref[:moba_w, :], preferred_element_type=F32)
           + jnp.dot(or_ref[0], w_ref[moba_w:, :], preferred_element_type=F32))
    x1 = x_ref[0] + ga_ref[0] * mix
    x1_ref[0] = x1
    h = _rms_mod(x1, g_ref[...], sh_ref[0], sc_ref[0])
    half = h.shape[-1] // 2
    hp_ref[0] = _pack_pair(h[:, :half], h[:, half:])
    h_hi = h.astype(BF16)
    h_lo = (h - h_hi.astype(F32)).astype(BF16)
    lg_ref[0] = (jnp.dot(h_hi, wrh_ref[...], preferred_element_type=F32)
                 + (jnp.dot(h_lo, wrh_ref[...], preferred_element_type=F32)
                    + jnp.dot(h_hi, wrl_ref[...], preferred_element_type=F32)))


def _out_proj(o_a, o_r, w_bf, x, gate_a, g, shift, scale, w_router):
    B, T, D = x.shape
    moba_w, ret_w = o_a.shape[-1], o_r.shape[-1]
    E = w_router.shape[1]
    tm = 256
    wr_hi = w_router.astype(BF16)
    wr_lo = (w_router - wr_hi.astype(F32)).astype(BF16)
    vec = pl.BlockSpec((1, 1, D), lambda b, i: (b, 0, 0))
    row = lambda w: pl.BlockSpec((1, tm, w), lambda b, i: (b, i, 0))
    return pl.pallas_call(
        functools.partial(_outproj_kernel, moba_w=moba_w),
        grid=(B, T // tm),
        in_specs=[row(moba_w), row(ret_w),
                  pl.BlockSpec((moba_w + ret_w, D), lambda b, i: (0, 0)),
                  row(D), vec,
                  pl.BlockSpec((1, D), lambda b, i: (0, 0)),
                  vec, vec,
                  pl.BlockSpec((D, E), lambda b, i: (0, 0)),
                  pl.BlockSpec((D, E), lambda b, i: (0, 0))],
        out_specs=[row(D), row(D // 2), row(E)],
        out_shape=[jax.ShapeDtypeStruct((B, T, D), F32),
                   jax.ShapeDtypeStruct((B, T, D // 2), U32),
                   jax.ShapeDtypeStruct((B, T, E), F32)],
        compiler_params=_cparams(("parallel", "parallel")),
        name="out_proj",
    )(o_a, o_r, w_bf, x, gate_a[:, None, :], g[None, :], shift[:, None, :], scale[:, None, :], wr_hi, wr_lo)


def _first_argmax(v, lane_f, width):
    m = jnp.max(v, axis=1, keepdims=True)
    idx = jnp.min(jnp.where(v == m, lane_f, float(width)), axis=1, keepdims=True)
    return m, idx


def _route_kernel(lg_ref, b_ref, selr_ref, wf_ref, rank_ref, cnt_ref, carry_ref):
    i = pl.program_id(0)

    @pl.when(i == 0)
    def _():
        carry_ref[...] = jnp.zeros_like(carry_ref)

    E = N_EXPERTS
    gsz = E // N_GROUPS
    s = jax.nn.sigmoid(lg_ref[...])
    biased = s + b_ref[...]
    tm = s.shape[0]
    lane = lax.broadcasted_iota(jnp.int32, (tm, E), 1)
    lane_f = lane.astype(F32)
    grp = lax.shift_right_logical(lane, gsz.bit_length() - 1)

    gscore = jnp.full((tm, E), -jnp.inf, F32)
    for gi in range(N_GROUPS):
        v = jnp.where(grp == gi, biased, -jnp.inf)
        m1, i1 = _first_argmax(v, lane_f, E)
        m2 = jnp.max(jnp.where(lane_f == i1, -jnp.inf, v), axis=1, keepdims=True)
        gscore = jnp.where(lane == gi, m1 + m2, gscore)

    emask = jnp.zeros((tm, E), jnp.bool_)
    grp_f = grp.astype(F32)
    for _ in range(TOPK_GROUPS):
        _, gi = _first_argmax(gscore, lane_f, E)
        emask = emask | (grp_f == gi)
        gscore = jnp.where(lane_f == gi, -jnp.inf, gscore)

    cand = jnp.where(emask, biased, NEG)
    selr = jnp.zeros((tm, E), F32)
    for r in range(TOP_K):
        _, ei = _first_argmax(cand, lane_f, E)
        pick = lane_f == ei
        selr = jnp.where(pick, float(r + 1), selr)
        cand = jnp.where(pick, -jnp.inf, cand)

    chosen = selr > 0.0
    w = jnp.where(chosen, s, 0.0)
    wsum = jnp.sum(w, axis=1, keepdims=True)
    selr_ref[...] = selr
    wf_ref[...] = w / wsum * ROUTE_SCALE

    onehot = chosen.astype(BF16)
    r_i = lax.broadcasted_iota(jnp.int32, (tm, tm), 0)
    c_i = lax.broadcasted_iota(jnp.int32, (tm, tm), 1)
    tri = (c_i < r_i).astype(BF16)
    carry = carry_ref[...]
    rank_ref[...] = jnp.dot(tri, onehot, preferred_element_type=F32) + carry
    carry = carry + jnp.sum(chosen.astype(F32), axis=0, keepdims=True)
    carry_ref[...] = carry
    cnt_ref[...] = carry


def _route(logits, bias):
    N, E = logits.shape
    tm = 512
    blk = pl.BlockSpec((tm, E), lambda i: (i, 0))
    one = pl.BlockSpec((1, E), lambda i: (0, 0))
    full = jax.ShapeDtypeStruct((N, E), F32)
    return pl.pallas_call(
        _route_kernel,
        grid=(N // tm,),
        in_specs=[blk, one],
        out_specs=[blk, blk, blk, one],
        out_shape=[full, full, full, jax.ShapeDtypeStruct((1, E), F32)],
        scratch_shapes=[pltpu.VMEM((1, E), F32)],
        compiler_params=_cparams(("arbitrary",)),
        name="route_topk",
    )(logits, bias[None, :])


def _dest_kernel(selr_ref, wf_ref, rank_ref, ps_ref, dest_ref, wk_ref):
    selr = selr_ref[...]
    destfull = rank_ref[...] + ps_ref[...]
    wf = wf_ref[...]
    for r in range(TOP_K):
        hit = selr == float(r + 1)
        dest_ref[:, r:r + 1] = jnp.sum(jnp.where(hit, destfull, 0.0), axis=1, keepdims=True).astype(jnp.int32)
        wk_ref[:, r:r + 1] = jnp.sum(jnp.where(hit, wf, 0.0), axis=1, keepdims=True)


def _dest(selr, wf, rank, pstart_f):
    N, E = selr.shape
    tm = 512
    blk = pl.BlockSpec((tm, E), lambda i: (i, 0))
    outb = pl.BlockSpec((tm, TOP_K), lambda i: (i, 0))
    return pl.pallas_call(
        _dest_kernel,
        grid=(N // tm,),
        in_specs=[blk, blk, blk, pl.BlockSpec((1, E), lambda i: (0, 0))],
        out_specs=[outb, outb],
        out_shape=[jax.ShapeDtypeStruct((N, TOP_K), jnp.int32), jax.ShapeDtypeStruct((N, TOP_K), F32)],
        compiler_params=_cparams(("parallel",)),
        name="route_dest",
    )(selr, wf, rank, pstart_f)


def _row_copy(src, s_row, dst, d_row, n, sem):
    return pltpu.make_async_copy(src.at[pl.ds(s_row, n)], dst.at[pl.ds(d_row, n)], sem)


def _dispatch_kernel(padlo_ref, padn_ref, dest_ref, h_ref, z_ref, xs_ref, sem, zsem, *, tt, n_exp):
    i = pl.program_id(0)

    def issue(t, _):
        for k in range(TOP_K):
            _row_copy(h_ref, t, xs_ref, dest_ref[0, 0, t * TOP_K + k], 1, sem).start()
        return 0

    lax.fori_loop(0, tt, issue, 0)

    def each_pad(fn):
        def per_expert(e, _):
            lo = padlo_ref[e]

            def one(r, _):
                fn(lo + r)
                return 0

            lax.fori_loop(0, padn_ref[e], one, 0)
            return 0

        lax.fori_loop(0, n_exp, per_expert, 0)

    @pl.when(i == 0)
    def _():
        each_pad(lambda r: _row_copy(z_ref, 0, xs_ref, r, 1, zsem).start())

    for k in range(TOP_K):
        _row_copy(h_ref, 0, xs_ref, 0, tt, sem).wait()

    @pl.when(i == 0)
    def _():
        each_pad(lambda r: _row_copy(z_ref, 0, xs_ref, 0, 1, zsem).wait())


def _dispatch(h2p, dest, pad_lo, pad_n, R):
    N, W = h2p.shape
    tt = TOK_TILE
    dest3 = dest.reshape(N // tt, 1, tt * TOP_K)
    zeros = jnp.zeros((SUBLANES, W), h2p.dtype)
    grid_spec = pltpu.PrefetchScalarGridSpec(
        num_scalar_prefetch=2,
        grid=(N // tt,),
        in_specs=[pl.BlockSpec((1, 1, tt * TOP_K), lambda i, lo, n: (i, 0, 0), memory_space=pltpu.SMEM),
                  pl.BlockSpec((tt, W), lambda i, lo, n: (i, 0)),
                  pl.BlockSpec((SUBLANES, W), lambda i, lo, n: (0, 0))],
        out_specs=pl.BlockSpec(memory_space=pl.ANY),
        scratch_shapes=[pltpu.SemaphoreType.DMA(()), pltpu.SemaphoreType.DMA(())],
    )
    return pl.pallas_call(
        functools.partial(_dispatch_kernel, tt=tt, n_exp=N_EXPERTS),
        grid_spec=grid_spec,
        out_shape=jax.ShapeDtypeStruct((R, W), h2p.dtype),
        compiler_params=_cparams(("arbitrary",)),
        name="moe_dispatch",
    )(pad_lo, pad_n, dest3, h2p, zeros)


def _expert_kernel(te_ref, nu_ref, first_ref, slot_ref, nxt_ref, xs_ref, wg_hbm, wu_hbm, wd_hbm, y_ref,
                   stg, stu, std, wgb, wub, wdb, sem):
    i = pl.program_id(0)
    n_chunks = 2

    def fetch(e, s):
        cps = []
        for m, (src, dst) in enumerate(((wg_hbm, stg), (wu_hbm, stu), (wd_hbm, std))):
            rows = dst.shape[1] // n_chunks
            for c in range(n_chunks):
                cps.append(pltpu.make_async_copy(src.at[e, pl.ds(c * rows, rows)],
                                                 dst.at[s, pl.ds(c * rows, rows)], sem.at[s, m]))
        return cps

    @pl.when(i == 0)
    def _():
        for cp in fetch(te_ref[0], 0):
            cp.start()

        @pl.when(nxt_ref[0, 0] >= 0)
        def _():
            for cp in fetch(nxt_ref[0, 0], 1):
                cp.start()

    active = i < nu_ref[0]

    @pl.when(active & (first_ref[i] == 1))
    def _():
        s = slot_ref[i]
        for cp in fetch(0, s):
            cp.wait()
        wgb[...] = stg[s].astype(BF16)
        wub[...] = stu[s].astype(BF16)
        wdb[...] = std[s].astype(BF16)

        @pl.when(nxt_ref[1, i] >= 0)
        def _():
            for cp in fetch(nxt_ref[1, i], s):
                cp.start()

    @pl.when(active)
    def _():
        lo, hi = _unpack_pair(xs_ref[...])
        lo = lo.astype(BF16)
        hi = hi.astype(BF16)
        half = lo.shape[-1]
        a = (jnp.dot(lo, wgb[:half, :], preferred_element_type=F32)
             + jnp.dot(hi, wgb[half:, :], preferred_element_type=F32))
        u = (jnp.dot(lo, wub[:half, :], preferred_element_type=F32)
             + jnp.dot(hi, wub[half:, :], preferred_element_type=F32))
        hmid = (a * jax.nn.sigmoid(a) * u).astype(BF16)
        y = jnp.dot(hmid, wdb[...], preferred_element_type=F32)
        y_ref[...] = _pack_pair(y[:, :half], y[:, half:])


def _experts(xs, tile_expert, n_used, first, slot, nxt, wg, wu, wd):
    R, W = xs.shape
    M = ROW_TILE
    _, D, F = wg.shape
    row = lambda i, te, nu, fi, sl, nx: (jnp.minimum(i, nu[0] - 1), 0)
    grid_spec = pltpu.PrefetchScalarGridSpec(
        num_scalar_prefetch=5,
        grid=(R // M,),
        in_specs=[pl.BlockSpec((M, W), row),
                  pl.BlockSpec(memory_space=pl.ANY),
                  pl.BlockSpec(memory_space=pl.ANY),
                  pl.BlockSpec(memory_space=pl.ANY)],
        out_specs=pl.BlockSpec((M, W), row),
        scratch_shapes=[pltpu.VMEM((2, D, F), F32), pltpu.VMEM((2, D, F), F32), pltpu.VMEM((2, F, D), F32),
                        pltpu.VMEM((D, F), BF16), pltpu.VMEM((D, F), BF16), pltpu.VMEM((F, D), BF16),
                        pltpu.SemaphoreType.DMA((2, 3))],
    )
    return pl.pallas_call(
        _expert_kernel,
        grid_spec=grid_spec,
        out_shape=jax.ShapeDtypeStruct((R, W), U32),
        compiler_params=_cparams(("arbitrary",)),
        name="moe_experts",
    )(tile_expert, n_used, first, slot, nxt, xs, wg, wu, wd)


def _final_kernel(dcur_ref, dnxt_ref, h_ref, wsg_ref, wsu_ref, wsd_ref, x1_ref, gf_ref, g_ref, wk_ref, y_ref,
                  o_ref, ybuf, sem, *, tt, n_tiles):
    i = pl.program_id(0)
    slot = lax.rem(i, 2)

    def gather(d_ref, s):
        def issue(t, _):
            for k in range(TOP_K):
                pltpu.make_async_copy(y_ref.at[pl.ds(d_ref[0, 0, t * TOP_K + k], 1)],
                                      ybuf.at[s, k, pl.ds(t, 1)], sem.at[s]).start()
            return 0

        lax.fori_loop(0, tt, issue, 0)

    @pl.when(i == 0)
    def _():
        gather(dcur_ref, 0)

    @pl.when(i + 1 < n_tiles)
    def _():
        gather(dnxt_ref, 1 - slot)

    lo, hi = _unpack_pair(h_ref[...])
    lo = lo.astype(BF16)
    hi = hi.astype(BF16)
    half = lo.shape[-1]
    a = (jnp.dot(lo, wsg_ref[:half, :], preferred_element_type=F32)
         + jnp.dot(hi, wsg_ref[half:, :], preferred_element_type=F32))
    u = (jnp.dot(lo, wsu_ref[:half, :], preferred_element_type=F32)
         + jnp.dot(hi, wsu_ref[half:, :], preferred_element_type=F32))
    hmid = (a * jax.nn.sigmoid(a) * u).astype(BF16)
    shared = jnp.dot(hmid, wsd_ref[...], preferred_element_type=F32)

    for k in range(TOP_K):
        pltpu.make_async_copy(y_ref.at[pl.ds(0, tt)], ybuf.at[slot, k], sem.at[slot]).wait()

    wk = wk_ref[...]
    r_lo = jnp.zeros((tt, half), F32)
    r_hi = jnp.zeros((tt, half), F32)
    for k in range(TOP_K):
        ylo, yhi = _unpack_pair(ybuf[slot, k])
        wcol = wk[:, k:k + 1]
        r_lo = r_lo + wcol * ylo
        r_hi = r_hi + wcol * yhi
    total = shared + jnp.concatenate([r_lo, r_hi], axis=1)
    x2 = x1_ref[...] + gf_ref[0] * total
    ms = jnp.mean(x2 * x2, axis=-1, keepdims=True)
    o_ref[...] = x2 * lax.rsqrt(ms + NORM_EPS) * g_ref[...]


def _final(h2p, wsg, wsu, wsd, x1, gate_f, norm_out, dest, wk, y, T):
    N, W = h2p.shape
    D = x1.shape[-1]
    F = wsg.shape[1]
    tt = TOK_TILE
    per_b = T // tt
    n_tiles = N // tt
    dest3 = dest.reshape(n_tiles, 1, tt * TOP_K)
    rowb = lambda w: pl.BlockSpec((tt, w), lambda i: (i, 0))
    const = lambda shp: pl.BlockSpec(shp, lambda i: (0,) * len(shp))
    dspec = lambda f: pl.BlockSpec((1, 1, tt * TOP_K), f, memory_space=pltpu.SMEM)
    return pl.pallas_call(
        functools.partial(_final_kernel, tt=tt, n_tiles=n_tiles),
        grid=(n_tiles,),
        in_specs=[dspec(lambda i: (i, 0, 0)),
                  dspec(lambda i: (jnp.minimum(i + 1, n_tiles - 1), 0, 0)),
                  rowb(W), const((D, F)), const((D, F)), const((F, D)), rowb(D),
                  pl.BlockSpec((1, 1, D), lambda i: (i // per_b, 0, 0)),
                  const((1, D)),
                  pl.BlockSpec((tt, TOP_K), lambda i: (i, 0)),
                  pl.BlockSpec(memory_space=pl.ANY)],
        out_specs=rowb(D),
        out_shape=jax.ShapeDtypeStruct((N, D), F32),
        scratch_shapes=[pltpu.VMEM((2, TOP_K, tt, W), U32), pltpu.SemaphoreType.DMA((2,))],
        compiler_params=_cparams(("arbitrary",)),
        name="moe_combine_final",
    )(dest3, dest3, h2p, wsg, wsu, wsd, x1, gate_f[:, None, :], norm_out[None, :], wk, y)


def kernel(x, c, positions, w_ada, b_ada, norm_mix, norm_ffn, norm_out, w_in, w_out, w_router, router_bias,
           w_gate, w_up, w_down, w_sh_gate, w_sh_up, w_sh_down):
    B, T, D = x.shape
    depth = w_ada.shape[0]
    assert depth == 1, "the final rmsnorm is fused into the layer's last kernel"
    moba_w = MOBA_HEADS * MOBA_HEAD_DIM
    ret_w = RET_HEADS * RET_HEAD_DIM
    N = B * T
    E, M = N_EXPERTS, ROW_TILE
    R = N * TOP_K + E * M
    n_tiles = R // M
    tabs = _rope_tables(positions)

    for l in range(depth):
        mod = _ada(c, w_ada[l], b_ada[l])
        shift_a, scale_a, gate_a, shift_f, scale_f, gate_f = jnp.split(mod, 6, axis=-1)

        proj = _in_proj(x, norm_mix[l], shift_a, scale_a, w_in[l].astype(BF16), tabs, moba_w, ret_w)
        o_a = _moba(proj, B, T, moba_w)
        o_r = _retention(proj, B, T, moba_w, ret_w)
        x1, h2p, logits = _out_proj(o_a, o_r, w_out[l].astype(BF16), x, gate_a, norm_ffn[l],
                                    shift_f, scale_f, w_router[l])

        h2p = h2p.reshape(N, D // 2)
        selr, wf, rank, counts = _route(logits.reshape(N, E), router_bias[l])
        cnt = counts[0].astype(jnp.int32)
        pcnt = (cnt + M - 1) // M * M
        pend = jnp.cumsum(pcnt)
        pstart = pend - pcnt
        tidx = jnp.arange(n_tiles, dtype=jnp.int32)
        tile_expert = jnp.minimum(jnp.sum(pend[None, :] <= (tidx * M)[:, None], axis=1), E - 1).astype(jnp.int32)
        n_used = (pend[-1:] // M).astype(jnp.int32)
        first = (((tidx == 0) | (tile_expert != jnp.roll(tile_expert, 1))) & (tidx < n_used[0])).astype(jnp.int32)
        slot = ((jnp.cumsum(first) - 1) % 2).astype(jnp.int32)
        end1 = pend[tile_expert] // M
        exp1 = tile_expert[jnp.minimum(end1, n_tiles - 1)]
        end2 = pend[exp1] // M
        exp2 = tile_expert[jnp.minimum(end2, n_tiles - 1)]
        has1 = end1 < n_used[0]
        nxt = jnp.stack([jnp.where(has1, exp1, -1),
                         jnp.where(has1 & (end2 < n_used[0]), exp2, -1)]).astype(jnp.int32)
        dest, wk = _dest(selr, wf, rank, pstart.astype(F32)[None, :])

        xs = _dispatch(h2p, dest, (pstart + cnt).astype(jnp.int32), (pcnt - cnt).astype(jnp.int32), R)
        y = _experts(xs, tile_expert, n_used, first, slot, nxt, w_gate[l], w_up[l], w_down[l])
        out = _final(h2p, w_sh_gate[l].astype(BF16), w_sh_up[l].astype(BF16), w_sh_down[l].astype(BF16),
                     x1.reshape(N, D), gate_f, norm_out, dest, wk, y, T)
        x = out.reshape(B, T, D)
    return x
```

```python
import functools

import jax
import jax.numpy as jnp
from jax import lax
from jax.experimental import pallas as pl
from jax.experimental.pallas import tpu as pltpu

MOBA_HEADS = 8
MOBA_HEAD_DIM = 128
MOBA_BLOCK = 256
MOBA_TOPK = 3
ROPE_THETA = 500000.0
ROPE_DIMS = 32
RET_HEADS = 4
RET_HEAD_DIM = 256
RET_ROPE_BASE = 10000.0
N_EXPERTS = 64
TOP_K = 8
N_GROUPS = 8
TOPK_GROUPS = 4
ROUTE_SCALE = 2.5
NORM_EPS = 1e-6
NEG = -1e30

LANES = 128
SUBLANES = 8
VMEM_LIMIT = 56 * 1024 * 1024

RET_CHUNK = 256
ROW_TILE = 256
TOK_TILE = 256

F32 = jnp.float32
BF16 = jnp.bfloat16
U32 = jnp.uint32


def _cparams(sem):
    return pltpu.CompilerParams(dimension_semantics=sem, vmem_limit_bytes=VMEM_LIMIT)


def _rms_mod(xf, g, shift, scale):
    ms = jnp.mean(xf * xf, axis=-1, keepdims=True)
    y = xf * lax.rsqrt(ms + NORM_EPS) * g
    return y * (1.0 + scale) + shift


def _pack_pair(lo, hi):
    lo_b = lax.bitcast_convert_type(lo.astype(BF16).astype(F32), U32)
    hi_b = lax.bitcast_convert_type(hi.astype(BF16).astype(F32), U32)
    return (lo_b >> 16) | hi_b


def _unpack_pair(p):
    lo = lax.bitcast_convert_type(p << 16, F32)
    hi = lax.bitcast_convert_type(p & jnp.uint32(0xFFFF0000), F32)
    return lo, hi


def _tables_kernel(pos_ref, invm_ref, invr_ref, mc_ref, ms1_ref, ms2_ref, rc_ref, rs_ref):
    pos = pos_ref[0].astype(F32)
    angm = pos * invm_ref[...]
    lane = lax.broadcasted_iota(jnp.int32, angm.shape, 1)
    half = ROPE_DIMS // 2
    c = jnp.cos(angm)
    s = jnp.sin(angm)
    mc_ref[0] = c
    ms1_ref[0] = jnp.where(lane < half, -s, 0.0)
    ms2_ref[0] = jnp.where((lane >= half) & (lane < ROPE_DIMS), s, 0.0)
    angr = pos * invr_ref[...]
    rc_ref[0] = jnp.cos(angr)
    rs_ref[0] = jnp.sin(angr)


def _rope_tables(positions):
    B, T = positions.shape
    tm = 512
    half = ROPE_DIMS // 2
    moba_inv = ROPE_THETA ** (-(jnp.arange(half, dtype=F32) * 2.0 / ROPE_DIMS))
    invm = jnp.concatenate([moba_inv, moba_inv, jnp.zeros((LANES - ROPE_DIMS,), F32)])[None, :]
    invr = (RET_ROPE_BASE ** (-jnp.linspace(0.0, 1.0, RET_HEAD_DIM // 2, dtype=F32)))[None, :]
    pos3 = positions.reshape(B, T, 1)
    tab = jax.ShapeDtypeStruct((B, T, LANES), F32)
    spec = pl.BlockSpec((1, tm, LANES), lambda b, i: (b, i, 0))
    return pl.pallas_call(
        _tables_kernel,
        grid=(B, T // tm),
        in_specs=[pl.BlockSpec((1, tm, 1), lambda b, i: (b, i, 0)),
                  pl.BlockSpec((1, LANES), lambda b, i: (0, 0)),
                  pl.BlockSpec((1, LANES), lambda b, i: (0, 0))],
        out_specs=[spec] * 5,
        out_shape=[tab] * 5,
        compiler_params=_cparams(("parallel", "parallel")),
        name="rope_tables",
    )(pos3, invm, invr)


def _ada_kernel(c_ref, w_ref, b_ref, o_ref):
    c = c_ref[...]
    sc = c * jax.nn.sigmoid(c)
    o_ref[...] = jnp.dot(sc, w_ref[...], preferred_element_type=F32,
                         precision=lax.Precision.HIGHEST) + b_ref[...]


def _ada(c, w_ada, b_ada):
    B, D = c.shape
    n_out = w_ada.shape[1]
    rows = SUBLANES
    tn = 1024
    c8 = jnp.zeros((rows, D), F32).at[:B].set(c)
    mod = pl.pallas_call(
        _ada_kernel,
        grid=(n_out // tn,),
        in_specs=[pl.BlockSpec((rows, D), lambda j: (0, 0)),
                  pl.BlockSpec((D, tn), lambda j: (0, j)),
                  pl.BlockSpec((1, tn), lambda j: (0, j))],
        out_specs=pl.BlockSpec((rows, tn), lambda j: (0, j)),
        out_shape=jax.ShapeDtypeStruct((rows, n_out), F32),
        compiler_params=_cparams(("parallel",)),
        name="adaln_mod",
    )(c8, w_ada, b_ada[None, :])
    return mod[:B]


def _inproj_kernel(x_ref, g_ref, sh_ref, sc_ref, w_ref, mc_ref, ms1_ref, ms2_ref, rc_ref, rs_ref,
                   o_ref, hn_ref, *, tn, moba_tiles, ret_lo, ret_k_lo, ret_hi):
    j = pl.program_id(2)

    @pl.when(j == 0)
    def _():
        h = _rms_mod(x_ref[0], g_ref[...], sh_ref[0], sc_ref[0])
        hn_ref[...] = h.astype(BF16)

    acc = jnp.dot(hn_ref[...], w_ref[...], preferred_element_type=F32)

    @pl.when(j < moba_tiles)
    def _():
        c, s1, s2 = mc_ref[0], ms1_ref[0], ms2_ref[0]
        half = ROPE_DIMS // 2
        for g in range(tn // LANES):
            a = acc[:, g * LANES:(g + 1) * LANES]
            r = a * c + pltpu.roll(a, LANES - half, 1) * s1 + pltpu.roll(a, half, 1) * s2
            o_ref[0, :, g * LANES:(g + 1) * LANES] = r.astype(o_ref.dtype)

    @pl.when((j >= ret_lo) & (j < ret_hi))
    def _():
        c, s = rc_ref[0], rs_ref[0]
        fac = jnp.where(j >= ret_k_lo, RET_HEAD_DIM ** -0.5, 1.0).astype(F32)
        hw = RET_HEAD_DIM // 2
        for g in range(tn // RET_HEAD_DIM):
            x1 = acc[:, g * RET_HEAD_DIM:g * RET_HEAD_DIM + hw]
            x2 = acc[:, g * RET_HEAD_DIM + hw:(g + 1) * RET_HEAD_DIM]
            o_ref[0, :, g * RET_HEAD_DIM:g * RET_HEAD_DIM + hw] = ((x1 * c - x2 * s) * fac).astype(o_ref.dtype)
            o_ref[0, :, g * RET_HEAD_DIM + hw:(g + 1) * RET_HEAD_DIM] = ((x2 * c + x1 * s) * fac).astype(o_ref.dtype)

    @pl.when(((j >= moba_tiles) & (j < ret_lo)) | (j >= ret_hi))
    def _():
        o_ref[0] = acc.astype(o_ref.dtype)


def _in_proj(x, g, shift, scale, w_bf, tabs, moba_w, ret_w):
    B, T, D = x.shape
    NC = w_bf.shape[1]
    tm, tn = 512, 1024
    mc, ms1, ms2, rc, rs = tabs
    kern = functools.partial(
        _inproj_kernel, tn=tn,
        moba_tiles=2 * moba_w // tn,
        ret_lo=3 * moba_w // tn,
        ret_k_lo=(3 * moba_w + ret_w) // tn,
        ret_hi=(3 * moba_w + 2 * ret_w) // tn)
    tab_spec = pl.BlockSpec((1, tm, LANES), lambda b, i, j: (b, i, 0))
    vec_spec = pl.BlockSpec((1, 1, D), lambda b, i, j: (b, 0, 0))
    return pl.pallas_call(
        kern,
        grid=(B, T // tm, NC // tn),
        in_specs=[pl.BlockSpec((1, tm, D), lambda b, i, j: (b, i, 0)),
                  pl.BlockSpec((1, D), lambda b, i, j: (0, 0)),
                  vec_spec, vec_spec,
                  pl.BlockSpec((D, tn), lambda b, i, j: (0, j)),
                  tab_spec, tab_spec, tab_spec, tab_spec, tab_spec],
        out_specs=pl.BlockSpec((1, tm, tn), lambda b, i, j: (b, i, j)),
        out_shape=jax.ShapeDtypeStruct((B, T, NC), BF16),
        scratch_shapes=[pltpu.VMEM((tm, D), BF16)],
        compiler_params=_cparams(("parallel", "parallel", "arbitrary")),
        name="in_proj",
    )(x, g[None, :], shift[:, None, :], scale[:, None, :], w_bf, mc, ms1, ms2, rc, rs)


def _moba_kernel(q_ref, k_ref, v_ref, o_ref, km_ref, vt_ref, sel_ref, qs_ref, s_ref, m_ref, l_ref, acc_ref,
                 *, nb, hp):
    qb = pl.program_id(2)
    BS, hd = MOBA_BLOCK, MOBA_HEAD_DIM

    @pl.when(qb == 0)
    def _():
        for h in range(hp):
            hs = slice(h * hd, (h + 1) * hd)
            for n in range(nb):
                kb = k_ref[0, n * BS:(n + 1) * BS, hs].astype(F32)
                km_ref[h, n:n + 1, :] = jnp.sum(kb, axis=0, keepdims=True) * (1.0 / BS)
                vt_ref[h, n] = v_ref[0, n * BS:(n + 1) * BS, hs].astype(F32).T.astype(BF16)

    scale = hd ** -0.5
    own = pl.multiple_of(qb * BS, BS)
    blk = lax.broadcasted_iota(jnp.int32, (nb, BS), 0)
    blk_f = blk.astype(F32)

    for h in range(hp):
        hs = slice(h * hd, (h + 1) * hd)
        qT = q_ref[0, :, hs].astype(F32).T
        gate = jnp.dot(km_ref[h], qT, preferred_element_type=F32,
                       precision=lax.Precision.HIGHEST)
        g = jnp.where(blk < qb, gate, NEG)
        sel = jnp.zeros((nb, BS), F32)
        for _ in range(MOBA_TOPK):
            m = jnp.max(g, axis=0, keepdims=True)
            idx = jnp.min(jnp.where(g == m, blk_f, float(nb)), axis=0, keepdims=True)
            pick = blk_f == idx
            sel = jnp.where(pick & (m > 0.5 * NEG), 1.0, sel)
            g = jnp.where(pick, -jnp.inf, g)
        sel_ref[h] = sel
        qs_ref[h] = (qT * scale).astype(BF16)
        m_ref[h] = jnp.full((1, BS), NEG, F32)

    n_pairs = lax.shift_right_logical(qb + 1, 1)

    def sweep_scores(j, _):
        for h in range(hp):
            mh = m_ref[h]
            for u in range(2):
                n = 2 * j + u
                off = pl.multiple_of(n * BS, BS)
                s = jnp.dot(k_ref[0, pl.ds(off, BS), h * hd:(h + 1) * hd], qs_ref[h],
                            preferred_element_type=F32)
                s = jnp.where(sel_ref[h, pl.ds(n, 1), :] > 0.0, s, NEG)
                s_ref[h, n] = s
                mh = jnp.maximum(mh, jnp.max(s, axis=0, keepdims=True))
            m_ref[h] = mh
        return 0

    lax.fori_loop(0, n_pairs, sweep_scores, 0)

    krow = lax.broadcasted_iota(jnp.int32, (BS, BS), 0)
    qcol = lax.broadcasted_iota(jnp.int32, (BS, BS), 1)
    for h in range(hp):
        hs = slice(h * hd, (h + 1) * hd)
        s = jnp.dot(k_ref[0, pl.ds(own, BS), hs], qs_ref[h], preferred_element_type=F32)
        s = jnp.where(krow <= qcol, s, NEG)
        m = jnp.maximum(m_ref[h], jnp.max(s, axis=0, keepdims=True))
        m_ref[h] = m
        p = jnp.exp(s - m)
        l_ref[h] = jnp.sum(p, axis=0, keepdims=True)
        acc_ref[h] = jnp.dot(vt_ref[h, qb], p.astype(BF16), preferred_element_type=F32)

    def sweep_values(j, _):
        for h in range(hp):
            mh = m_ref[h]
            p0 = jnp.exp(s_ref[h, 2 * j] - mh)
            p1 = jnp.exp(s_ref[h, 2 * j + 1] - mh)
            l_ref[h] = l_ref[h] + (jnp.sum(p0, axis=0, keepdims=True) + jnp.sum(p1, axis=0, keepdims=True))
            acc_ref[h] = acc_ref[h] + (
                jnp.dot(vt_ref[h, 2 * j], p0.astype(BF16), preferred_element_type=F32)
                + jnp.dot(vt_ref[h, 2 * j + 1], p1.astype(BF16), preferred_element_type=F32))
        return 0

    lax.fori_loop(0, n_pairs, sweep_values, 0)

    for h in range(hp):
        o_ref[0, :, h * hd:(h + 1) * hd] = (acc_ref[h] / l_ref[h]).T.astype(o_ref.dtype)


def _moba_kernel_v2(q_ref, k_ref, v_ref, o_ref, km_ref, vt_ref, sel_ref, *, nb, hp):
    qb = pl.program_id(2)
    BS, hd = MOBA_BLOCK, MOBA_HEAD_DIM

    @pl.when(qb == 0)
    def _():
        for h in range(hp):
            hs = slice(h * hd, (h + 1) * hd)
            for n in range(nb):
                kb = k_ref[0, n * BS:(n + 1) * BS, hs].astype(F32)
                km_ref[h, n:n + 1, :] = jnp.sum(kb, axis=0, keepdims=True) * (1.0 / BS)
                vt_ref[h, n] = v_ref[0, n * BS:(n + 1) * BS, hs].astype(F32).T.astype(BF16)

    scale = hd ** -0.5
    own = pl.multiple_of(qb * BS, BS)
    blk = lax.broadcasted_iota(jnp.int32, (nb, BS), 0)
    blk_f = blk.astype(F32)
    krow = lax.broadcasted_iota(jnp.int32, (BS, BS), 0)
    qcol = lax.broadcasted_iota(jnp.int32, (BS, BS), 1)

    qs_all, init = [], []
    for h in range(hp):
        hs = slice(h * hd, (h + 1) * hd)
        qT = q_ref[0, :, hs].astype(F32).T
        gate = jnp.dot(km_ref[h], qT, preferred_element_type=F32,
                       precision=lax.Precision.HIGHEST)
        g = jnp.where(blk < qb, gate, NEG)
        sel = jnp.zeros((nb, BS), F32)
        for _ in range(MOBA_TOPK):
            m = jnp.max(g, axis=0, keepdims=True)
            idx = jnp.min(jnp.where(g == m, blk_f, float(nb)), axis=0, keepdims=True)
            pick = blk_f == idx
            sel = jnp.where(pick & (m > 0.5 * NEG), 1.0, sel)
            g = jnp.where(pick, -jnp.inf, g)
        sel_ref[h] = sel

        qs = (qT * scale).astype(BF16)
        s = jnp.dot(k_ref[0, pl.ds(own, BS), hs], qs, preferred_element_type=F32)
        s = jnp.where(krow <= qcol, s, NEG)
        m0 = jnp.max(s, axis=0, keepdims=True)
        p = jnp.exp(s - m0)
        l0 = jnp.sum(p, axis=0, keepdims=True)
        acc0 = jnp.dot(vt_ref[h, qb], p.astype(BF16), preferred_element_type=F32)
        qs_all.append(qs)
        init.append((m0, l0, acc0))

    def body(n, carry):
        off = pl.multiple_of(n * BS, BS)
        out = []
        for h in range(hp):
            m, l, acc = carry[h]
            selrow = sel_ref[h, pl.ds(n, 1), :]
            s = jnp.dot(k_ref[0, pl.ds(off, BS), h * hd:(h + 1) * hd], qs_all[h],
                        preferred_element_type=F32)
            s = jnp.where(selrow > 0.0, s, NEG)
            m_new = jnp.maximum(m, jnp.max(s, axis=0, keepdims=True))
            alpha = jnp.exp(m - m_new)
            p = jnp.exp(s - m_new)
            l = alpha * l + jnp.sum(p, axis=0, keepdims=True)
            acc = alpha * acc + jnp.dot(vt_ref[h, n], p.astype(BF16), preferred_element_type=F32)
            out.append((m_new, l, acc))
        return tuple(out)

    final = lax.fori_loop(0, qb, body, tuple(init))
    for h in range(hp):
        _, l, acc = final[h]
        o_ref[0, :, h * hd:(h + 1) * hd] = (acc / l).T.astype(o_ref.dtype)


def _moba(proj, B, T, moba_w):
    H, hd, BS = MOBA_HEADS, MOBA_HEAD_DIM, MOBA_BLOCK
    nb = T // BS
    hp = 2
    gw = hp * hd
    gpw = moba_w // gw
    return pl.pallas_call(
        functools.partial(_moba_kernel, nb=nb, hp=hp),
        grid=(B, H // hp, nb),
        in_specs=[pl.BlockSpec((1, BS, gw), lambda b, h, i: (b, i, h)),
                  pl.BlockSpec((1, T, gw), lambda b, h, i: (b, 0, gpw + h)),
                  pl.BlockSpec((1, T, gw), lambda b, h, i: (b, 0, 2 * gpw + h))],
        out_specs=pl.BlockSpec((1, BS, gw), lambda b, h, i: (b, i, h)),
        out_shape=jax.ShapeDtypeStruct((B, T, moba_w), BF16),
        scratch_shapes=[pltpu.VMEM((hp, nb, hd), F32),
                        pltpu.VMEM((hp, nb, hd, BS), BF16),
                        pltpu.VMEM((hp, nb, BS), F32),
                        pltpu.VMEM((hp, hd, BS), BF16),
                        pltpu.VMEM((hp, nb, BS, BS), F32),
                        pltpu.VMEM((hp, 1, BS), F32),
                        pltpu.VMEM((hp, 1, BS), F32),
                        pltpu.VMEM((hp, hd, BS), F32)],
        compiler_params=_cparams(("parallel", "parallel", "arbitrary")),
        name="moba_attn",
    )(proj, proj, proj)


def _ret_kernel(q_ref, k_ref, v_ref, g_ref, dm_ref, xi_ref, zeta_ref, cd_ref, o_ref, s_ref):
    c = pl.program_id(1)
    d = RET_HEAD_DIM

    @pl.when(c == 0)
    def _():
        s_ref[...] = jnp.zeros_like(s_ref)

    for h in range(RET_HEADS):
        sl = slice(h * d, (h + 1) * d)
        q = q_ref[0, :, sl]
        k = k_ref[0, :, sl]
        v = v_ref[0, :, sl]
        inner = lax.dot_general(q, k, (((1,), (1,)), ((), ())), preferred_element_type=F32) * dm_ref[h]
        S = s_ref[h]
        o = (jnp.dot(inner.astype(BF16), v, preferred_element_type=F32)
             + jnp.dot(q, S.astype(BF16), preferred_element_type=F32) * xi_ref[h])
        kz = (k.astype(F32) * zeta_ref[h]).astype(BF16)
        s_ref[h] = S * cd_ref[h] + lax.dot_general(kz, v, (((0,), (0,)), ((), ())),
                                                   preferred_element_type=F32)
        mu = jnp.mean(o, axis=-1, keepdims=True)
        dlt = o - mu
        var = jnp.mean(dlt * dlt, axis=-1, keepdims=True)
        on = dlt * lax.rsqrt(var + NORM_EPS)
        gg = g_ref[0, :, sl].astype(F32)
        o_ref[0, :, sl] = (on * (gg * jax.nn.sigmoid(gg))).astype(o_ref.dtype)


def _retention(proj, B, T, moba_w, ret_w):
    C, H = RET_CHUNK, RET_HEADS
    gamma = 1.0 - jnp.exp2(-5.0 - jnp.arange(H, dtype=F32))
    log_g = jnp.log(gamma)
    pos = jnp.arange(C, dtype=F32)
    diff = pos[:, None] - pos[None, :]
    dmask = jnp.where(diff >= 0, jnp.exp(jnp.maximum(diff, 0.0) * log_g[:, None, None]), 0.0)
    xi = jnp.exp((pos + 1.0) * log_g[:, None])[:, :, None]
    zeta = jnp.exp((C - 1.0 - pos) * log_g[:, None])[:, :, None]
    cd = jnp.exp(C * log_g)[:, None, None]
    base = 3 * moba_w // ret_w
    col = lambda off: pl.BlockSpec((1, C, ret_w), lambda b, c: (b, c, base + off))
    full = lambda shp: pl.BlockSpec(shp, lambda b, c: (0,) * len(shp))
    return pl.pallas_call(
        _ret_kernel,
        grid=(B, T // C),
        in_specs=[col(0), col(1), col(2), col(3),
                  full((H, C, C)), full((H, C, 1)), full((H, C, 1)), full((H, 1, 1))],
        out_specs=pl.BlockSpec((1, C, ret_w), lambda b, c: (b, c, 0)),
        out_shape=jax.ShapeDtypeStruct((B, T, ret_w), BF16),
        scratch_shapes=[pltpu.VMEM((H, RET_HEAD_DIM, RET_HEAD_DIM), F32)],
        compiler_params=_cparams(("parallel", "arbitrary")),
        name="retention",
    )(proj, proj, proj, proj, dmask, xi, zeta, cd)


def _outproj_kernel(oa_ref, or_ref, w_ref, x_ref, ga_ref, g_ref, sh_ref, sc_ref, wrh_ref, wrl_ref,
                    x1_ref, hp_ref, lg_ref, *, moba_w):
    mix = (jnp.dot(oa_ref[0], w_ref[:moba_w, :], preferred_element_type=F32)
           + jnp.dot(or_ref[0], w_ref[moba_w:, :], preferred_element_type=F32))
    x1 = x_ref[0] + ga_ref[0] * mix
    x1_ref[0] = x1
    h = _rms_mod(x1, g_ref[...], sh_ref[0], sc_ref[0])
    half = h.shape[-1] // 2
    hp_ref[0] = _pack_pair(h[:, :half], h[:, half:])
    h_hi = h.astype(BF16)
    h_lo = (h - h_hi.astype(F32)).astype(BF16)
    lg_ref[0] = (jnp.dot(h_hi, wrh_ref[...], preferred_element_type=F32)
                 + (jnp.dot(h_lo, wrh_ref[...], preferred_element_type=F32)
                    + jnp.dot(h_hi, wrl_ref[...], preferred_element_type=F32)))


def _out_proj(o_a, o_r, w_bf, x, gate_a, g, shift, scale, w_router):
    B, T, D = x.shape
    moba_w, ret_w = o_a.shape[-1], o_r.shape[-1]
    E = w_router.shape[1]
    tm = 256
    wr_hi = w_router.astype(BF16)
    wr_lo = (w_router - wr_hi.astype(F32)).astype(BF16)
    vec = pl.BlockSpec((1, 1, D), lambda b, i: (b, 0, 0))
    row = lambda w: pl.BlockSpec((1, tm, w), lambda b, i: (b, i, 0))
    return pl.pallas_call(
        functools.partial(_outproj_kernel, moba_w=moba_w),
        grid=(B, T // tm),
        in_specs=[row(moba_w), row(ret_w),
                  pl.BlockSpec((moba_w + ret_w, D), lambda b, i: (0, 0)),
                  row(D), vec,
                  pl.BlockSpec((1, D), lambda b, i: (0, 0)),
                  vec, vec,
                  pl.BlockSpec((D, E), lambda b, i: (0, 0)),
                  pl.BlockSpec((D, E), lambda b, i: (0, 0))],
        out_specs=[row(D), row(D // 2), row(E)],
        out_shape=[jax.ShapeDtypeStruct((B, T, D), F32),
                   jax.ShapeDtypeStruct((B, T, D // 2), U32),
                   jax.ShapeDtypeStruct((B, T, E), F32)],
        compiler_params=_cparams(("parallel", "parallel")),
        name="out_proj",
    )(o_a, o_r, w_bf, x, gate_a[:, None, :], g[None, :], shift[:, None, :], scale[:, None, :], wr_hi, wr_lo)


def _first_argmax(v, lane_f, width):
    m = jnp.max(v, axis=1, keepdims=True)
    idx = jnp.min(jnp.where(v == m, lane_f, float(width)), axis=1, keepdims=True)
    return m, idx


def _route_kernel(lg_ref, b_ref, selr_ref, wf_ref, rank_ref, cnt_ref, carry_ref):
    i = pl.program_id(0)

    @pl.when(i == 0)
    def _():
        carry_ref[...] = jnp.zeros_like(carry_ref)

    E = N_EXPERTS
    gsz = E // N_GROUPS
    s = jax.nn.sigmoid(lg_ref[...])
    biased = s + b_ref[...]
    tm = s.shape[0]
    lane = lax.broadcasted_iota(jnp.int32, (tm, E), 1)
    lane_f = lane.astype(F32)
    grp = lax.shift_right_logical(lane, gsz.bit_length() - 1)

    gscore = jnp.full((tm, E), -jnp.inf, F32)
    for gi in range(N_GROUPS):
        v = jnp.where(grp == gi, biased, -jnp.inf)
        m1, i1 = _first_argmax(v, lane_f, E)
        m2 = jnp.max(jnp.where(lane_f == i1, -jnp.inf, v), axis=1, keepdims=True)
        gscore = jnp.where(lane == gi, m1 + m2, gscore)

    emask = jnp.zeros((tm, E), jnp.bool_)
    grp_f = grp.astype(F32)
    for _ in range(TOPK_GROUPS):
        _, gi = _first_argmax(gscore, lane_f, E)
        emask = emask | (grp_f == gi)
        gscore = jnp.where(lane_f == gi, -jnp.inf, gscore)

    cand = jnp.where(emask, biased, NEG)
    selr = jnp.zeros((tm, E), F32)
    for r in range(TOP_K):
        _, ei = _first_argmax(cand, lane_f, E)
        pick = lane_f == ei
        selr = jnp.where(pick, float(r + 1), selr)
        cand = jnp.where(pick, -jnp.inf, cand)

    chosen = selr > 0.0
    w = jnp.where(chosen, s, 0.0)
    wsum = jnp.sum(w, axis=1, keepdims=True)
    selr_ref[...] = selr
    wf_ref[...] = w / wsum * ROUTE_SCALE

    onehot = chosen.astype(BF16)
    r_i = lax.broadcasted_iota(jnp.int32, (tm, tm), 0)
    c_i = lax.broadcasted_iota(jnp.int32, (tm, tm), 1)
    tri = (c_i < r_i).astype(BF16)
    carry = carry_ref[...]
    rank_ref[...] = jnp.dot(tri, onehot, preferred_element_type=F32) + carry
    carry = carry + jnp.sum(chosen.astype(F32), axis=0, keepdims=True)
    carry_ref[...] = carry
    cnt_ref[...] = carry


def _route(logits, bias):
    N, E = logits.shape
    tm = 512
    blk = pl.BlockSpec((tm, E), lambda i: (i, 0))
    one = pl.BlockSpec((1, E), lambda i: (0, 0))
    full = jax.ShapeDtypeStruct((N, E), F32)
    return pl.pallas_call(
        _route_kernel,
        grid=(N // tm,),
        in_specs=[blk, one],
        out_specs=[blk, blk, blk, one],
        out_shape=[full, full, full, jax.ShapeDtypeStruct((1, E), F32)],
        scratch_shapes=[pltpu.VMEM((1, E), F32)],
        compiler_params=_cparams(("arbitrary",)),
        name="route_topk",
    )(logits, bias[None, :])


def _dest_kernel(selr_ref, wf_ref, rank_ref, ps_ref, dest_ref, wk_ref):
    selr = selr_ref[...]
    destfull = rank_ref[...] + ps_ref[...]
    wf = wf_ref[...]
    for r in range(TOP_K):
        hit = selr == float(r + 1)
        dest_ref[:, r:r + 1] = jnp.sum(jnp.where(hit, destfull, 0.0), axis=1, keepdims=True).astype(jnp.int32)
        wk_ref[:, r:r + 1] = jnp.sum(jnp.where(hit, wf, 0.0), axis=1, keepdims=True)


def _dest(selr, wf, rank, pstart_f):
    N, E = selr.shape
    tm = 512
    blk = pl.BlockSpec((tm, E), lambda i: (i, 0))
    outb = pl.BlockSpec((tm, TOP_K), lambda i: (i, 0))
    return pl.pallas_call(
        _dest_kernel,
        grid=(N // tm,),
        in_specs=[blk, blk, blk, pl.BlockSpec((1, E), lambda i: (0, 0))],
        out_specs=[outb, outb],
        out_shape=[jax.ShapeDtypeStruct((N, TOP_K), jnp.int32), jax.ShapeDtypeStruct((N, TOP_K), F32)],
        compiler_params=_cparams(("parallel",)),
        name="route_dest",
    )(selr, wf, rank, pstart_f)


def _row_copy(src, s_row, dst, d_row, n, sem):
    return pltpu.make_async_copy(src.at[pl.ds(s_row, n)], dst.at[pl.ds(d_row, n)], sem)


def _dispatch_kernel(padlo_ref, padn_ref, dest_ref, h_ref, z_ref, xs_ref, sem, zsem, *, tt, n_exp):
    i = pl.program_id(0)

    def issue(t, _):
        for k in range(TOP_K):
            _row_copy(h_ref, t, xs_ref, dest_ref[0, 0, t * TOP_K + k], 1, sem).start(priority=k % 2)
        return 0

    lax.fori_loop(0, tt, issue, 0, unroll=2)

    def each_pad(fn):
        def per_expert(e, _):
            lo = padlo_ref[e]

            def one(r, _):
                fn(lo + r)
                return 0

            lax.fori_loop(0, padn_ref[e], one, 0)
            return 0

        lax.fori_loop(0, n_exp, per_expert, 0)

    @pl.when(i == 0)
    def _():
        each_pad(lambda r: _row_copy(z_ref, 0, xs_ref, r, 1, zsem).start())

    for k in range(TOP_K):
        _row_copy(h_ref, 0, xs_ref, 0, tt, sem).wait()

    @pl.when(i == 0)
    def _():
        each_pad(lambda r: _row_copy(z_ref, 0, xs_ref, 0, 1, zsem).wait())


def _dispatch(h2p, dest, pad_lo, pad_n, R):
    N, W = h2p.shape
    tt = TOK_TILE
    dest3 = dest.reshape(N // tt, 1, tt * TOP_K)
    zeros = jnp.zeros((SUBLANES, W), h2p.dtype)
    grid_spec = pltpu.PrefetchScalarGridSpec(
        num_scalar_prefetch=2,
        grid=(N // tt,),
        in_specs=[pl.BlockSpec((1, 1, tt * TOP_K), lambda i, lo, n: (i, 0, 0), memory_space=pltpu.SMEM),
                  pl.BlockSpec((tt, W), lambda i, lo, n: (i, 0)),
                  pl.BlockSpec((SUBLANES, W), lambda i, lo, n: (0, 0))],
        out_specs=pl.BlockSpec(memory_space=pl.ANY),
        scratch_shapes=[pltpu.SemaphoreType.DMA(()), pltpu.SemaphoreType.DMA(())],
    )
    return pl.pallas_call(
        functools.partial(_dispatch_kernel, tt=tt, n_exp=N_EXPERTS),
        grid_spec=grid_spec,
        out_shape=jax.ShapeDtypeStruct((R, W), h2p.dtype),
        compiler_params=_cparams(("arbitrary",)),
        name="moe_dispatch",
    )(pad_lo, pad_n, dest3, h2p, zeros)


def _expert_kernel(te_ref, nu_ref, first_ref, slot_ref, nxt_ref, xs_ref, wg_hbm, wu_hbm, wd_hbm, y_ref,
                   stg, stu, std, wgb, wub, wdb, sem):
    i = pl.program_id(0)
    n_chunks = 2

    def fetch(e, s):
        cps = []
        for m, (src, dst) in enumerate(((wg_hbm, stg), (wu_hbm, stu), (wd_hbm, std))):
            rows = dst.shape[1] // n_chunks
            for c in range(n_chunks):
                cps.append(pltpu.make_async_copy(src.at[e, pl.ds(c * rows, rows)],
                                                 dst.at[s, pl.ds(c * rows, rows)], sem.at[s, m]))
        return cps

    @pl.when(i == 0)
    def _():
        for cp in fetch(te_ref[0], 0):
            cp.start()

        @pl.when(nxt_ref[0, 0] >= 0)
        def _():
            for cp in fetch(nxt_ref[0, 0], 1):
                cp.start()

    active = i < nu_ref[0]

    @pl.when(active & (first_ref[i] == 1))
    def _():
        s = slot_ref[i]
        for cp in fetch(0, s):
            cp.wait()
        wgb[...] = stg[s].astype(BF16)
        wub[...] = stu[s].astype(BF16)
        wdb[...] = std[s].astype(BF16)

        @pl.when(nxt_ref[1, i] >= 0)
        def _():
            for cp in fetch(nxt_ref[1, i], s):
                cp.start()

    @pl.when(active)
    def _():
        lo, hi = _unpack_pair(xs_ref[...])
        lo = lo.astype(BF16)
        hi = hi.astype(BF16)
        half = lo.shape[-1]
        a = (jnp.dot(lo, wgb[:half, :], preferred_element_type=F32)
             + jnp.dot(hi, wgb[half:, :], preferred_element_type=F32))
        u = (jnp.dot(lo, wub[:half, :], preferred_element_type=F32)
             + jnp.dot(hi, wub[half:, :], preferred_element_type=F32))
        hmid = (a * jax.nn.sigmoid(a) * u).astype(BF16)
        y = jnp.dot(hmid, wdb[...], preferred_element_type=F32)
        y_ref[...] = _pack_pair(y[:, :half], y[:, half:])


def _experts(xs, tile_expert, n_used, first, slot, nxt, wg, wu, wd):
    R, W = xs.shape
    M = ROW_TILE
    _, D, F = wg.shape
    row = lambda i, te, nu, fi, sl, nx: (jnp.minimum(i, nu[0] - 1), 0)
    grid_spec = pltpu.PrefetchScalarGridSpec(
        num_scalar_prefetch=5,
        grid=(R // M,),
        in_specs=[pl.BlockSpec((M, W), row),
                  pl.BlockSpec(memory_space=pl.ANY),
                  pl.BlockSpec(memory_space=pl.ANY),
                  pl.BlockSpec(memory_space=pl.ANY)],
        out_specs=pl.BlockSpec((M, W), row),
        scratch_shapes=[pltpu.VMEM((2, D, F), F32), pltpu.VMEM((2, D, F), F32), pltpu.VMEM((2, F, D), F32),
                        pltpu.VMEM((D, F), BF16), pltpu.VMEM((D, F), BF16), pltpu.VMEM((F, D), BF16),
                        pltpu.SemaphoreType.DMA((2, 3))],
    )
    return pl.pallas_call(
        _expert_kernel,
        grid_spec=grid_spec,
        out_shape=jax.ShapeDtypeStruct((R, W), U32),
        compiler_params=_cparams(("arbitrary",)),
        name="moe_experts",
    )(tile_expert, n_used, first, slot, nxt, xs, wg, wu, wd)


def _final_kernel(dcur_ref, dnxt_ref, h_ref, wsg_ref, wsu_ref, wsd_ref, x1_ref, gf_ref, g_ref, wk_ref, y_ref,
                  o_ref, ybuf, sem, *, tt, n_tiles):
    i = pl.program_id(0)
    slot = lax.rem(i, 2)

    def gather(d_ref, s):
        def issue(t, _):
            for k in range(TOP_K):
                pltpu.make_async_copy(y_ref.at[pl.ds(d_ref[0, 0, t * TOP_K + k], 1)],
                                      ybuf.at[s, k, pl.ds(t, 1)], sem.at[s]).start(priority=k % 2)
            return 0

        lax.fori_loop(0, tt, issue, 0, unroll=2)

    @pl.when(i == 0)
    def _():
        gather(dcur_ref, 0)

    for p in range(2):
        @pl.when((i + 1 < n_tiles) & (slot == p))
        def _(p=p):
            gather(dnxt_ref, 1 - p)

    lo, hi = _unpack_pair(h_ref[...])
    lo = lo.astype(BF16)
    hi = hi.astype(BF16)
    half = lo.shape[-1]
    a = (jnp.dot(lo, wsg_ref[:half, :], preferred_element_type=F32)
         + jnp.dot(hi, wsg_ref[half:, :], preferred_element_type=F32))
    u = (jnp.dot(lo, wsu_ref[:half, :], preferred_element_type=F32)
         + jnp.dot(hi, wsu_ref[half:, :], preferred_element_type=F32))
    hmid = (a * jax.nn.sigmoid(a) * u).astype(BF16)
    shared = jnp.dot(hmid, wsd_ref[...], preferred_element_type=F32)

    for k in range(TOP_K):
        pltpu.make_async_copy(y_ref.at[pl.ds(0, tt)], ybuf.at[slot, k], sem.at[slot]).wait()

    wk = wk_ref[...]
    r_lo = jnp.zeros((tt, half), F32)
    r_hi = jnp.zeros((tt, half), F32)
    for k in range(TOP_K):
        ylo, yhi = _unpack_pair(ybuf[slot, k])
        wcol = wk[:, k:k + 1]
        r_lo = r_lo + wcol * ylo
        r_hi = r_hi + wcol * yhi
    total = shared + jnp.concatenate([r_lo, r_hi], axis=1)
    x2 = x1_ref[...] + gf_ref[0] * total
    ms = jnp.mean(x2 * x2, axis=-1, keepdims=True)
    o_ref[...] = x2 * lax.rsqrt(ms + NORM_EPS) * g_ref[...]


def _final(h2p, wsg, wsu, wsd, x1, gate_f, norm_out, dest, wk, y, T):
    N, W = h2p.shape
    D = x1.shape[-1]
    F = wsg.shape[1]
    tt = TOK_TILE
    per_b = T // tt
    n_tiles = N // tt
    dest3 = dest.reshape(n_tiles, 1, tt * TOP_K)
    rowb = lambda w: pl.BlockSpec((tt, w), lambda i: (i, 0))
    const = lambda shp: pl.BlockSpec(shp, lambda i: (0,) * len(shp))
    dspec = lambda f: pl.BlockSpec((1, 1, tt * TOP_K), f, memory_space=pltpu.SMEM)
    return pl.pallas_call(
        functools.partial(_final_kernel, tt=tt, n_tiles=n_tiles),
        grid=(n_tiles,),
        in_specs=[dspec(lambda i: (i, 0, 0)),
                  dspec(lambda i: (jnp.minimum(i + 1, n_tiles - 1), 0, 0)),
                  rowb(W), const((D, F)), const((D, F)), const((F, D)), rowb(D),
                  pl.BlockSpec((1, 1, D), lambda i: (i // per_b, 0, 0)),
                  const((1, D)),
                  pl.BlockSpec((tt, TOP_K), lambda i: (i, 0)),
                  pl.BlockSpec(memory_space=pl.ANY)],
        out_specs=rowb(D),
        out_shape=jax.ShapeDtypeStruct((N, D), F32),
        scratch_shapes=[pltpu.VMEM((2, TOP_K, tt, W), U32), pltpu.SemaphoreType.DMA((2,))],
        compiler_params=_cparams(("arbitrary",)),
        name="moe_combine_final",
    )(dest3, dest3, h2p, wsg, wsu, wsd, x1, gate_f[:, None, :], norm_out[None, :], wk, y)


def kernel(x, c, positions, w_ada, b_ada, norm_mix, norm_ffn, norm_out, w_in, w_out, w_router, router_bias,
           w_gate, w_up, w_down, w_sh_gate, w_sh_up, w_sh_down):
    B, T, D = x.shape
    depth = w_ada.shape[0]
    assert depth == 1, "the final rmsnorm is fused into the layer's last kernel"
    moba_w = MOBA_HEADS * MOBA_HEAD_DIM
    ret_w = RET_HEADS * RET_HEAD_DIM
    N = B * T
    E, M = N_EXPERTS, ROW_TILE
    R = N * TOP_K + E * M
    n_tiles = R // M
    tabs = _rope_tables(positions)

    for l in range(depth):
        mod = _ada(c, w_ada[l], b_ada[l])
        shift_a, scale_a, gate_a, shift_f, scale_f, gate_f = jnp.split(mod, 6, axis=-1)

        proj = _in_proj(x, norm_mix[l], shift_a, scale_a, w_in[l].astype(BF16), tabs, moba_w, ret_w)
        o_a = _moba(proj, B, T, moba_w)
        o_r = _retention(proj, B, T, moba_w, ret_w)
        x1, h2p, logits = _out_proj(o_a, o_r, w_out[l].astype(BF16), x, gate_a, norm_ffn[l],
                                    shift_f, scale_f, w_router[l])

        h2p = h2p.reshape(N, D // 2)
        selr, wf, rank, counts = _route(logits.reshape(N, E), router_bias[l])
        cnt = counts[0].astype(jnp.int32)
        pcnt = (cnt + M - 1) // M * M
        pend = jnp.cumsum(pcnt)
        pstart = pend - pcnt
        tidx = jnp.arange(n_tiles, dtype=jnp.int32)
        end_tile = pend // M
        eids = jnp.arange(E, dtype=jnp.int32)
        owner = lambda v: jnp.minimum(jnp.sum(end_tile[None, :] <= v[:, None], axis=1), E - 1).astype(jnp.int32)
        end_of = lambda e: jnp.sum(jnp.where(e[:, None] == eids[None, :], end_tile[None, :], 0), axis=1)
        tile_expert = owner(tidx)
        n_used = end_tile[-1:].astype(jnp.int32)
        first = (((tidx == 0) | (tile_expert != jnp.roll(tile_expert, 1))) & (tidx < n_used[0])).astype(jnp.int32)
        slot = ((jnp.cumsum(first) - 1) % 2).astype(jnp.int32)
        end1 = end_of(tile_expert)
        exp1 = owner(end1)
        end2 = end_of(exp1)
        exp2 = owner(end2)
        has1 = end1 < n_used[0]
        nxt = jnp.stack([jnp.where(has1, exp1, -1),
                         jnp.where(has1 & (end2 < n_used[0]), exp2, -1)]).astype(jnp.int32)
        dest, wk = _dest(selr, wf, rank, pstart.astype(F32)[None, :])

        xs = _dispatch(h2p, dest, (pstart + cnt).astype(jnp.int32), (pcnt - cnt).astype(jnp.int32), R)
        y = _experts(xs, tile_expert, n_used, first, slot, nxt, w_gate[l], w_up[l], w_down[l])
        out = _final(h2p, w_sh_gate[l].astype(BF16), w_sh_up[l].astype(BF16), w_sh_down[l].astype(BF16),
                     x1.reshape(N, D), gate_f, norm_out, dest, wk, y, T)
        x = out.reshape(B, T, D)
    return x
```

```python
import functools

import jax
import jax.numpy as jnp
from jax import lax
from jax.experimental import pallas as pl
from jax.experimental.pallas import tpu as pltpu

MOBA_HEADS = 8
MOBA_HEAD_DIM = 128
MOBA_BLOCK = 256
MOBA_TOPK = 3
ROPE_THETA = 500000.0
ROPE_DIMS = 32
RET_HEADS = 4
RET_HEAD_DIM = 256
RET_ROPE_BASE = 10000.0
N_EXPERTS = 64
TOP_K = 8
N_GROUPS = 8
TOPK_GROUPS = 4
ROUTE_SCALE = 2.5
NORM_EPS = 1e-6
NEG = -1e30

LANES = 128
SUBLANES = 8
VMEM_LIMIT = 56 * 1024 * 1024

RET_CHUNK = 256
ROW_TILE = 256
TOK_TILE = 256

F32 = jnp.float32
BF16 = jnp.bfloat16
U32 = jnp.uint32


def _cparams(sem):
    return pltpu.CompilerParams(dimension_semantics=sem, vmem_limit_bytes=VMEM_LIMIT)


def _rms_mod(xf, g, shift, scale):
    ms = jnp.mean(xf * xf, axis=-1, keepdims=True)
    y = xf * lax.rsqrt(ms + NORM_EPS) * g
    return y * (1.0 + scale) + shift


def _pack_pair(lo, hi):
    lo_b = lax.bitcast_convert_type(lo.astype(BF16).astype(F32), U32)
    hi_b = lax.bitcast_convert_type(hi.astype(BF16).astype(F32), U32)
    return (lo_b >> 16) | hi_b


def _unpack_pair(p):
    lo = lax.bitcast_convert_type(p << 16, F32)
    hi = lax.bitcast_convert_type(p & jnp.uint32(0xFFFF0000), F32)
    return lo, hi


def _store_rows_as_tiles(ref, val, base=0):
    n = val.shape[0]
    for c in range(SUBLANES):
        ref[pl.ds(base * SUBLANES + c, n, stride=SUBLANES), :] = val[:, c * LANES:(c + 1) * LANES]


def _load_tiles_as_rows(ref, n, base=0):
    return jnp.concatenate([ref[pl.ds(base * SUBLANES + c, n, stride=SUBLANES), :] for c in range(SUBLANES)],
                           axis=1)


def _flat_tiles(ref):
    rows = 1
    for d in ref.shape[:-2]:
        rows *= d
    return ref.reshape(rows * SUBLANES, LANES)


def _tables_kernel(pos_ref, invm_ref, invr_ref, mc_ref, ms1_ref, ms2_ref, rc_ref, rs_ref):
    pos = pos_ref[0].astype(F32)
    angm = pos * invm_ref[...]
    lane = lax.broadcasted_iota(jnp.int32, angm.shape, 1)
    half = ROPE_DIMS // 2
    c = jnp.cos(angm)
    s = jnp.sin(angm)
    mc_ref[0] = c
    ms1_ref[0] = jnp.where(lane < half, -s, 0.0)
    ms2_ref[0] = jnp.where((lane >= half) & (lane < ROPE_DIMS), s, 0.0)
    angr = pos * invr_ref[...]
    rc_ref[0] = jnp.cos(angr)
    rs_ref[0] = jnp.sin(angr)


def _rope_tables(positions):
    B, T = positions.shape
    tm = 512
    half = ROPE_DIMS // 2
    moba_inv = ROPE_THETA ** (-(jnp.arange(half, dtype=F32) * 2.0 / ROPE_DIMS))
    invm = jnp.concatenate([moba_inv, moba_inv, jnp.zeros((LANES - ROPE_DIMS,), F32)])[None, :]
    invr = (RET_ROPE_BASE ** (-jnp.linspace(0.0, 1.0, RET_HEAD_DIM // 2, dtype=F32)))[None, :]
    pos3 = positions.reshape(B, T, 1)
    tab = jax.ShapeDtypeStruct((B, T, LANES), F32)
    spec = pl.BlockSpec((1, tm, LANES), lambda b, i: (b, i, 0))
    return pl.pallas_call(
        _tables_kernel,
        grid=(B, T // tm),
        in_specs=[pl.BlockSpec((1, tm, 1), lambda b, i: (b, i, 0)),
                  pl.BlockSpec((1, LANES), lambda b, i: (0, 0)),
                  pl.BlockSpec((1, LANES), lambda b, i: (0, 0))],
        out_specs=[spec] * 5,
        out_shape=[tab] * 5,
        compiler_params=_cparams(("parallel", "parallel")),
        name="rope_tables",
    )(pos3, invm, invr)


def _ada_kernel(c_ref, w_ref, b_ref, o_ref):
    c = c_ref[...]
    sc = c * jax.nn.sigmoid(c)
    o_ref[...] = jnp.dot(sc, w_ref[...], preferred_element_type=F32,
                         precision=lax.Precision.HIGHEST) + b_ref[...]


def _ada(c, w_ada, b_ada):
    B, D = c.shape
    n_out = w_ada.shape[1]
    rows = SUBLANES
    tn = 1024
    c8 = jnp.zeros((rows, D), F32).at[:B].set(c)
    mod = pl.pallas_call(
        _ada_kernel,
        grid=(n_out // tn,),
        in_specs=[pl.BlockSpec((rows, D), lambda j: (0, 0)),
                  pl.BlockSpec((D, tn), lambda j: (0, j)),
                  pl.BlockSpec((1, tn), lambda j: (0, j))],
        out_specs=pl.BlockSpec((rows, tn), lambda j: (0, j)),
        out_shape=jax.ShapeDtypeStruct((rows, n_out), F32),
        compiler_params=_cparams(("parallel",)),
        name="adaln_mod",
    )(c8, w_ada, b_ada[None, :])
    return mod[:B]


def _inproj_kernel(x_ref, g_ref, sh_ref, sc_ref, w_ref, mc_ref, ms1_ref, ms2_ref, rc_ref, rs_ref,
                   o_ref, hn_ref, *, tn, moba_tiles, ret_lo, ret_k_lo, ret_hi):
    j = pl.program_id(2)

    @pl.when(j == 0)
    def _():
        h = _rms_mod(x_ref[0], g_ref[...], sh_ref[0], sc_ref[0])
        hn_ref[...] = h.astype(BF16)

    acc = jnp.dot(hn_ref[...], w_ref[...], preferred_element_type=F32)

    @pl.when(j < moba_tiles)
    def _():
        c, s1, s2 = mc_ref[0], ms1_ref[0], ms2_ref[0]
        half = ROPE_DIMS // 2
        for g in range(tn // LANES):
            a = acc[:, g * LANES:(g + 1) * LANES]
            r = a * c + pltpu.roll(a, LANES - half, 1) * s1 + pltpu.roll(a, half, 1) * s2
            o_ref[0, :, g * LANES:(g + 1) * LANES] = r.astype(o_ref.dtype)

    @pl.when((j >= ret_lo) & (j < ret_hi))
    def _():
        c, s = rc_ref[0], rs_ref[0]
        fac = jnp.where(j >= ret_k_lo, RET_HEAD_DIM ** -0.5, 1.0).astype(F32)
        hw = RET_HEAD_DIM // 2
        for g in range(tn // RET_HEAD_DIM):
            x1 = acc[:, g * RET_HEAD_DIM:g * RET_HEAD_DIM + hw]
            x2 = acc[:, g * RET_HEAD_DIM + hw:(g + 1) * RET_HEAD_DIM]
            o_ref[0, :, g * RET_HEAD_DIM:g * RET_HEAD_DIM + hw] = ((x1 * c - x2 * s) * fac).astype(o_ref.dtype)
            o_ref[0, :, g * RET_HEAD_DIM + hw:(g + 1) * RET_HEAD_DIM] = ((x2 * c + x1 * s) * fac).astype(o_ref.dtype)

    @pl.when(((j >= moba_tiles) & (j < ret_lo)) | (j >= ret_hi))
    def _():
        o_ref[0] = acc.astype(o_ref.dtype)


def _in_proj(x, g, shift, scale, w_bf, tabs, moba_w, ret_w):
    B, T, D = x.shape
    NC = w_bf.shape[1]
    tm, tn = 512, 1024
    mc, ms1, ms2, rc, rs = tabs
    kern = functools.partial(
        _inproj_kernel, tn=tn,
        moba_tiles=2 * moba_w // tn,
        ret_lo=3 * moba_w // tn,
        ret_k_lo=(3 * moba_w + ret_w) // tn,
        ret_hi=(3 * moba_w + 2 * ret_w) // tn)
    tab_spec = pl.BlockSpec((1, tm, LANES), lambda b, i, j: (b, i, 0))
    vec_spec = pl.BlockSpec((1, 1, D), lambda b, i, j: (b, 0, 0))
    return pl.pallas_call(
        kern,
        grid=(B, T // tm, NC // tn),
        in_specs=[pl.BlockSpec((1, tm, D), lambda b, i, j: (b, i, 0)),
                  pl.BlockSpec((1, D), lambda b, i, j: (0, 0)),
                  vec_spec, vec_spec,
                  pl.BlockSpec((D, tn), lambda b, i, j: (0, j)),
                  tab_spec, tab_spec, tab_spec, tab_spec, tab_spec],
        out_specs=pl.BlockSpec((1, tm, tn), lambda b, i, j: (b, i, j)),
        out_shape=jax.ShapeDtypeStruct((B, T, NC), BF16),
        scratch_shapes=[pltpu.VMEM((tm, D), BF16)],
        compiler_params=_cparams(("parallel", "parallel", "arbitrary")),
        name="in_proj",
    )(x, g[None, :], shift[:, None, :], scale[:, None, :], w_bf, mc, ms1, ms2, rc, rs)


def _moba_kernel(q_ref, k_ref, v_ref, o_ref, km_ref, vt_ref, sel_ref, qs_ref, s_ref, m_ref, l_ref, acc_ref,
                 *, nb, hp):
    qb = pl.program_id(2)
    BS, hd = MOBA_BLOCK, MOBA_HEAD_DIM

    @pl.when(qb == 0)
    def _():
        for h in range(hp):
            hs = slice(h * hd, (h + 1) * hd)
            for n in range(nb):
                kb = k_ref[0, n * BS:(n + 1) * BS, hs].astype(F32)
                km_ref[h, n:n + 1, :] = jnp.sum(kb, axis=0, keepdims=True) * (1.0 / BS)
                vt_ref[h, n] = v_ref[0, n * BS:(n + 1) * BS, hs].astype(F32).T.astype(BF16)

    scale = hd ** -0.5
    own = pl.multiple_of(qb * BS, BS)
    blk = lax.broadcasted_iota(jnp.int32, (nb, BS), 0)
    blk_f = blk.astype(F32)

    for h in range(hp):
        hs = slice(h * hd, (h + 1) * hd)
        qT = q_ref[0, :, hs].astype(F32).T
        gate = jnp.dot(km_ref[h], qT, preferred_element_type=F32,
                       precision=lax.Precision.HIGHEST)
        g = jnp.where(blk < qb, gate, NEG)
        sel = jnp.zeros((nb, BS), F32)
        for _ in range(MOBA_TOPK):
            m = jnp.max(g, axis=0, keepdims=True)
            idx = jnp.min(jnp.where(g == m, blk_f, float(nb)), axis=0, keepdims=True)
            pick = blk_f == idx
            sel = jnp.where(pick & (m > 0.5 * NEG), 1.0, sel)
            g = jnp.where(pick, -jnp.inf, g)
        sel_ref[h] = sel
        qs_ref[h] = (qT * scale).astype(BF16)
        m_ref[h] = jnp.full((1, BS), NEG, F32)

    n_pairs = lax.shift_right_logical(qb + 1, 1)

    def sweep_scores(j, _):
        for h in range(hp):
            mh = m_ref[h]
            for u in range(2):
                n = 2 * j + u
                off = pl.multiple_of(n * BS, BS)
                s = jnp.dot(k_ref[0, pl.ds(off, BS), h * hd:(h + 1) * hd], qs_ref[h],
                            preferred_element_type=F32)
                s = jnp.where(sel_ref[h, pl.ds(n, 1), :] > 0.0, s, NEG)
                s_ref[h, n] = s
                mh = jnp.maximum(mh, jnp.max(s, axis=0, keepdims=True))
            m_ref[h] = mh
        return 0

    lax.fori_loop(0, n_pairs, sweep_scores, 0)

    krow = lax.broadcasted_iota(jnp.int32, (BS, BS), 0)
    qcol = lax.broadcasted_iota(jnp.int32, (BS, BS), 1)
    for h in range(hp):
        hs = slice(h * hd, (h + 1) * hd)
        s = jnp.dot(k_ref[0, pl.ds(own, BS), hs], qs_ref[h], preferred_element_type=F32)
        s = jnp.where(krow <= qcol, s, NEG)
        m = jnp.maximum(m_ref[h], jnp.max(s, axis=0, keepdims=True))
        m_ref[h] = m
        p = jnp.exp(s - m)
        l_ref[h] = jnp.sum(p, axis=0, keepdims=True)
        acc_ref[h] = jnp.dot(vt_ref[h, qb], p.astype(BF16), preferred_element_type=F32)

    def sweep_values(j, _):
        for h in range(hp):
            mh = m_ref[h]
            p0 = jnp.exp(s_ref[h, 2 * j] - mh)
            p1 = jnp.exp(s_ref[h, 2 * j + 1] - mh)
            l_ref[h] = l_ref[h] + (jnp.sum(p0, axis=0, keepdims=True) + jnp.sum(p1, axis=0, keepdims=True))
            acc_ref[h] = acc_ref[h] + (
                jnp.dot(vt_ref[h, 2 * j], p0.astype(BF16), preferred_element_type=F32)
                + jnp.dot(vt_ref[h, 2 * j + 1], p1.astype(BF16), preferred_element_type=F32))
        return 0

    lax.fori_loop(0, n_pairs, sweep_values, 0)

    for h in range(hp):
        o_ref[0, :, h * hd:(h + 1) * hd] = (acc_ref[h] / l_ref[h]).T.astype(o_ref.dtype)


def _moba_kernel_v2(q_ref, k_ref, v_ref, o_ref, km_ref, vt_ref, sel_ref, *, nb, hp):
    qb = pl.program_id(2)
    BS, hd = MOBA_BLOCK, MOBA_HEAD_DIM

    @pl.when(qb == 0)
    def _():
        for h in range(hp):
            hs = slice(h * hd, (h + 1) * hd)
            for n in range(nb):
                kb = k_ref[0, n * BS:(n + 1) * BS, hs].astype(F32)
                km_ref[h, n:n + 1, :] = jnp.sum(kb, axis=0, keepdims=True) * (1.0 / BS)
                vt_ref[h, n] = v_ref[0, n * BS:(n + 1) * BS, hs].astype(F32).T.astype(BF16)

    scale = hd ** -0.5
    own = pl.multiple_of(qb * BS, BS)
    blk = lax.broadcasted_iota(jnp.int32, (nb, BS), 0)
    blk_f = blk.astype(F32)
    krow = lax.broadcasted_iota(jnp.int32, (BS, BS), 0)
    qcol = lax.broadcasted_iota(jnp.int32, (BS, BS), 1)

    qs_all, init = [], []
    for h in range(hp):
        hs = slice(h * hd, (h + 1) * hd)
        qT = q_ref[0, :, hs].astype(F32).T
        gate = jnp.dot(km_ref[h], qT, preferred_element_type=F32,
                       precision=lax.Precision.HIGHEST)
        g = jnp.where(blk < qb, gate, NEG)
        sel = jnp.zeros((nb, BS), F32)
        for _ in range(MOBA_TOPK):
            m = jnp.max(g, axis=0, keepdims=True)
            idx = jnp.min(jnp.where(g == m, blk_f, float(nb)), axis=0, keepdims=True)
            pick = blk_f == idx
            sel = jnp.where(pick & (m > 0.5 * NEG), 1.0, sel)
            g = jnp.where(pick, -jnp.inf, g)
        sel_ref[h] = sel

        qs = (qT * scale).astype(BF16)
        s = jnp.dot(k_ref[0, pl.ds(own, BS), hs], qs, preferred_element_type=F32)
        s = jnp.where(krow <= qcol, s, NEG)
        m0 = jnp.max(s, axis=0, keepdims=True)
        p = jnp.exp(s - m0)
        l0 = jnp.sum(p, axis=0, keepdims=True)
        acc0 = jnp.dot(vt_ref[h, qb], p.astype(BF16), preferred_element_type=F32)
        qs_all.append(qs)
        init.append((m0, l0, acc0))

    def body(n, carry):
        off = pl.multiple_of(n * BS, BS)
        out = []
        for h in range(hp):
            m, l, acc = carry[h]
            selrow = sel_ref[h, pl.ds(n, 1), :]
            s = jnp.dot(k_ref[0, pl.ds(off, BS), h * hd:(h + 1) * hd], qs_all[h],
                        preferred_element_type=F32)
            s = jnp.where(selrow > 0.0, s, NEG)
            m_new = jnp.maximum(m, jnp.max(s, axis=0, keepdims=True))
            alpha = jnp.exp(m - m_new)
            p = jnp.exp(s - m_new)
            l = alpha * l + jnp.sum(p, axis=0, keepdims=True)
            acc = alpha * acc + jnp.dot(vt_ref[h, n], p.astype(BF16), preferred_element_type=F32)
            out.append((m_new, l, acc))
        return tuple(out)

    final = lax.fori_loop(0, qb, body, tuple(init))
    for h in range(hp):
        _, l, acc = final[h]
        o_ref[0, :, h * hd:(h + 1) * hd] = (acc / l).T.astype(o_ref.dtype)


def _moba(proj, B, T, moba_w):
    H, hd, BS = MOBA_HEADS, MOBA_HEAD_DIM, MOBA_BLOCK
    nb = T // BS
    hp = 2
    gw = hp * hd
    gpw = moba_w // gw
    return pl.pallas_call(
        functools.partial(_moba_kernel, nb=nb, hp=hp),
        grid=(B, H // hp, nb),
        in_specs=[pl.BlockSpec((1, BS, gw), lambda b, h, i: (b, i, h)),
                  pl.BlockSpec((1, T, gw), lambda b, h, i: (b, 0, gpw + h)),
                  pl.BlockSpec((1, T, gw), lambda b, h, i: (b, 0, 2 * gpw + h))],
        out_specs=pl.BlockSpec((1, BS, gw), lambda b, h, i: (b, i, h)),
        out_shape=jax.ShapeDtypeStruct((B, T, moba_w), BF16),
        scratch_shapes=[pltpu.VMEM((hp, nb, hd), F32),
                        pltpu.VMEM((hp, nb, hd, BS), BF16),
                        pltpu.VMEM((hp, nb, BS), F32),
                        pltpu.VMEM((hp, hd, BS), BF16),
                        pltpu.VMEM((hp, nb, BS, BS), F32),
                        pltpu.VMEM((hp, 1, BS), F32),
                        pltpu.VMEM((hp, 1, BS), F32),
                        pltpu.VMEM((hp, hd, BS), F32)],
        compiler_params=_cparams(("parallel", "parallel", "arbitrary")),
        name="moba_attn",
    )(proj, proj, proj)


def _ret_kernel(q_ref, k_ref, v_ref, g_ref, dm_ref, xi_ref, zeta_ref, cd_ref, o_ref, s_ref):
    c = pl.program_id(1)
    d = RET_HEAD_DIM

    @pl.when(c == 0)
    def _():
        s_ref[...] = jnp.zeros_like(s_ref)

    for h in range(RET_HEADS):
        sl = slice(h * d, (h + 1) * d)
        q = q_ref[0, :, sl]
        k = k_ref[0, :, sl]
        v = v_ref[0, :, sl]
        inner = lax.dot_general(q, k, (((1,), (1,)), ((), ())), preferred_element_type=F32) * dm_ref[h]
        S = s_ref[h]
        o = (jnp.dot(inner.astype(BF16), v, preferred_element_type=F32)
             + jnp.dot(q, S.astype(BF16), preferred_element_type=F32) * xi_ref[h])
        kz = (k.astype(F32) * zeta_ref[h]).astype(BF16)
        s_ref[h] = S * cd_ref[h] + lax.dot_general(kz, v, (((0,), (0,)), ((), ())),
                                                   preferred_element_type=F32)
        mu = jnp.mean(o, axis=-1, keepdims=True)
        dlt = o - mu
        var = jnp.mean(dlt * dlt, axis=-1, keepdims=True)
        on = dlt * lax.rsqrt(var + NORM_EPS)
        gg = g_ref[0, :, sl].astype(F32)
        o_ref[0, :, sl] = (on * (gg * jax.nn.sigmoid(gg))).astype(o_ref.dtype)


def _retention(proj, B, T, moba_w, ret_w):
    C, H = RET_CHUNK, RET_HEADS
    gamma = 1.0 - jnp.exp2(-5.0 - jnp.arange(H, dtype=F32))
    log_g = jnp.log(gamma)
    pos = jnp.arange(C, dtype=F32)
    diff = pos[:, None] - pos[None, :]
    dmask = jnp.where(diff >= 0, jnp.exp(jnp.maximum(diff, 0.0) * log_g[:, None, None]), 0.0)
    xi = jnp.exp((pos + 1.0) * log_g[:, None])[:, :, None]
    zeta = jnp.exp((C - 1.0 - pos) * log_g[:, None])[:, :, None]
    cd = jnp.exp(C * log_g)[:, None, None]
    base = 3 * moba_w // ret_w
    col = lambda off: pl.BlockSpec((1, C, ret_w), lambda b, c: (b, c, base + off))
    full = lambda shp: pl.BlockSpec(shp, lambda b, c: (0,) * len(shp))
    return pl.pallas_call(
        _ret_kernel,
        grid=(B, T // C),
        in_specs=[col(0), col(1), col(2), col(3),
                  full((H, C, C)), full((H, C, 1)), full((H, C, 1)), full((H, 1, 1))],
        out_specs=pl.BlockSpec((1, C, ret_w), lambda b, c: (b, c, 0)),
        out_shape=jax.ShapeDtypeStruct((B, T, ret_w), BF16),
        scratch_shapes=[pltpu.VMEM((H, RET_HEAD_DIM, RET_HEAD_DIM), F32)],
        compiler_params=_cparams(("parallel", "arbitrary")),
        name="retention",
    )(proj, proj, proj, proj, dmask, xi, zeta, cd)


def _outproj_kernel(oa_ref, or_ref, w_ref, x_ref, ga_ref, g_ref, sh_ref, sc_ref, wrh_ref, wrl_ref,
                    x1_ref, hp_ref, lg_ref, *, moba_w):
    mix = (jnp.dot(oa_ref[0], w_ref[:moba_w, :], preferred_element_type=F32)
           + jnp.dot(or_ref[0], w_ref[moba_w:, :], preferred_element_type=F32))
    x1 = x_ref[0] + ga_ref[0] * mix
    x1_ref[0] = x1
    h = _rms_mod(x1, g_ref[...], sh_ref[0], sc_ref[0])
    half = h.shape[-1] // 2
    _store_rows_as_tiles(_flat_tiles(hp_ref), _pack_pair(h[:, :half], h[:, half:]))
    h_hi = h.astype(BF16)
    h_lo = (h - h_hi.astype(F32)).astype(BF16)
    lg_ref[0] = (jnp.dot(h_hi, wrh_ref[...], preferred_element_type=F32)
                 + (jnp.dot(h_lo, wrh_ref[...], preferred_element_type=F32)
                    + jnp.dot(h_hi, wrl_ref[...], preferred_element_type=F32)))


def _out_proj(o_a, o_r, w_bf, x, gate_a, g, shift, scale, w_router):
    B, T, D = x.shape
    moba_w, ret_w = o_a.shape[-1], o_r.shape[-1]
    E = w_router.shape[1]
    tm = 256
    wr_hi = w_router.astype(BF16)
    wr_lo = (w_router - wr_hi.astype(F32)).astype(BF16)
    vec = pl.BlockSpec((1, 1, D), lambda b, i: (b, 0, 0))
    row = lambda w: pl.BlockSpec((1, tm, w), lambda b, i: (b, i, 0))
    return pl.pallas_call(
        functools.partial(_outproj_kernel, moba_w=moba_w),
        grid=(B, T // tm),
        in_specs=[row(moba_w), row(ret_w),
                  pl.BlockSpec((moba_w + ret_w, D), lambda b, i: (0, 0)),
                  row(D), vec,
                  pl.BlockSpec((1, D), lambda b, i: (0, 0)),
                  vec, vec,
                  pl.BlockSpec((D, E), lambda b, i: (0, 0)),
                  pl.BlockSpec((D, E), lambda b, i: (0, 0))],
        out_specs=[row(D), pl.BlockSpec((1, tm, SUBLANES, LANES), lambda b, i: (b, i, 0, 0)), row(E)],
        out_shape=[jax.ShapeDtypeStruct((B, T, D), F32),
                   jax.ShapeDtypeStruct((B, T, SUBLANES, LANES), U32),
                   jax.ShapeDtypeStruct((B, T, E), F32)],
        compiler_params=_cparams(("parallel", "parallel")),
        name="out_proj",
    )(o_a, o_r, w_bf, x, gate_a[:, None, :], g[None, :], shift[:, None, :], scale[:, None, :], wr_hi, wr_lo)


def _first_argmax(v, lane_f, width):
    m = jnp.max(v, axis=1, keepdims=True)
    idx = jnp.min(jnp.where(v == m, lane_f, float(width)), axis=1, keepdims=True)
    return m, idx


def _route_kernel(lg_ref, b_ref, selr_ref, wf_ref, rank_ref, cnt_ref, carry_ref):
    i = pl.program_id(0)

    @pl.when(i == 0)
    def _():
        carry_ref[...] = jnp.zeros_like(carry_ref)

    E = N_EXPERTS
    gsz = E // N_GROUPS
    s = jax.nn.sigmoid(lg_ref[...])
    biased = s + b_ref[...]
    tm = s.shape[0]
    lane = lax.broadcasted_iota(jnp.int32, (tm, E), 1)
    lane_f = lane.astype(F32)
    grp = lax.shift_right_logical(lane, gsz.bit_length() - 1)

    gscore = jnp.full((tm, E), -jnp.inf, F32)
    for gi in range(N_GROUPS):
        v = jnp.where(grp == gi, biased, -jnp.inf)
        m1, i1 = _first_argmax(v, lane_f, E)
        m2 = jnp.max(jnp.where(lane_f == i1, -jnp.inf, v), axis=1, keepdims=True)
        gscore = jnp.where(lane == gi, m1 + m2, gscore)

    emask = jnp.zeros((tm, E), jnp.bool_)
    grp_f = grp.astype(F32)
    for _ in range(TOPK_GROUPS):
        _, gi = _first_argmax(gscore, lane_f, E)
        emask = emask | (grp_f == gi)
        gscore = jnp.where(lane_f == gi, -jnp.inf, gscore)

    cand = jnp.where(emask, biased, NEG)
    selr = jnp.zeros((tm, E), F32)
    for r in range(TOP_K):
        _, ei = _first_argmax(cand, lane_f, E)
        pick = lane_f == ei
        selr = jnp.where(pick, float(r + 1), selr)
        cand = jnp.where(pick, -jnp.inf, cand)

    chosen = selr > 0.0
    w = jnp.where(chosen, s, 0.0)
    wsum = jnp.sum(w, axis=1, keepdims=True)
    selr_ref[...] = selr
    wf_ref[...] = w / wsum * ROUTE_SCALE

    onehot = chosen.astype(BF16)
    r_i = lax.broadcasted_iota(jnp.int32, (tm, tm), 0)
    c_i = lax.broadcasted_iota(jnp.int32, (tm, tm), 1)
    tri = (c_i < r_i).astype(BF16)
    carry = carry_ref[...]
    rank_ref[...] = jnp.dot(tri, onehot, preferred_element_type=F32) + carry
    carry = carry + jnp.sum(chosen.astype(F32), axis=0, keepdims=True)
    carry_ref[...] = carry
    cnt_ref[...] = carry


def _route(logits, bias):
    N, E = logits.shape
    tm = 512
    blk = pl.BlockSpec((tm, E), lambda i: (i, 0))
    one = pl.BlockSpec((1, E), lambda i: (0, 0))
    full = jax.ShapeDtypeStruct((N, E), F32)
    return pl.pallas_call(
        _route_kernel,
        grid=(N // tm,),
        in_specs=[blk, one],
        out_specs=[blk, blk, blk, one],
        out_shape=[full, full, full, jax.ShapeDtypeStruct((1, E), F32)],
        scratch_shapes=[pltpu.VMEM((1, E), F32)],
        compiler_params=_cparams(("arbitrary",)),
        name="route_topk",
    )(logits, bias[None, :])


def _dest_kernel(selr_ref, wf_ref, rank_ref, ps_ref, dest_ref, wk_ref):
    selr = selr_ref[...]
    destfull = rank_ref[...] + ps_ref[...]
    wf = wf_ref[...]
    for r in range(TOP_K):
        hit = selr == float(r + 1)
        dest_ref[:, r:r + 1] = jnp.sum(jnp.where(hit, destfull, 0.0), axis=1, keepdims=True).astype(jnp.int32)
        wk_ref[:, r:r + 1] = jnp.sum(jnp.where(hit, wf, 0.0), axis=1, keepdims=True)


def _dest(selr, wf, rank, pstart_f):
    N, E = selr.shape
    tm = 512
    blk = pl.BlockSpec((tm, E), lambda i: (i, 0))
    outb = pl.BlockSpec((tm, TOP_K), lambda i: (i, 0))
    return pl.pallas_call(
        _dest_kernel,
        grid=(N // tm,),
        in_specs=[blk, blk, blk, pl.BlockSpec((1, E), lambda i: (0, 0))],
        out_specs=[outb, outb],
        out_shape=[jax.ShapeDtypeStruct((N, TOP_K), jnp.int32), jax.ShapeDtypeStruct((N, TOP_K), F32)],
        compiler_params=_cparams(("parallel",)),
        name="route_dest",
    )(selr, wf, rank, pstart_f)


def _row_copy(src, s_row, dst, d_row, n, sem):
    return pltpu.make_async_copy(src.at[pl.ds(s_row, n)], dst.at[pl.ds(d_row, n)], sem)


def _dispatch_kernel(padlo_ref, padn_ref, dest_ref, h_ref, z_ref, xs_ref, sem, zsem, *, tt, n_exp):
    i = pl.program_id(0)

    def issue(t, _):
        for k in range(TOP_K):
            _row_copy(h_ref, t, xs_ref, dest_ref[0, 0, t * TOP_K + k], 1, sem).start(priority=k % 2)
        return 0

    lax.fori_loop(0, tt, issue, 0, unroll=2)

    def each_pad(fn):
        def per_expert(e, _):
            lo = padlo_ref[e]

            def one(r, _):
                fn(lo + r)
                return 0

            lax.fori_loop(0, padn_ref[e], one, 0)
            return 0

        lax.fori_loop(0, n_exp, per_expert, 0)

    @pl.when(i == 0)
    def _():
        each_pad(lambda r: _row_copy(z_ref, 0, xs_ref, r, 1, zsem).start())

    for k in range(TOP_K):
        _row_copy(h_ref, 0, xs_ref, 0, tt, sem).wait()

    @pl.when(i == 0)
    def _():
        each_pad(lambda r: _row_copy(z_ref, 0, xs_ref, 0, 1, zsem).wait())


def _dispatch(h2p, dest, pad_lo, pad_n, R):
    N = h2p.shape[0]
    tile = h2p.shape[1:]
    tt = TOK_TILE
    dest3 = dest.reshape(N // tt, 1, tt * TOP_K)
    zeros = jnp.zeros((SUBLANES,) + tile, h2p.dtype)
    grid_spec = pltpu.PrefetchScalarGridSpec(
        num_scalar_prefetch=2,
        grid=(N // tt,),
        in_specs=[pl.BlockSpec((1, 1, tt * TOP_K), lambda i, lo, n: (i, 0, 0), memory_space=pltpu.SMEM),
                  pl.BlockSpec((tt,) + tile, lambda i, lo, n: (i, 0, 0)),
                  pl.BlockSpec((SUBLANES,) + tile, lambda i, lo, n: (0, 0, 0))],
        out_specs=pl.BlockSpec(memory_space=pl.ANY),
        scratch_shapes=[pltpu.SemaphoreType.DMA(()), pltpu.SemaphoreType.DMA(())],
    )
    return pl.pallas_call(
        functools.partial(_dispatch_kernel, tt=tt, n_exp=N_EXPERTS),
        grid_spec=grid_spec,
        out_shape=jax.ShapeDtypeStruct((R,) + tile, h2p.dtype),
        compiler_params=_cparams(("arbitrary",)),
        name="moe_dispatch",
    )(pad_lo, pad_n, dest3, h2p, zeros)


def _expert_kernel(te_ref, nu_ref, first_ref, slot_ref, nxt_ref, xs_ref, wg_hbm, wu_hbm, wd_hbm, y_ref,
                   stg, stu, std, wgb, wub, wdb, sem):
    i = pl.program_id(0)
    n_chunks = 2

    def fetch(e, s):
        cps = []
        for m, (src, dst) in enumerate(((wg_hbm, stg), (wu_hbm, stu), (wd_hbm, std))):
            rows = dst.shape[1] // n_chunks
            for c in range(n_chunks):
                cps.append(pltpu.make_async_copy(src.at[e, pl.ds(c * rows, rows)],
                                                 dst.at[s, pl.ds(c * rows, rows)], sem.at[s, m]))
        return cps

    @pl.when(i == 0)
    def _():
        for cp in fetch(te_ref[0], 0):
            cp.start()

        @pl.when(nxt_ref[0, 0] >= 0)
        def _():
            for cp in fetch(nxt_ref[0, 0], 1):
                cp.start()

    active = i < nu_ref[0]

    @pl.when(active & (first_ref[i] == 1))
    def _():
        s = slot_ref[i]
        for cp in fetch(0, s):
            cp.wait()
        wgb[...] = stg[s].astype(BF16)
        wub[...] = stu[s].astype(BF16)
        wdb[...] = std[s].astype(BF16)

        @pl.when(nxt_ref[1, i] >= 0)
        def _():
            for cp in fetch(nxt_ref[1, i], s):
                cp.start()

    @pl.when(active)
    def _():
        m_rows = xs_ref.shape[0]
        lo, hi = _unpack_pair(_load_tiles_as_rows(_flat_tiles(xs_ref), m_rows))
        lo = lo.astype(BF16)
        hi = hi.astype(BF16)
        half = lo.shape[-1]
        a = (jnp.dot(lo, wgb[:half, :], preferred_element_type=F32)
             + jnp.dot(hi, wgb[half:, :], preferred_element_type=F32))
        u = (jnp.dot(lo, wub[:half, :], preferred_element_type=F32)
             + jnp.dot(hi, wub[half:, :], preferred_element_type=F32))
        hmid = (a * jax.nn.sigmoid(a) * u).astype(BF16)
        y = jnp.dot(hmid, wdb[...], preferred_element_type=F32)
        _store_rows_as_tiles(_flat_tiles(y_ref), _pack_pair(y[:, :half], y[:, half:]))


def _experts(xs, tile_expert, n_used, first, slot, nxt, wg, wu, wd):
    R = xs.shape[0]
    tile = xs.shape[1:]
    M = ROW_TILE
    _, D, F = wg.shape
    row = lambda i, te, nu, fi, sl, nx: (jnp.minimum(i, nu[0] - 1), 0, 0)
    grid_spec = pltpu.PrefetchScalarGridSpec(
        num_scalar_prefetch=5,
        grid=(R // M,),
        in_specs=[pl.BlockSpec((M,) + tile, row),
                  pl.BlockSpec(memory_space=pl.ANY),
                  pl.BlockSpec(memory_space=pl.ANY),
                  pl.BlockSpec(memory_space=pl.ANY)],
        out_specs=pl.BlockSpec((M,) + tile, row),
        scratch_shapes=[pltpu.VMEM((2, D, F), F32), pltpu.VMEM((2, D, F), F32), pltpu.VMEM((2, F, D), F32),
                        pltpu.VMEM((D, F), BF16), pltpu.VMEM((D, F), BF16), pltpu.VMEM((F, D), BF16),
                        pltpu.SemaphoreType.DMA((2, 3))],
    )
    return pl.pallas_call(
        _expert_kernel,
        grid_spec=grid_spec,
        out_shape=jax.ShapeDtypeStruct((R,) + tile, U32),
        compiler_params=_cparams(("arbitrary",)),
        name="moe_experts",
    )(tile_expert, n_used, first, slot, nxt, xs, wg, wu, wd)


def _final_kernel(dcur_ref, dnxt_ref, h_ref, wsg_ref, wsu_ref, wsd_ref, x1_ref, gf_ref, g_ref, wk_ref, y_ref,
                  o_ref, ybuf, sem, *, tt, n_tiles):
    i = pl.program_id(0)
    slot = lax.rem(i, 2)

    def gather(d_ref, s):
        def issue(t, _):
            for k in range(TOP_K):
                pltpu.make_async_copy(y_ref.at[pl.ds(d_ref[0, 0, t * TOP_K + k], 1)],
                                      ybuf.at[s, k, pl.ds(t, 1)], sem.at[s]).start(priority=k % 2)
            return 0

        lax.fori_loop(0, tt, issue, 0, unroll=2)

    @pl.when(i == 0)
    def _():
        gather(dcur_ref, 0)

    for p in range(2):
        @pl.when((i + 1 < n_tiles) & (slot == p))
        def _(p=p):
            gather(dnxt_ref, 1 - p)

    lo, hi = _unpack_pair(_load_tiles_as_rows(_flat_tiles(h_ref), tt))
    lo = lo.astype(BF16)
    hi = hi.astype(BF16)
    half = lo.shape[-1]
    a = (jnp.dot(lo, wsg_ref[:half, :], preferred_element_type=F32)
         + jnp.dot(hi, wsg_ref[half:, :], preferred_element_type=F32))
    u = (jnp.dot(lo, wsu_ref[:half, :], preferred_element_type=F32)
         + jnp.dot(hi, wsu_ref[half:, :], preferred_element_type=F32))
    hmid = (a * jax.nn.sigmoid(a) * u).astype(BF16)
    shared = jnp.dot(hmid, wsd_ref[...], preferred_element_type=F32)

    for k in range(TOP_K):
        pltpu.make_async_copy(y_ref.at[pl.ds(0, tt)], ybuf.at[slot, k], sem.at[slot]).wait()

    wk = wk_ref[...]
    yflat = _flat_tiles(ybuf)

    for p in range(2):
        @pl.when(slot == p)
        def _(p=p):
            r_lo = jnp.zeros((tt, half), F32)
            r_hi = jnp.zeros((tt, half), F32)
            for k in range(TOP_K):
                ylo, yhi = _unpack_pair(_load_tiles_as_rows(yflat, tt, base=(p * TOP_K + k) * tt))
                wcol = wk[:, k:k + 1]
                r_lo = r_lo + wcol * ylo
                r_hi = r_hi + wcol * yhi
            total = shared + jnp.concatenate([r_lo, r_hi], axis=1)
            x2 = x1_ref[...] + gf_ref[0] * total
            ms = jnp.mean(x2 * x2, axis=-1, keepdims=True)
            o_ref[...] = x2 * lax.rsqrt(ms + NORM_EPS) * g_ref[...]


def _final(h2p, wsg, wsu, wsd, x1, gate_f, norm_out, dest, wk, y, T):
    N = h2p.shape[0]
    tile = h2p.shape[1:]
    D = x1.shape[-1]
    F = wsg.shape[1]
    tt = TOK_TILE
    per_b = T // tt
    n_tiles = N // tt
    dest3 = dest.reshape(n_tiles, 1, tt * TOP_K)
    rowb = lambda w: pl.BlockSpec((tt, w), lambda i: (i, 0))
    const = lambda shp: pl.BlockSpec(shp, lambda i: (0,) * len(shp))
    dspec = lambda f: pl.BlockSpec((1, 1, tt * TOP_K), f, memory_space=pltpu.SMEM)
    return pl.pallas_call(
        functools.partial(_final_kernel, tt=tt, n_tiles=n_tiles),
        grid=(n_tiles,),
        in_specs=[dspec(lambda i: (i, 0, 0)),
                  dspec(lambda i: (jnp.minimum(i + 1, n_tiles - 1), 0, 0)),
                  pl.BlockSpec((tt,) + tile, lambda i: (i, 0, 0)),
                  const((D, F)), const((D, F)), const((F, D)), rowb(D),
                  pl.BlockSpec((1, 1, D), lambda i: (i // per_b, 0, 0)),
                  const((1, D)),
                  pl.BlockSpec((tt, TOP_K), lambda i: (i, 0)),
                  pl.BlockSpec(memory_space=pl.ANY)],
        out_specs=rowb(D),
        out_shape=jax.ShapeDtypeStruct((N, D), F32),
        scratch_shapes=[pltpu.VMEM((2, TOP_K, tt) + tile, U32), pltpu.SemaphoreType.DMA((2,))],
        compiler_params=_cparams(("arbitrary",)),
        name="moe_combine_final",
    )(dest3, dest3, h2p, wsg, wsu, wsd, x1, gate_f[:, None, :], norm_out[None, :], wk, y)


def kernel(x, c, positions, w_ada, b_ada, norm_mix, norm_ffn, norm_out, w_in, w_out, w_router, router_bias,
           w_gate, w_up, w_down, w_sh_gate, w_sh_up, w_sh_down):
    B, T, D = x.shape
    depth = w_ada.shape[0]
    assert depth == 1, "the final rmsnorm is fused into the layer's last kernel"
    moba_w = MOBA_HEADS * MOBA_HEAD_DIM
    ret_w = RET_HEADS * RET_HEAD_DIM
    N = B * T
    E, M = N_EXPERTS, ROW_TILE
    R = N * TOP_K + E * M
    n_tiles = R // M
    tabs = _rope_tables(positions)

    for l in range(depth):
        mod = _ada(c, w_ada[l], b_ada[l])
        shift_a, scale_a, gate_a, shift_f, scale_f, gate_f = jnp.split(mod, 6, axis=-1)

        proj = _in_proj(x, norm_mix[l], shift_a, scale_a, w_in[l].astype(BF16), tabs, moba_w, ret_w)
        o_a = _moba(proj, B, T, moba_w)
        o_r = _retention(proj, B, T, moba_w, ret_w)
        x1, h2p, logits = _out_proj(o_a, o_r, w_out[l].astype(BF16), x, gate_a, norm_ffn[l],
                                    shift_f, scale_f, w_router[l])

        assert D // 2 == SUBLANES * LANES, "a packed row must fill exactly one (SUBLANES, LANES) tile"
        h2p = h2p.reshape(N, SUBLANES, LANES)
        selr, wf, rank, counts = _route(logits.reshape(N, E), router_bias[l])
        cnt = counts[0].astype(jnp.int32)
        pcnt = (cnt + M - 1) // M * M
        pend = jnp.cumsum(pcnt)
        pstart = pend - pcnt
        tidx = jnp.arange(n_tiles, dtype=jnp.int32)
        end_tile = pend // M
        eids = jnp.arange(E, dtype=jnp.int32)
        owner = lambda v: jnp.minimum(jnp.sum(end_tile[None, :] <= v[:, None], axis=1), E - 1).astype(jnp.int32)
        end_of = lambda e: jnp.sum(jnp.where(e[:, None] == eids[None, :], end_tile[None, :], 0), axis=1)
        tile_expert = owner(tidx)
        n_used = end_tile[-1:].astype(jnp.int32)
        first = (((tidx == 0) | (tile_expert != jnp.roll(tile_expert, 1))) & (tidx < n_used[0])).astype(jnp.int32)
        slot = ((jnp.cumsum(first) - 1) % 2).astype(jnp.int32)
        end1 = end_of(tile_expert)
        exp1 = owner(end1)
        end2 = end_of(exp1)
        exp2 = owner(end2)
        has1 = end1 < n_used[0]
        nxt = jnp.stack([jnp.where(has1, exp1, -1),
                         jnp.where(has1 & (end2 < n_used[0]), exp2, -1)]).astype(jnp.int32)
        dest, wk = _dest(selr, wf, rank, pstart.astype(F32)[None, :])

        xs = _dispatch(h2p, dest, (pstart + cnt).astype(jnp.int32), (pcnt - cnt).astype(jnp.int32), R)
        y = _experts(xs, tile_expert, n_used, first, slot, nxt, w_gate[l], w_up[l], w_down[l])
        out = _final(h2p, w_sh_gate[l].astype(BF16), w_sh_up[l].astype(BF16), w_sh_down[l].astype(BF16),
                     x1.reshape(N, D), gate_f, norm_out, dest, wk, y, T)
        x = out.reshape(B, T, D)
    return x
```

```python
import functools

import jax
import jax.numpy as jnp
from jax import lax
from jax.experimental import pallas as pl
from jax.experimental.pallas import tpu as pltpu

MOBA_HEADS = 8
MOBA_HEAD_DIM = 128
MOBA_BLOCK = 256
MOBA_TOPK = 3
ROPE_THETA = 500000.0
ROPE_DIMS = 32
RET_HEADS = 4
RET_HEAD_DIM = 256
RET_ROPE_BASE = 10000.0
N_EXPERTS = 64
TOP_K = 8
N_GROUPS = 8
TOPK_GROUPS = 4
ROUTE_SCALE = 2.5
NORM_EPS = 1e-6
NEG = -1e30

LANES = 128
SUBLANES = 8
VMEM_LIMIT = 56 * 1024 * 1024

RET_CHUNK = 256
ROW_TILE = 256
TOK_TILE = 256

F32 = jnp.float32
BF16 = jnp.bfloat16
U32 = jnp.uint32


def _cparams(sem):
    return pltpu.CompilerParams(dimension_semantics=sem, vmem_limit_bytes=VMEM_LIMIT)


def _rms_mod(xf, g, shift, scale):
    ms = jnp.mean(xf * xf, axis=-1, keepdims=True)
    y = xf * lax.rsqrt(ms + NORM_EPS) * g
    return y * (1.0 + scale) + shift


def _pack_pair(lo, hi):
    lo_b = lax.bitcast_convert_type(lo.astype(BF16).astype(F32), U32)
    hi_b = lax.bitcast_convert_type(hi.astype(BF16).astype(F32), U32)
    return (lo_b >> 16) | hi_b


def _unpack_pair(p):
    lo = lax.bitcast_convert_type(p << 16, F32)
    hi = lax.bitcast_convert_type(p & jnp.uint32(0xFFFF0000), F32)
    return lo, hi


def _store_rows_as_tiles(ref, val, base=0):
    n = val.shape[0]
    for c in range(SUBLANES):
        ref[pl.ds(base * SUBLANES + c, n, stride=SUBLANES), :] = val[:, c * LANES:(c + 1) * LANES]


def _load_tiles_as_rows(ref, n, base=0):
    return jnp.concatenate([ref[pl.ds(base * SUBLANES + c, n, stride=SUBLANES), :] for c in range(SUBLANES)],
                           axis=1)


def _flat_tiles(ref):
    rows = 1
    for d in ref.shape[:-2]:
        rows *= d
    return ref.reshape(rows * SUBLANES, LANES)


def _tables_kernel(pos_ref, invm_ref, invr_ref, mc_ref, ms1_ref, ms2_ref, rc_ref, rs_ref):
    pos = pos_ref[0].astype(F32)
    angm = pos * invm_ref[...]
    lane = lax.broadcasted_iota(jnp.int32, angm.shape, 1)
    half = ROPE_DIMS // 2
    c = jnp.cos(angm)
    s = jnp.sin(angm)
    mc_ref[0] = c
    ms1_ref[0] = jnp.where(lane < half, -s, 0.0)
    ms2_ref[0] = jnp.where((lane >= half) & (lane < ROPE_DIMS), s, 0.0)
    angr = pos * invr_ref[...]
    rc_ref[0] = jnp.cos(angr)
    rs_ref[0] = jnp.sin(angr)


def _rope_tables(positions):
    B, T = positions.shape
    tm = 512
    half = ROPE_DIMS // 2
    moba_inv = ROPE_THETA ** (-(jnp.arange(half, dtype=F32) * 2.0 / ROPE_DIMS))
    invm = jnp.concatenate([moba_inv, moba_inv, jnp.zeros((LANES - ROPE_DIMS,), F32)])[None, :]
    invr = (RET_ROPE_BASE ** (-jnp.linspace(0.0, 1.0, RET_HEAD_DIM // 2, dtype=F32)))[None, :]
    pos3 = positions.reshape(B, T, 1)
    tab = jax.ShapeDtypeStruct((B, T, LANES), F32)
    spec = pl.BlockSpec((1, tm, LANES), lambda b, i: (b, i, 0))
    return pl.pallas_call(
        _tables_kernel,
        grid=(B, T // tm),
        in_specs=[pl.BlockSpec((1, tm, 1), lambda b, i: (b, i, 0)),
                  pl.BlockSpec((1, LANES), lambda b, i: (0, 0)),
                  pl.BlockSpec((1, LANES), lambda b, i: (0, 0))],
        out_specs=[spec] * 5,
        out_shape=[tab] * 5,
        compiler_params=_cparams(("parallel", "parallel")),
        name="rope_tables",
    )(pos3, invm, invr)


def _ada_kernel(ct_ref, w_ref, b_ref, o_ref):
    ct = ct_ref[...]
    sct = ct * jax.nn.sigmoid(ct)
    w = w_ref[...]
    for b in range(ct.shape[1]):
        o_ref[b:b + 1, :] = jnp.sum(w * sct[:, b:b + 1], axis=0, keepdims=True) + b_ref[...]


def _ada(c, w_ada, b_ada):
    B, D = c.shape
    n_out = w_ada.shape[1]
    tn = 1024
    return pl.pallas_call(
        _ada_kernel,
        grid=(n_out // tn,),
        in_specs=[pl.BlockSpec((D, B), lambda j: (0, 0)),
                  pl.BlockSpec((D, tn), lambda j: (0, j)),
                  pl.BlockSpec((1, tn), lambda j: (0, j))],
        out_specs=pl.BlockSpec((B, tn), lambda j: (0, j)),
        out_shape=jax.ShapeDtypeStruct((B, n_out), F32),
        compiler_params=_cparams(("parallel",)),
        name="adaln_mod",
    )(c.T, w_ada, b_ada[None, :])


def _inproj_kernel(x_ref, g_ref, sh_ref, sc_ref, w_ref, mc_ref, ms1_ref, ms2_ref, rc_ref, rs_ref,
                   o_ref, hn_ref, *, tn, moba_tiles, ret_lo, ret_k_lo, ret_hi):
    j = pl.program_id(2)

    @pl.when(j == 0)
    def _():
        h = _rms_mod(x_ref[0], g_ref[...], sh_ref[0], sc_ref[0])
        hn_ref[...] = h.astype(BF16)

    acc = jnp.dot(hn_ref[...], w_ref[...], preferred_element_type=F32)

    @pl.when(j < moba_tiles)
    def _():
        c, s1, s2 = mc_ref[0], ms1_ref[0], ms2_ref[0]
        half = ROPE_DIMS // 2
        for g in range(tn // LANES):
            a = acc[:, g * LANES:(g + 1) * LANES]
            r = a * c + pltpu.roll(a, LANES - half, 1) * s1 + pltpu.roll(a, half, 1) * s2
            o_ref[0, :, g * LANES:(g + 1) * LANES] = r.astype(o_ref.dtype)

    @pl.when((j >= ret_lo) & (j < ret_hi))
    def _():
        c, s = rc_ref[0], rs_ref[0]
        fac = jnp.where(j >= ret_k_lo, RET_HEAD_DIM ** -0.5, 1.0).astype(F32)
        hw = RET_HEAD_DIM // 2
        for g in range(tn // RET_HEAD_DIM):
            x1 = acc[:, g * RET_HEAD_DIM:g * RET_HEAD_DIM + hw]
            x2 = acc[:, g * RET_HEAD_DIM + hw:(g + 1) * RET_HEAD_DIM]
            o_ref[0, :, g * RET_HEAD_DIM:g * RET_HEAD_DIM + hw] = ((x1 * c - x2 * s) * fac).astype(o_ref.dtype)
            o_ref[0, :, g * RET_HEAD_DIM + hw:(g + 1) * RET_HEAD_DIM] = ((x2 * c + x1 * s) * fac).astype(o_ref.dtype)

    @pl.when(((j >= moba_tiles) & (j < ret_lo)) | (j >= ret_hi))
    def _():
        o_ref[0] = acc.astype(o_ref.dtype)


def _in_proj(x, g, shift, scale, w_bf, tabs, moba_w, ret_w):
    B, T, D = x.shape
    NC = w_bf.shape[1]
    tm, tn = 1024, 1024
    mc, ms1, ms2, rc, rs = tabs
    kern = functools.partial(
        _inproj_kernel, tn=tn,
        moba_tiles=2 * moba_w // tn,
        ret_lo=3 * moba_w // tn,
        ret_k_lo=(3 * moba_w + ret_w) // tn,
        ret_hi=(3 * moba_w + 2 * ret_w) // tn)
    tab_spec = pl.BlockSpec((1, tm, LANES), lambda b, i, j: (b, i, 0))
    vec_spec = pl.BlockSpec((1, 1, D), lambda b, i, j: (b, 0, 0))
    return pl.pallas_call(
        kern,
        grid=(B, T // tm, NC // tn),
        in_specs=[pl.BlockSpec((1, tm, D), lambda b, i, j: (b, i, 0)),
                  pl.BlockSpec((1, D), lambda b, i, j: (0, 0)),
                  vec_spec, vec_spec,
                  pl.BlockSpec((D, tn), lambda b, i, j: (0, j)),
                  tab_spec, tab_spec, tab_spec, tab_spec, tab_spec],
        out_specs=pl.BlockSpec((1, tm, tn), lambda b, i, j: (b, i, j)),
        out_shape=jax.ShapeDtypeStruct((B, T, NC), BF16),
        scratch_shapes=[pltpu.VMEM((tm, D), BF16)],
        compiler_params=_cparams(("parallel", "parallel", "arbitrary")),
        name="in_proj",
    )(x, g[None, :], shift[:, None, :], scale[:, None, :], w_bf, mc, ms1, ms2, rc, rs)


def _moba_kernel(q_ref, k_ref, v_ref, o_ref, km_ref, vt_ref, sel_ref, qs_ref, s_ref, m_ref, l_ref, acc_ref,
                 *, nb, hp):
    qb = pl.program_id(2)
    BS, hd = MOBA_BLOCK, MOBA_HEAD_DIM

    @pl.when(qb == 0)
    def _():
        for h in range(hp):
            hs = slice(h * hd, (h + 1) * hd)
            for n in range(nb):
                kb = k_ref[0, n * BS:(n + 1) * BS, hs].astype(F32)
                km_ref[h, n:n + 1, :] = jnp.sum(kb, axis=0, keepdims=True) * (1.0 / BS)
                vt_ref[h, n] = v_ref[0, n * BS:(n + 1) * BS, hs].astype(F32).T.astype(BF16)

    scale = hd ** -0.5
    own = pl.multiple_of(qb * BS, BS)
    blk = lax.broadcasted_iota(jnp.int32, (nb, BS), 0)
    blk_f = blk.astype(F32)

    for h in range(hp):
        hs = slice(h * hd, (h + 1) * hd)
        qT = q_ref[0, :, hs].astype(F32).T
        gate = jnp.dot(km_ref[h], qT, preferred_element_type=F32,
                       precision=lax.Precision.HIGHEST)
        g = jnp.where(blk < qb, gate, NEG)
        sel = jnp.zeros((nb, BS), F32)
        for _ in range(MOBA_TOPK):
            m = jnp.max(g, axis=0, keepdims=True)
            idx = jnp.min(jnp.where(g == m, blk_f, float(nb)), axis=0, keepdims=True)
            pick = blk_f == idx
            sel = jnp.where(pick & (m > 0.5 * NEG), 1.0, sel)
            g = jnp.where(pick, -jnp.inf, g)
        sel_ref[h] = sel
        qs_ref[h] = (qT * scale).astype(BF16)
        m_ref[h] = jnp.full((1, BS), NEG, F32)

    n_pairs = lax.shift_right_logical(qb + 1, 1)

    def sweep_scores(j, _):
        for h in range(hp):
            mh = m_ref[h]
            for u in range(2):
                n = 2 * j + u
                off = pl.multiple_of(n * BS, BS)
                s = jnp.dot(k_ref[0, pl.ds(off, BS), h * hd:(h + 1) * hd], qs_ref[h],
                            preferred_element_type=F32)
                s = jnp.where(sel_ref[h, pl.ds(n, 1), :] > 0.0, s, NEG)
                s_ref[h, n] = s
                mh = jnp.maximum(mh, jnp.max(s, axis=0, keepdims=True))
            m_ref[h] = mh
        return 0

    lax.fori_loop(0, n_pairs, sweep_scores, 0)

    krow = lax.broadcasted_iota(jnp.int32, (BS, BS), 0)
    qcol = lax.broadcasted_iota(jnp.int32, (BS, BS), 1)
    for h in range(hp):
        hs = slice(h * hd, (h + 1) * hd)
        s = jnp.dot(k_ref[0, pl.ds(own, BS), hs], qs_ref[h], preferred_element_type=F32)
        s = jnp.where(krow <= qcol, s, NEG)
        m = jnp.maximum(m_ref[h], jnp.max(s, axis=0, keepdims=True))
        m_ref[h] = m
        p = jnp.exp(s - m)
        l_ref[h] = jnp.sum(p, axis=0, keepdims=True)
        acc_ref[h] = jnp.dot(vt_ref[h, qb], p.astype(BF16), preferred_element_type=F32)

    def sweep_values(j, _):
        for h in range(hp):
            mh = m_ref[h]
            p0 = jnp.exp(s_ref[h, 2 * j] - mh)
            p1 = jnp.exp(s_ref[h, 2 * j + 1] - mh)
            l_ref[h] = l_ref[h] + (jnp.sum(p0, axis=0, keepdims=True) + jnp.sum(p1, axis=0, keepdims=True))
            acc_ref[h] = acc_ref[h] + (
                jnp.dot(vt_ref[h, 2 * j], p0.astype(BF16), preferred_element_type=F32)
                + jnp.dot(vt_ref[h, 2 * j + 1], p1.astype(BF16), preferred_element_type=F32))
        return 0

    lax.fori_loop(0, n_pairs, sweep_values, 0)

    for h in range(hp):
        o_ref[0, :, h * hd:(h + 1) * hd] = (acc_ref[h] / l_ref[h]).T.astype(o_ref.dtype)


def _moba_kernel_v2(q_ref, k_ref, v_ref, o_ref, km_ref, vt_ref, sel_ref, *, nb, hp):
    qb = pl.program_id(2)
    BS, hd = MOBA_BLOCK, MOBA_HEAD_DIM

    @pl.when(qb == 0)
    def _():
        for h in range(hp):
            hs = slice(h * hd, (h + 1) * hd)
            for n in range(nb):
                kb = k_ref[0, n * BS:(n + 1) * BS, hs].astype(F32)
                km_ref[h, n:n + 1, :] = jnp.sum(kb, axis=0, keepdims=True) * (1.0 / BS)
                vt_ref[h, n] = v_ref[0, n * BS:(n + 1) * BS, hs].astype(F32).T.astype(BF16)

    scale = hd ** -0.5
    own = pl.multiple_of(qb * BS, BS)
    blk = lax.broadcasted_iota(jnp.int32, (nb, BS), 0)
    blk_f = blk.astype(F32)
    krow = lax.broadcasted_iota(jnp.int32, (BS, BS), 0)
    qcol = lax.broadcasted_iota(jnp.int32, (BS, BS), 1)

    qs_all, init = [], []
    for h in range(hp):
        hs = slice(h * hd, (h + 1) * hd)
        qT = q_ref[0, :, hs].astype(F32).T
        gate = jnp.dot(km_ref[h], qT, preferred_element_type=F32,
                       precision=lax.Precision.HIGHEST)
        g = jnp.where(blk < qb, gate, NEG)
        sel = jnp.zeros((nb, BS), F32)
        for _ in range(MOBA_TOPK):
            m = jnp.max(g, axis=0, keepdims=True)
            idx = jnp.min(jnp.where(g == m, blk_f, float(nb)), axis=0, keepdims=True)
            pick = blk_f == idx
            sel = jnp.where(pick & (m > 0.5 * NEG), 1.0, sel)
            g = jnp.where(pick, -jnp.inf, g)
        sel_ref[h] = sel

        qs = (qT * scale).astype(BF16)
        s = jnp.dot(k_ref[0, pl.ds(own, BS), hs], qs, preferred_element_type=F32)
        s = jnp.where(krow <= qcol, s, NEG)
        m0 = jnp.max(s, axis=0, keepdims=True)
        p = jnp.exp(s - m0)
        l0 = jnp.sum(p, axis=0, keepdims=True)
        acc0 = jnp.dot(vt_ref[h, qb], p.astype(BF16), preferred_element_type=F32)
        qs_all.append(qs)
        init.append((m0, l0, acc0))

    def body(n, carry):
        off = pl.multiple_of(n * BS, BS)
        out = []
        for h in range(hp):
            m, l, acc = carry[h]
            selrow = sel_ref[h, pl.ds(n, 1), :]
            s = jnp.dot(k_ref[0, pl.ds(off, BS), h * hd:(h + 1) * hd], qs_all[h],
                        preferred_element_type=F32)
            s = jnp.where(selrow > 0.0, s, NEG)
            m_new = jnp.maximum(m, jnp.max(s, axis=0, keepdims=True))
            alpha = jnp.exp(m - m_new)
            p = jnp.exp(s - m_new)
            l = alpha * l + jnp.sum(p, axis=0, keepdims=True)
            acc = alpha * acc + jnp.dot(vt_ref[h, n], p.astype(BF16), preferred_element_type=F32)
            out.append((m_new, l, acc))
        return tuple(out)

    final = lax.fori_loop(0, qb, body, tuple(init))
    for h in range(hp):
        _, l, acc = final[h]
        o_ref[0, :, h * hd:(h + 1) * hd] = (acc / l).T.astype(o_ref.dtype)


def _moba(proj, B, T, moba_w):
    H, hd, BS = MOBA_HEADS, MOBA_HEAD_DIM, MOBA_BLOCK
    nb = T // BS
    hp = 4
    gw = hp * hd
    gpw = moba_w // gw
    return pl.pallas_call(
        functools.partial(_moba_kernel, nb=nb, hp=hp),
        grid=(B, H // hp, nb),
        in_specs=[pl.BlockSpec((1, BS, gw), lambda b, h, i: (b, i, h)),
                  pl.BlockSpec((1, T, gw), lambda b, h, i: (b, 0, gpw + h)),
                  pl.BlockSpec((1, T, gw), lambda b, h, i: (b, 0, 2 * gpw + h))],
        out_specs=pl.BlockSpec((1, BS, gw), lambda b, h, i: (b, i, h)),
        out_shape=jax.ShapeDtypeStruct((B, T, moba_w), BF16),
        scratch_shapes=[pltpu.VMEM((hp, nb, hd), F32),
                        pltpu.VMEM((hp, nb, hd, BS), BF16),
                        pltpu.VMEM((hp, nb, BS), F32),
                        pltpu.VMEM((hp, hd, BS), BF16),
                        pltpu.VMEM((hp, nb, BS, BS), F32),
                        pltpu.VMEM((hp, 1, BS), F32),
                        pltpu.VMEM((hp, 1, BS), F32),
                        pltpu.VMEM((hp, hd, BS), F32)],
        compiler_params=_cparams(("parallel", "parallel", "arbitrary")),
        name="moba_attn",
    )(proj, proj, proj)


def _ret_kernel(q_ref, k_ref, v_ref, g_ref, dm_ref, xi_ref, zeta_ref, cd_ref, o_ref, s_ref):
    c = pl.program_id(1)
    d = RET_HEAD_DIM

    @pl.when(c == 0)
    def _():
        s_ref[...] = jnp.zeros_like(s_ref)

    for h in range(RET_HEADS):
        sl = slice(h * d, (h + 1) * d)
        q = q_ref[0, :, sl]
        k = k_ref[0, :, sl]
        v = v_ref[0, :, sl]
        inner = lax.dot_general(q, k, (((1,), (1,)), ((), ())), preferred_element_type=F32) * dm_ref[h]
        S = s_ref[h]
        o = (jnp.dot(inner.astype(BF16), v, preferred_element_type=F32)
             + jnp.dot(q, S.astype(BF16), preferred_element_type=F32) * xi_ref[h])
        kz = (k.astype(F32) * zeta_ref[h]).astype(BF16)
        s_ref[h] = S * cd_ref[h] + lax.dot_general(kz, v, (((0,), (0,)), ((), ())),
                                                   preferred_element_type=F32)
        mu = jnp.mean(o, axis=-1, keepdims=True)
        dlt = o - mu
        var = jnp.mean(dlt * dlt, axis=-1, keepdims=True)
        on = dlt * lax.rsqrt(var + NORM_EPS)
        gg = g_ref[0, :, sl].astype(F32)
        o_ref[0, :, sl] = (on * (gg * jax.nn.sigmoid(gg))).astype(o_ref.dtype)


def _retention(proj, B, T, moba_w, ret_w):
    C, H = RET_CHUNK, RET_HEADS
    gamma = 1.0 - jnp.exp2(-5.0 - jnp.arange(H, dtype=F32))
    log_g = jnp.log(gamma)
    pos = jnp.arange(C, dtype=F32)
    diff = pos[:, None] - pos[None, :]
    dmask = jnp.where(diff >= 0, jnp.exp(jnp.maximum(diff, 0.0) * log_g[:, None, None]), 0.0)
    xi = jnp.exp((pos + 1.0) * log_g[:, None])[:, :, None]
    zeta = jnp.exp((C - 1.0 - pos) * log_g[:, None])[:, :, None]
    cd = jnp.exp(C * log_g)[:, None, None]
    base = 3 * moba_w // ret_w
    col = lambda off: pl.BlockSpec((1, C, ret_w), lambda b, c: (b, c, base + off))
    full = lambda shp: pl.BlockSpec(shp, lambda b, c: (0,) * len(shp))
    return pl.pallas_call(
        _ret_kernel,
        grid=(B, T // C),
        in_specs=[col(0), col(1), col(2), col(3),
                  full((H, C, C)), full((H, C, 1)), full((H, C, 1)), full((H, 1, 1))],
        out_specs=pl.BlockSpec((1, C, ret_w), lambda b, c: (b, c, 0)),
        out_shape=jax.ShapeDtypeStruct((B, T, ret_w), BF16),
        scratch_shapes=[pltpu.VMEM((H, RET_HEAD_DIM, RET_HEAD_DIM), F32)],
        compiler_params=_cparams(("parallel", "arbitrary")),
        name="retention",
    )(proj, proj, proj, proj, dmask, xi, zeta, cd)


def _outproj_kernel(oa_ref, or_ref, w_ref, x_ref, ga_ref, g_ref, sh_ref, sc_ref, wrh_ref, wrl_ref,
                    x1_ref, hp_ref, lg_ref, *, moba_w):
    mix = (jnp.dot(oa_ref[0], w_ref[:moba_w, :], preferred_element_type=F32)
           + jnp.dot(or_ref[0], w_ref[moba_w:, :], preferred_element_type=F32))
    x1 = x_ref[0] + ga_ref[0] * mix
    x1_ref[0] = x1
    h = _rms_mod(x1, g_ref[...], sh_ref[0], sc_ref[0])
    half = h.shape[-1] // 2
    _store_rows_as_tiles(_flat_tiles(hp_ref), _pack_pair(h[:, :half], h[:, half:]))
    h_hi = h.astype(BF16)
    h_lo = (h - h_hi.astype(F32)).astype(BF16)
    lg_ref[0] = (jnp.dot(h_hi, wrh_ref[...], preferred_element_type=F32)
                 + (jnp.dot(h_lo, wrh_ref[...], preferred_element_type=F32)
                    + jnp.dot(h_hi, wrl_ref[...], preferred_element_type=F32)))


def _out_proj(o_a, o_r, w_bf, x, gate_a, g, shift, scale, w_router):
    B, T, D = x.shape
    moba_w, ret_w = o_a.shape[-1], o_r.shape[-1]
    E = w_router.shape[1]
    tm = 256
    wr_hi = w_router.astype(BF16)
    wr_lo = (w_router - wr_hi.astype(F32)).astype(BF16)
    vec = pl.BlockSpec((1, 1, D), lambda b, i: (b, 0, 0))
    row = lambda w: pl.BlockSpec((1, tm, w), lambda b, i: (b, i, 0))
    return pl.pallas_call(
        functools.partial(_outproj_kernel, moba_w=moba_w),
        grid=(B, T // tm),
        in_specs=[row(moba_w), row(ret_w),
                  pl.BlockSpec((moba_w + ret_w, D), lambda b, i: (0, 0)),
                  row(D), vec,
                  pl.BlockSpec((1, D), lambda b, i: (0, 0)),
                  vec, vec,
                  pl.BlockSpec((D, E), lambda b, i: (0, 0)),
                  pl.BlockSpec((D, E), lambda b, i: (0, 0))],
        out_specs=[row(D), pl.BlockSpec((1, tm, SUBLANES, LANES), lambda b, i: (b, i, 0, 0)), row(E)],
        out_shape=[jax.ShapeDtypeStruct((B, T, D), F32),
                   jax.ShapeDtypeStruct((B, T, SUBLANES, LANES), U32),
                   jax.ShapeDtypeStruct((B, T, E), F32)],
        compiler_params=_cparams(("parallel", "parallel")),
        name="out_proj",
    )(o_a, o_r, w_bf, x, gate_a[:, None, :], g[None, :], shift[:, None, :], scale[:, None, :], wr_hi, wr_lo)


def _route_kernel_t(lg_ref, b_ref, selr_ref, wf_ref, rank_ref, cnt_ref, carry_ref):
    @pl.when((pl.program_id(0) == 0) & (pl.program_id(1) == 0))
    def _():
        carry_ref[...] = jnp.zeros_like(carry_ref)

    E = N_EXPERTS
    gsz = E // N_GROUPS
    assert gsz == SUBLANES and N_GROUPS == SUBLANES, "a routing group is one sublane tile of experts"
    s = jax.nn.sigmoid(lg_ref[0].T)
    biased = s + b_ref[...]
    tm = s.shape[1]
    sub = lax.broadcasted_iota(jnp.int32, (SUBLANES, tm), 0).astype(F32)
    eid = lax.broadcasted_iota(jnp.int32, (E, tm), 0).astype(F32)

    def first_argmax(v, ids, width):
        m = jnp.max(v, axis=0, keepdims=True)
        idx = jnp.min(jnp.where(v == m, ids, float(width)), axis=0, keepdims=True)
        return m, idx

    gscore = jnp.zeros((N_GROUPS, tm), F32)
    for gi in range(N_GROUPS):
        v = biased[gi * gsz:(gi + 1) * gsz, :]
        m1, i1 = first_argmax(v, sub, gsz)
        m2 = jnp.max(jnp.where(sub == i1, -jnp.inf, v), axis=0, keepdims=True)
        gscore = jnp.where(sub == float(gi), m1 + m2, gscore)

    gsel = jnp.zeros((N_GROUPS, tm), F32)
    for _ in range(TOPK_GROUPS):
        _, gi = first_argmax(gscore, sub, N_GROUPS)
        pick = sub == gi
        gsel = jnp.where(pick, 1.0, gsel)
        gscore = jnp.where(pick, -jnp.inf, gscore)

    cand = jnp.concatenate(
        [jnp.where(gsel[gi:gi + 1, :] > 0.0, biased[gi * gsz:(gi + 1) * gsz, :], NEG) for gi in range(N_GROUPS)],
        axis=0)
    selr = jnp.zeros((E, tm), F32)
    for r in range(TOP_K):
        _, ei = first_argmax(cand, eid, E)
        pick = eid == ei
        selr = jnp.where(pick, float(r + 1), selr)
        cand = jnp.where(pick, -jnp.inf, cand)

    chosen = selr > 0.0
    w = jnp.where(chosen, s, 0.0)
    wsum = jnp.sum(w, axis=0, keepdims=True)
    selr_ref[0] = selr
    wf_ref[0] = w / wsum * ROUTE_SCALE

    onehot = chosen.astype(BF16)
    c_i = lax.broadcasted_iota(jnp.int32, (tm, tm), 0)
    r_i = lax.broadcasted_iota(jnp.int32, (tm, tm), 1)
    tri = (c_i < r_i).astype(BF16)
    carry = carry_ref[...]
    rank_ref[0] = jnp.dot(onehot, tri, preferred_element_type=F32) + carry
    carry = carry + jnp.sum(chosen.astype(F32), axis=1, keepdims=True)
    carry_ref[...] = carry
    cnt_ref[...] = carry


def _route_t(logits_t, bias):
    B, T, E = logits_t.shape
    tm = 512
    blk = pl.BlockSpec((1, E, tm), lambda b, i: (b, 0, i))
    col = pl.BlockSpec((E, 1), lambda b, i: (0, 0))
    full = jax.ShapeDtypeStruct((B, E, T), F32)
    return pl.pallas_call(
        _route_kernel_t,
        grid=(B, T // tm),
        in_specs=[pl.BlockSpec((1, tm, E), lambda b, i: (b, i, 0)), col],
        out_specs=[blk, blk, blk, col],
        out_shape=[full, full, full, jax.ShapeDtypeStruct((E, 1), F32)],
        scratch_shapes=[pltpu.VMEM((E, 1), F32)],
        compiler_params=_cparams(("arbitrary", "arbitrary")),
        name="route_topk",
    )(logits_t, bias[:, None])


def _dest_kernel_t(selr_ref, wf_ref, rank_ref, ps_ref, dest_ref, wk_ref):
    selr = selr_ref[0]
    destfull = rank_ref[0] + ps_ref[...]
    wf = wf_ref[0]
    for r in range(TOP_K):
        hit = selr == float(r + 1)
        dest_ref[0, r:r + 1, :] = jnp.sum(jnp.where(hit, destfull, 0.0), axis=0, keepdims=True).astype(jnp.int32)
        wk_ref[0, r:r + 1, :] = jnp.sum(jnp.where(hit, wf, 0.0), axis=0, keepdims=True)


def _dest_t(selr, wf, rank, pstart_f):
    B, E, T = selr.shape
    tm = 512
    blk = pl.BlockSpec((1, E, tm), lambda b, i: (b, 0, i))
    outb = pl.BlockSpec((1, TOP_K, tm), lambda b, i: (b, 0, i))
    return pl.pallas_call(
        _dest_kernel_t,
        grid=(B, T // tm),
        in_specs=[blk, blk, blk, pl.BlockSpec((E, 1), lambda b, i: (0, 0))],
        out_specs=[outb, outb],
        out_shape=[jax.ShapeDtypeStruct((B, TOP_K, T), jnp.int32), jax.ShapeDtypeStruct((B, TOP_K, T), F32)],
        compiler_params=_cparams(("parallel", "parallel")),
        name="route_dest",
    )(selr, wf, rank, pstart_f)


def _first_argmax(v, lane_f, width):
    m = jnp.max(v, axis=1, keepdims=True)
    idx = jnp.min(jnp.where(v == m, lane_f, float(width)), axis=1, keepdims=True)
    return m, idx


def _route_kernel(lg_ref, b_ref, selr_ref, wf_ref, rank_ref, cnt_ref, carry_ref):
    i = pl.program_id(0)

    @pl.when(i == 0)
    def _():
        carry_ref[...] = jnp.zeros_like(carry_ref)

    E = N_EXPERTS
    gsz = E // N_GROUPS
    s = jax.nn.sigmoid(lg_ref[...])
    biased = s + b_ref[...]
    tm = s.shape[0]
    lane = lax.broadcasted_iota(jnp.int32, (tm, E), 1)
    lane_f = lane.astype(F32)
    grp = lax.shift_right_logical(lane, gsz.bit_length() - 1)

    gscore = jnp.full((tm, E), -jnp.inf, F32)
    for gi in range(N_GROUPS):
        v = jnp.where(grp == gi, biased, -jnp.inf)
        m1, i1 = _first_argmax(v, lane_f, E)
        m2 = jnp.max(jnp.where(lane_f == i1, -jnp.inf, v), axis=1, keepdims=True)
        gscore = jnp.where(lane == gi, m1 + m2, gscore)

    emask = jnp.zeros((tm, E), jnp.bool_)
    grp_f = grp.astype(F32)
    for _ in range(TOPK_GROUPS):
        _, gi = _first_argmax(gscore, lane_f, E)
        emask = emask | (grp_f == gi)
        gscore = jnp.where(lane_f == gi, -jnp.inf, gscore)

    cand = jnp.where(emask, biased, NEG)
    selr = jnp.zeros((tm, E), F32)
    for r in range(TOP_K):
        _, ei = _first_argmax(cand, lane_f, E)
        pick = lane_f == ei
        selr = jnp.where(pick, float(r + 1), selr)
        cand = jnp.where(pick, -jnp.inf, cand)

    chosen = selr > 0.0
    w = jnp.where(chosen, s, 0.0)
    wsum = jnp.sum(w, axis=1, keepdims=True)
    selr_ref[...] = selr
    wf_ref[...] = w / wsum * ROUTE_SCALE

    onehot = chosen.astype(BF16)
    r_i = lax.broadcasted_iota(jnp.int32, (tm, tm), 0)
    c_i = lax.broadcasted_iota(jnp.int32, (tm, tm), 1)
    tri = (c_i < r_i).astype(BF16)
    carry = carry_ref[...]
    rank_ref[...] = jnp.dot(tri, onehot, preferred_element_type=F32) + carry
    carry = carry + jnp.sum(chosen.astype(F32), axis=0, keepdims=True)
    carry_ref[...] = carry
    cnt_ref[...] = carry


def _route(logits, bias):
    N, E = logits.shape
    tm = 512
    blk = pl.BlockSpec((tm, E), lambda i: (i, 0))
    one = pl.BlockSpec((1, E), lambda i: (0, 0))
    full = jax.ShapeDtypeStruct((N, E), F32)
    return pl.pallas_call(
        _route_kernel,
        grid=(N // tm,),
        in_specs=[blk, one],
        out_specs=[blk, blk, blk, one],
        out_shape=[full, full, full, jax.ShapeDtypeStruct((1, E), F32)],
        scratch_shapes=[pltpu.VMEM((1, E), F32)],
        compiler_params=_cparams(("arbitrary",)),
        name="route_topk",
    )(logits, bias[None, :])


def _dest_kernel(selr_ref, wf_ref, rank_ref, ps_ref, dest_ref, wk_ref):
    selr = selr_ref[...]
    destfull = rank_ref[...] + ps_ref[...]
    wf = wf_ref[...]
    for r in range(TOP_K):
        hit = selr == float(r + 1)
        dest_ref[:, r:r + 1] = jnp.sum(jnp.where(hit, destfull, 0.0), axis=1, keepdims=True).astype(jnp.int32)
        wk_ref[:, r:r + 1] = jnp.sum(jnp.where(hit, wf, 0.0), axis=1, keepdims=True)


def _dest(selr, wf, rank, pstart_f):
    N, E = selr.shape
    tm = 512
    blk = pl.BlockSpec((tm, E), lambda i: (i, 0))
    outb = pl.BlockSpec((tm, TOP_K), lambda i: (i, 0))
    return pl.pallas_call(
        _dest_kernel,
        grid=(N // tm,),
        in_specs=[blk, blk, blk, pl.BlockSpec((1, E), lambda i: (0, 0))],
        out_specs=[outb, outb],
        out_shape=[jax.ShapeDtypeStruct((N, TOP_K), jnp.int32), jax.ShapeDtypeStruct((N, TOP_K), F32)],
        compiler_params=_cparams(("parallel",)),
        name="route_dest",
    )(selr, wf, rank, pstart_f)


def _row_copy(src, s_row, dst, d_row, n, sem):
    return pltpu.make_async_copy(src.at[pl.ds(s_row, n)], dst.at[pl.ds(d_row, n)], sem)


def _dispatch_kernel(padlo_ref, padn_ref, dest_ref, h_ref, z_ref, xs_ref, sem, zsem, *, tt, n_exp):
    i = pl.program_id(0)

    def issue(t, _):
        for k in range(TOP_K):
            _row_copy(h_ref, t, xs_ref, dest_ref[0, 0, k * tt + t], 1, sem).start(priority=k % 2)
        return 0

    lax.fori_loop(0, tt, issue, 0, unroll=2)

    def each_pad(fn):
        def per_expert(e, _):
            lo = padlo_ref[e]

            def one(r, _):
                fn(lo + r)
                return 0

            lax.fori_loop(0, padn_ref[e], one, 0)
            return 0

        lax.fori_loop(0, n_exp, per_expert, 0)

    @pl.when(i == 0)
    def _():
        each_pad(lambda r: _row_copy(z_ref, 0, xs_ref, r, 1, zsem).start())

    for k in range(TOP_K):
        _row_copy(h_ref, 0, xs_ref, 0, tt, sem).wait()

    @pl.when(i == 0)
    def _():
        each_pad(lambda r: _row_copy(z_ref, 0, xs_ref, 0, 1, zsem).wait())


def _dispatch(h2p, dest, pad_lo, pad_n, R):
    N = h2p.shape[0]
    tile = h2p.shape[1:]
    tt = TOK_TILE
    dest3 = dest
    zeros = jnp.zeros((SUBLANES,) + tile, h2p.dtype)
    grid_spec = pltpu.PrefetchScalarGridSpec(
        num_scalar_prefetch=2,
        grid=(N // tt,),
        in_specs=[pl.BlockSpec((1, 1, tt * TOP_K), lambda i, lo, n: (i, 0, 0), memory_space=pltpu.SMEM),
                  pl.BlockSpec((tt,) + tile, lambda i, lo, n: (i, 0, 0)),
                  pl.BlockSpec((SUBLANES,) + tile, lambda i, lo, n: (0, 0, 0))],
        out_specs=pl.BlockSpec(memory_space=pl.ANY),
        scratch_shapes=[pltpu.SemaphoreType.DMA(()), pltpu.SemaphoreType.DMA(())],
    )
    return pl.pallas_call(
        functools.partial(_dispatch_kernel, tt=tt, n_exp=N_EXPERTS),
        grid_spec=grid_spec,
        out_shape=jax.ShapeDtypeStruct((R,) + tile, h2p.dtype),
        compiler_params=_cparams(("arbitrary",)),
        name="moe_dispatch",
    )(pad_lo, pad_n, dest3, h2p, zeros)


def _expert_kernel(te_ref, nu_ref, first_ref, slot_ref, nxt_ref, xs_ref, wg_hbm, wu_hbm, wd_hbm, y_ref,
                   stg, stu, std, wgb, wub, wdb, sem):
    i = pl.program_id(0)
    n_chunks = 2

    def fetch(e, s):
        cps = []
        for m, (src, dst) in enumerate(((wg_hbm, stg), (wu_hbm, stu), (wd_hbm, std))):
            rows = dst.shape[1] // n_chunks
            for c in range(n_chunks):
                cps.append(pltpu.make_async_copy(src.at[e, pl.ds(c * rows, rows)],
                                                 dst.at[s, pl.ds(c * rows, rows)], sem.at[s, m]))
        return cps

    @pl.when(i == 0)
    def _():
        for cp in fetch(te_ref[0], 0):
            cp.start()

        @pl.when(nxt_ref[0, 0] >= 0)
        def _():
            for cp in fetch(nxt_ref[0, 0], 1):
                cp.start()

    active = i < nu_ref[0]

    @pl.when(active & (first_ref[i] == 1))
    def _():
        s = slot_ref[i]
        for cp in fetch(0, s):
            cp.wait()
        wgb[...] = stg[s].astype(BF16)
        wub[...] = stu[s].astype(BF16)
        wdb[...] = std[s].astype(BF16)

        @pl.when(nxt_ref[1, i] >= 0)
        def _():
            for cp in fetch(nxt_ref[1, i], s):
                cp.start()

    @pl.when(active)
    def _():
        m_rows = xs_ref.shape[0]
        lo, hi = _unpack_pair(_load_tiles_as_rows(_flat_tiles(xs_ref), m_rows))
        lo = lo.astype(BF16)
        hi = hi.astype(BF16)
        half = lo.shape[-1]
        a = (jnp.dot(lo, wgb[:half, :], preferred_element_type=F32)
             + jnp.dot(hi, wgb[half:, :], preferred_element_type=F32))
        u = (jnp.dot(lo, wub[:half, :], preferred_element_type=F32)
             + jnp.dot(hi, wub[half:, :], preferred_element_type=F32))
        hmid = (a * jax.nn.sigmoid(a) * u).astype(BF16)
        y = jnp.dot(hmid, wdb[...], preferred_element_type=F32)
        _store_rows_as_tiles(_flat_tiles(y_ref), _pack_pair(y[:, :half], y[:, half:]))


def _experts(xs, tile_expert, n_used, first, slot, nxt, wg, wu, wd):
    R = xs.shape[0]
    tile = xs.shape[1:]
    M = ROW_TILE
    _, D, F = wg.shape
    row = lambda i, te, nu, fi, sl, nx: (jnp.minimum(i, nu[0] - 1), 0, 0)
    grid_spec = pltpu.PrefetchScalarGridSpec(
        num_scalar_prefetch=5,
        grid=(R // M,),
        in_specs=[pl.BlockSpec((M,) + tile, row),
                  pl.BlockSpec(memory_space=pl.ANY),
                  pl.BlockSpec(memory_space=pl.ANY),
                  pl.BlockSpec(memory_space=pl.ANY)],
        out_specs=pl.BlockSpec((M,) + tile, row),
        scratch_shapes=[pltpu.VMEM((2, D, F), F32), pltpu.VMEM((2, D, F), F32), pltpu.VMEM((2, F, D), F32),
                        pltpu.VMEM((D, F), BF16), pltpu.VMEM((D, F), BF16), pltpu.VMEM((F, D), BF16),
                        pltpu.SemaphoreType.DMA((2, 3))],
    )
    return pl.pallas_call(
        _expert_kernel,
        grid_spec=grid_spec,
        out_shape=jax.ShapeDtypeStruct((R,) + tile, U32),
        compiler_params=_cparams(("arbitrary",)),
        name="moe_experts",
    )(tile_expert, n_used, first, slot, nxt, xs, wg, wu, wd)


def _final_kernel(dcur_ref, dnxt_ref, h_ref, wsg_ref, wsu_ref, wsd_ref, x1_ref, gf_ref, g_ref, wk_ref, y_ref,
                  o_ref, ybuf, sem, *, tt, n_tiles):
    i = pl.program_id(0)
    slot = lax.rem(i, 2)

    def gather(d_ref, s):
        def issue(t, _):
            for k in range(TOP_K):
                pltpu.make_async_copy(y_ref.at[pl.ds(d_ref[0, 0, k * tt + t], 1)],
                                      ybuf.at[s, k, pl.ds(t, 1)], sem.at[s]).start(priority=k % 2)
            return 0

        lax.fori_loop(0, tt, issue, 0, unroll=2)

    @pl.when(i == 0)
    def _():
        gather(dcur_ref, 0)

    for p in range(2):
        @pl.when((i + 1 < n_tiles) & (slot == p))
        def _(p=p):
            gather(dnxt_ref, 1 - p)

    lo, hi = _unpack_pair(_load_tiles_as_rows(_flat_tiles(h_ref), tt))
    lo = lo.astype(BF16)
    hi = hi.astype(BF16)
    half = lo.shape[-1]
    a = (jnp.dot(lo, wsg_ref[:half, :], preferred_element_type=F32)
         + jnp.dot(hi, wsg_ref[half:, :], preferred_element_type=F32))
    u = (jnp.dot(lo, wsu_ref[:half, :], preferred_element_type=F32)
         + jnp.dot(hi, wsu_ref[half:, :], preferred_element_type=F32))
    hmid = (a * jax.nn.sigmoid(a) * u).astype(BF16)
    shared = jnp.dot(hmid, wsd_ref[...], preferred_element_type=F32)

    for k in range(TOP_K):
        pltpu.make_async_copy(y_ref.at[pl.ds(0, tt)], ybuf.at[slot, k], sem.at[slot]).wait()

    wk = wk_ref[...]
    yflat = _flat_tiles(ybuf)

    for p in range(2):
        @pl.when(slot == p)
        def _(p=p):
            r_lo = jnp.zeros((tt, half), F32)
            r_hi = jnp.zeros((tt, half), F32)
            for k in range(TOP_K):
                ylo, yhi = _unpack_pair(_load_tiles_as_rows(yflat, tt, base=(p * TOP_K + k) * tt))
                wcol = wk[:, k:k + 1]
                r_lo = r_lo + wcol * ylo
                r_hi = r_hi + wcol * yhi
            total = shared + jnp.concatenate([r_lo, r_hi], axis=1)
            x2 = x1_ref[...] + gf_ref[0] * total
            ms = jnp.mean(x2 * x2, axis=-1, keepdims=True)
            o_ref[...] = x2 * lax.rsqrt(ms + NORM_EPS) * g_ref[...]


def _final(h2p, wsg, wsu, wsd, x1, gate_f, norm_out, dest, wk, y, T):
    N = h2p.shape[0]
    tile = h2p.shape[1:]
    D = x1.shape[-1]
    F = wsg.shape[1]
    tt = TOK_TILE
    per_b = T // tt
    n_tiles = N // tt
    dest3 = dest
    rowb = lambda w: pl.BlockSpec((tt, w), lambda i: (i, 0))
    const = lambda shp: pl.BlockSpec(shp, lambda i: (0,) * len(shp))
    dspec = lambda f: pl.BlockSpec((1, 1, tt * TOP_K), f, memory_space=pltpu.SMEM)
    return pl.pallas_call(
        functools.partial(_final_kernel, tt=tt, n_tiles=n_tiles),
        grid=(n_tiles,),
        in_specs=[dspec(lambda i: (i, 0, 0)),
                  dspec(lambda i: (jnp.minimum(i + 1, n_tiles - 1), 0, 0)),
                  pl.BlockSpec((tt,) + tile, lambda i: (i, 0, 0)),
                  const((D, F)), const((D, F)), const((F, D)), rowb(D),
                  pl.BlockSpec((1, 1, D), lambda i: (i // per_b, 0, 0)),
                  const((1, D)),
                  pl.BlockSpec((tt, TOP_K), lambda i: (i, 0)),
                  pl.BlockSpec(memory_space=pl.ANY)],
        out_specs=rowb(D),
        out_shape=jax.ShapeDtypeStruct((N, D), F32),
        scratch_shapes=[pltpu.VMEM((2, TOP_K, tt) + tile, U32), pltpu.SemaphoreType.DMA((2,))],
        compiler_params=_cparams(("arbitrary",)),
        name="moe_combine_final",
    )(dest3, dest3, h2p, wsg, wsu, wsd, x1, gate_f[:, None, :], norm_out[None, :], wk, y)


def kernel(x, c, positions, w_ada, b_ada, norm_mix, norm_ffn, norm_out, w_in, w_out, w_router, router_bias,
           w_gate, w_up, w_down, w_sh_gate, w_sh_up, w_sh_down):
    B, T, D = x.shape
    depth = w_ada.shape[0]
    assert depth == 1, "the final rmsnorm is fused into the layer's last kernel"
    moba_w = MOBA_HEADS * MOBA_HEAD_DIM
    ret_w = RET_HEADS * RET_HEAD_DIM
    N = B * T
    E, M = N_EXPERTS, ROW_TILE
    R = N * TOP_K + E * M
    n_tiles = R // M
    tabs = _rope_tables(positions)

    for l in range(depth):
        mod = _ada(c, w_ada[l], b_ada[l])
        shift_a, scale_a, gate_a, shift_f, scale_f, gate_f = jnp.split(mod, 6, axis=-1)

        proj = _in_proj(x, norm_mix[l], shift_a, scale_a, w_in[l].astype(BF16), tabs, moba_w, ret_w)
        o_a = _moba(proj, B, T, moba_w)
        o_r = _retention(proj, B, T, moba_w, ret_w)
        x1, h2p, logits = _out_proj(o_a, o_r, w_out[l].astype(BF16), x, gate_a, norm_ffn[l],
                                    shift_f, scale_f, w_router[l])

        assert D // 2 == SUBLANES * LANES, "a packed row must fill exactly one (SUBLANES, LANES) tile"
        h2p = h2p.reshape(N, SUBLANES, LANES)
        selr, wf, rank, counts = _route_t(logits, router_bias[l])
        cnt = counts[:, 0].astype(jnp.int32)
        pcnt = (cnt + M - 1) // M * M
        pend = jnp.cumsum(pcnt)
        pstart = pend - pcnt
        tidx = jnp.arange(n_tiles, dtype=jnp.int32)
        end_tile = pend // M
        eids = jnp.arange(E, dtype=jnp.int32)
        owner = lambda v: jnp.minimum(jnp.sum(end_tile[None, :] <= v[:, None], axis=1), E - 1).astype(jnp.int32)
        end_of = lambda e: jnp.sum(jnp.where(e[:, None] == eids[None, :], end_tile[None, :], 0), axis=1)
        tile_expert = owner(tidx)
        n_used = end_tile[-1:].astype(jnp.int32)
        first = (((tidx == 0) | (tile_expert != jnp.roll(tile_expert, 1))) & (tidx < n_used[0])).astype(jnp.int32)
        slot = ((jnp.cumsum(first) - 1) % 2).astype(jnp.int32)
        end1 = end_of(tile_expert)
        exp1 = owner(end1)
        end2 = end_of(exp1)
        exp2 = owner(end2)
        has1 = end1 < n_used[0]
        nxt = jnp.stack([jnp.where(has1, exp1, -1),
                         jnp.where(has1 & (end2 < n_used[0]), exp2, -1)]).astype(jnp.int32)
        dest, wk = _dest_t(selr, wf, rank, pstart.astype(F32)[:, None])
        tt = TOK_TILE
        dest = dest.reshape(B, TOP_K, T // tt, tt).transpose(0, 2, 1, 3).reshape(N // tt, 1, TOP_K * tt)
        wk = wk.transpose(0, 2, 1).reshape(N, TOP_K)

        xs = _dispatch(h2p, dest, (pstart + cnt).astype(jnp.int32), (pcnt - cnt).astype(jnp.int32), R)
        y = _experts(xs, tile_expert, n_used, first, slot, nxt, w_gate[l], w_up[l], w_down[l])
        out = _final(h2p, w_sh_gate[l].astype(BF16), w_sh_up[l].astype(BF16), w_sh_down[l].astype(BF16),
                     x1.reshape(N, D), gate_f, norm_out, dest, wk, y, T)
        x = out.reshape(B, T, D)
    return x
```

```python
import functools

import jax
import jax.numpy as jnp
from jax import lax
from jax.experimental import pallas as pl
from jax.experimental.pallas import tpu as pltpu

MOBA_HEADS = 8
MOBA_HEAD_DIM = 128
MOBA_BLOCK = 256
MOBA_TOPK = 3
ROPE_THETA = 500000.0
ROPE_DIMS = 32
RET_HEADS = 4
RET_HEAD_DIM = 256
RET_ROPE_BASE = 10000.0
N_EXPERTS = 64
TOP_K = 8
N_GROUPS = 8
TOPK_GROUPS = 4
ROUTE_SCALE = 2.5
NORM_EPS = 1e-6
NEG = -1e30

LANES = 128
SUBLANES = 8
VMEM_LIMIT = 56 * 1024 * 1024

RET_CHUNK = 256
ROW_TILE = 256
TOK_TILE = 256

F32 = jnp.float32
BF16 = jnp.bfloat16
U32 = jnp.uint32


def _cparams(sem):
    return pltpu.CompilerParams(dimension_semantics=sem, vmem_limit_bytes=VMEM_LIMIT)


def _rms_mod(xf, g, shift, scale):
    ms = jnp.mean(xf * xf, axis=-1, keepdims=True)
    y = xf * lax.rsqrt(ms + NORM_EPS) * g
    return y * (1.0 + scale) + shift


def _pack_pair(lo, hi):
    lo_b = lax.bitcast_convert_type(lo.astype(BF16).astype(F32), U32)
    hi_b = lax.bitcast_convert_type(hi.astype(BF16).astype(F32), U32)
    return (lo_b >> 16) | hi_b


def _unpack_pair(p):
    lo = lax.bitcast_convert_type(p << 16, F32)
    hi = lax.bitcast_convert_type(p & jnp.uint32(0xFFFF0000), F32)
    return lo, hi


def _store_rows_as_tiles(ref, val, base=0):
    n = val.shape[0]
    for c in range(SUBLANES):
        ref[pl.ds(base * SUBLANES + c, n, stride=SUBLANES), :] = val[:, c * LANES:(c + 1) * LANES]


def _load_tiles_as_rows(ref, n, base=0):
    return jnp.concatenate([ref[pl.ds(base * SUBLANES + c, n, stride=SUBLANES), :] for c in range(SUBLANES)],
                           axis=1)


def _flat_tiles(ref):
    rows = 1
    for d in ref.shape[:-2]:
        rows *= d
    return ref.reshape(rows * SUBLANES, LANES)


def _tables_kernel(pos_ref, invm_ref, invr_ref, mc_ref, ms1_ref, ms2_ref, rc_ref, rs_ref):
    pos = pos_ref[0].astype(F32)
    angm = pos * invm_ref[...]
    lane = lax.broadcasted_iota(jnp.int32, angm.shape, 1)
    half = ROPE_DIMS // 2
    c = jnp.cos(angm)
    s = jnp.sin(angm)
    mc_ref[0] = c
    ms1_ref[0] = jnp.where(lane < half, -s, 0.0)
    ms2_ref[0] = jnp.where((lane >= half) & (lane < ROPE_DIMS), s, 0.0)
    angr = pos * invr_ref[...]
    rc_ref[0] = jnp.cos(angr)
    rs_ref[0] = jnp.sin(angr)


def _rope_tables(positions):
    B, T = positions.shape
    tm = 512
    half = ROPE_DIMS // 2
    moba_inv = ROPE_THETA ** (-(jnp.arange(half, dtype=F32) * 2.0 / ROPE_DIMS))
    invm = jnp.concatenate([moba_inv, moba_inv, jnp.zeros((LANES - ROPE_DIMS,), F32)])[None, :]
    invr = (RET_ROPE_BASE ** (-jnp.linspace(0.0, 1.0, RET_HEAD_DIM // 2, dtype=F32)))[None, :]
    pos3 = positions.reshape(B, T, 1)
    tab = jax.ShapeDtypeStruct((B, T, LANES), F32)
    spec = pl.BlockSpec((1, tm, LANES), lambda b, i: (b, i, 0))
    return pl.pallas_call(
        _tables_kernel,
        grid=(B, T // tm),
        in_specs=[pl.BlockSpec((1, tm, 1), lambda b, i: (b, i, 0)),
                  pl.BlockSpec((1, LANES), lambda b, i: (0, 0)),
                  pl.BlockSpec((1, LANES), lambda b, i: (0, 0))],
        out_specs=[spec] * 5,
        out_shape=[tab] * 5,
        compiler_params=_cparams(("parallel", "parallel")),
        name="rope_tables",
    )(pos3, invm, invr)


def _ada_kernel(ct_ref, w_ref, b_ref, o_ref):
    ct = ct_ref[...]
    sct = ct * jax.nn.sigmoid(ct)
    w = w_ref[...]
    for b in range(ct.shape[1]):
        o_ref[b:b + 1, :] = jnp.sum(w * sct[:, b:b + 1], axis=0, keepdims=True) + b_ref[...]


def _ada(c, w_ada, b_ada):
    B, D = c.shape
    n_out = w_ada.shape[1]
    tn = 1024
    return pl.pallas_call(
        _ada_kernel,
        grid=(n_out // tn,),
        in_specs=[pl.BlockSpec((D, B), lambda j: (0, 0)),
                  pl.BlockSpec((D, tn), lambda j: (0, j)),
                  pl.BlockSpec((1, tn), lambda j: (0, j))],
        out_specs=pl.BlockSpec((B, tn), lambda j: (0, j)),
        out_shape=jax.ShapeDtypeStruct((B, n_out), F32),
        compiler_params=_cparams(("parallel",)),
        name="adaln_mod",
    )(c.T, w_ada, b_ada[None, :])


def _inproj_kernel(x_ref, g_ref, sh_ref, sc_ref, w_ref, mc_ref, ms1_ref, ms2_ref, rc_ref, rs_ref,
                   o_ref, hn_ref, *, tn, moba_tiles, ret_lo, ret_k_lo, ret_hi):
    j = pl.program_id(2)

    @pl.when(j == 0)
    def _():
        h = _rms_mod(x_ref[0], g_ref[...], sh_ref[0], sc_ref[0])
        hn_ref[...] = h.astype(BF16)

    acc = jnp.dot(hn_ref[...], w_ref[...], preferred_element_type=F32)

    @pl.when(j < moba_tiles)
    def _():
        c, s1, s2 = mc_ref[0], ms1_ref[0], ms2_ref[0]
        half = ROPE_DIMS // 2
        for g in range(tn // LANES):
            a = acc[:, g * LANES:(g + 1) * LANES]
            r = a * c + pltpu.roll(a, LANES - half, 1) * s1 + pltpu.roll(a, half, 1) * s2
            o_ref[0, :, g * LANES:(g + 1) * LANES] = r.astype(o_ref.dtype)

    @pl.when((j >= ret_lo) & (j < ret_hi))
    def _():
        c, s = rc_ref[0], rs_ref[0]
        fac = jnp.where(j >= ret_k_lo, RET_HEAD_DIM ** -0.5, 1.0).astype(F32)
        hw = RET_HEAD_DIM // 2
        for g in range(tn // RET_HEAD_DIM):
            x1 = acc[:, g * RET_HEAD_DIM:g * RET_HEAD_DIM + hw]
            x2 = acc[:, g * RET_HEAD_DIM + hw:(g + 1) * RET_HEAD_DIM]
            o_ref[0, :, g * RET_HEAD_DIM:g * RET_HEAD_DIM + hw] = ((x1 * c - x2 * s) * fac).astype(o_ref.dtype)
            o_ref[0, :, g * RET_HEAD_DIM + hw:(g + 1) * RET_HEAD_DIM] = ((x2 * c + x1 * s) * fac).astype(o_ref.dtype)

    @pl.when(((j >= moba_tiles) & (j < ret_lo)) | (j >= ret_hi))
    def _():
        o_ref[0] = acc.astype(o_ref.dtype)


def _in_proj(x, g, shift, scale, w_bf, tabs, moba_w, ret_w):
    B, T, D = x.shape
    NC = w_bf.shape[1]
    tm, tn = 1024, 1024
    mc, ms1, ms2, rc, rs = tabs
    kern = functools.partial(
        _inproj_kernel, tn=tn,
        moba_tiles=2 * moba_w // tn,
        ret_lo=3 * moba_w // tn,
        ret_k_lo=(3 * moba_w + ret_w) // tn,
        ret_hi=(3 * moba_w + 2 * ret_w) // tn)
    tab_spec = pl.BlockSpec((1, tm, LANES), lambda b, i, j: (b, i, 0))
    vec_spec = pl.BlockSpec((1, 1, D), lambda b, i, j: (b, 0, 0))
    return pl.pallas_call(
        kern,
        grid=(B, T // tm, NC // tn),
        in_specs=[pl.BlockSpec((1, tm, D), lambda b, i, j: (b, i, 0)),
                  pl.BlockSpec((1, D), lambda b, i, j: (0, 0)),
                  vec_spec, vec_spec,
                  pl.BlockSpec((D, tn), lambda b, i, j: (0, j)),
                  tab_spec, tab_spec, tab_spec, tab_spec, tab_spec],
        out_specs=pl.BlockSpec((1, tm, tn), lambda b, i, j: (b, i, j)),
        out_shape=jax.ShapeDtypeStruct((B, T, NC), BF16),
        scratch_shapes=[pltpu.VMEM((tm, D), BF16)],
        compiler_params=_cparams(("parallel", "parallel", "arbitrary")),
        name="in_proj",
    )(x, g[None, :], shift[:, None, :], scale[:, None, :], w_bf, mc, ms1, ms2, rc, rs)


def _moba_kernel(q_ref, k_ref, v_ref, o_ref, km_ref, vt_ref, sel_ref, qs_ref, s_ref, m_ref, acc_ref,
                 *, nb, hp):
    qb = pl.program_id(2)
    BS, hd = MOBA_BLOCK, MOBA_HEAD_DIM

    @pl.when(qb == 0)
    def _():
        for h in range(hp):
            hs = slice(h * hd, (h + 1) * hd)
            for n in range(nb):
                kb = k_ref[0, n * BS:(n + 1) * BS, hs].astype(F32)
                km_ref[h, n:n + 1, :] = jnp.sum(kb, axis=0, keepdims=True) * (1.0 / BS)
                vt_ref[h, n, :hd, :] = v_ref[0, n * BS:(n + 1) * BS, hs].astype(F32).T.astype(BF16)
                vt_ref[h, n, hd:, :] = jnp.ones((vt_ref.shape[2] - hd, BS), BF16)

    scale = hd ** -0.5
    own = pl.multiple_of(qb * BS, BS)
    blk = lax.broadcasted_iota(jnp.int32, (nb, BS), 0)
    blk_f = blk.astype(F32)

    for h in range(hp):
        hs = slice(h * hd, (h + 1) * hd)
        qT = q_ref[0, :, hs].astype(F32).T
        gate = jnp.dot(km_ref[h], qT, preferred_element_type=F32,
                       precision=lax.Precision.HIGHEST)
        g = jnp.where(blk < qb, gate, NEG)
        sel = jnp.zeros((nb, BS), F32)
        for _ in range(MOBA_TOPK):
            m = jnp.max(g, axis=0, keepdims=True)
            idx = jnp.min(jnp.where(g == m, blk_f, float(nb)), axis=0, keepdims=True)
            pick = blk_f == idx
            sel = jnp.where(pick & (m > 0.5 * NEG), 1.0, sel)
            g = jnp.where(pick, -jnp.inf, g)
        sel_ref[h] = sel
        qs_ref[h] = (qT * scale).astype(BF16)
        m_ref[h] = jnp.full((1, BS), NEG, F32)

    n_pairs = lax.shift_right_logical(qb + 1, 1)

    def sweep_scores(j, _):
        for h in range(hp):
            mh = m_ref[h]
            for u in range(2):
                n = 2 * j + u
                off = pl.multiple_of(n * BS, BS)
                s = jnp.dot(k_ref[0, pl.ds(off, BS), h * hd:(h + 1) * hd], qs_ref[h],
                            preferred_element_type=F32)
                s = jnp.where(sel_ref[h, pl.ds(n, 1), :] > 0.0, s, NEG)
                s_ref[h, n] = s
                mh = jnp.maximum(mh, jnp.max(s, axis=0, keepdims=True))
            m_ref[h] = mh
        return 0

    lax.fori_loop(0, n_pairs, sweep_scores, 0)

    krow = lax.broadcasted_iota(jnp.int32, (BS, BS), 0)
    qcol = lax.broadcasted_iota(jnp.int32, (BS, BS), 1)
    for h in range(hp):
        hs = slice(h * hd, (h + 1) * hd)
        s = jnp.dot(k_ref[0, pl.ds(own, BS), hs], qs_ref[h], preferred_element_type=F32)
        s = jnp.where(krow <= qcol, s, NEG)
        m = jnp.maximum(m_ref[h], jnp.max(s, axis=0, keepdims=True))
        m_ref[h] = m
        p = jnp.exp((s - m).astype(BF16))
        acc_ref[h] = jnp.dot(vt_ref[h, qb], p, preferred_element_type=F32)

    def sweep_values(j, _):
        for h in range(hp):
            mh = m_ref[h]
            p0 = jnp.exp((s_ref[h, 2 * j] - mh).astype(BF16))
            p1 = jnp.exp((s_ref[h, 2 * j + 1] - mh).astype(BF16))
            acc_ref[h] = acc_ref[h] + (
                jnp.dot(vt_ref[h, 2 * j], p0, preferred_element_type=F32)
                + jnp.dot(vt_ref[h, 2 * j + 1], p1, preferred_element_type=F32))
        return 0

    lax.fori_loop(0, n_pairs, sweep_values, 0)

    for h in range(hp):
        acc = acc_ref[h]
        o_ref[0, :, h * hd:(h + 1) * hd] = (acc[:hd, :] / acc[hd:hd + 1, :]).T.astype(o_ref.dtype)


def _moba_kernel_v2(q_ref, k_ref, v_ref, o_ref, km_ref, vt_ref, sel_ref, *, nb, hp):
    qb = pl.program_id(2)
    BS, hd = MOBA_BLOCK, MOBA_HEAD_DIM

    @pl.when(qb == 0)
    def _():
        for h in range(hp):
            hs = slice(h * hd, (h + 1) * hd)
            for n in range(nb):
                kb = k_ref[0, n * BS:(n + 1) * BS, hs].astype(F32)
                km_ref[h, n:n + 1, :] = jnp.sum(kb, axis=0, keepdims=True) * (1.0 / BS)
                vt_ref[h, n] = v_ref[0, n * BS:(n + 1) * BS, hs].astype(F32).T.astype(BF16)

    scale = hd ** -0.5
    own = pl.multiple_of(qb * BS, BS)
    blk = lax.broadcasted_iota(jnp.int32, (nb, BS), 0)
    blk_f = blk.astype(F32)
    krow = lax.broadcasted_iota(jnp.int32, (BS, BS), 0)
    qcol = lax.broadcasted_iota(jnp.int32, (BS, BS), 1)

    qs_all, init = [], []
    for h in range(hp):
        hs = slice(h * hd, (h + 1) * hd)
        qT = q_ref[0, :, hs].astype(F32).T
        gate = jnp.dot(km_ref[h], qT, preferred_element_type=F32,
                       precision=lax.Precision.HIGHEST)
        g = jnp.where(blk < qb, gate, NEG)
        sel = jnp.zeros((nb, BS), F32)
        for _ in range(MOBA_TOPK):
            m = jnp.max(g, axis=0, keepdims=True)
            idx = jnp.min(jnp.where(g == m, blk_f, float(nb)), axis=0, keepdims=True)
            pick = blk_f == idx
            sel = jnp.where(pick & (m > 0.5 * NEG), 1.0, sel)
            g = jnp.where(pick, -jnp.inf, g)
        sel_ref[h] = sel

        qs = (qT * scale).astype(BF16)
        s = jnp.dot(k_ref[0, pl.ds(own, BS), hs], qs, preferred_element_type=F32)
        s = jnp.where(krow <= qcol, s, NEG)
        m0 = jnp.max(s, axis=0, keepdims=True)
        p = jnp.exp(s - m0)
        l0 = jnp.sum(p, axis=0, keepdims=True)
        acc0 = jnp.dot(vt_ref[h, qb], p.astype(BF16), preferred_element_type=F32)
        qs_all.append(qs)
        init.append((m0, l0, acc0))

    def body(n, carry):
        off = pl.multiple_of(n * BS, BS)
        out = []
        for h in range(hp):
            m, l, acc = carry[h]
            selrow = sel_ref[h, pl.ds(n, 1), :]
            s = jnp.dot(k_ref[0, pl.ds(off, BS), h * hd:(h + 1) * hd], qs_all[h],
                        preferred_element_type=F32)
            s = jnp.where(selrow > 0.0, s, NEG)
            m_new = jnp.maximum(m, jnp.max(s, axis=0, keepdims=True))
            alpha = jnp.exp(m - m_new)
            p = jnp.exp(s - m_new)
            l = alpha * l + jnp.sum(p, axis=0, keepdims=True)
            acc = alpha * acc + jnp.dot(vt_ref[h, n], p.astype(BF16), preferred_element_type=F32)
            out.append((m_new, l, acc))
        return tuple(out)

    final = lax.fori_loop(0, qb, body, tuple(init))
    for h in range(hp):
        _, l, acc = final[h]
        o_ref[0, :, h * hd:(h + 1) * hd] = (acc / l).T.astype(o_ref.dtype)


def _moba(proj, B, T, moba_w):
    H, hd, BS = MOBA_HEADS, MOBA_HEAD_DIM, MOBA_BLOCK
    nb = T // BS
    hp = 4
    ones_rows = 2 * SUBLANES
    gw = hp * hd
    gpw = moba_w // gw
    return pl.pallas_call(
        functools.partial(_moba_kernel, nb=nb, hp=hp),
        grid=(B, H // hp, nb),
        in_specs=[pl.BlockSpec((1, BS, gw), lambda b, h, i: (b, i, h)),
                  pl.BlockSpec((1, T, gw), lambda b, h, i: (b, 0, gpw + h)),
                  pl.BlockSpec((1, T, gw), lambda b, h, i: (b, 0, 2 * gpw + h))],
        out_specs=pl.BlockSpec((1, BS, gw), lambda b, h, i: (b, i, h)),
        out_shape=jax.ShapeDtypeStruct((B, T, moba_w), BF16),
        scratch_shapes=[pltpu.VMEM((hp, nb, hd), F32),
                        pltpu.VMEM((hp, nb, hd + ones_rows, BS), BF16),
                        pltpu.VMEM((hp, nb, BS), F32),
                        pltpu.VMEM((hp, hd, BS), BF16),
                        pltpu.VMEM((hp, nb, BS, BS), F32),
                        pltpu.VMEM((hp, 1, BS), F32),
                        pltpu.VMEM((hp, hd + ones_rows, BS), F32)],
        compiler_params=_cparams(("parallel", "parallel", "arbitrary")),
        name="moba_attn",
    )(proj, proj, proj)


def _ret_kernel(q_ref, k_ref, v_ref, g_ref, dm_ref, xi_ref, zeta_ref, cd_ref, o_ref, s_ref):
    c = pl.program_id(1)
    d = RET_HEAD_DIM

    @pl.when(c == 0)
    def _():
        s_ref[...] = jnp.zeros_like(s_ref)

    for h in range(RET_HEADS):
        sl = slice(h * d, (h + 1) * d)
        q = q_ref[0, :, sl]
        k = k_ref[0, :, sl]
        v = v_ref[0, :, sl]
        inner = lax.dot_general(q, k, (((1,), (1,)), ((), ())), preferred_element_type=F32) * dm_ref[h]
        S = s_ref[h]
        o = (jnp.dot(inner.astype(BF16), v, preferred_element_type=F32)
             + jnp.dot(q, S.astype(BF16), preferred_element_type=F32) * xi_ref[h])
        kz = (k.astype(F32) * zeta_ref[h]).astype(BF16)
        s_ref[h] = S * cd_ref[h] + lax.dot_general(kz, v, (((0,), (0,)), ((), ())),
                                                   preferred_element_type=F32)
        mu = jnp.mean(o, axis=-1, keepdims=True)
        dlt = o - mu
        var = jnp.mean(dlt * dlt, axis=-1, keepdims=True)
        on = dlt * lax.rsqrt(var + NORM_EPS)
        gg = g_ref[0, :, sl].astype(F32)
        o_ref[0, :, sl] = (on * (gg * jax.nn.sigmoid(gg))).astype(o_ref.dtype)


def _retention(proj, B, T, moba_w, ret_w):
    C, H = RET_CHUNK, RET_HEADS
    gamma = 1.0 - jnp.exp2(-5.0 - jnp.arange(H, dtype=F32))
    log_g = jnp.log(gamma)
    pos = jnp.arange(C, dtype=F32)
    diff = pos[:, None] - pos[None, :]
    dmask = jnp.where(diff >= 0, jnp.exp(jnp.maximum(diff, 0.0) * log_g[:, None, None]), 0.0)
    xi = jnp.exp((pos + 1.0) * log_g[:, None])[:, :, None]
    zeta = jnp.exp((C - 1.0 - pos) * log_g[:, None])[:, :, None]
    cd = jnp.exp(C * log_g)[:, None, None]
    base = 3 * moba_w // ret_w
    col = lambda off: pl.BlockSpec((1, C, ret_w), lambda b, c: (b, c, base + off))
    full = lambda shp: pl.BlockSpec(shp, lambda b, c: (0,) * len(shp))
    return pl.pallas_call(
        _ret_kernel,
        grid=(B, T // C),
        in_specs=[col(0), col(1), col(2), col(3),
                  full((H, C, C)), full((H, C, 1)), full((H, C, 1)), full((H, 1, 1))],
        out_specs=pl.BlockSpec((1, C, ret_w), lambda b, c: (b, c, 0)),
        out_shape=jax.ShapeDtypeStruct((B, T, ret_w), BF16),
        scratch_shapes=[pltpu.VMEM((H, RET_HEAD_DIM, RET_HEAD_DIM), F32)],
        compiler_params=_cparams(("parallel", "arbitrary")),
        name="retention",
    )(proj, proj, proj, proj, dmask, xi, zeta, cd)


def _outproj_kernel(oa_ref, or_ref, w_ref, x_ref, ga_ref, g_ref, sh_ref, sc_ref, wrh_ref, wrl_ref,
                    x1_ref, hp_ref, lg_ref, *, moba_w):
    mix = (jnp.dot(oa_ref[0], w_ref[:moba_w, :], preferred_element_type=F32)
           + jnp.dot(or_ref[0], w_ref[moba_w:, :], preferred_element_type=F32))
    x1 = x_ref[0] + ga_ref[0] * mix
    x1_ref[0] = x1
    h = _rms_mod(x1, g_ref[...], sh_ref[0], sc_ref[0])
    half = h.shape[-1] // 2
    _store_rows_as_tiles(_flat_tiles(hp_ref), _pack_pair(h[:, :half], h[:, half:]))
    h_hi = h.astype(BF16)
    h_lo = (h - h_hi.astype(F32)).astype(BF16)
    lg_ref[0] = (jnp.dot(h_hi, wrh_ref[...], preferred_element_type=F32)
                 + (jnp.dot(h_lo, wrh_ref[...], preferred_element_type=F32)
                    + jnp.dot(h_hi, wrl_ref[...], preferred_element_type=F32)))


def _out_proj(o_a, o_r, w_bf, x, gate_a, g, shift, scale, w_router):
    B, T, D = x.shape
    moba_w, ret_w = o_a.shape[-1], o_r.shape[-1]
    E = w_router.shape[1]
    tm = 256
    wr_hi = w_router.astype(BF16)
    wr_lo = (w_router - wr_hi.astype(F32)).astype(BF16)
    vec = pl.BlockSpec((1, 1, D), lambda b, i: (b, 0, 0))
    row = lambda w: pl.BlockSpec((1, tm, w), lambda b, i: (b, i, 0))
    return pl.pallas_call(
        functools.partial(_outproj_kernel, moba_w=moba_w),
        grid=(B, T // tm),
        in_specs=[row(moba_w), row(ret_w),
                  pl.BlockSpec((moba_w + ret_w, D), lambda b, i: (0, 0)),
                  row(D), vec,
                  pl.BlockSpec((1, D), lambda b, i: (0, 0)),
                  vec, vec,
                  pl.BlockSpec((D, E), lambda b, i: (0, 0)),
                  pl.BlockSpec((D, E), lambda b, i: (0, 0))],
        out_specs=[row(D), pl.BlockSpec((1, tm, SUBLANES, LANES), lambda b, i: (b, i, 0, 0)), row(E)],
        out_shape=[jax.ShapeDtypeStruct((B, T, D), F32),
                   jax.ShapeDtypeStruct((B, T, SUBLANES, LANES), U32),
                   jax.ShapeDtypeStruct((B, T, E), F32)],
        compiler_params=_cparams(("parallel", "parallel")),
        name="out_proj",
    )(o_a, o_r, w_bf, x, gate_a[:, None, :], g[None, :], shift[:, None, :], scale[:, None, :], wr_hi, wr_lo)


def _route_kernel_t(lg_ref, b_ref, selr_ref, wf_ref, rank_ref, cnt_ref, carry_ref):
    @pl.when((pl.program_id(0) == 0) & (pl.program_id(1) == 0))
    def _():
        carry_ref[...] = jnp.zeros_like(carry_ref)

    E = N_EXPERTS
    gsz = E // N_GROUPS
    assert gsz == SUBLANES and N_GROUPS == SUBLANES, "a routing group is one sublane tile of experts"
    s = jax.nn.sigmoid(lg_ref[0].T)
    biased = s + b_ref[...]
    tm = s.shape[1]
    sub = lax.broadcasted_iota(jnp.int32, (SUBLANES, tm), 0).astype(F32)
    eid = lax.broadcasted_iota(jnp.int32, (E, tm), 0).astype(F32)

    def first_argmax(v, ids, width):
        m = jnp.max(v, axis=0, keepdims=True)
        idx = jnp.min(jnp.where(v == m, ids, float(width)), axis=0, keepdims=True)
        return m, idx

    gscore = jnp.zeros((N_GROUPS, tm), F32)
    for gi in range(N_GROUPS):
        v = biased[gi * gsz:(gi + 1) * gsz, :]
        m1, i1 = first_argmax(v, sub, gsz)
        m2 = jnp.max(jnp.where(sub == i1, -jnp.inf, v), axis=0, keepdims=True)
        gscore = jnp.where(sub == float(gi), m1 + m2, gscore)

    gsel = jnp.zeros((N_GROUPS, tm), F32)
    for _ in range(TOPK_GROUPS):
        _, gi = first_argmax(gscore, sub, N_GROUPS)
        pick = sub == gi
        gsel = jnp.where(pick, 1.0, gsel)
        gscore = jnp.where(pick, -jnp.inf, gscore)

    cand = jnp.concatenate(
        [jnp.where(gsel[gi:gi + 1, :] > 0.0, biased[gi * gsz:(gi + 1) * gsz, :], NEG) for gi in range(N_GROUPS)],
        axis=0)
    selr = jnp.zeros((E, tm), F32)
    for r in range(TOP_K):
        _, ei = first_argmax(cand, eid, E)
        pick = eid == ei
        selr = jnp.where(pick, float(r + 1), selr)
        cand = jnp.where(pick, -jnp.inf, cand)

    chosen = selr > 0.0
    w = jnp.where(chosen, s, 0.0)
    wsum = jnp.sum(w, axis=0, keepdims=True)
    selr_ref[0] = selr
    wf_ref[0] = w / wsum * ROUTE_SCALE

    onehot = chosen.astype(BF16)
    c_i = lax.broadcasted_iota(jnp.int32, (tm, tm), 0)
    r_i = lax.broadcasted_iota(jnp.int32, (tm, tm), 1)
    tri = (c_i < r_i).astype(BF16)
    carry = carry_ref[...]
    rank_ref[0] = jnp.dot(onehot, tri, preferred_element_type=F32) + carry
    carry = carry + jnp.sum(chosen.astype(F32), axis=1, keepdims=True)
    carry_ref[...] = carry
    cnt_ref[...] = carry


def _route_t(logits_t, bias):
    B, T, E = logits_t.shape
    tm = 512
    blk = pl.BlockSpec((1, E, tm), lambda b, i: (b, 0, i))
    col = pl.BlockSpec((E, 1), lambda b, i: (0, 0))
    full = jax.ShapeDtypeStruct((B, E, T), F32)
    return pl.pallas_call(
        _route_kernel_t,
        grid=(B, T // tm),
        in_specs=[pl.BlockSpec((1, tm, E), lambda b, i: (b, i, 0)), col],
        out_specs=[blk, blk, blk, col],
        out_shape=[full, full, full, jax.ShapeDtypeStruct((E, 1), F32)],
        scratch_shapes=[pltpu.VMEM((E, 1), F32)],
        compiler_params=_cparams(("arbitrary", "arbitrary")),
        name="route_topk",
    )(logits_t, bias[:, None])


def _dest_kernel_t(selr_ref, wf_ref, rank_ref, ps_ref, dest_ref, wk_ref):
    selr = selr_ref[0]
    destfull = rank_ref[0] + ps_ref[...]
    wf = wf_ref[0]
    for r in range(TOP_K):
        hit = selr == float(r + 1)
        dest_ref[0, r:r + 1, :] = jnp.sum(jnp.where(hit, destfull, 0.0), axis=0, keepdims=True).astype(jnp.int32)
        wk_ref[0, r:r + 1, :] = jnp.sum(jnp.where(hit, wf, 0.0), axis=0, keepdims=True)


def _dest_t(selr, wf, rank, pstart_f):
    B, E, T = selr.shape
    tm = 512
    blk = pl.BlockSpec((1, E, tm), lambda b, i: (b, 0, i))
    outb = pl.BlockSpec((1, TOP_K, tm), lambda b, i: (b, 0, i))
    return pl.pallas_call(
        _dest_kernel_t,
        grid=(B, T // tm),
        in_specs=[blk, blk, blk, pl.BlockSpec((E, 1), lambda b, i: (0, 0))],
        out_specs=[outb, outb],
        out_shape=[jax.ShapeDtypeStruct((B, TOP_K, T), jnp.int32), jax.ShapeDtypeStruct((B, TOP_K, T), F32)],
        compiler_params=_cparams(("parallel", "parallel")),
        name="route_dest",
    )(selr, wf, rank, pstart_f)


def _first_argmax(v, lane_f, width):
    m = jnp.max(v, axis=1, keepdims=True)
    idx = jnp.min(jnp.where(v == m, lane_f, float(width)), axis=1, keepdims=True)
    return m, idx


def _route_kernel(lg_ref, b_ref, selr_ref, wf_ref, rank_ref, cnt_ref, carry_ref):
    i = pl.program_id(0)

    @pl.when(i == 0)
    def _():
        carry_ref[...] = jnp.zeros_like(carry_ref)

    E = N_EXPERTS
    gsz = E // N_GROUPS
    s = jax.nn.sigmoid(lg_ref[...])
    biased = s + b_ref[...]
    tm = s.shape[0]
    lane = lax.broadcasted_iota(jnp.int32, (tm, E), 1)
    lane_f = lane.astype(F32)
    grp = lax.shift_right_logical(lane, gsz.bit_length() - 1)

    gscore = jnp.full((tm, E), -jnp.inf, F32)
    for gi in range(N_GROUPS):
        v = jnp.where(grp == gi, biased, -jnp.inf)
        m1, i1 = _first_argmax(v, lane_f, E)
        m2 = jnp.max(jnp.where(lane_f == i1, -jnp.inf, v), axis=1, keepdims=True)
        gscore = jnp.where(lane == gi, m1 + m2, gscore)

    emask = jnp.zeros((tm, E), jnp.bool_)
    grp_f = grp.astype(F32)
    for _ in range(TOPK_GROUPS):
        _, gi = _first_argmax(gscore, lane_f, E)
        emask = emask | (grp_f == gi)
        gscore = jnp.where(lane_f == gi, -jnp.inf, gscore)

    cand = jnp.where(emask, biased, NEG)
    selr = jnp.zeros((tm, E), F32)
    for r in range(TOP_K):
        _, ei = _first_argmax(cand, lane_f, E)
        pick = lane_f == ei
        selr = jnp.where(pick, float(r + 1), selr)
        cand = jnp.where(pick, -jnp.inf, cand)

    chosen = selr > 0.0
    w = jnp.where(chosen, s, 0.0)
    wsum = jnp.sum(w, axis=1, keepdims=True)
    selr_ref[...] = selr
    wf_ref[...] = w / wsum * ROUTE_SCALE

    onehot = chosen.astype(BF16)
    r_i = lax.broadcasted_iota(jnp.int32, (tm, tm), 0)
    c_i = lax.broadcasted_iota(jnp.int32, (tm, tm), 1)
    tri = (c_i < r_i).astype(BF16)
    carry = carry_ref[...]
    rank_ref[...] = jnp.dot(tri, onehot, preferred_element_type=F32) + carry
    carry = carry + jnp.sum(chosen.astype(F32), axis=0, keepdims=True)
    carry_ref[...] = carry
    cnt_ref[...] = carry


def _route(logits, bias):
    N, E = logits.shape
    tm = 512
    blk = pl.BlockSpec((tm, E), lambda i: (i, 0))
    one = pl.BlockSpec((1, E), lambda i: (0, 0))
    full = jax.ShapeDtypeStruct((N, E), F32)
    return pl.pallas_call(
        _route_kernel,
        grid=(N // tm,),
        in_specs=[blk, one],
        out_specs=[blk, blk, blk, one],
        out_shape=[full, full, full, jax.ShapeDtypeStruct((1, E), F32)],
        scratch_shapes=[pltpu.VMEM((1, E), F32)],
        compiler_params=_cparams(("arbitrary",)),
        name="route_topk",
    )(logits, bias[None, :])


def _dest_kernel(selr_ref, wf_ref, rank_ref, ps_ref, dest_ref, wk_ref):
    selr = selr_ref[...]
    destfull = rank_ref[...] + ps_ref[...]
    wf = wf_ref[...]
    for r in range(TOP_K):
        hit = selr == float(r + 1)
        dest_ref[:, r:r + 1] = jnp.sum(jnp.where(hit, destfull, 0.0), axis=1, keepdims=True).astype(jnp.int32)
        wk_ref[:, r:r + 1] = jnp.sum(jnp.where(hit, wf, 0.0), axis=1, keepdims=True)


def _dest(selr, wf, rank, pstart_f):
    N, E = selr.shape
    tm = 512
    blk = pl.BlockSpec((tm, E), lambda i: (i, 0))
    outb = pl.BlockSpec((tm, TOP_K), lambda i: (i, 0))
    return pl.pallas_call(
        _dest_kernel,
        grid=(N // tm,),
        in_specs=[blk, blk, blk, pl.BlockSpec((1, E), lambda i: (0, 0))],
        out_specs=[outb, outb],
        out_shape=[jax.ShapeDtypeStruct((N, TOP_K), jnp.int32), jax.ShapeDtypeStruct((N, TOP_K), F32)],
        compiler_params=_cparams(("parallel",)),
        name="route_dest",
    )(selr, wf, rank, pstart_f)


def _row_copy(src, s_row, dst, d_row, n, sem):
    return pltpu.make_async_copy(src.at[pl.ds(s_row, n)], dst.at[pl.ds(d_row, n)], sem)


def _dispatch_kernel(padlo_ref, padn_ref, dest_ref, h_ref, z_ref, xs_ref, sem, zsem, *, tt, n_exp):
    i = pl.program_id(0)

    def issue(t, _):
        for k in range(TOP_K):
            _row_copy(h_ref, t, xs_ref, dest_ref[0, 0, k * tt + t], 1, sem).start(priority=k % 2)
        return 0

    lax.fori_loop(0, tt, issue, 0, unroll=2)

    def each_pad(fn):
        def per_expert(e, _):
            lo = padlo_ref[e]

            def one(r, _):
                fn(lo + r)
                return 0

            lax.fori_loop(0, padn_ref[e], one, 0)
            return 0

        lax.fori_loop(0, n_exp, per_expert, 0)

    @pl.when(i == 0)
    def _():
        each_pad(lambda r: _row_copy(z_ref, 0, xs_ref, r, 1, zsem).start())

    for k in range(TOP_K):
        _row_copy(h_ref, 0, xs_ref, 0, tt, sem).wait()

    @pl.when(i == 0)
    def _():
        each_pad(lambda r: _row_copy(z_ref, 0, xs_ref, 0, 1, zsem).wait())


def _dispatch(h2p, dest, pad_lo, pad_n, R):
    N = h2p.shape[0]
    tile = h2p.shape[1:]
    tt = TOK_TILE
    dest3 = dest
    zeros = jnp.zeros((SUBLANES,) + tile, h2p.dtype)
    grid_spec = pltpu.PrefetchScalarGridSpec(
        num_scalar_prefetch=2,
        grid=(N // tt,),
        in_specs=[pl.BlockSpec((1, 1, tt * TOP_K), lambda i, lo, n: (i, 0, 0), memory_space=pltpu.SMEM),
                  pl.BlockSpec((tt,) + tile, lambda i, lo, n: (i, 0, 0)),
                  pl.BlockSpec((SUBLANES,) + tile, lambda i, lo, n: (0, 0, 0))],
        out_specs=pl.BlockSpec(memory_space=pl.ANY),
        scratch_shapes=[pltpu.SemaphoreType.DMA(()), pltpu.SemaphoreType.DMA(())],
    )
    return pl.pallas_call(
        functools.partial(_dispatch_kernel, tt=tt, n_exp=N_EXPERTS),
        grid_spec=grid_spec,
        out_shape=jax.ShapeDtypeStruct((R,) + tile, h2p.dtype),
        compiler_params=_cparams(("arbitrary",)),
        name="moe_dispatch",
    )(pad_lo, pad_n, dest3, h2p, zeros)


def _expert_kernel(te_ref, nu_ref, first_ref, slot_ref, nxt_ref, xs_ref, wg_hbm, wu_hbm, wd_hbm, y_ref,
                   stg, stu, std, wgb, wub, wdb, sem):
    i = pl.program_id(0)
    n_chunks = 2

    def fetch(e, s):
        cps = []
        for m, (src, dst) in enumerate(((wg_hbm, stg), (wu_hbm, stu), (wd_hbm, std))):
            rows = dst.shape[1] // n_chunks
            for c in range(n_chunks):
                cps.append(pltpu.make_async_copy(src.at[e, pl.ds(c * rows, rows)],
                                                 dst.at[s, pl.ds(c * rows, rows)], sem.at[s, m]))
        return cps

    @pl.when(i == 0)
    def _():
        for cp in fetch(te_ref[0], 0):
            cp.start()

        @pl.when(nxt_ref[0, 0] >= 0)
        def _():
            for cp in fetch(nxt_ref[0, 0], 1):
                cp.start()

    active = i < nu_ref[0]

    def swiglu_tile(wg, wu, wd):
        m_rows = xs_ref.shape[0]
        lo, hi = _unpack_pair(_load_tiles_as_rows(_flat_tiles(xs_ref), m_rows))
        lo = lo.astype(BF16)
        hi = hi.astype(BF16)
        half = lo.shape[-1]
        a = (jnp.dot(lo, wg[:half, :], preferred_element_type=F32)
             + jnp.dot(hi, wg[half:, :], preferred_element_type=F32))
        u = (jnp.dot(lo, wu[:half, :], preferred_element_type=F32)
             + jnp.dot(hi, wu[half:, :], preferred_element_type=F32))
        hmid = (a * jax.nn.sigmoid(a) * u).astype(BF16)
        y = jnp.dot(hmid, wd, preferred_element_type=F32)
        _store_rows_as_tiles(_flat_tiles(y_ref), _pack_pair(y[:, :half], y[:, half:]))

    is_first = first_ref[i] == 1

    @pl.when(active & is_first)
    def _():
        s = slot_ref[i]
        for cp in fetch(0, s):
            cp.wait()
        wg = stg[s].astype(BF16)
        wu = stu[s].astype(BF16)
        wd = std[s].astype(BF16)
        wgb[...] = wg
        wub[...] = wu
        wdb[...] = wd
        swiglu_tile(wg, wu, wd)

        @pl.when(nxt_ref[1, i] >= 0)
        def _():
            for cp in fetch(nxt_ref[1, i], s):
                cp.start()

    @pl.when(active & jnp.logical_not(is_first))
    def _():
        swiglu_tile(wgb[...], wub[...], wdb[...])


def _experts(xs, tile_expert, n_used, first, slot, nxt, wg, wu, wd):
    R = xs.shape[0]
    tile = xs.shape[1:]
    M = ROW_TILE
    _, D, F = wg.shape
    row = lambda i, te, nu, fi, sl, nx: (jnp.minimum(i, nu[0] - 1), 0, 0)
    grid_spec = pltpu.PrefetchScalarGridSpec(
        num_scalar_prefetch=5,
        grid=(R // M,),
        in_specs=[pl.BlockSpec((M,) + tile, row),
                  pl.BlockSpec(memory_space=pl.ANY),
                  pl.BlockSpec(memory_space=pl.ANY),
                  pl.BlockSpec(memory_space=pl.ANY)],
        out_specs=pl.BlockSpec((M,) + tile, row),
        scratch_shapes=[pltpu.VMEM((2, D, F), F32), pltpu.VMEM((2, D, F), F32), pltpu.VMEM((2, F, D), F32),
                        pltpu.VMEM((D, F), BF16), pltpu.VMEM((D, F), BF16), pltpu.VMEM((F, D), BF16),
                        pltpu.SemaphoreType.DMA((2, 3))],
    )
    return pl.pallas_call(
        _expert_kernel,
        grid_spec=grid_spec,
        out_shape=jax.ShapeDtypeStruct((R,) + tile, U32),
        compiler_params=_cparams(("arbitrary",)),
        name="moe_experts",
    )(tile_expert, n_used, first, slot, nxt, xs, wg, wu, wd)


def _final_kernel(dcur_ref, dnxt_ref, h_ref, wsg_ref, wsu_ref, wsd_ref, x1_ref, gf_ref, g_ref, wk_ref, y_ref,
                  o_ref, ybuf, sem, *, tt, n_tiles):
    i = pl.program_id(0)
    slot = lax.rem(i, 2)

    def gather(d_ref, s):
        def issue(t, _):
            for k in range(TOP_K):
                pltpu.make_async_copy(y_ref.at[pl.ds(d_ref[0, 0, k * tt + t], 1)],
                                      ybuf.at[s, k, pl.ds(t, 1)], sem.at[s]).start(priority=k % 2)
            return 0

        lax.fori_loop(0, tt, issue, 0, unroll=2)

    @pl.when(i == 0)
    def _():
        gather(dcur_ref, 0)

    for p in range(2):
        @pl.when((i + 1 < n_tiles) & (slot == p))
        def _(p=p):
            gather(dnxt_ref, 1 - p)

    lo, hi = _unpack_pair(_load_tiles_as_rows(_flat_tiles(h_ref), tt))
    lo = lo.astype(BF16)
    hi = hi.astype(BF16)
    half = lo.shape[-1]
    a = (jnp.dot(lo, wsg_ref[:half, :], preferred_element_type=F32)
         + jnp.dot(hi, wsg_ref[half:, :], preferred_element_type=F32))
    u = (jnp.dot(lo, wsu_ref[:half, :], preferred_element_type=F32)
         + jnp.dot(hi, wsu_ref[half:, :], preferred_element_type=F32))
    hmid = (a * jax.nn.sigmoid(a) * u).astype(BF16)
    shared = jnp.dot(hmid, wsd_ref[...], preferred_element_type=F32)

    for k in range(TOP_K):
        pltpu.make_async_copy(y_ref.at[pl.ds(0, tt)], ybuf.at[slot, k], sem.at[slot]).wait()

    wk = wk_ref[...]
    yflat = _flat_tiles(ybuf)

    for p in range(2):
        @pl.when(slot == p)
        def _(p=p):
            r_lo = jnp.zeros((tt, half), F32)
            r_hi = jnp.zeros((tt, half), F32)
            for k in range(TOP_K):
                ylo, yhi = _unpack_pair(_load_tiles_as_rows(yflat, tt, base=(p * TOP_K + k) * tt))
                wcol = wk[:, k:k + 1]
                r_lo = r_lo + wcol * ylo
                r_hi = r_hi + wcol * yhi
            total = shared + jnp.concatenate([r_lo, r_hi], axis=1)
            x2 = x1_ref[...] + gf_ref[0] * total
            ms = jnp.mean(x2 * x2, axis=-1, keepdims=True)
            o_ref[...] = x2 * lax.rsqrt(ms + NORM_EPS) * g_ref[...]


def _final(h2p, wsg, wsu, wsd, x1, gate_f, norm_out, dest, wk, y, T):
    N = h2p.shape[0]
    tile = h2p.shape[1:]
    D = x1.shape[-1]
    F = wsg.shape[1]
    tt = TOK_TILE
    per_b = T // tt
    n_tiles = N // tt
    dest3 = dest
    rowb = lambda w: pl.BlockSpec((tt, w), lambda i: (i, 0))
    const = lambda shp: pl.BlockSpec(shp, lambda i: (0,) * len(shp))
    dspec = lambda f: pl.BlockSpec((1, 1, tt * TOP_K), f, memory_space=pltpu.SMEM)
    return pl.pallas_call(
        functools.partial(_final_kernel, tt=tt, n_tiles=n_tiles),
        grid=(n_tiles,),
        in_specs=[dspec(lambda i: (i, 0, 0)),
                  dspec(lambda i: (jnp.minimum(i + 1, n_tiles - 1), 0, 0)),
                  pl.BlockSpec((tt,) + tile, lambda i: (i, 0, 0)),
                  const((D, F)), const((D, F)), const((F, D)), rowb(D),
                  pl.BlockSpec((1, 1, D), lambda i: (i // per_b, 0, 0)),
                  const((1, D)),
                  pl.BlockSpec((tt, TOP_K), lambda i: (i, 0)),
                  pl.BlockSpec(memory_space=pl.ANY)],
        out_specs=rowb(D),
        out_shape=jax.ShapeDtypeStruct((N, D), F32),
        scratch_shapes=[pltpu.VMEM((2, TOP_K, tt) + tile, U32), pltpu.SemaphoreType.DMA((2,))],
        compiler_params=_cparams(("arbitrary",)),
        name="moe_combine_final",
    )(dest3, dest3, h2p, wsg, wsu, wsd, x1, gate_f[:, None, :], norm_out[None, :], wk, y)


def kernel(x, c, positions, w_ada, b_ada, norm_mix, norm_ffn, norm_out, w_in, w_out, w_router, router_bias,
           w_gate, w_up, w_down, w_sh_gate, w_sh_up, w_sh_down):
    B, T, D = x.shape
    depth = w_ada.shape[0]
    assert depth == 1, "the final rmsnorm is fused into the layer's last kernel"
    moba_w = MOBA_HEADS * MOBA_HEAD_DIM
    ret_w = RET_HEADS * RET_HEAD_DIM
    N = B * T
    E, M = N_EXPERTS, ROW_TILE
    R = N * TOP_K + E * M
    n_tiles = R // M
    tabs = _rope_tables(positions)

    for l in range(depth):
        mod = _ada(c, w_ada[l], b_ada[l])
        shift_a, scale_a, gate_a, shift_f, scale_f, gate_f = jnp.split(mod, 6, axis=-1)

        proj = _in_proj(x, norm_mix[l], shift_a, scale_a, w_in[l].astype(BF16), tabs, moba_w, ret_w)
        o_a = _moba(proj, B, T, moba_w)
        o_r = _retention(proj, B, T, moba_w, ret_w)
        x1, h2p, logits = _out_proj(o_a, o_r, w_out[l].astype(BF16), x, gate_a, norm_ffn[l],
                                    shift_f, scale_f, w_router[l])

        assert D // 2 == SUBLANES * LANES, "a packed row must fill exactly one (SUBLANES, LANES) tile"
        h2p = h2p.reshape(N, SUBLANES, LANES)
        selr, wf, rank, counts = _route_t(logits, router_bias[l])
        cnt = counts[:, 0].astype(jnp.int32)
        pcnt = (cnt + M - 1) // M * M
        pend = jnp.cumsum(pcnt)
        pstart = pend - pcnt
        tidx = jnp.arange(n_tiles, dtype=jnp.int32)
        end_tile = pend // M
        eids = jnp.arange(E, dtype=jnp.int32)
        owner = lambda v: jnp.minimum(jnp.sum(end_tile[None, :] <= v[:, None], axis=1), E - 1).astype(jnp.int32)
        end_of = lambda e: jnp.sum(jnp.where(e[:, None] == eids[None, :], end_tile[None, :], 0), axis=1)
        tile_expert = owner(tidx)
        n_used = end_tile[-1:].astype(jnp.int32)
        first = (((tidx == 0) | (tile_expert != jnp.roll(tile_expert, 1))) & (tidx < n_used[0])).astype(jnp.int32)
        slot = ((jnp.cumsum(first) - 1) % 2).astype(jnp.int32)
        end1 = end_of(tile_expert)
        exp1 = owner(end1)
        end2 = end_of(exp1)
        exp2 = owner(end2)
        has1 = end1 < n_used[0]
        nxt = jnp.stack([jnp.where(has1, exp1, -1),
                         jnp.where(has1 & (end2 < n_used[0]), exp2, -1)]).astype(jnp.int32)
        dest, wk = _dest_t(selr, wf, rank, pstart.astype(F32)[:, None])
        tt = TOK_TILE
        dest = dest.reshape(B, TOP_K, T // tt, tt).transpose(0, 2, 1, 3).reshape(N // tt, 1, TOP_K * tt)
        wk = wk.transpose(0, 2, 1).reshape(N, TOP_K)

        xs = _dispatch(h2p, dest, (pstart + cnt).astype(jnp.int32), (pcnt - cnt).astype(jnp.int32), R)
        y = _experts(xs, tile_expert, n_used, first, slot, nxt, w_gate[l], w_up[l], w_down[l])
        out = _final(h2p, w_sh_gate[l].astype(BF16), w_sh_up[l].astype(BF16), w_sh_down[l].astype(BF16),
                     x1.reshape(N, D), gate_f, norm_out, dest, wk, y, T)
        x = out.reshape(B, T, D)
    return x
```

```python
import functools

import jax
import jax.numpy as jnp
from jax import lax
from jax.experimental import pallas as pl
from jax.experimental.pallas import tpu as pltpu

MOBA_HEADS = 8
MOBA_HEAD_DIM = 128
MOBA_BLOCK = 256
MOBA_TOPK = 3
ROPE_THETA = 500000.0
ROPE_DIMS = 32
RET_HEADS = 4
RET_HEAD_DIM = 256
RET_ROPE_BASE = 10000.0
N_EXPERTS = 64
TOP_K = 8
N_GROUPS = 8
TOPK_GROUPS = 4
ROUTE_SCALE = 2.5
NORM_EPS = 1e-6
NEG = -1e30

LANES = 128
SUBLANES = 8
VMEM_LIMIT = 56 * 1024 * 1024

RET_CHUNK = 256
ROW_TILE = 256
TOK_TILE = 256

F32 = jnp.float32
BF16 = jnp.bfloat16
U32 = jnp.uint32


def _cparams(sem):
    return pltpu.CompilerParams(dimension_semantics=sem, vmem_limit_bytes=VMEM_LIMIT)


def _rms_mod(xf, g, shift, scale):
    ms = jnp.mean(xf * xf, axis=-1, keepdims=True)
    y = xf * lax.rsqrt(ms + NORM_EPS) * g
    return y * (1.0 + scale) + shift


def _pack_pair(lo, hi):
    lo_b = lax.bitcast_convert_type(lo.astype(BF16).astype(F32), U32)
    hi_b = lax.bitcast_convert_type(hi.astype(BF16).astype(F32), U32)
    return (lo_b >> 16) | hi_b


def _unpack_pair(p):
    lo = lax.bitcast_convert_type(p << 16, F32)
    hi = lax.bitcast_convert_type(p & jnp.uint32(0xFFFF0000), F32)
    return lo, hi


def _store_rows_as_tiles(ref, val, base=0):
    n = val.shape[0]
    for c in range(SUBLANES):
        ref[pl.ds(base * SUBLANES + c, n, stride=SUBLANES), :] = val[:, c * LANES:(c + 1) * LANES]


def _load_tiles_as_rows(ref, n, base=0):
    return jnp.concatenate([ref[pl.ds(base * SUBLANES + c, n, stride=SUBLANES), :] for c in range(SUBLANES)],
                           axis=1)


def _flat_tiles(ref):
    rows = 1
    for d in ref.shape[:-2]:
        rows *= d
    return ref.reshape(rows * SUBLANES, LANES)


def _tables_kernel(pos_ref, invm_ref, invr_ref, mc_ref, ms1_ref, ms2_ref, rc_ref, rs_ref):
    pos = pos_ref[0].astype(F32)
    angm = pos * invm_ref[...]
    lane = lax.broadcasted_iota(jnp.int32, angm.shape, 1)
    half = ROPE_DIMS // 2
    c = jnp.cos(angm)
    s = jnp.sin(angm)
    mc_ref[0] = c
    ms1_ref[0] = jnp.where(lane < half, -s, 0.0)
    ms2_ref[0] = jnp.where((lane >= half) & (lane < ROPE_DIMS), s, 0.0)
    angr = pos * invr_ref[...]
    rc_ref[0] = jnp.cos(angr)
    rs_ref[0] = jnp.sin(angr)


def _rope_tables(positions):
    B, T = positions.shape
    tm = 512
    half = ROPE_DIMS // 2
    moba_inv = ROPE_THETA ** (-(jnp.arange(half, dtype=F32) * 2.0 / ROPE_DIMS))
    invm = jnp.concatenate([moba_inv, moba_inv, jnp.zeros((LANES - ROPE_DIMS,), F32)])[None, :]
    invr = (RET_ROPE_BASE ** (-jnp.linspace(0.0, 1.0, RET_HEAD_DIM // 2, dtype=F32)))[None, :]
    pos3 = positions.reshape(B, T, 1)
    tab = jax.ShapeDtypeStruct((B, T, LANES), F32)
    spec = pl.BlockSpec((1, tm, LANES), lambda b, i: (b, i, 0))
    return pl.pallas_call(
        _tables_kernel,
        grid=(B, T // tm),
        in_specs=[pl.BlockSpec((1, tm, 1), lambda b, i: (b, i, 0)),
                  pl.BlockSpec((1, LANES), lambda b, i: (0, 0)),
                  pl.BlockSpec((1, LANES), lambda b, i: (0, 0))],
        out_specs=[spec] * 5,
        out_shape=[tab] * 5,
        compiler_params=_cparams(("parallel", "parallel")),
        name="rope_tables",
    )(pos3, invm, invr)


def _ada_kernel(ct_ref, w_ref, b_ref, o_ref):
    ct = ct_ref[...]
    sct = ct * jax.nn.sigmoid(ct)
    w = w_ref[...]
    for b in range(ct.shape[1]):
        o_ref[b:b + 1, :] = jnp.sum(w * sct[:, b:b + 1], axis=0, keepdims=True) + b_ref[...]


def _ada(c, w_ada, b_ada):
    B, D = c.shape
    n_out = w_ada.shape[1]
    tn = 1024
    return pl.pallas_call(
        _ada_kernel,
        grid=(n_out // tn,),
        in_specs=[pl.BlockSpec((D, B), lambda j: (0, 0)),
                  pl.BlockSpec((D, tn), lambda j: (0, j)),
                  pl.BlockSpec((1, tn), lambda j: (0, j))],
        out_specs=pl.BlockSpec((B, tn), lambda j: (0, j)),
        out_shape=jax.ShapeDtypeStruct((B, n_out), F32),
        compiler_params=_cparams(("parallel",)),
        name="adaln_mod",
    )(c.T, w_ada, b_ada[None, :])


def _inproj_kernel(x_ref, g_ref, sh_ref, sc_ref, w_ref, mc_ref, ms1_ref, ms2_ref, rc_ref, rs_ref,
                   o_ref, hn_ref, *, tn, moba_tiles, ret_lo, ret_k_lo, ret_hi):
    j = pl.program_id(2)

    @pl.when(j == 0)
    def _():
        h = _rms_mod(x_ref[0], g_ref[...], sh_ref[0], sc_ref[0])
        hn_ref[...] = h.astype(BF16)

    acc = jnp.dot(hn_ref[...], w_ref[...], preferred_element_type=F32)

    @pl.when(j < moba_tiles)
    def _():
        c, s1, s2 = mc_ref[0], ms1_ref[0], ms2_ref[0]
        half = ROPE_DIMS // 2
        for g in range(tn // LANES):
            a = acc[:, g * LANES:(g + 1) * LANES]
            r = a * c + pltpu.roll(a, LANES - half, 1) * s1 + pltpu.roll(a, half, 1) * s2
            o_ref[0, :, g * LANES:(g + 1) * LANES] = r.astype(o_ref.dtype)

    @pl.when((j >= ret_lo) & (j < ret_hi))
    def _():
        c, s = rc_ref[0], rs_ref[0]
        fac = jnp.where(j >= ret_k_lo, RET_HEAD_DIM ** -0.5, 1.0).astype(F32)
        hw = RET_HEAD_DIM // 2
        for g in range(tn // RET_HEAD_DIM):
            x1 = acc[:, g * RET_HEAD_DIM:g * RET_HEAD_DIM + hw]
            x2 = acc[:, g * RET_HEAD_DIM + hw:(g + 1) * RET_HEAD_DIM]
            o_ref[0, :, g * RET_HEAD_DIM:g * RET_HEAD_DIM + hw] = ((x1 * c - x2 * s) * fac).astype(o_ref.dtype)
            o_ref[0, :, g * RET_HEAD_DIM + hw:(g + 1) * RET_HEAD_DIM] = ((x2 * c + x1 * s) * fac).astype(o_ref.dtype)

    @pl.when(((j >= moba_tiles) & (j < ret_lo)) | (j >= ret_hi))
    def _():
        o_ref[0] = acc.astype(o_ref.dtype)


def _in_proj(x, g, shift, scale, w_bf, tabs, moba_w, ret_w):
    B, T, D = x.shape
    NC = w_bf.shape[1]
    tm, tn = 1024, 1024
    mc, ms1, ms2, rc, rs = tabs
    kern = functools.partial(
        _inproj_kernel, tn=tn,
        moba_tiles=2 * moba_w // tn,
        ret_lo=3 * moba_w // tn,
        ret_k_lo=(3 * moba_w + ret_w) // tn,
        ret_hi=(3 * moba_w + 2 * ret_w) // tn)
    tab_spec = pl.BlockSpec((1, tm, LANES), lambda b, i, j: (b, i, 0))
    vec_spec = pl.BlockSpec((1, 1, D), lambda b, i, j: (b, 0, 0))
    return pl.pallas_call(
        kern,
        grid=(B, T // tm, NC // tn),
        in_specs=[pl.BlockSpec((1, tm, D), lambda b, i, j: (b, i, 0)),
                  pl.BlockSpec((1, D), lambda b, i, j: (0, 0)),
                  vec_spec, vec_spec,
                  pl.BlockSpec((D, tn), lambda b, i, j: (0, j)),
                  tab_spec, tab_spec, tab_spec, tab_spec, tab_spec],
        out_specs=pl.BlockSpec((1, tm, tn), lambda b, i, j: (b, i, j)),
        out_shape=jax.ShapeDtypeStruct((B, T, NC), BF16),
        scratch_shapes=[pltpu.VMEM((tm, D), BF16)],
        compiler_params=_cparams(("parallel", "parallel", "arbitrary")),
        name="in_proj",
    )(x, g[None, :], shift[:, None, :], scale[:, None, :], w_bf, mc, ms1, ms2, rc, rs)


def _moba_kernel(q_ref, k_ref, v_ref, o_ref, km_ref, vt_ref, sel_ref, qs_ref, s_ref, m_ref, acc_ref,
                 *, nb, hp):
    qb = pl.program_id(2)
    BS, hd = MOBA_BLOCK, MOBA_HEAD_DIM

    @pl.when(qb == 0)
    def _():
        for h in range(hp):
            hs = slice(h * hd, (h + 1) * hd)
            for n in range(nb):
                kb = k_ref[0, n * BS:(n + 1) * BS, hs].astype(F32)
                km_ref[h, n:n + 1, :] = jnp.sum(kb, axis=0, keepdims=True) * (1.0 / BS)
                vt_ref[h, n, :hd, :] = v_ref[0, n * BS:(n + 1) * BS, hs].astype(F32).T.astype(BF16)
                vt_ref[h, n, hd:, :] = jnp.ones((vt_ref.shape[2] - hd, BS), BF16)

    scale = hd ** -0.5
    own = pl.multiple_of(qb * BS, BS)
    blk = lax.broadcasted_iota(jnp.int32, (nb, BS), 0)
    blk_f = blk.astype(F32)

    for h in range(hp):
        hs = slice(h * hd, (h + 1) * hd)
        qT = q_ref[0, :, hs].astype(F32).T
        gate = jnp.dot(km_ref[h], qT, preferred_element_type=F32,
                       precision=lax.Precision.HIGHEST)
        g = jnp.where(blk < qb, gate, NEG)
        sel = jnp.zeros((nb, BS), F32)
        for _ in range(MOBA_TOPK):
            m = jnp.max(g, axis=0, keepdims=True)
            idx = jnp.min(jnp.where(g == m, blk_f, float(nb)), axis=0, keepdims=True)
            pick = blk_f == idx
            sel = jnp.where(pick & (m > 0.5 * NEG), 1.0, sel)
            g = jnp.where(pick, -jnp.inf, g)
        sel_ref[h] = sel
        qs_ref[h] = (qT * scale).astype(BF16)
        m_ref[h] = jnp.full((1, BS), NEG, F32)

    n_pairs = lax.shift_right_logical(qb + 1, 1)

    def sweep_scores(j, _):
        for h in range(hp):
            mh = m_ref[h]
            for u in range(2):
                n = 2 * j + u
                off = pl.multiple_of(n * BS, BS)
                s = jnp.dot(k_ref[0, pl.ds(off, BS), h * hd:(h + 1) * hd], qs_ref[h],
                            preferred_element_type=F32)
                s = jnp.where(sel_ref[h, pl.ds(n, 1), :] > 0.0, s, NEG)
                s_ref[h, n] = s
                mh = jnp.maximum(mh, jnp.max(s, axis=0, keepdims=True))
            m_ref[h] = mh
        return 0

    lax.fori_loop(0, n_pairs, sweep_scores, 0)

    krow = lax.broadcasted_iota(jnp.int32, (BS, BS), 0)
    qcol = lax.broadcasted_iota(jnp.int32, (BS, BS), 1)
    for h in range(hp):
        hs = slice(h * hd, (h + 1) * hd)
        s = jnp.dot(k_ref[0, pl.ds(own, BS), hs], qs_ref[h], preferred_element_type=F32)
        s = jnp.where(krow <= qcol, s, NEG)
        m = jnp.maximum(m_ref[h], jnp.max(s, axis=0, keepdims=True))
        m_ref[h] = m
        p = jnp.exp((s - m).astype(BF16))
        acc_ref[h] = jnp.dot(vt_ref[h, qb], p, preferred_element_type=F32)

    def sweep_values(j, _):
        for h in range(hp):
            mh = m_ref[h]
            p0 = jnp.exp((s_ref[h, 2 * j] - mh).astype(BF16))
            p1 = jnp.exp((s_ref[h, 2 * j + 1] - mh).astype(BF16))
            acc_ref[h] = acc_ref[h] + (
                jnp.dot(vt_ref[h, 2 * j], p0, preferred_element_type=F32)
                + jnp.dot(vt_ref[h, 2 * j + 1], p1, preferred_element_type=F32))
        return 0

    lax.fori_loop(0, n_pairs, sweep_values, 0)

    for h in range(hp):
        acc = acc_ref[h]
        o_ref[0, :, h * hd:(h + 1) * hd] = (acc[:hd, :] / acc[hd:hd + 1, :]).T.astype(o_ref.dtype)


def _moba(proj, B, T, moba_w):
    H, hd, BS = MOBA_HEADS, MOBA_HEAD_DIM, MOBA_BLOCK
    nb = T // BS
    hp = 4
    ones_rows = 2 * SUBLANES
    gw = hp * hd
    gpw = moba_w // gw
    return pl.pallas_call(
        functools.partial(_moba_kernel, nb=nb, hp=hp),
        grid=(B, H // hp, nb),
        in_specs=[pl.BlockSpec((1, BS, gw), lambda b, h, i: (b, i, h)),
                  pl.BlockSpec((1, T, gw), lambda b, h, i: (b, 0, gpw + h)),
                  pl.BlockSpec((1, T, gw), lambda b, h, i: (b, 0, 2 * gpw + h))],
        out_specs=pl.BlockSpec((1, BS, gw), lambda b, h, i: (b, i, h)),
        out_shape=jax.ShapeDtypeStruct((B, T, moba_w), BF16),
        scratch_shapes=[pltpu.VMEM((hp, nb, hd), F32),
                        pltpu.VMEM((hp, nb, hd + ones_rows, BS), BF16),
                        pltpu.VMEM((hp, nb, BS), F32),
                        pltpu.VMEM((hp, hd, BS), BF16),
                        pltpu.VMEM((hp, nb, BS, BS), F32),
                        pltpu.VMEM((hp, 1, BS), F32),
                        pltpu.VMEM((hp, hd + ones_rows, BS), F32)],
        compiler_params=_cparams(("parallel", "parallel", "arbitrary")),
        name="moba_attn",
    )(proj, proj, proj)


def _ret_kernel(q_ref, k_ref, v_ref, g_ref, dm_ref, xi_ref, zeta_ref, cd_ref, o_ref, s_ref):
    c = pl.program_id(1)
    d = RET_HEAD_DIM

    @pl.when(c == 0)
    def _():
        s_ref[...] = jnp.zeros_like(s_ref)

    for h in range(RET_HEADS):
        sl = slice(h * d, (h + 1) * d)
        q = q_ref[0, :, sl]
        k = k_ref[0, :, sl]
        v = v_ref[0, :, sl]
        inner = lax.dot_general(q, k, (((1,), (1,)), ((), ())), preferred_element_type=F32) * dm_ref[h]
        S = s_ref[h]
        o = (jnp.dot(inner.astype(BF16), v, preferred_element_type=F32)
             + jnp.dot(q, S.astype(BF16), preferred_element_type=F32) * xi_ref[h])
        kz = (k.astype(F32) * zeta_ref[h]).astype(BF16)
        s_ref[h] = S * cd_ref[h] + lax.dot_general(kz, v, (((0,), (0,)), ((), ())),
                                                   preferred_element_type=F32)
        mu = jnp.mean(o, axis=-1, keepdims=True)
        dlt = o - mu
        var = jnp.mean(dlt * dlt, axis=-1, keepdims=True)
        on = dlt * lax.rsqrt(var + NORM_EPS)
        gg = g_ref[0, :, sl].astype(F32)
        o_ref[0, :, sl] = (on * (gg * jax.nn.sigmoid(gg))).astype(o_ref.dtype)


def _retention(proj, B, T, moba_w, ret_w):
    C, H = RET_CHUNK, RET_HEADS
    gamma = 1.0 - jnp.exp2(-5.0 - jnp.arange(H, dtype=F32))
    log_g = jnp.log(gamma)
    pos = jnp.arange(C, dtype=F32)
    diff = pos[:, None] - pos[None, :]
    dmask = jnp.where(diff >= 0, jnp.exp(jnp.maximum(diff, 0.0) * log_g[:, None, None]), 0.0)
    xi = jnp.exp((pos + 1.0) * log_g[:, None])[:, :, None]
    zeta = jnp.exp((C - 1.0 - pos) * log_g[:, None])[:, :, None]
    cd = jnp.exp(C * log_g)[:, None, None]
    base = 3 * moba_w // ret_w
    col = lambda off: pl.BlockSpec((1, C, ret_w), lambda b, c: (b, c, base + off))
    full = lambda shp: pl.BlockSpec(shp, lambda b, c: (0,) * len(shp))
    return pl.pallas_call(
        _ret_kernel,
        grid=(B, T // C),
        in_specs=[col(0), col(1), col(2), col(3),
                  full((H, C, C)), full((H, C, 1)), full((H, C, 1)), full((H, 1, 1))],
        out_specs=pl.BlockSpec((1, C, ret_w), lambda b, c: (b, c, 0)),
        out_shape=jax.ShapeDtypeStruct((B, T, ret_w), BF16),
        scratch_shapes=[pltpu.VMEM((H, RET_HEAD_DIM, RET_HEAD_DIM), F32)],
        compiler_params=_cparams(("parallel", "arbitrary")),
        name="retention",
    )(proj, proj, proj, proj, dmask, xi, zeta, cd)


def _outproj_kernel(oa_ref, or_ref, w_ref, x_ref, ga_ref, g_ref, sh_ref, sc_ref, wrh_ref, wrl_ref,
                    x1_ref, hp_ref, lg_ref, *, moba_w):
    mix = (jnp.dot(oa_ref[0], w_ref[:moba_w, :], preferred_element_type=F32)
           + jnp.dot(or_ref[0], w_ref[moba_w:, :], preferred_element_type=F32))
    x1 = x_ref[0] + ga_ref[0] * mix
    x1_ref[0] = x1
    h = _rms_mod(x1, g_ref[...], sh_ref[0], sc_ref[0])
    half = h.shape[-1] // 2
    _store_rows_as_tiles(_flat_tiles(hp_ref), _pack_pair(h[:, :half], h[:, half:]))
    h_hi = h.astype(BF16)
    h_lo = (h - h_hi.astype(F32)).astype(BF16)
    lg_ref[0] = (jnp.dot(h_hi, wrh_ref[...], preferred_element_type=F32)
                 + (jnp.dot(h_lo, wrh_ref[...], preferred_element_type=F32)
                    + jnp.dot(h_hi, wrl_ref[...], preferred_element_type=F32)))


def _out_proj(o_a, o_r, w_bf, x, gate_a, g, shift, scale, w_router):
    B, T, D = x.shape
    moba_w, ret_w = o_a.shape[-1], o_r.shape[-1]
    E = w_router.shape[1]
    tm = 256
    wr_hi = w_router.astype(BF16)
    wr_lo = (w_router - wr_hi.astype(F32)).astype(BF16)
    vec = pl.BlockSpec((1, 1, D), lambda b, i: (b, 0, 0))
    row = lambda w: pl.BlockSpec((1, tm, w), lambda b, i: (b, i, 0))
    return pl.pallas_call(
        functools.partial(_outproj_kernel, moba_w=moba_w),
        grid=(B, T // tm),
        in_specs=[row(moba_w), row(ret_w),
                  pl.BlockSpec((moba_w + ret_w, D), lambda b, i: (0, 0)),
                  row(D), vec,
                  pl.BlockSpec((1, D), lambda b, i: (0, 0)),
                  vec, vec,
                  pl.BlockSpec((D, E), lambda b, i: (0, 0)),
                  pl.BlockSpec((D, E), lambda b, i: (0, 0))],
        out_specs=[row(D), pl.BlockSpec((1, tm, SUBLANES, LANES), lambda b, i: (b, i, 0, 0)), row(E)],
        out_shape=[jax.ShapeDtypeStruct((B, T, D), F32),
                   jax.ShapeDtypeStruct((B, T, SUBLANES, LANES), U32),
                   jax.ShapeDtypeStruct((B, T, E), F32)],
        compiler_params=_cparams(("parallel", "parallel")),
        name="out_proj",
    )(o_a, o_r, w_bf, x, gate_a[:, None, :], g[None, :], shift[:, None, :], scale[:, None, :], wr_hi, wr_lo)


def _route_kernel_t(lg_ref, b_ref, selr_ref, wf_ref, rank_ref, cnt_ref, carry_ref):
    @pl.when((pl.program_id(0) == 0) & (pl.program_id(1) == 0))
    def _():
        carry_ref[...] = jnp.zeros_like(carry_ref)

    E = N_EXPERTS
    gsz = E // N_GROUPS
    assert gsz == SUBLANES and N_GROUPS == SUBLANES, "a routing group is one sublane tile of experts"
    s = jax.nn.sigmoid(lg_ref[0].T)
    biased = s + b_ref[...]
    tm = s.shape[1]
    sub = lax.broadcasted_iota(jnp.int32, (SUBLANES, tm), 0).astype(F32)
    eid = lax.broadcasted_iota(jnp.int32, (E, tm), 0).astype(F32)

    def first_argmax(v, ids, width):
        m = jnp.max(v, axis=0, keepdims=True)
        idx = jnp.min(jnp.where(v == m, ids, float(width)), axis=0, keepdims=True)
        return m, idx

    gscore = jnp.zeros((N_GROUPS, tm), F32)
    for gi in range(N_GROUPS):
        v = biased[gi * gsz:(gi + 1) * gsz, :]
        m1, i1 = first_argmax(v, sub, gsz)
        m2 = jnp.max(jnp.where(sub == i1, -jnp.inf, v), axis=0, keepdims=True)
        gscore = jnp.where(sub == float(gi), m1 + m2, gscore)

    gsel = jnp.zeros((N_GROUPS, tm), F32)
    for _ in range(TOPK_GROUPS):
        _, gi = first_argmax(gscore, sub, N_GROUPS)
        pick = sub == gi
        gsel = jnp.where(pick, 1.0, gsel)
        gscore = jnp.where(pick, -jnp.inf, gscore)

    cand = jnp.concatenate(
        [jnp.where(gsel[gi:gi + 1, :] > 0.0, biased[gi * gsz:(gi + 1) * gsz, :], NEG) for gi in range(N_GROUPS)],
        axis=0)
    selr = jnp.zeros((E, tm), F32)
    for r in range(TOP_K):
        _, ei = first_argmax(cand, eid, E)
        pick = eid == ei
        selr = jnp.where(pick, float(r + 1), selr)
        cand = jnp.where(pick, -jnp.inf, cand)

    chosen = selr > 0.0
    w = jnp.where(chosen, s, 0.0)
    wsum = jnp.sum(w, axis=0, keepdims=True)
    selr_ref[0] = selr
    wf_ref[0] = w / wsum * ROUTE_SCALE

    onehot = chosen.astype(BF16)
    c_i = lax.broadcasted_iota(jnp.int32, (tm, tm), 0)
    r_i = lax.broadcasted_iota(jnp.int32, (tm, tm), 1)
    tri = (c_i < r_i).astype(BF16)
    carry = carry_ref[...]
    rank_ref[0] = jnp.dot(onehot, tri, preferred_element_type=F32) + carry
    carry = carry + jnp.sum(chosen.astype(F32), axis=1, keepdims=True)
    carry_ref[...] = carry
    cnt_ref[...] = carry


def _route_t(logits_t, bias):
    B, T, E = logits_t.shape
    tm = 512
    blk = pl.BlockSpec((1, E, tm), lambda b, i: (b, 0, i))
    col = pl.BlockSpec((E, 1), lambda b, i: (0, 0))
    full = jax.ShapeDtypeStruct((B, E, T), F32)
    return pl.pallas_call(
        _route_kernel_t,
        grid=(B, T // tm),
        in_specs=[pl.BlockSpec((1, tm, E), lambda b, i: (b, i, 0)), col],
        out_specs=[blk, blk, blk, col],
        out_shape=[full, full, full, jax.ShapeDtypeStruct((E, 1), F32)],
        scratch_shapes=[pltpu.VMEM((E, 1), F32)],
        compiler_params=_cparams(("arbitrary", "arbitrary")),
        name="route_topk",
    )(logits_t, bias[:, None])


def _dest_kernel_t(selr_ref, wf_ref, rank_ref, ps_ref, dest_ref, wk_ref):
    selr = selr_ref[0]
    destfull = rank_ref[0] + ps_ref[...]
    wf = wf_ref[0]
    for r in range(TOP_K):
        hit = selr == float(r + 1)
        dest_ref[0, r:r + 1, :] = jnp.sum(jnp.where(hit, destfull, 0.0), axis=0, keepdims=True).astype(jnp.int32)
        wk_ref[0, r:r + 1, :] = jnp.sum(jnp.where(hit, wf, 0.0), axis=0, keepdims=True)


def _dest_t(selr, wf, rank, pstart_f):
    B, E, T = selr.shape
    tm = 512
    blk = pl.BlockSpec((1, E, tm), lambda b, i: (b, 0, i))
    outb = pl.BlockSpec((1, TOP_K, tm), lambda b, i: (b, 0, i))
    return pl.pallas_call(
        _dest_kernel_t,
        grid=(B, T // tm),
        in_specs=[blk, blk, blk, pl.BlockSpec((E, 1), lambda b, i: (0, 0))],
        out_specs=[outb, outb],
        out_shape=[jax.ShapeDtypeStruct((B, TOP_K, T), jnp.int32), jax.ShapeDtypeStruct((B, TOP_K, T), F32)],
        compiler_params=_cparams(("parallel", "parallel")),
        name="route_dest",
    )(selr, wf, rank, pstart_f)


def _row_copy(src, s_row, dst, d_row, n, sem):
    return pltpu.make_async_copy(src.at[pl.ds(s_row, n)], dst.at[pl.ds(d_row, n)], sem)


def _dispatch_kernel(padlo_ref, padn_ref, dest_ref, h_ref, z_ref, xs_ref, sem, zsem, *, tt, n_exp):
    i = pl.program_id(0)

    def issue(t, _):
        for k in range(TOP_K):
            _row_copy(h_ref, t, xs_ref, dest_ref[0, 0, k * tt + t], 1, sem).start(priority=k % 2)
        return 0

    lax.fori_loop(0, tt, issue, 0, unroll=2)

    def each_pad(fn):
        def per_expert(e, _):
            lo = padlo_ref[e]

            def one(r, _):
                fn(lo + r)
                return 0

            lax.fori_loop(0, padn_ref[e], one, 0)
            return 0

        lax.fori_loop(0, n_exp, per_expert, 0)

    @pl.when(i == 0)
    def _():
        each_pad(lambda r: _row_copy(z_ref, 0, xs_ref, r, 1, zsem).start())

    for k in range(TOP_K):
        _row_copy(h_ref, 0, xs_ref, 0, tt, sem).wait()

    @pl.when(i == 0)
    def _():
        each_pad(lambda r: _row_copy(z_ref, 0, xs_ref, 0, 1, zsem).wait())


def _dispatch(h2p, dest, pad_lo, pad_n, R):
    N = h2p.shape[0]
    tile = h2p.shape[1:]
    tt = TOK_TILE
    dest3 = dest
    zeros = jnp.zeros((SUBLANES,) + tile, h2p.dtype)
    grid_spec = pltpu.PrefetchScalarGridSpec(
        num_scalar_prefetch=2,
        grid=(N // tt,),
        in_specs=[pl.BlockSpec((1, 1, tt * TOP_K), lambda i, lo, n: (i, 0, 0), memory_space=pltpu.SMEM),
                  pl.BlockSpec((tt,) + tile, lambda i, lo, n: (i, 0, 0)),
                  pl.BlockSpec((SUBLANES,) + tile, lambda i, lo, n: (0, 0, 0))],
        out_specs=pl.BlockSpec(memory_space=pl.ANY),
        scratch_shapes=[pltpu.SemaphoreType.DMA(()), pltpu.SemaphoreType.DMA(())],
    )
    return pl.pallas_call(
        functools.partial(_dispatch_kernel, tt=tt, n_exp=N_EXPERTS),
        grid_spec=grid_spec,
        out_shape=jax.ShapeDtypeStruct((R,) + tile, h2p.dtype),
        compiler_params=_cparams(("arbitrary",)),
        name="moe_dispatch",
    )(pad_lo, pad_n, dest3, h2p, zeros)


def _expert_kernel(te_ref, nu_ref, first_ref, slot_ref, nxt_ref, xs_ref, wg_hbm, wu_hbm, wd_hbm, y_ref,
                   stg, stu, std, wgb, wub, wdb, sem):
    i = pl.program_id(0)
    n_chunks = 2

    def fetch(e, s):
        cps = []
        for m, (src, dst) in enumerate(((wg_hbm, stg), (wu_hbm, stu), (wd_hbm, std))):
            rows = dst.shape[1] // n_chunks
            for c in range(n_chunks):
                cps.append(pltpu.make_async_copy(src.at[e, pl.ds(c * rows, rows)],
                                                 dst.at[s, pl.ds(c * rows, rows)], sem.at[s, m]))
        return cps

    @pl.when(i == 0)
    def _():
        for cp in fetch(te_ref[0], 0):
            cp.start()

        @pl.when(nxt_ref[0, 0] >= 0)
        def _():
            for cp in fetch(nxt_ref[0, 0], 1):
                cp.start()

    active = i < nu_ref[0]

    def swiglu_tile(wg, wu, wd):
        m_rows = xs_ref.shape[0]
        lo, hi = _unpack_pair(_load_tiles_as_rows(_flat_tiles(xs_ref), m_rows))
        lo = lo.astype(BF16)
        hi = hi.astype(BF16)
        half = lo.shape[-1]
        a = (jnp.dot(lo, wg[:half, :], preferred_element_type=F32)
             + jnp.dot(hi, wg[half:, :], preferred_element_type=F32))
        u = (jnp.dot(lo, wu[:half, :], preferred_element_type=F32)
             + jnp.dot(hi, wu[half:, :], preferred_element_type=F32))
        hmid = (a * jax.nn.sigmoid(a) * u).astype(BF16)
        y = jnp.dot(hmid, wd, preferred_element_type=F32)
        _store_rows_as_tiles(_flat_tiles(y_ref), _pack_pair(y[:, :half], y[:, half:]))

    @pl.when(active & (first_ref[i] == 1))
    def _():
        s = slot_ref[i]
        for cp in fetch(0, s):
            cp.wait()
        wgb[...] = stg[s].astype(BF16)
        wub[...] = stu[s].astype(BF16)
        wdb[...] = std[s].astype(BF16)

        @pl.when(nxt_ref[1, i] >= 0)
        def _():
            for cp in fetch(nxt_ref[1, i], s):
                cp.start()

    @pl.when(active)
    def _():
        swiglu_tile(wgb[...], wub[...], wdb[...])


def _experts(xs, tile_expert, n_used, first, slot, nxt, wg, wu, wd):
    R = xs.shape[0]
    tile = xs.shape[1:]
    M = ROW_TILE
    _, D, F = wg.shape
    row = lambda i, te, nu, fi, sl, nx: (jnp.minimum(i, nu[0] - 1), 0, 0)
    grid_spec = pltpu.PrefetchScalarGridSpec(
        num_scalar_prefetch=5,
        grid=(R // M,),
        in_specs=[pl.BlockSpec((M,) + tile, row),
                  pl.BlockSpec(memory_space=pl.ANY),
                  pl.BlockSpec(memory_space=pl.ANY),
                  pl.BlockSpec(memory_space=pl.ANY)],
        out_specs=pl.BlockSpec((M,) + tile, row),
        scratch_shapes=[pltpu.VMEM((2, D, F), F32), pltpu.VMEM((2, D, F), F32), pltpu.VMEM((2, F, D), F32),
                        pltpu.VMEM((D, F), BF16), pltpu.VMEM((D, F), BF16), pltpu.VMEM((F, D), BF16),
                        pltpu.SemaphoreType.DMA((2, 3))],
    )
    return pl.pallas_call(
        _expert_kernel,
        grid_spec=grid_spec,
        out_shape=jax.ShapeDtypeStruct((R,) + tile, U32),
        compiler_params=_cparams(("arbitrary",)),
        name="moe_experts",
    )(tile_expert, n_used, first, slot, nxt, xs, wg, wu, wd)


def _final_kernel(dcur_ref, dnxt_ref, h_ref, wsg_ref, wsu_ref, wsd_ref, x1_ref, gf_ref, g_ref, wk_ref, y_ref,
                  o_ref, ybuf, sem, *, tt, n_tiles):
    i = pl.program_id(0)
    slot = lax.rem(i, 2)

    def gather(d_ref, s):
        def issue(t, _):
            for k in range(TOP_K):
                pltpu.make_async_copy(y_ref.at[pl.ds(d_ref[0, 0, k * tt + t], 1)],
                                      ybuf.at[s, k, pl.ds(t, 1)], sem.at[s]).start(priority=k % 2)
            return 0

        lax.fori_loop(0, tt, issue, 0, unroll=2)

    @pl.when(i == 0)
    def _():
        gather(dcur_ref, 0)

    for p in range(2):
        @pl.when((i + 1 < n_tiles) & (slot == p))
        def _(p=p):
            gather(dnxt_ref, 1 - p)

    lo, hi = _unpack_pair(_load_tiles_as_rows(_flat_tiles(h_ref), tt))
    lo = lo.astype(BF16)
    hi = hi.astype(BF16)
    half = lo.shape[-1]
    a = (jnp.dot(lo, wsg_ref[:half, :], preferred_element_type=F32)
         + jnp.dot(hi, wsg_ref[half:, :], preferred_element_type=F32))
    u = (jnp.dot(lo, wsu_ref[:half, :], preferred_element_type=F32)
         + jnp.dot(hi, wsu_ref[half:, :], preferred_element_type=F32))
    hmid = (a * jax.nn.sigmoid(a) * u).astype(BF16)
    shared = jnp.dot(hmid, wsd_ref[...], preferred_element_type=F32)

    for k in range(TOP_K):
        pltpu.make_async_copy(y_ref.at[pl.ds(0, tt)], ybuf.at[slot, k], sem.at[slot]).wait()

    wk = wk_ref[...]
    yflat = _flat_tiles(ybuf)

    for p in range(2):
        @pl.when(slot == p)
        def _(p=p):
            r_lo = jnp.zeros((tt, half), F32)
            r_hi = jnp.zeros((tt, half), F32)
            for k in range(TOP_K):
                ylo, yhi = _unpack_pair(_load_tiles_as_rows(yflat, tt, base=(p * TOP_K + k) * tt))
                wcol = wk[:, k:k + 1]
                r_lo = r_lo + wcol * ylo
                r_hi = r_hi + wcol * yhi
            total = shared + jnp.concatenate([r_lo, r_hi], axis=1)
            x2 = x1_ref[...] + gf_ref[0] * total
            ms = jnp.mean(x2 * x2, axis=-1, keepdims=True)
            o_ref[...] = x2 * lax.rsqrt(ms + NORM_EPS) * g_ref[...]


def _final(h2p, wsg, wsu, wsd, x1, gate_f, norm_out, dest, wk, y, T):
    N = h2p.shape[0]
    tile = h2p.shape[1:]
    D = x1.shape[-1]
    F = wsg.shape[1]
    tt = TOK_TILE
    per_b = T // tt
    n_tiles = N // tt
    dest3 = dest
    rowb = lambda w: pl.BlockSpec((tt, w), lambda i: (i, 0))
    const = lambda shp: pl.BlockSpec(shp, lambda i: (0,) * len(shp))
    dspec = lambda f: pl.BlockSpec((1, 1, tt * TOP_K), f, memory_space=pltpu.SMEM)
    return pl.pallas_call(
        functools.partial(_final_kernel, tt=tt, n_tiles=n_tiles),
        grid=(n_tiles,),
        in_specs=[dspec(lambda i: (i, 0, 0)),
                  dspec(lambda i: (jnp.minimum(i + 1, n_tiles - 1), 0, 0)),
                  pl.BlockSpec((tt,) + tile, lambda i: (i, 0, 0)),
                  const((D, F)), const((D, F)), const((F, D)), rowb(D),
                  pl.BlockSpec((1, 1, D), lambda i: (i // per_b, 0, 0)),
                  const((1, D)),
                  pl.BlockSpec((tt, TOP_K), lambda i: (i, 0)),
                  pl.BlockSpec(memory_space=pl.ANY)],
        out_specs=rowb(D),
        out_shape=jax.ShapeDtypeStruct((N, D), F32),
        scratch_shapes=[pltpu.VMEM((2, TOP_K, tt) + tile, U32), pltpu.SemaphoreType.DMA((2,))],
        compiler_params=_cparams(("arbitrary",)),
        name="moe_combine_final",
    )(dest3, dest3, h2p, wsg, wsu, wsd, x1, gate_f[:, None, :], norm_out[None, :], wk, y)


def kernel(x, c, positions, w_ada, b_ada, norm_mix, norm_ffn, norm_out, w_in, w_out, w_router, router_bias,
           w_gate, w_up, w_down, w_sh_gate, w_sh_up, w_sh_down):
    B, T, D = x.shape
    depth = w_ada.shape[0]
    assert depth == 1, "the final rmsnorm is fused into the layer's last kernel"
    moba_w = MOBA_HEADS * MOBA_HEAD_DIM
    ret_w = RET_HEADS * RET_HEAD_DIM
    N = B * T
    E, M = N_EXPERTS, ROW_TILE
    R = N * TOP_K + E * M
    n_tiles = R // M
    tabs = _rope_tables(positions)

    for l in range(depth):
        mod = _ada(c, w_ada[l], b_ada[l])
        shift_a, scale_a, gate_a, shift_f, scale_f, gate_f = jnp.split(mod, 6, axis=-1)

        proj = _in_proj(x, norm_mix[l], shift_a, scale_a, w_in[l].astype(BF16), tabs, moba_w, ret_w)
        o_a = _moba(proj, B, T, moba_w)
        o_r = _retention(proj, B, T, moba_w, ret_w)
        x1, h2p, logits = _out_proj(o_a, o_r, w_out[l].astype(BF16), x, gate_a, norm_ffn[l],
                                    shift_f, scale_f, w_router[l])

        assert D // 2 == SUBLANES * LANES, "a packed row must fill exactly one (SUBLANES, LANES) tile"
        h2p = h2p.reshape(N, SUBLANES, LANES)
        selr, wf, rank, counts = _route_t(logits, router_bias[l])
        cnt = counts[:, 0].astype(jnp.int32)
        pcnt = (cnt + M - 1) // M * M
        pend = jnp.cumsum(pcnt)
        pstart = pend - pcnt
        tidx = jnp.arange(n_tiles, dtype=jnp.int32)
        end_tile = pend // M
        eids = jnp.arange(E, dtype=jnp.int32)
        owner = lambda v: jnp.minimum(jnp.sum(end_tile[None, :] <= v[:, None], axis=1), E - 1).astype(jnp.int32)
        end_of = lambda e: jnp.sum(jnp.where(e[:, None] == eids[None, :], end_tile[None, :], 0), axis=1)
        tile_expert = owner(tidx)
        n_used = end_tile[-1:].astype(jnp.int32)
        first = (((tidx == 0) | (tile_expert != jnp.roll(tile_expert, 1))) & (tidx < n_used[0])).astype(jnp.int32)
        slot = ((jnp.cumsum(first) - 1) % 2).astype(jnp.int32)
        end1 = end_of(tile_expert)
        exp1 = owner(end1)
        end2 = end_of(exp1)
        exp2 = owner(end2)
        has1 = end1 < n_used[0]
        nxt = jnp.stack([jnp.where(has1, exp1, -1),
                         jnp.where(has1 & (end2 < n_used[0]), exp2, -1)]).astype(jnp.int32)
        dest, wk = _dest_t(selr, wf, rank, pstart.astype(F32)[:, None])
        tt = TOK_TILE
        dest = dest.reshape(B, TOP_K, T // tt, tt).transpose(0, 2, 1, 3).reshape(N // tt, 1, TOP_K * tt)
        wk = wk.transpose(0, 2, 1).reshape(N, TOP_K)

        xs = _dispatch(h2p, dest, (pstart + cnt).astype(jnp.int32), (pcnt - cnt).astype(jnp.int32), R)
        y = _experts(xs, tile_expert, n_used, first, slot, nxt, w_gate[l], w_up[l], w_down[l])
        out = _final(h2p, w_sh_gate[l].astype(BF16), w_sh_up[l].astype(BF16), w_sh_down[l].astype(BF16),
                     x1.reshape(N, D), gate_f, norm_out, dest, wk, y, T)
        x = out.reshape(B, T, D)
    return x
```

```python
import functools

import jax
import jax.numpy as jnp
from jax import lax
from jax.experimental import pallas as pl
from jax.experimental.pallas import tpu as pltpu

MOBA_HEADS = 8
MOBA_HEAD_DIM = 128
MOBA_BLOCK = 256
MOBA_TOPK = 3
ROPE_THETA = 500000.0
ROPE_DIMS = 32
RET_HEADS = 4
RET_HEAD_DIM = 256
RET_ROPE_BASE = 10000.0
N_EXPERTS = 64
TOP_K = 8
N_GROUPS = 8
TOPK_GROUPS = 4
ROUTE_SCALE = 2.5
NORM_EPS = 1e-6
NEG = -1e30

LANES = 128
SUBLANES = 8
VMEM_LIMIT = 56 * 1024 * 1024

RET_CHUNK = 256
ROW_TILE = 256
TOK_TILE = 256

F32 = jnp.float32
BF16 = jnp.bfloat16
U32 = jnp.uint32


def _cparams(sem):
    return pltpu.CompilerParams(dimension_semantics=sem, vmem_limit_bytes=VMEM_LIMIT)


def _rms_mod(xf, g, shift, scale):
    ms = jnp.mean(xf * xf, axis=-1, keepdims=True)
    y = xf * lax.rsqrt(ms + NORM_EPS) * g
    return y * (1.0 + scale) + shift


def _pack_pair(lo, hi):
    lo_b = lax.bitcast_convert_type(lo.astype(BF16).astype(F32), U32)
    hi_b = lax.bitcast_convert_type(hi.astype(BF16).astype(F32), U32)
    return (lo_b >> 16) | hi_b


def _unpack_pair(p):
    lo = lax.bitcast_convert_type(p << 16, F32)
    hi = lax.bitcast_convert_type(p & jnp.uint32(0xFFFF0000), F32)
    return lo, hi


def _store_rows_as_tiles(ref, val, base=0):
    n = val.shape[0]
    for c in range(SUBLANES):
        ref[pl.ds(base * SUBLANES + c, n, stride=SUBLANES), :] = val[:, c * LANES:(c + 1) * LANES]


def _load_tiles_as_rows(ref, n, base=0):
    return jnp.concatenate([ref[pl.ds(base * SUBLANES + c, n, stride=SUBLANES), :] for c in range(SUBLANES)],
                           axis=1)


def _flat_tiles(ref):
    rows = 1
    for d in ref.shape[:-2]:
        rows *= d
    return ref.reshape(rows * SUBLANES, LANES)


def _tables_kernel(pos_ref, invm_ref, invr_ref, mc_ref, ms1_ref, ms2_ref, rc_ref, rs_ref):
    pos = pos_ref[0].astype(F32)
    angm = pos * invm_ref[...]
    lane = lax.broadcasted_iota(jnp.int32, angm.shape, 1)
    half = ROPE_DIMS // 2
    c = jnp.cos(angm)
    s = jnp.sin(angm)
    mc_ref[0] = c
    ms1_ref[0] = jnp.where(lane < half, -s, 0.0)
    ms2_ref[0] = jnp.where((lane >= half) & (lane < ROPE_DIMS), s, 0.0)
    angr = pos * invr_ref[...]
    rc_ref[0] = jnp.cos(angr)
    rs_ref[0] = jnp.sin(angr)


def _rope_tables(positions):
    B, T = positions.shape
    tm = 512
    half = ROPE_DIMS // 2
    moba_inv = ROPE_THETA ** (-(jnp.arange(half, dtype=F32) * 2.0 / ROPE_DIMS))
    invm = jnp.concatenate([moba_inv, moba_inv, jnp.zeros((LANES - ROPE_DIMS,), F32)])[None, :]
    invr = (RET_ROPE_BASE ** (-jnp.linspace(0.0, 1.0, RET_HEAD_DIM // 2, dtype=F32)))[None, :]
    pos3 = positions.reshape(B, T, 1)
    tab = jax.ShapeDtypeStruct((B, T, LANES), F32)
    spec = pl.BlockSpec((1, tm, LANES), lambda b, i: (b, i, 0))
    return pl.pallas_call(
        _tables_kernel,
        grid=(B, T // tm),
        in_specs=[pl.BlockSpec((1, tm, 1), lambda b, i: (b, i, 0)),
                  pl.BlockSpec((1, LANES), lambda b, i: (0, 0)),
                  pl.BlockSpec((1, LANES), lambda b, i: (0, 0))],
        out_specs=[spec] * 5,
        out_shape=[tab] * 5,
        compiler_params=_cparams(("parallel", "parallel")),
        name="rope_tables",
    )(pos3, invm, invr)


def _ada_kernel(ct_ref, w_ref, b_ref, o_ref):
    ct = ct_ref[...]
    sct = ct * jax.nn.sigmoid(ct)
    w = w_ref[...]
    for b in range(ct.shape[1]):
        o_ref[b:b + 1, :] = jnp.sum(w * sct[:, b:b + 1], axis=0, keepdims=True) + b_ref[...]


def _ada(c, w_ada, b_ada):
    B, D = c.shape
    n_out = w_ada.shape[1]
    tn = 1024
    return pl.pallas_call(
        _ada_kernel,
        grid=(n_out // tn,),
        in_specs=[pl.BlockSpec((D, B), lambda j: (0, 0)),
                  pl.BlockSpec((D, tn), lambda j: (0, j)),
                  pl.BlockSpec((1, tn), lambda j: (0, j))],
        out_specs=pl.BlockSpec((B, tn), lambda j: (0, j)),
        out_shape=jax.ShapeDtypeStruct((B, n_out), F32),
        compiler_params=_cparams(("parallel",)),
        name="adaln_mod",
    )(c.T, w_ada, b_ada[None, :])


def _inproj_kernel(x_ref, g_ref, sh_ref, sc_ref, w_ref, mc_ref, ms1_ref, ms2_ref, rc_ref, rs_ref,
                   o_ref, hn_ref, *, tn, moba_tiles, ret_lo, ret_k_lo, ret_hi):
    j = pl.program_id(2)

    @pl.when(j == 0)
    def _():
        h = _rms_mod(x_ref[0], g_ref[...], sh_ref[0], sc_ref[0])
        hn_ref[...] = h.astype(BF16)

    acc = jnp.dot(hn_ref[...], w_ref[...], preferred_element_type=F32)

    @pl.when(j < moba_tiles)
    def _():
        c, s1, s2 = mc_ref[0], ms1_ref[0], ms2_ref[0]
        half = ROPE_DIMS // 2
        for g in range(tn // LANES):
            a = acc[:, g * LANES:(g + 1) * LANES]
            r = a * c + pltpu.roll(a, LANES - half, 1) * s1 + pltpu.roll(a, half, 1) * s2
            o_ref[0, :, g * LANES:(g + 1) * LANES] = r.astype(o_ref.dtype)

    @pl.when((j >= ret_lo) & (j < ret_hi))
    def _():
        c, s = rc_ref[0], rs_ref[0]
        fac = jnp.where(j >= ret_k_lo, RET_HEAD_DIM ** -0.5, 1.0).astype(F32)
        hw = RET_HEAD_DIM // 2
        for g in range(tn // RET_HEAD_DIM):
            x1 = acc[:, g * RET_HEAD_DIM:g * RET_HEAD_DIM + hw]
            x2 = acc[:, g * RET_HEAD_DIM + hw:(g + 1) * RET_HEAD_DIM]
            o_ref[0, :, g * RET_HEAD_DIM:g * RET_HEAD_DIM + hw] = ((x1 * c - x2 * s) * fac).astype(o_ref.dtype)
            o_ref[0, :, g * RET_HEAD_DIM + hw:(g + 1) * RET_HEAD_DIM] = ((x2 * c + x1 * s) * fac).astype(o_ref.dtype)

    @pl.when(((j >= moba_tiles) & (j < ret_lo)) | (j >= ret_hi))
    def _():
        o_ref[0] = acc.astype(o_ref.dtype)


def _in_proj(x, g, shift, scale, w_bf, tabs, moba_w, ret_w):
    B, T, D = x.shape
    NC = w_bf.shape[1]
    tm, tn = 1024, 1024
    mc, ms1, ms2, rc, rs = tabs
    kern = functools.partial(
        _inproj_kernel, tn=tn,
        moba_tiles=2 * moba_w // tn,
        ret_lo=3 * moba_w // tn,
        ret_k_lo=(3 * moba_w + ret_w) // tn,
        ret_hi=(3 * moba_w + 2 * ret_w) // tn)
    tab_spec = pl.BlockSpec((1, tm, LANES), lambda b, i, j: (b, i, 0))
    vec_spec = pl.BlockSpec((1, 1, D), lambda b, i, j: (b, 0, 0))
    return pl.pallas_call(
        kern,
        grid=(B, T // tm, NC // tn),
        in_specs=[pl.BlockSpec((1, tm, D), lambda b, i, j: (b, i, 0)),
                  pl.BlockSpec((1, D), lambda b, i, j: (0, 0)),
                  vec_spec, vec_spec,
                  pl.BlockSpec((D, tn), lambda b, i, j: (0, j)),
                  tab_spec, tab_spec, tab_spec, tab_spec, tab_spec],
        out_specs=pl.BlockSpec((1, tm, tn), lambda b, i, j: (b, i, j)),
        out_shape=jax.ShapeDtypeStruct((B, T, NC), BF16),
        scratch_shapes=[pltpu.VMEM((tm, D), BF16)],
        compiler_params=_cparams(("parallel", "parallel", "arbitrary")),
        name="in_proj",
    )(x, g[None, :], shift[:, None, :], scale[:, None, :], w_bf, mc, ms1, ms2, rc, rs)


def _moba_kernel(q_ref, k_ref, v_ref, o_ref, km_ref, vt_ref, sel_ref, qs_ref, s_ref, m_ref, acc_ref,
                 *, nb, hp):
    qb = pl.program_id(2)
    BS, hd = MOBA_BLOCK, MOBA_HEAD_DIM

    @pl.when(qb == 0)
    def _():
        for h in range(hp):
            hs = slice(h * hd, (h + 1) * hd)
            for n in range(nb):
                kb = k_ref[0, n * BS:(n + 1) * BS, hs].astype(F32)
                km_ref[h, n:n + 1, :] = jnp.sum(kb, axis=0, keepdims=True) * (1.0 / BS)
                vt_ref[h, n, :hd, :] = v_ref[0, n * BS:(n + 1) * BS, hs].astype(F32).T.astype(BF16)
                vt_ref[h, n, hd:, :] = jnp.ones((vt_ref.shape[2] - hd, BS), BF16)

    scale = hd ** -0.5
    own = pl.multiple_of(qb * BS, BS)
    blk = lax.broadcasted_iota(jnp.int32, (nb, BS), 0)
    blk_f = blk.astype(F32)

    for h in range(hp):
        hs = slice(h * hd, (h + 1) * hd)
        qT = q_ref[0, :, hs].astype(F32).T
        gate = jnp.dot(km_ref[h], qT, preferred_element_type=F32,
                       precision=lax.Precision.HIGHEST)
        g = jnp.where(blk < qb, gate, NEG)
        sel = jnp.zeros((nb, BS), F32)
        for _ in range(MOBA_TOPK):
            m = jnp.max(g, axis=0, keepdims=True)
            idx = jnp.min(jnp.where(g == m, blk_f, float(nb)), axis=0, keepdims=True)
            pick = blk_f == idx
            sel = jnp.where(pick & (m > 0.5 * NEG), 1.0, sel)
            g = jnp.where(pick, -jnp.inf, g)
        sel_ref[h] = sel
        qs_ref[h] = (qT * scale).astype(BF16)
        m_ref[h] = jnp.full((1, BS), NEG, F32)

    n_pairs = lax.shift_right_logical(qb + 1, 1)

    def sweep_scores(j, _):
        for h in range(hp):
            mh = m_ref[h]
            for u in range(2):
                n = 2 * j + u
                off = pl.multiple_of(n * BS, BS)
                s = jnp.dot(k_ref[0, pl.ds(off, BS), h * hd:(h + 1) * hd], qs_ref[h],
                            preferred_element_type=F32)
                s = jnp.where(sel_ref[h, pl.ds(n, 1), :] > 0.0, s, NEG)
                s_ref[h, n] = s
                mh = jnp.maximum(mh, jnp.max(s, axis=0, keepdims=True))
            m_ref[h] = mh
        return 0

    lax.fori_loop(0, n_pairs, sweep_scores, 0)

    krow = lax.broadcasted_iota(jnp.int32, (BS, BS), 0)
    qcol = lax.broadcasted_iota(jnp.int32, (BS, BS), 1)
    for h in range(hp):
        hs = slice(h * hd, (h + 1) * hd)
        s = jnp.dot(k_ref[0, pl.ds(own, BS), hs], qs_ref[h], preferred_element_type=F32)
        s = jnp.where(krow <= qcol, s, NEG)
        m = jnp.maximum(m_ref[h], jnp.max(s, axis=0, keepdims=True))
        m_ref[h] = m
        p = jnp.exp((s - m).astype(BF16))
        acc_ref[h] = jnp.dot(vt_ref[h, qb], p, preferred_element_type=F32)

    def sweep_values(j, _):
        for h in range(hp):
            mh = m_ref[h]
            p0 = jnp.exp((s_ref[h, 2 * j] - mh).astype(BF16))
            p1 = jnp.exp((s_ref[h, 2 * j + 1] - mh).astype(BF16))
            acc_ref[h] = acc_ref[h] + (
                jnp.dot(vt_ref[h, 2 * j], p0, preferred_element_type=F32)
                + jnp.dot(vt_ref[h, 2 * j + 1], p1, preferred_element_type=F32))
        return 0

    lax.fori_loop(0, n_pairs, sweep_values, 0)

    for h in range(hp):
        acc = acc_ref[h]
        o_ref[0, :, h * hd:(h + 1) * hd] = (acc[:hd, :] / acc[hd:hd + 1, :]).T.astype(o_ref.dtype)


def _moba(proj, B, T, moba_w):
    H, hd, BS = MOBA_HEADS, MOBA_HEAD_DIM, MOBA_BLOCK
    nb = T // BS
    hp = 4
    ones_rows = 2 * SUBLANES
    gw = hp * hd
    gpw = moba_w // gw
    return pl.pallas_call(
        functools.partial(_moba_kernel, nb=nb, hp=hp),
        grid=(B, H // hp, nb),
        in_specs=[pl.BlockSpec((1, BS, gw), lambda b, h, i: (b, i, h)),
                  pl.BlockSpec((1, T, gw), lambda b, h, i: (b, 0, gpw + h)),
                  pl.BlockSpec((1, T, gw), lambda b, h, i: (b, 0, 2 * gpw + h))],
        out_specs=pl.BlockSpec((1, BS, gw), lambda b, h, i: (b, i, h)),
        out_shape=jax.ShapeDtypeStruct((B, T, moba_w), BF16),
        scratch_shapes=[pltpu.VMEM((hp, nb, hd), F32),
                        pltpu.VMEM((hp, nb, hd + ones_rows, BS), BF16),
                        pltpu.VMEM((hp, nb, BS), F32),
                        pltpu.VMEM((hp, hd, BS), BF16),
                        pltpu.VMEM((hp, nb, BS, BS), F32),
                        pltpu.VMEM((hp, 1, BS), F32),
                        pltpu.VMEM((hp, hd + ones_rows, BS), F32)],
        compiler_params=_cparams(("parallel", "parallel", "arbitrary")),
        name="moba_attn",
    )(proj, proj, proj)


def _ret_kernel(q_ref, k_ref, v_ref, g_ref, dm_ref, xi_ref, zeta_ref, cd_ref, o_ref, s_ref):
    c = pl.program_id(1)
    d = RET_HEAD_DIM

    @pl.when(c == 0)
    def _():
        s_ref[...] = jnp.zeros_like(s_ref)

    for h in range(RET_HEADS):
        sl = slice(h * d, (h + 1) * d)
        q = q_ref[0, :, sl]
        k = k_ref[0, :, sl]
        v = v_ref[0, :, sl]
        inner = lax.dot_general(q, k, (((1,), (1,)), ((), ())), preferred_element_type=F32) * dm_ref[h]
        S = s_ref[h]
        o = (jnp.dot(inner.astype(BF16), v, preferred_element_type=F32)
             + jnp.dot(q, S.astype(BF16), preferred_element_type=F32) * xi_ref[h])
        kz = (k.astype(F32) * zeta_ref[h]).astype(BF16)
        s_ref[h] = S * cd_ref[h] + lax.dot_general(kz, v, (((0,), (0,)), ((), ())),
                                                   preferred_element_type=F32)
        mu = jnp.mean(o, axis=-1, keepdims=True)
        dlt = o - mu
        var = jnp.mean(dlt * dlt, axis=-1, keepdims=True)
        on = dlt * lax.rsqrt(var + NORM_EPS)
        gg = g_ref[0, :, sl].astype(F32)
        o_ref[0, :, sl] = (on * (gg * jax.nn.sigmoid(gg))).astype(o_ref.dtype)


def _retention(proj, B, T, moba_w, ret_w):
    C, H = RET_CHUNK, RET_HEADS
    gamma = 1.0 - jnp.exp2(-5.0 - jnp.arange(H, dtype=F32))
    log_g = jnp.log(gamma)
    pos = jnp.arange(C, dtype=F32)
    diff = pos[:, None] - pos[None, :]
    dmask = jnp.where(diff >= 0, jnp.exp(jnp.maximum(diff, 0.0) * log_g[:, None, None]), 0.0)
    xi = jnp.exp((pos + 1.0) * log_g[:, None])[:, :, None]
    zeta = jnp.exp((C - 1.0 - pos) * log_g[:, None])[:, :, None]
    cd = jnp.exp(C * log_g)[:, None, None]
    base = 3 * moba_w // ret_w
    col = lambda off: pl.BlockSpec((1, C, ret_w), lambda b, c: (b, c, base + off))
    full = lambda shp: pl.BlockSpec(shp, lambda b, c: (0,) * len(shp))
    return pl.pallas_call(
        _ret_kernel,
        grid=(B, T // C),
        in_specs=[col(0), col(1), col(2), col(3),
                  full((H, C, C)), full((H, C, 1)), full((H, C, 1)), full((H, 1, 1))],
        out_specs=pl.BlockSpec((1, C, ret_w), lambda b, c: (b, c, 0)),
        out_shape=jax.ShapeDtypeStruct((B, T, ret_w), BF16),
        scratch_shapes=[pltpu.VMEM((H, RET_HEAD_DIM, RET_HEAD_DIM), F32)],
        compiler_params=_cparams(("parallel", "arbitrary")),
        name="retention",
    )(proj, proj, proj, proj, dmask, xi, zeta, cd)


def _outproj_kernel(oa_ref, or_ref, w_ref, x_ref, ga_ref, g_ref, sh_ref, sc_ref, wrh_ref, wrl_ref,
                    x1_ref, hp_ref, lg_ref, *, moba_w):
    mix = (jnp.dot(oa_ref[0], w_ref[:moba_w, :], preferred_element_type=F32)
           + jnp.dot(or_ref[0], w_ref[moba_w:, :], preferred_element_type=F32))
    x1 = x_ref[0] + ga_ref[0] * mix
    x1_ref[0] = x1
    h = _rms_mod(x1, g_ref[...], sh_ref[0], sc_ref[0])
    half = h.shape[-1] // 2
    _store_rows_as_tiles(_flat_tiles(hp_ref), _pack_pair(h[:, :half], h[:, half:]))
    h_hi = h.astype(BF16)
    h_lo = (h - h_hi.astype(F32)).astype(BF16)
    lg_ref[0] = (jnp.dot(h_hi, wrh_ref[...], preferred_element_type=F32)
                 + (jnp.dot(h_lo, wrh_ref[...], preferred_element_type=F32)
                    + jnp.dot(h_hi, wrl_ref[...], preferred_element_type=F32)))


def _out_proj(o_a, o_r, w_bf, x, gate_a, g, shift, scale, w_router):
    B, T, D = x.shape
    moba_w, ret_w = o_a.shape[-1], o_r.shape[-1]
    E = w_router.shape[1]
    tm = 256
    wr_hi = w_router.astype(BF16)
    wr_lo = (w_router - wr_hi.astype(F32)).astype(BF16)
    vec = pl.BlockSpec((1, 1, D), lambda b, i: (b, 0, 0))
    row = lambda w: pl.BlockSpec((1, tm, w), lambda b, i: (b, i, 0))
    return pl.pallas_call(
        functools.partial(_outproj_kernel, moba_w=moba_w),
        grid=(B, T // tm),
        in_specs=[row(moba_w), row(ret_w),
                  pl.BlockSpec((moba_w + ret_w, D), lambda b, i: (0, 0)),
                  row(D), vec,
                  pl.BlockSpec((1, D), lambda b, i: (0, 0)),
                  vec, vec,
                  pl.BlockSpec((D, E), lambda b, i: (0, 0)),
                  pl.BlockSpec((D, E), lambda b, i: (0, 0))],
        out_specs=[row(D), pl.BlockSpec((1, tm, SUBLANES, LANES), lambda b, i: (b, i, 0, 0)), row(E)],
        out_shape=[jax.ShapeDtypeStruct((B, T, D), F32),
                   jax.ShapeDtypeStruct((B, T, SUBLANES, LANES), U32),
                   jax.ShapeDtypeStruct((B, T, E), F32)],
        compiler_params=_cparams(("parallel", "parallel")),
        name="out_proj",
    )(o_a, o_r, w_bf, x, gate_a[:, None, :], g[None, :], shift[:, None, :], scale[:, None, :], wr_hi, wr_lo)


def _route_kernel_t(lg_ref, b_ref, selr_ref, wf_ref, rank_ref, cnt_ref, carry_ref):
    @pl.when((pl.program_id(0) == 0) & (pl.program_id(1) == 0))
    def _():
        carry_ref[...] = jnp.zeros_like(carry_ref)

    E = N_EXPERTS
    gsz = E // N_GROUPS
    assert gsz == SUBLANES and N_GROUPS == SUBLANES, "a routing group is one sublane tile of experts"
    s = jax.nn.sigmoid(lg_ref[0].T)
    biased = s + b_ref[...]
    tm = s.shape[1]
    sub = lax.broadcasted_iota(jnp.int32, (SUBLANES, tm), 0).astype(F32)
    eid = lax.broadcasted_iota(jnp.int32, (E, tm), 0).astype(F32)

    def first_argmax(v, ids, width):
        m = jnp.max(v, axis=0, keepdims=True)
        idx = jnp.min(jnp.where(v == m, ids, float(width)), axis=0, keepdims=True)
        return m, idx

    gscore = jnp.zeros((N_GROUPS, tm), F32)
    for gi in range(N_GROUPS):
        v = biased[gi * gsz:(gi + 1) * gsz, :]
        m1, i1 = first_argmax(v, sub, gsz)
        m2 = jnp.max(jnp.where(sub == i1, -jnp.inf, v), axis=0, keepdims=True)
        gscore = jnp.where(sub == float(gi), m1 + m2, gscore)

    gsel = jnp.zeros((N_GROUPS, tm), F32)
    for _ in range(TOPK_GROUPS):
        _, gi = first_argmax(gscore, sub, N_GROUPS)
        pick = sub == gi
        gsel = jnp.where(pick, 1.0, gsel)
        gscore = jnp.where(pick, -jnp.inf, gscore)

    cand = jnp.concatenate(
        [jnp.where(gsel[gi:gi + 1, :] > 0.0, biased[gi * gsz:(gi + 1) * gsz, :], NEG) for gi in range(N_GROUPS)],
        axis=0)
    selr = jnp.zeros((E, tm), F32)
    for r in range(TOP_K):
        _, ei = first_argmax(cand, eid, E)
        pick = eid == ei
        selr = jnp.where(pick, float(r + 1), selr)
        cand = jnp.where(pick, -jnp.inf, cand)

    chosen = selr > 0.0
    w = jnp.where(chosen, s, 0.0)
    wsum = jnp.sum(w, axis=0, keepdims=True)
    selr_ref[0] = selr
    wf_ref[0] = w / wsum * ROUTE_SCALE

    onehot = chosen.astype(BF16)
    c_i = lax.broadcasted_iota(jnp.int32, (tm, tm), 0)
    r_i = lax.broadcasted_iota(jnp.int32, (tm, tm), 1)
    tri = (c_i < r_i).astype(BF16)
    carry = carry_ref[...]
    rank_ref[0] = jnp.dot(onehot, tri, preferred_element_type=F32) + carry
    carry = carry + jnp.sum(chosen.astype(F32), axis=1, keepdims=True)
    carry_ref[...] = carry
    cnt_ref[...] = carry


def _route_t(logits_t, bias):
    B, T, E = logits_t.shape
    tm = 512
    blk = pl.BlockSpec((1, E, tm), lambda b, i: (b, 0, i))
    col = pl.BlockSpec((E, 1), lambda b, i: (0, 0))
    full = jax.ShapeDtypeStruct((B, E, T), F32)
    return pl.pallas_call(
        _route_kernel_t,
        grid=(B, T // tm),
        in_specs=[pl.BlockSpec((1, tm, E), lambda b, i: (b, i, 0)), col],
        out_specs=[blk, blk, blk, col],
        out_shape=[full, full, full, jax.ShapeDtypeStruct((E, 1), F32)],
        scratch_shapes=[pltpu.VMEM((E, 1), F32)],
        compiler_params=_cparams(("arbitrary", "arbitrary")),
        name="route_topk",
    )(logits_t, bias[:, None])


def _dest_kernel_t(selr_ref, wf_ref, rank_ref, ps_ref, dest_ref, wk_ref):
    selr = selr_ref[0]
    destfull = rank_ref[0] + ps_ref[...]
    wf = wf_ref[0]
    for r in range(TOP_K):
        hit = selr == float(r + 1)
        dest_ref[0, r:r + 1, :] = jnp.sum(jnp.where(hit, destfull, 0.0), axis=0, keepdims=True).astype(jnp.int32)
        wk_ref[0, r:r + 1, :] = jnp.sum(jnp.where(hit, wf, 0.0), axis=0, keepdims=True)


def _dest_t(selr, wf, rank, pstart_f):
    B, E, T = selr.shape
    tm = 512
    blk = pl.BlockSpec((1, E, tm), lambda b, i: (b, 0, i))
    outb = pl.BlockSpec((1, TOP_K, tm), lambda b, i: (b, 0, i))
    return pl.pallas_call(
        _dest_kernel_t,
        grid=(B, T // tm),
        in_specs=[blk, blk, blk, pl.BlockSpec((E, 1), lambda b, i: (0, 0))],
        out_specs=[outb, outb],
        out_shape=[jax.ShapeDtypeStruct((B, TOP_K, T), jnp.int32), jax.ShapeDtypeStruct((B, TOP_K, T), F32)],
        compiler_params=_cparams(("parallel", "parallel")),
        name="route_dest",
    )(selr, wf, rank, pstart_f)


def _row_copy(src, s_row, dst, d_row, n, sem):
    return pltpu.make_async_copy(src.at[pl.ds(s_row, n)], dst.at[pl.ds(d_row, n)], sem)


def _dispatch_kernel(padlo_ref, padn_ref, dest_ref, h_ref, z_ref, xs_ref, sem, zsem, *, tt, n_exp):
    i = pl.program_id(0)

    def issue(t, _):
        for k in range(TOP_K):
            _row_copy(h_ref, t, xs_ref, dest_ref[0, 0, k * tt + t], 1, sem).start(priority=k % 2)
        return 0

    lax.fori_loop(0, tt, issue, 0, unroll=2)

    def each_pad(fn):
        def per_expert(e, _):
            lo = padlo_ref[e]

            def one(r, _):
                fn(lo + r)
                return 0

            lax.fori_loop(0, padn_ref[e], one, 0)
            return 0

        lax.fori_loop(0, n_exp, per_expert, 0)

    @pl.when(i == 0)
    def _():
        each_pad(lambda r: _row_copy(z_ref, 0, xs_ref, r, 1, zsem).start())

    for k in range(TOP_K):
        _row_copy(h_ref, 0, xs_ref, 0, tt, sem).wait()

    @pl.when(i == 0)
    def _():
        each_pad(lambda r: _row_copy(z_ref, 0, xs_ref, 0, 1, zsem).wait())


def _dispatch(h2p, dest, pad_lo, pad_n, R):
    N = h2p.shape[0]
    tile = h2p.shape[1:]
    tt = TOK_TILE
    dest3 = dest
    zeros = jnp.zeros((SUBLANES,) + tile, h2p.dtype)
    grid_spec = pltpu.PrefetchScalarGridSpec(
        num_scalar_prefetch=2,
        grid=(N // tt,),
        in_specs=[pl.BlockSpec((1, 1, tt * TOP_K), lambda i, lo, n: (i, 0, 0), memory_space=pltpu.SMEM),
                  pl.BlockSpec((tt,) + tile, lambda i, lo, n: (i, 0, 0)),
                  pl.BlockSpec((SUBLANES,) + tile, lambda i, lo, n: (0, 0, 0))],
        out_specs=pl.BlockSpec(memory_space=pl.ANY),
        scratch_shapes=[pltpu.SemaphoreType.DMA(()), pltpu.SemaphoreType.DMA(())],
    )
    return pl.pallas_call(
        functools.partial(_dispatch_kernel, tt=tt, n_exp=N_EXPERTS),
        grid_spec=grid_spec,
        out_shape=jax.ShapeDtypeStruct((R,) + tile, h2p.dtype),
        compiler_params=_cparams(("arbitrary",)),
        name="moe_dispatch",
    )(pad_lo, pad_n, dest3, h2p, zeros)


def _expert_kernel(te_ref, nu_ref, first_ref, slot_ref, nxt_ref, xs_ref, wg_hbm, wu_hbm, wd_hbm, y_ref,
                   stg, stu, std, wgb, wub, wdb, sem):
    i = pl.program_id(0)
    n_chunks = 2

    def fetch(e, s):
        cps = []
        for m, (src, dst) in enumerate(((wg_hbm, stg), (wu_hbm, stu), (wd_hbm, std))):
            rows = dst.shape[1] // n_chunks
            for c in range(n_chunks):
                cps.append(pltpu.make_async_copy(src.at[e, pl.ds(c * rows, rows)],
                                                 dst.at[s, pl.ds(c * rows, rows)], sem.at[s, m]))
        return cps

    @pl.when(i == 0)
    def _():
        for cp in fetch(te_ref[0], 0):
            cp.start()

        @pl.when(nxt_ref[0, 0] >= 0)
        def _():
            for cp in fetch(nxt_ref[0, 0], 1):
                cp.start()

    active = i < nu_ref[0]

    def swiglu_tile(wg, wu, wd):
        m_rows = xs_ref.shape[0]
        lo, hi = _unpack_pair(_load_tiles_as_rows(_flat_tiles(xs_ref), m_rows))
        lo = lo.astype(BF16)
        hi = hi.astype(BF16)
        half = lo.shape[-1]
        a = (jnp.dot(lo, wg[:half, :], preferred_element_type=F32)
             + jnp.dot(hi, wg[half:, :], preferred_element_type=F32))
        u = (jnp.dot(lo, wu[:half, :], preferred_element_type=F32)
             + jnp.dot(hi, wu[half:, :], preferred_element_type=F32))
        hmid = (a * jax.nn.sigmoid(a) * u).astype(BF16)
        y = jnp.dot(hmid, wd, preferred_element_type=F32)
        _store_rows_as_tiles(_flat_tiles(y_ref), _pack_pair(y[:, :half], y[:, half:]))

    @pl.when(active & (first_ref[i] == 1))
    def _():
        s = slot_ref[i]
        for cp in fetch(0, s):
            cp.wait()
        for src, dst in ((stg, wgb), (stu, wub), (std, wdb)):
            rows = 32 * SUBLANES * LANES // dst.shape[1]

            def cast_chunk(r, _, src=src, dst=dst, rows=rows):
                off = pl.multiple_of(r * rows, rows)
                dst[pl.ds(off, rows), :] = src[s, pl.ds(off, rows), :].astype(BF16)
                return 0

            lax.fori_loop(0, dst.shape[0] // rows, cast_chunk, 0, unroll=2)

        @pl.when(nxt_ref[1, i] >= 0)
        def _():
            for cp in fetch(nxt_ref[1, i], s):
                cp.start()

    @pl.when(active)
    def _():
        swiglu_tile(wgb[...], wub[...], wdb[...])


def _experts(xs, tile_expert, n_used, first, slot, nxt, wg, wu, wd):
    R = xs.shape[0]
    tile = xs.shape[1:]
    M = ROW_TILE
    _, D, F = wg.shape
    row = lambda i, te, nu, fi, sl, nx: (jnp.minimum(i, nu[0] - 1), 0, 0)
    grid_spec = pltpu.PrefetchScalarGridSpec(
        num_scalar_prefetch=5,
        grid=(R // M,),
        in_specs=[pl.BlockSpec((M,) + tile, row),
                  pl.BlockSpec(memory_space=pl.ANY),
                  pl.BlockSpec(memory_space=pl.ANY),
                  pl.BlockSpec(memory_space=pl.ANY)],
        out_specs=pl.BlockSpec((M,) + tile, row),
        scratch_shapes=[pltpu.VMEM((2, D, F), F32), pltpu.VMEM((2, D, F), F32), pltpu.VMEM((2, F, D), F32),
                        pltpu.VMEM((D, F), BF16), pltpu.VMEM((D, F), BF16), pltpu.VMEM((F, D), BF16),
                        pltpu.SemaphoreType.DMA((2, 3))],
    )
    return pl.pallas_call(
        _expert_kernel,
        grid_spec=grid_spec,
        out_shape=jax.ShapeDtypeStruct((R,) + tile, U32),
        compiler_params=_cparams(("arbitrary",)),
        name="moe_experts",
    )(tile_expert, n_used, first, slot, nxt, xs, wg, wu, wd)


def _final_kernel(dcur_ref, dnxt_ref, h_ref, wsg_ref, wsu_ref, wsd_ref, x1_ref, gf_ref, g_ref, wk_ref, y_ref,
                  o_ref, ybuf, sem, *, tt, n_tiles):
    i = pl.program_id(0)
    slot = lax.rem(i, 2)

    def gather(d_ref, s):
        def issue(t, _):
            for k in range(TOP_K):
                pltpu.make_async_copy(y_ref.at[pl.ds(d_ref[0, 0, k * tt + t], 1)],
                                      ybuf.at[s, k, pl.ds(t, 1)], sem.at[s]).start(priority=k % 2)
            return 0

        lax.fori_loop(0, tt, issue, 0, unroll=2)

    @pl.when(i == 0)
    def _():
        gather(dcur_ref, 0)

    for p in range(2):
        @pl.when((i + 1 < n_tiles) & (slot == p))
        def _(p=p):
            gather(dnxt_ref, 1 - p)

    lo, hi = _unpack_pair(_load_tiles_as_rows(_flat_tiles(h_ref), tt))
    lo = lo.astype(BF16)
    hi = hi.astype(BF16)
    half = lo.shape[-1]
    a = (jnp.dot(lo, wsg_ref[:half, :], preferred_element_type=F32)
         + jnp.dot(hi, wsg_ref[half:, :], preferred_element_type=F32))
    u = (jnp.dot(lo, wsu_ref[:half, :], preferred_element_type=F32)
         + jnp.dot(hi, wsu_ref[half:, :], preferred_element_type=F32))
    hmid = (a * jax.nn.sigmoid(a) * u).astype(BF16)
    shared = jnp.dot(hmid, wsd_ref[...], preferred_element_type=F32)

    for k in range(TOP_K):
        pltpu.make_async_copy(y_ref.at[pl.ds(0, tt)], ybuf.at[slot, k], sem.at[slot]).wait()

    wk = wk_ref[...]
    yflat = _flat_tiles(ybuf)

    for p in range(2):
        @pl.when(slot == p)
        def _(p=p):
            r_lo = jnp.zeros((tt, half), F32)
            r_hi = jnp.zeros((tt, half), F32)
            for k in range(TOP_K):
                ylo, yhi = _unpack_pair(_load_tiles_as_rows(yflat, tt, base=(p * TOP_K + k) * tt))
                wcol = wk[:, k:k + 1]
                r_lo = r_lo + wcol * ylo
                r_hi = r_hi + wcol * yhi
            total = shared + jnp.concatenate([r_lo, r_hi], axis=1)
            x2 = x1_ref[...] + gf_ref[0] * total
            ms = jnp.mean(x2 * x2, axis=-1, keepdims=True)
            o_ref[...] = x2 * lax.rsqrt(ms + NORM_EPS) * g_ref[...]


def _final(h2p, wsg, wsu, wsd, x1, gate_f, norm_out, dest, wk, y, T):
    N = h2p.shape[0]
    tile = h2p.shape[1:]
    D = x1.shape[-1]
    F = wsg.shape[1]
    tt = TOK_TILE
    per_b = T // tt
    n_tiles = N // tt
    dest3 = dest
    rowb = lambda w: pl.BlockSpec((tt, w), lambda i: (i, 0))
    const = lambda shp: pl.BlockSpec(shp, lambda i: (0,) * len(shp))
    dspec = lambda f: pl.BlockSpec((1, 1, tt * TOP_K), f, memory_space=pltpu.SMEM)
    return pl.pallas_call(
        functools.partial(_final_kernel, tt=tt, n_tiles=n_tiles),
        grid=(n_tiles,),
        in_specs=[dspec(lambda i: (i, 0, 0)),
                  dspec(lambda i: (jnp.minimum(i + 1, n_tiles - 1), 0, 0)),
                  pl.BlockSpec((tt,) + tile, lambda i: (i, 0, 0)),
                  const((D, F)), const((D, F)), const((F, D)), rowb(D),
                  pl.BlockSpec((1, 1, D), lambda i: (i // per_b, 0, 0)),
                  const((1, D)),
                  pl.BlockSpec((tt, TOP_K), lambda i: (i, 0)),
                  pl.BlockSpec(memory_space=pl.ANY)],
        out_specs=rowb(D),
        out_shape=jax.ShapeDtypeStruct((N, D), F32),
        scratch_shapes=[pltpu.VMEM((2, TOP_K, tt) + tile, U32), pltpu.SemaphoreType.DMA((2,))],
        compiler_params=_cparams(("arbitrary",)),
        name="moe_combine_final",
    )(dest3, dest3, h2p, wsg, wsu, wsd, x1, gate_f[:, None, :], norm_out[None, :], wk, y)


def kernel(x, c, positions, w_ada, b_ada, norm_mix, norm_ffn, norm_out, w_in, w_out, w_router, router_bias,
           w_gate, w_up, w_down, w_sh_gate, w_sh_up, w_sh_down):
    B, T, D = x.shape
    depth = w_ada.shape[0]
    assert depth == 1, "the final rmsnorm is fused into the layer's last kernel"
    moba_w = MOBA_HEADS * MOBA_HEAD_DIM
    ret_w = RET_HEADS * RET_HEAD_DIM
    N = B * T
    E, M = N_EXPERTS, ROW_TILE
    R = N * TOP_K + E * M
    n_tiles = R // M
    tabs = _rope_tables(positions)

    for l in range(depth):
        mod = _ada(c, w_ada[l], b_ada[l])
        shift_a, scale_a, gate_a, shift_f, scale_f, gate_f = jnp.split(mod, 6, axis=-1)

        proj = _in_proj(x, norm_mix[l], shift_a, scale_a, w_in[l].astype(BF16), tabs, moba_w, ret_w)
        o_a = _moba(proj, B, T, moba_w)
        o_r = _retention(proj, B, T, moba_w, ret_w)
        x1, h2p, logits = _out_proj(o_a, o_r, w_out[l].astype(BF16), x, gate_a, norm_ffn[l],
                                    shift_f, scale_f, w_router[l])

        assert D // 2 == SUBLANES * LANES, "a packed row must fill exactly one (SUBLANES, LANES) tile"
        h2p = h2p.reshape(N, SUBLANES, LANES)
        selr, wf, rank, counts = _route_t(logits, router_bias[l])
        cnt = counts[:, 0].astype(jnp.int32)
        pcnt = (cnt + M - 1) // M * M
        pend = jnp.cumsum(pcnt)
        pstart = pend - pcnt
        tidx = jnp.arange(n_tiles, dtype=jnp.int32)
        end_tile = pend // M
        eids = jnp.arange(E, dtype=jnp.int32)
        owner = lambda v: jnp.minimum(jnp.sum(end_tile[None, :] <= v[:, None], axis=1), E - 1).astype(jnp.int32)
        end_of = lambda e: jnp.sum(jnp.where(e[:, None] == eids[None, :], end_tile[None, :], 0), axis=1)
        tile_expert = owner(tidx)
        n_used = end_tile[-1:].astype(jnp.int32)
        first = (((tidx == 0) | (tile_expert != jnp.roll(tile_expert, 1))) & (tidx < n_used[0])).astype(jnp.int32)
        slot = ((jnp.cumsum(first) - 1) % 2).astype(jnp.int32)
        end1 = end_of(tile_expert)
        exp1 = owner(end1)
        end2 = end_of(exp1)
        exp2 = owner(end2)
        has1 = end1 < n_used[0]
        nxt = jnp.stack([jnp.where(has1, exp1, -1),
                         jnp.where(has1 & (end2 < n_used[0]), exp2, -1)]).astype(jnp.int32)
        dest, wk = _dest_t(selr, wf, rank, pstart.astype(F32)[:, None])
        tt = TOK_TILE
        dest = dest.reshape(B, TOP_K, T // tt, tt).transpose(0, 2, 1, 3).reshape(N // tt, 1, TOP_K * tt)
        wk = wk.transpose(0, 2, 1).reshape(N, TOP_K)

        xs = _dispatch(h2p, dest, (pstart + cnt).astype(jnp.int32), (pcnt - cnt).astype(jnp.int32), R)
        y = _experts(xs, tile_expert, n_used, first, slot, nxt, w_gate[l], w_up[l], w_down[l])
        out = _final(h2p, w_sh_gate[l].astype(BF16), w_sh_up[l].astype(BF16), w_sh_down[l].astype(BF16),
                     x1.reshape(N, D), gate_f, norm_out, dest, wk, y, T)
        x = out.reshape(B, T, D)
    return x
```

```python
import functools

import jax
import jax.numpy as jnp
from jax import lax
from jax.experimental import pallas as pl
from jax.experimental.pallas import tpu as pltpu

MOBA_HEADS = 8
MOBA_HEAD_DIM = 128
MOBA_BLOCK = 256
MOBA_TOPK = 3
ROPE_THETA = 500000.0
ROPE_DIMS = 32
RET_HEADS = 4
RET_HEAD_DIM = 256
RET_ROPE_BASE = 10000.0
N_EXPERTS = 64
TOP_K = 8
N_GROUPS = 8
TOPK_GROUPS = 4
ROUTE_SCALE = 2.5
NORM_EPS = 1e-6
NEG = -1e30

LANES = 128
SUBLANES = 8
VMEM_LIMIT = 56 * 1024 * 1024

RET_CHUNK = 256
ROW_TILE = 256
TOK_TILE = 256

F32 = jnp.float32
BF16 = jnp.bfloat16
U32 = jnp.uint32


def _cparams(sem):
    return pltpu.CompilerParams(dimension_semantics=sem, vmem_limit_bytes=VMEM_LIMIT)


def _rms_mod(xf, g, shift, scale):
    ms = jnp.mean(xf * xf, axis=-1, keepdims=True)
    y = xf * lax.rsqrt(ms + NORM_EPS) * g
    return y * (1.0 + scale) + shift


def _pack_pair(lo, hi):
    lo_b = lax.bitcast_convert_type(lo.astype(BF16).astype(F32), U32)
    hi_b = lax.bitcast_convert_type(hi.astype(BF16).astype(F32), U32)
    return (lo_b >> 16) | hi_b


def _unpack_pair(p):
    lo = lax.bitcast_convert_type(p << 16, F32)
    hi = lax.bitcast_convert_type(p & jnp.uint32(0xFFFF0000), F32)
    return lo, hi


def _store_rows_as_tiles(ref, val, base=0):
    n = val.shape[0]
    for c in range(SUBLANES):
        ref[pl.ds(base * SUBLANES + c, n, stride=SUBLANES), :] = val[:, c * LANES:(c + 1) * LANES]


def _load_tiles_as_rows(ref, n, base=0):
    return jnp.concatenate([ref[pl.ds(base * SUBLANES + c, n, stride=SUBLANES), :] for c in range(SUBLANES)],
                           axis=1)


def _flat_tiles(ref):
    rows = 1
    for d in ref.shape[:-2]:
        rows *= d
    return ref.reshape(rows * SUBLANES, LANES)


def _tables_kernel(pos_ref, invm_ref, invr_ref, mc_ref, ms1_ref, ms2_ref, rc_ref, rs_ref):
    pos = pos_ref[0].astype(F32)
    angm = pos * invm_ref[...]
    lane = lax.broadcasted_iota(jnp.int32, angm.shape, 1)
    half = ROPE_DIMS // 2
    c = jnp.cos(angm)
    s = jnp.sin(angm)
    mc_ref[0] = c
    ms1_ref[0] = jnp.where(lane < half, -s, 0.0)
    ms2_ref[0] = jnp.where((lane >= half) & (lane < ROPE_DIMS), s, 0.0)
    angr = pos * invr_ref[...]
    rc_ref[0] = jnp.cos(angr)
    rs_ref[0] = jnp.sin(angr)


def _rope_tables(positions):
    B, T = positions.shape
    tm = 512
    half = ROPE_DIMS // 2
    moba_inv = ROPE_THETA ** (-(jnp.arange(half, dtype=F32) * 2.0 / ROPE_DIMS))
    invm = jnp.concatenate([moba_inv, moba_inv, jnp.zeros((LANES - ROPE_DIMS,), F32)])[None, :]
    invr = (RET_ROPE_BASE ** (-jnp.linspace(0.0, 1.0, RET_HEAD_DIM // 2, dtype=F32)))[None, :]
    pos3 = positions.reshape(B, T, 1)
    tab = jax.ShapeDtypeStruct((B, T, LANES), F32)
    spec = pl.BlockSpec((1, tm, LANES), lambda b, i: (b, i, 0))
    return pl.pallas_call(
        _tables_kernel,
        grid=(B, T // tm),
        in_specs=[pl.BlockSpec((1, tm, 1), lambda b, i: (b, i, 0)),
                  pl.BlockSpec((1, LANES), lambda b, i: (0, 0)),
                  pl.BlockSpec((1, LANES), lambda b, i: (0, 0))],
        out_specs=[spec] * 5,
        out_shape=[tab] * 5,
        compiler_params=_cparams(("parallel", "parallel")),
        name="rope_tables",
    )(pos3, invm, invr)


def _ada_kernel(ct_ref, w_ref, b_ref, o_ref):
    ct = ct_ref[...]
    sct = ct * jax.nn.sigmoid(ct)
    w = w_ref[...]
    for b in range(ct.shape[1]):
        o_ref[b:b + 1, :] = jnp.sum(w * sct[:, b:b + 1], axis=0, keepdims=True) + b_ref[...]


def _ada(c, w_ada, b_ada):
    B, D = c.shape
    n_out = w_ada.shape[1]
    tn = 1024
    return pl.pallas_call(
        _ada_kernel,
        grid=(n_out // tn,),
        in_specs=[pl.BlockSpec((D, B), lambda j: (0, 0)),
                  pl.BlockSpec((D, tn), lambda j: (0, j)),
                  pl.BlockSpec((1, tn), lambda j: (0, j))],
        out_specs=pl.BlockSpec((B, tn), lambda j: (0, j)),
        out_shape=jax.ShapeDtypeStruct((B, n_out), F32),
        compiler_params=_cparams(("parallel",)),
        name="adaln_mod",
    )(c.T, w_ada, b_ada[None, :])


def _inproj_kernel(x_ref, g_ref, sh_ref, sc_ref, w_ref, mc_ref, ms1_ref, ms2_ref, rc_ref, rs_ref,
                   o_ref, hn_ref, *, tn, moba_tiles, ret_lo, ret_k_lo, ret_hi):
    j = pl.program_id(2)

    @pl.when(j == 0)
    def _():
        h = _rms_mod(x_ref[0], g_ref[...], sh_ref[0], sc_ref[0])
        hn_ref[...] = h.astype(BF16)

    acc = jnp.dot(hn_ref[...], w_ref[...], preferred_element_type=F32)

    @pl.when(j < moba_tiles)
    def _():
        c, s1, s2 = mc_ref[0], ms1_ref[0], ms2_ref[0]
        half = ROPE_DIMS // 2
        for g in range(tn // LANES):
            a = acc[:, g * LANES:(g + 1) * LANES]
            r = a * c + pltpu.roll(a, LANES - half, 1) * s1 + pltpu.roll(a, half, 1) * s2
            o_ref[0, :, g * LANES:(g + 1) * LANES] = r.astype(o_ref.dtype)

    @pl.when((j >= ret_lo) & (j < ret_hi))
    def _():
        c, s = rc_ref[0], rs_ref[0]
        fac = jnp.where(j >= ret_k_lo, RET_HEAD_DIM ** -0.5, 1.0).astype(F32)
        hw = RET_HEAD_DIM // 2
        for g in range(tn // RET_HEAD_DIM):
            x1 = acc[:, g * RET_HEAD_DIM:g * RET_HEAD_DIM + hw]
            x2 = acc[:, g * RET_HEAD_DIM + hw:(g + 1) * RET_HEAD_DIM]
            o_ref[0, :, g * RET_HEAD_DIM:g * RET_HEAD_DIM + hw] = ((x1 * c - x2 * s) * fac).astype(o_ref.dtype)
            o_ref[0, :, g * RET_HEAD_DIM + hw:(g + 1) * RET_HEAD_DIM] = ((x2 * c + x1 * s) * fac).astype(o_ref.dtype)

    @pl.when(((j >= moba_tiles) & (j < ret_lo)) | (j >= ret_hi))
    def _():
        o_ref[0] = acc.astype(o_ref.dtype)


def _in_proj(x, g, shift, scale, w_bf, tabs, moba_w, ret_w):
    B, T, D = x.shape
    NC = w_bf.shape[1]
    tm, tn = 1024, 1024
    mc, ms1, ms2, rc, rs = tabs
    kern = functools.partial(
        _inproj_kernel, tn=tn,
        moba_tiles=2 * moba_w // tn,
        ret_lo=3 * moba_w // tn,
        ret_k_lo=(3 * moba_w + ret_w) // tn,
        ret_hi=(3 * moba_w + 2 * ret_w) // tn)
    tab_spec = pl.BlockSpec((1, tm, LANES), lambda b, i, j: (b, i, 0))
    vec_spec = pl.BlockSpec((1, 1, D), lambda b, i, j: (b, 0, 0))
    return pl.pallas_call(
        kern,
        grid=(B, T // tm, NC // tn),
        in_specs=[pl.BlockSpec((1, tm, D), lambda b, i, j: (b, i, 0)),
                  pl.BlockSpec((1, D), lambda b, i, j: (0, 0)),
                  vec_spec, vec_spec,
                  pl.BlockSpec((D, tn), lambda b, i, j: (0, j)),
                  tab_spec, tab_spec, tab_spec, tab_spec, tab_spec],
        out_specs=pl.BlockSpec((1, tm, tn), lambda b, i, j: (b, i, j)),
        out_shape=jax.ShapeDtypeStruct((B, T, NC), BF16),
        scratch_shapes=[pltpu.VMEM((tm, D), BF16)],
        compiler_params=_cparams(("parallel", "parallel", "arbitrary")),
        name="in_proj",
    )(x, g[None, :], shift[:, None, :], scale[:, None, :], w_bf, mc, ms1, ms2, rc, rs)


def _moba_kernel(q_ref, k_ref, v_ref, o_ref, km_ref, vt_ref, sel_ref, qs_ref, s_ref, m_ref, acc_ref,
                 *, nb, hp):
    qb = pl.program_id(2)
    BS, hd = MOBA_BLOCK, MOBA_HEAD_DIM

    @pl.when(qb == 0)
    def _():
        for h in range(hp):
            hs = slice(h * hd, (h + 1) * hd)
            for n in range(nb):
                kb = k_ref[0, n * BS:(n + 1) * BS, hs].astype(F32)
                km_ref[h, n:n + 1, :] = jnp.sum(kb, axis=0, keepdims=True) * (1.0 / BS)
                vt_ref[h, n, :hd, :] = v_ref[0, n * BS:(n + 1) * BS, hs].astype(F32).T.astype(BF16)
                vt_ref[h, n, hd:, :] = jnp.ones((vt_ref.shape[2] - hd, BS), BF16)

    scale = hd ** -0.5
    own = pl.multiple_of(qb * BS, BS)
    blk = lax.broadcasted_iota(jnp.int32, (nb, BS), 0)
    blk_f = blk.astype(F32)

    for h in range(hp):
        hs = slice(h * hd, (h + 1) * hd)
        qT = q_ref[0, :, hs].astype(F32).T
        gate = jnp.dot(km_ref[h], qT, preferred_element_type=F32,
                       precision=lax.Precision.HIGHEST)
        g = jnp.where(blk < qb, gate, NEG)
        sel = jnp.zeros((nb, BS), F32)
        for _ in range(MOBA_TOPK):
            m = jnp.max(g, axis=0, keepdims=True)
            idx = jnp.min(jnp.where(g == m, blk_f, float(nb)), axis=0, keepdims=True)
            pick = blk_f == idx
            sel = jnp.where(pick & (m > 0.5 * NEG), 1.0, sel)
            g = jnp.where(pick, -jnp.inf, g)
        sel_ref[h] = sel
        qs_ref[h] = (qT * scale).astype(BF16)
        m_ref[h] = jnp.full((1, BS), NEG, F32)

    n_pairs = lax.shift_right_logical(qb + 1, 1)

    def sweep_scores(j, _):
        for h in range(hp):
            mh = m_ref[h]
            for u in range(2):
                n = 2 * j + u
                off = pl.multiple_of(n * BS, BS)
                s = jnp.dot(k_ref[0, pl.ds(off, BS), h * hd:(h + 1) * hd], qs_ref[h],
                            preferred_element_type=F32)
                s = jnp.where(sel_ref[h, pl.ds(n, 1), :] > 0.0, s, NEG)
                s_ref[h, n] = s
                mh = jnp.maximum(mh, jnp.max(s, axis=0, keepdims=True))
            m_ref[h] = mh
        return 0

    lax.fori_loop(0, n_pairs, sweep_scores, 0)

    krow = lax.broadcasted_iota(jnp.int32, (BS, BS), 0)
    qcol = lax.broadcasted_iota(jnp.int32, (BS, BS), 1)
    for h in range(hp):
        hs = slice(h * hd, (h + 1) * hd)
        s = jnp.dot(k_ref[0, pl.ds(own, BS), hs], qs_ref[h], preferred_element_type=F32)
        s = jnp.where(krow <= qcol, s, NEG)
        m = jnp.maximum(m_ref[h], jnp.max(s, axis=0, keepdims=True))
        m_ref[h] = m
        p = jnp.exp((s - m).astype(BF16))
        acc_ref[h] = jnp.dot(vt_ref[h, qb], p, preferred_element_type=F32)

    def sweep_values(j, _):
        for h in range(hp):
            mh = m_ref[h]
            p0 = jnp.exp((s_ref[h, 2 * j] - mh).astype(BF16))
            p1 = jnp.exp((s_ref[h, 2 * j + 1] - mh).astype(BF16))
            acc_ref[h] = acc_ref[h] + (
                jnp.dot(vt_ref[h, 2 * j], p0, preferred_element_type=F32)
                + jnp.dot(vt_ref[h, 2 * j + 1], p1, preferred_element_type=F32))
        return 0

    lax.fori_loop(0, n_pairs, sweep_values, 0)

    for h in range(hp):
        acc = acc_ref[h]
        o_ref[0, :, h * hd:(h + 1) * hd] = (acc[:hd, :] / acc[hd:hd + 1, :]).T.astype(o_ref.dtype)


def _moba(proj, B, T, moba_w):
    H, hd, BS = MOBA_HEADS, MOBA_HEAD_DIM, MOBA_BLOCK
    nb = T // BS
    hp = 4
    ones_rows = 2 * SUBLANES
    gw = hp * hd
    gpw = moba_w // gw
    return pl.pallas_call(
        functools.partial(_moba_kernel, nb=nb, hp=hp),
        grid=(B, H // hp, nb),
        in_specs=[pl.BlockSpec((1, BS, gw), lambda b, h, i: (b, i, h)),
                  pl.BlockSpec((1, T, gw), lambda b, h, i: (b, 0, gpw + h)),
                  pl.BlockSpec((1, T, gw), lambda b, h, i: (b, 0, 2 * gpw + h))],
        out_specs=pl.BlockSpec((1, BS, gw), lambda b, h, i: (b, i, h)),
        out_shape=jax.ShapeDtypeStruct((B, T, moba_w), BF16),
        scratch_shapes=[pltpu.VMEM((hp, nb, hd), F32),
                        pltpu.VMEM((hp, nb, hd + ones_rows, BS), BF16),
                        pltpu.VMEM((hp, nb, BS), F32),
                        pltpu.VMEM((hp, hd, BS), BF16),
                        pltpu.VMEM((hp, nb, BS, BS), F32),
                        pltpu.VMEM((hp, 1, BS), F32),
                        pltpu.VMEM((hp, hd + ones_rows, BS), F32)],
        compiler_params=_cparams(("parallel", "parallel", "arbitrary")),
        name="moba_attn",
    )(proj, proj, proj)


def _ret_kernel(q_ref, k_ref, v_ref, g_ref, dm_ref, xi_ref, zeta_ref, cd_ref, o_ref, s_ref):
    c = pl.program_id(1)
    d = RET_HEAD_DIM

    @pl.when(c == 0)
    def _():
        s_ref[...] = jnp.zeros_like(s_ref)

    for h in range(RET_HEADS):
        sl = slice(h * d, (h + 1) * d)
        q = q_ref[0, :, sl]
        k = k_ref[0, :, sl]
        v = v_ref[0, :, sl]
        inner = lax.dot_general(q, k, (((1,), (1,)), ((), ())), preferred_element_type=F32) * dm_ref[h]
        S = s_ref[h]
        o = (jnp.dot(inner.astype(BF16), v, preferred_element_type=F32)
             + jnp.dot(q, S.astype(BF16), preferred_element_type=F32) * xi_ref[h])
        kz = (k.astype(F32) * zeta_ref[h]).astype(BF16)
        s_ref[h] = S * cd_ref[h] + lax.dot_general(kz, v, (((0,), (0,)), ((), ())),
                                                   preferred_element_type=F32)
        mu = jnp.mean(o, axis=-1, keepdims=True)
        dlt = o - mu
        var = jnp.mean(dlt * dlt, axis=-1, keepdims=True)
        on = dlt * lax.rsqrt(var + NORM_EPS)
        gg = g_ref[0, :, sl].astype(F32)
        o_ref[0, :, sl] = (on * (gg * jax.nn.sigmoid(gg))).astype(o_ref.dtype)


def _retention(proj, B, T, moba_w, ret_w):
    C, H = RET_CHUNK, RET_HEADS
    gamma = 1.0 - jnp.exp2(-5.0 - jnp.arange(H, dtype=F32))
    log_g = jnp.log(gamma)
    pos = jnp.arange(C, dtype=F32)
    diff = pos[:, None] - pos[None, :]
    dmask = jnp.where(diff >= 0, jnp.exp(jnp.maximum(diff, 0.0) * log_g[:, None, None]), 0.0)
    xi = jnp.exp((pos + 1.0) * log_g[:, None])[:, :, None]
    zeta = jnp.exp((C - 1.0 - pos) * log_g[:, None])[:, :, None]
    cd = jnp.exp(C * log_g)[:, None, None]
    base = 3 * moba_w // ret_w
    col = lambda off: pl.BlockSpec((1, C, ret_w), lambda b, c: (b, c, base + off))
    full = lambda shp: pl.BlockSpec(shp, lambda b, c: (0,) * len(shp))
    return pl.pallas_call(
        _ret_kernel,
        grid=(B, T // C),
        in_specs=[col(0), col(1), col(2), col(3),
                  full((H, C, C)), full((H, C, 1)), full((H, C, 1)), full((H, 1, 1))],
        out_specs=pl.BlockSpec((1, C, ret_w), lambda b, c: (b, c, 0)),
        out_shape=jax.ShapeDtypeStruct((B, T, ret_w), BF16),
        scratch_shapes=[pltpu.VMEM((H, RET_HEAD_DIM, RET_HEAD_DIM), F32)],
        compiler_params=_cparams(("parallel", "arbitrary")),
        name="retention",
    )(proj, proj, proj, proj, dmask, xi, zeta, cd)


def _outproj_kernel(oa_ref, or_ref, w_ref, x_ref, ga_ref, g_ref, sh_ref, sc_ref, wrh_ref, wrl_ref,
                    x1_ref, hp_ref, lg_ref, *, moba_w):
    mix = (jnp.dot(oa_ref[0], w_ref[:moba_w, :], preferred_element_type=F32)
           + jnp.dot(or_ref[0], w_ref[moba_w:, :], preferred_element_type=F32))
    x1 = x_ref[0] + ga_ref[0] * mix
    x1_ref[0] = x1
    h = _rms_mod(x1, g_ref[...], sh_ref[0], sc_ref[0])
    half = h.shape[-1] // 2
    _store_rows_as_tiles(_flat_tiles(hp_ref), _pack_pair(h[:, :half], h[:, half:]))
    h_hi = h.astype(BF16)
    h_lo = (h - h_hi.astype(F32)).astype(BF16)
    lg_ref[0] = (jnp.dot(h_hi, wrh_ref[...], preferred_element_type=F32)
                 + (jnp.dot(h_lo, wrh_ref[...], preferred_element_type=F32)
                    + jnp.dot(h_hi, wrl_ref[...], preferred_element_type=F32)))


def _out_proj(o_a, o_r, w_bf, x, gate_a, g, shift, scale, w_router):
    B, T, D = x.shape
    moba_w, ret_w = o_a.shape[-1], o_r.shape[-1]
    E = w_router.shape[1]
    tm = 256
    wr_hi = w_router.astype(BF16)
    wr_lo = (w_router - wr_hi.astype(F32)).astype(BF16)
    vec = pl.BlockSpec((1, 1, D), lambda b, i: (b, 0, 0))
    row = lambda w: pl.BlockSpec((1, tm, w), lambda b, i: (b, i, 0))
    return pl.pallas_call(
        functools.partial(_outproj_kernel, moba_w=moba_w),
        grid=(B, T // tm),
        in_specs=[row(moba_w), row(ret_w),
                  pl.BlockSpec((moba_w + ret_w, D), lambda b, i: (0, 0)),
                  row(D), vec,
                  pl.BlockSpec((1, D), lambda b, i: (0, 0)),
                  vec, vec,
                  pl.BlockSpec((D, E), lambda b, i: (0, 0)),
                  pl.BlockSpec((D, E), lambda b, i: (0, 0))],
        out_specs=[row(D), pl.BlockSpec((1, tm, SUBLANES, LANES), lambda b, i: (b, i, 0, 0)), row(E)],
        out_shape=[jax.ShapeDtypeStruct((B, T, D), F32),
                   jax.ShapeDtypeStruct((B, T, SUBLANES, LANES), U32),
                   jax.ShapeDtypeStruct((B, T, E), F32)],
        compiler_params=_cparams(("parallel", "parallel")),
        name="out_proj",
    )(o_a, o_r, w_bf, x, gate_a[:, None, :], g[None, :], shift[:, None, :], scale[:, None, :], wr_hi, wr_lo)


def _route_kernel_t(lg_ref, b_ref, selr_ref, wf_ref, rank_ref, cnt_ref, carry_ref):
    @pl.when((pl.program_id(0) == 0) & (pl.program_id(1) == 0))
    def _():
        carry_ref[...] = jnp.zeros_like(carry_ref)

    E = N_EXPERTS
    gsz = E // N_GROUPS
    assert gsz == SUBLANES and N_GROUPS == SUBLANES, "a routing group is one sublane tile of experts"
    s = jax.nn.sigmoid(lg_ref[0].T)
    biased = s + b_ref[...]
    tm = s.shape[1]
    sub = lax.broadcasted_iota(jnp.int32, (SUBLANES, tm), 0).astype(F32)
    eid = lax.broadcasted_iota(jnp.int32, (E, tm), 0).astype(F32)

    def first_argmax(v, ids, width):
        m = jnp.max(v, axis=0, keepdims=True)
        idx = jnp.min(jnp.where(v == m, ids, float(width)), axis=0, keepdims=True)
        return m, idx

    gscore = jnp.zeros((N_GROUPS, tm), F32)
    for gi in range(N_GROUPS):
        v = biased[gi * gsz:(gi + 1) * gsz, :]
        m1, i1 = first_argmax(v, sub, gsz)
        m2 = jnp.max(jnp.where(sub == i1, -jnp.inf, v), axis=0, keepdims=True)
        gscore = jnp.where(sub == float(gi), m1 + m2, gscore)

    gsel = jnp.zeros((N_GROUPS, tm), F32)
    for _ in range(TOPK_GROUPS):
        _, gi = first_argmax(gscore, sub, N_GROUPS)
        pick = sub == gi
        gsel = jnp.where(pick, 1.0, gsel)
        gscore = jnp.where(pick, -jnp.inf, gscore)

    cand = jnp.concatenate(
        [jnp.where(gsel[gi:gi + 1, :] > 0.0, biased[gi * gsz:(gi + 1) * gsz, :], NEG) for gi in range(N_GROUPS)],
        axis=0)
    selr = jnp.zeros((E, tm), F32)
    for r in range(TOP_K):
        _, ei = first_argmax(cand, eid, E)
        pick = eid == ei
        selr = jnp.where(pick, float(r + 1), selr)
        cand = jnp.where(pick, -jnp.inf, cand)

    chosen = selr > 0.0
    w = jnp.where(chosen, s, 0.0)
    wsum = jnp.sum(w, axis=0, keepdims=True)
    selr_ref[0] = selr
    wf_ref[0] = w / wsum * ROUTE_SCALE

    onehot = chosen.astype(BF16)
    c_i = lax.broadcasted_iota(jnp.int32, (tm, tm), 0)
    r_i = lax.broadcasted_iota(jnp.int32, (tm, tm), 1)
    tri = (c_i < r_i).astype(BF16)
    carry = carry_ref[...]
    rank_ref[0] = jnp.dot(onehot, tri, preferred_element_type=F32) + carry
    carry = carry + jnp.sum(chosen.astype(F32), axis=1, keepdims=True)
    carry_ref[...] = carry
    cnt_ref[...] = carry


def _route_t(logits_t, bias):
    B, T, E = logits_t.shape
    tm = 512
    blk = pl.BlockSpec((1, E, tm), lambda b, i: (b, 0, i))
    col = pl.BlockSpec((E, 1), lambda b, i: (0, 0))
    full = jax.ShapeDtypeStruct((B, E, T), F32)
    return pl.pallas_call(
        _route_kernel_t,
        grid=(B, T // tm),
        in_specs=[pl.BlockSpec((1, tm, E), lambda b, i: (b, i, 0)), col],
        out_specs=[blk, blk, blk, col],
        out_shape=[full, full, full, jax.ShapeDtypeStruct((E, 1), F32)],
        scratch_shapes=[pltpu.VMEM((E, 1), F32)],
        compiler_params=_cparams(("arbitrary", "arbitrary")),
        name="route_topk",
    )(logits_t, bias[:, None])


def _dest_kernel_t(selr_ref, wf_ref, rank_ref, ps_ref, dest_ref, wk_ref):
    selr = selr_ref[0]
    destfull = rank_ref[0] + ps_ref[...]
    wf = wf_ref[0]
    for r in range(TOP_K):
        hit = selr == float(r + 1)
        dest_ref[0, r:r + 1, :] = jnp.sum(jnp.where(hit, destfull, 0.0), axis=0, keepdims=True).astype(jnp.int32)
        wk_ref[0, r:r + 1, :] = jnp.sum(jnp.where(hit, wf, 0.0), axis=0, keepdims=True)


def _dest_t(selr, wf, rank, pstart_f):
    B, E, T = selr.shape
    tm = 512
    blk = pl.BlockSpec((1, E, tm), lambda b, i: (b, 0, i))
    outb = pl.BlockSpec((1, TOP_K, tm), lambda b, i: (b, 0, i))
    return pl.pallas_call(
        _dest_kernel_t,
        grid=(B, T // tm),
        in_specs=[blk, blk, blk, pl.BlockSpec((E, 1), lambda b, i: (0, 0))],
        out_specs=[outb, outb],
        out_shape=[jax.ShapeDtypeStruct((B, TOP_K, T), jnp.int32), jax.ShapeDtypeStruct((B, TOP_K, T), F32)],
        compiler_params=_cparams(("parallel", "parallel")),
        name="route_dest",
    )(selr, wf, rank, pstart_f)


def _row_copy(src, s_row, dst, d_row, n, sem):
    return pltpu.make_async_copy(src.at[pl.ds(s_row, n)], dst.at[pl.ds(d_row, n)], sem)


def _dispatch_kernel(padlo_ref, padn_ref, dest_ref, h_ref, z_ref, xs_ref, sem, zsem, *, tt, n_exp):
    i = pl.program_id(0)

    def issue(t, _):
        for k in range(TOP_K):
            _row_copy(h_ref, t, xs_ref, dest_ref[0, 0, k * tt + t], 1, sem).start(priority=k % 2)
        return 0

    lax.fori_loop(0, tt, issue, 0, unroll=2)

    def each_pad(fn):
        def per_expert(e, _):
            lo = padlo_ref[e]

            def one(r, _):
                fn(lo + r)
                return 0

            lax.fori_loop(0, padn_ref[e], one, 0)
            return 0

        lax.fori_loop(0, n_exp, per_expert, 0)

    @pl.when(i == 0)
    def _():
        each_pad(lambda r: _row_copy(z_ref, 0, xs_ref, r, 1, zsem).start())

    for k in range(TOP_K):
        _row_copy(h_ref, 0, xs_ref, 0, tt, sem).wait()

    @pl.when(i == 0)
    def _():
        each_pad(lambda r: _row_copy(z_ref, 0, xs_ref, 0, 1, zsem).wait())


def _dispatch(h2p, dest, pad_lo, pad_n, R):
    N = h2p.shape[0]
    tile = h2p.shape[1:]
    tt = TOK_TILE
    dest3 = dest
    zeros = jnp.zeros((SUBLANES,) + tile, h2p.dtype)
    grid_spec = pltpu.PrefetchScalarGridSpec(
        num_scalar_prefetch=2,
        grid=(N // tt,),
        in_specs=[pl.BlockSpec((1, 1, tt * TOP_K), lambda i, lo, n: (i, 0, 0), memory_space=pltpu.SMEM),
                  pl.BlockSpec((tt,) + tile, lambda i, lo, n: (i, 0, 0)),
                  pl.BlockSpec((SUBLANES,) + tile, lambda i, lo, n: (0, 0, 0))],
        out_specs=pl.BlockSpec(memory_space=pl.ANY),
        scratch_shapes=[pltpu.SemaphoreType.DMA(()), pltpu.SemaphoreType.DMA(())],
    )
    return pl.pallas_call(
        functools.partial(_dispatch_kernel, tt=tt, n_exp=N_EXPERTS),
        grid_spec=grid_spec,
        out_shape=jax.ShapeDtypeStruct((R,) + tile, h2p.dtype),
        compiler_params=_cparams(("arbitrary",)),
        name="moe_dispatch",
    )(pad_lo, pad_n, dest3, h2p, zeros)


def _expert_kernel(te_ref, nu_ref, first_ref, slot_ref, nxt_ref, xs_ref, wg_hbm, wu_hbm, wd_hbm, y_ref,
                   stg, stu, std, wgb, wub, wdb, sem):
    i = pl.program_id(0)
    n_chunks = 2

    def fetch(e, s):
        cps = []
        for m, (src, dst) in enumerate(((wg_hbm, stg), (wu_hbm, stu), (wd_hbm, std))):
            rows = dst.shape[1] // n_chunks
            for c in range(n_chunks):
                cps.append(pltpu.make_async_copy(src.at[e, pl.ds(c * rows, rows)],
                                                 dst.at[s, pl.ds(c * rows, rows)], sem.at[s, m]))
        return cps

    @pl.when(i == 0)
    def _():
        for cp in fetch(te_ref[0], 0):
            cp.start(priority=1)

        @pl.when(nxt_ref[0, 0] >= 0)
        def _():
            for cp in fetch(nxt_ref[0, 0], 1):
                cp.start(priority=1)

    active = i < nu_ref[0]

    def swiglu_tile(wg, wu, wd):
        m_rows = xs_ref.shape[0]
        lo, hi = _unpack_pair(_load_tiles_as_rows(_flat_tiles(xs_ref), m_rows))
        lo = lo.astype(BF16)
        hi = hi.astype(BF16)
        half = lo.shape[-1]
        a = (jnp.dot(lo, wg[:half, :], preferred_element_type=F32)
             + jnp.dot(hi, wg[half:, :], preferred_element_type=F32))
        u = (jnp.dot(lo, wu[:half, :], preferred_element_type=F32)
             + jnp.dot(hi, wu[half:, :], preferred_element_type=F32))
        hmid = (a * jax.nn.sigmoid(a) * u).astype(BF16)
        y = jnp.dot(hmid, wd, preferred_element_type=F32)
        _store_rows_as_tiles(_flat_tiles(y_ref), _pack_pair(y[:, :half], y[:, half:]))

    @pl.when(active & (first_ref[i] == 1))
    def _():
        s = slot_ref[i]
        for cp in fetch(0, s):
            cp.wait()
        for src, dst in ((stg, wgb), (stu, wub), (std, wdb)):
            rows = 32 * SUBLANES * LANES // dst.shape[1]

            def cast_chunk(r, _, src=src, dst=dst, rows=rows):
                off = pl.multiple_of(r * rows, rows)
                dst[pl.ds(off, rows), :] = src[s, pl.ds(off, rows), :].astype(BF16)
                return 0

            lax.fori_loop(0, dst.shape[0] // rows, cast_chunk, 0, unroll=2)

        @pl.when(nxt_ref[1, i] >= 0)
        def _():
            for cp in fetch(nxt_ref[1, i], s):
                cp.start(priority=1)

    @pl.when(active)
    def _():
        swiglu_tile(wgb[...], wub[...], wdb[...])


def _experts(xs, tile_expert, n_used, first, slot, nxt, wg, wu, wd):
    R = xs.shape[0]
    tile = xs.shape[1:]
    M = ROW_TILE
    _, D, F = wg.shape
    row = lambda i, te, nu, fi, sl, nx: (jnp.minimum(i, nu[0] - 1), 0, 0)
    grid_spec = pltpu.PrefetchScalarGridSpec(
        num_scalar_prefetch=5,
        grid=(R // M,),
        in_specs=[pl.BlockSpec((M,) + tile, row),
                  pl.BlockSpec(memory_space=pl.ANY),
                  pl.BlockSpec(memory_space=pl.ANY),
                  pl.BlockSpec(memory_space=pl.ANY)],
        out_specs=pl.BlockSpec((M,) + tile, row),
        scratch_shapes=[pltpu.VMEM((2, D, F), F32), pltpu.VMEM((2, D, F), F32), pltpu.VMEM((2, F, D), F32),
                        pltpu.VMEM((D, F), BF16), pltpu.VMEM((D, F), BF16), pltpu.VMEM((F, D), BF16),
                        pltpu.SemaphoreType.DMA((2, 3))],
    )
    return pl.pallas_call(
        _expert_kernel,
        grid_spec=grid_spec,
        out_shape=jax.ShapeDtypeStruct((R,) + tile, U32),
        compiler_params=_cparams(("arbitrary",)),
        name="moe_experts",
    )(tile_expert, n_used, first, slot, nxt, xs, wg, wu, wd)


def _final_kernel(dcur_ref, dnxt_ref, h_ref, wsg_ref, wsu_ref, wsd_ref, x1_ref, gf_ref, g_ref, wk_ref, y_ref,
                  o_ref, ybuf, sem, *, tt, n_tiles):
    i = pl.program_id(0)
    slot = lax.rem(i, 2)

    def gather(d_ref, s):
        def issue(t, _):
            for k in range(TOP_K):
                pltpu.make_async_copy(y_ref.at[pl.ds(d_ref[0, 0, k * tt + t], 1)],
                                      ybuf.at[s, k, pl.ds(t, 1)], sem.at[s]).start(priority=k % 2)
            return 0

        lax.fori_loop(0, tt, issue, 0, unroll=2)

    @pl.when(i == 0)
    def _():
        gather(dcur_ref, 0)

    for p in range(2):
        @pl.when((i + 1 < n_tiles) & (slot == p))
        def _(p=p):
            gather(dnxt_ref, 1 - p)

    lo, hi = _unpack_pair(_load_tiles_as_rows(_flat_tiles(h_ref), tt))
    lo = lo.astype(BF16)
    hi = hi.astype(BF16)
    half = lo.shape[-1]
    a = (jnp.dot(lo, wsg_ref[:half, :], preferred_element_type=F32)
         + jnp.dot(hi, wsg_ref[half:, :], preferred_element_type=F32))
    u = (jnp.dot(lo, wsu_ref[:half, :], preferred_element_type=F32)
         + jnp.dot(hi, wsu_ref[half:, :], preferred_element_type=F32))
    hmid = (a * jax.nn.sigmoid(a) * u).astype(BF16)
    shared = jnp.dot(hmid, wsd_ref[...], preferred_element_type=F32)

    for k in range(TOP_K):
        pltpu.make_async_copy(y_ref.at[pl.ds(0, tt)], ybuf.at[slot, k], sem.at[slot]).wait()

    wk = wk_ref[...]
    yflat = _flat_tiles(ybuf)

    for p in range(2):
        @pl.when(slot == p)
        def _(p=p):
            r_lo = jnp.zeros((tt, half), F32)
            r_hi = jnp.zeros((tt, half), F32)
            for k in range(TOP_K):
                ylo, yhi = _unpack_pair(_load_tiles_as_rows(yflat, tt, base=(p * TOP_K + k) * tt))
                wcol = wk[:, k:k + 1]
                r_lo = r_lo + wcol * ylo
                r_hi = r_hi + wcol * yhi
            total = shared + jnp.concatenate([r_lo, r_hi], axis=1)
            x2 = x1_ref[...] + gf_ref[0] * total
            ms = jnp.mean(x2 * x2, axis=-1, keepdims=True)
            o_ref[...] = x2 * lax.rsqrt(ms + NORM_EPS) * g_ref[...]


def _final(h2p, wsg, wsu, wsd, x1, gate_f, norm_out, dest, wk, y, T):
    N = h2p.shape[0]
    tile = h2p.shape[1:]
    D = x1.shape[-1]
    F = wsg.shape[1]
    tt = TOK_TILE
    per_b = T // tt
    n_tiles = N // tt
    dest3 = dest
    rowb = lambda w: pl.BlockSpec((tt, w), lambda i: (i, 0))
    const = lambda shp: pl.BlockSpec(shp, lambda i: (0,) * len(shp))
    dspec = lambda f: pl.BlockSpec((1, 1, tt * TOP_K), f, memory_space=pltpu.SMEM)
    return pl.pallas_call(
        functools.partial(_final_kernel, tt=tt, n_tiles=n_tiles),
        grid=(n_tiles,),
        in_specs=[dspec(lambda i: (i, 0, 0)),
                  dspec(lambda i: (jnp.minimum(i + 1, n_tiles - 1), 0, 0)),
                  pl.BlockSpec((tt,) + tile, lambda i: (i, 0, 0)),
                  const((D, F)), const((D, F)), const((F, D)), rowb(D),
                  pl.BlockSpec((1, 1, D), lambda i: (i // per_b, 0, 0)),
                  const((1, D)),
                  pl.BlockSpec((tt, TOP_K), lambda i: (i, 0)),
                  pl.BlockSpec(memory_space=pl.ANY)],
        out_specs=rowb(D),
        out_shape=jax.ShapeDtypeStruct((N, D), F32),
        scratch_shapes=[pltpu.VMEM((2, TOP_K, tt) + tile, U32), pltpu.SemaphoreType.DMA((2,))],
        compiler_params=_cparams(("arbitrary",)),
        name="moe_combine_final",
    )(dest3, dest3, h2p, wsg, wsu, wsd, x1, gate_f[:, None, :], norm_out[None, :], wk, y)


def kernel(x, c, positions, w_ada, b_ada, norm_mix, norm_ffn, norm_out, w_in, w_out, w_router, router_bias,
           w_gate, w_up, w_down, w_sh_gate, w_sh_up, w_sh_down):
    B, T, D = x.shape
    depth = w_ada.shape[0]
    assert depth == 1, "the final rmsnorm is fused into the layer's last kernel"
    moba_w = MOBA_HEADS * MOBA_HEAD_DIM
    ret_w = RET_HEADS * RET_HEAD_DIM
    N = B * T
    E, M = N_EXPERTS, ROW_TILE
    R = N * TOP_K + E * M
    n_tiles = R // M
    tabs = _rope_tables(positions)

    for l in range(depth):
        mod = _ada(c, w_ada[l], b_ada[l])
        shift_a, scale_a, gate_a, shift_f, scale_f, gate_f = jnp.split(mod, 6, axis=-1)

        proj = _in_proj(x, norm_mix[l], shift_a, scale_a, w_in[l].astype(BF16), tabs, moba_w, ret_w)
        o_a = _moba(proj, B, T, moba_w)
        o_r = _retention(proj, B, T, moba_w, ret_w)
        x1, h2p, logits = _out_proj(o_a, o_r, w_out[l].astype(BF16), x, gate_a, norm_ffn[l],
                                    shift_f, scale_f, w_router[l])

        assert D // 2 == SUBLANES * LANES, "a packed row must fill exactly one (SUBLANES, LANES) tile"
        h2p = h2p.reshape(N, SUBLANES, LANES)
        selr, wf, rank, counts = _route_t(logits, router_bias[l])
        cnt = counts[:, 0].astype(jnp.int32)
        pcnt = (cnt + M - 1) // M * M
        pend = jnp.cumsum(pcnt)
        pstart = pend - pcnt
        tidx = jnp.arange(n_tiles, dtype=jnp.int32)
        end_tile = pend // M
        eids = jnp.arange(E, dtype=jnp.int32)
        owner = lambda v: jnp.minimum(jnp.sum(end_tile[None, :] <= v[:, None], axis=1), E - 1).astype(jnp.int32)
        end_of = lambda e: jnp.sum(jnp.where(e[:, None] == eids[None, :], end_tile[None, :], 0), axis=1)
        tile_expert = owner(tidx)
        n_used = end_tile[-1:].astype(jnp.int32)
        first = (((tidx == 0) | (tile_expert != jnp.roll(tile_expert, 1))) & (tidx < n_used[0])).astype(jnp.int32)
        slot = ((jnp.cumsum(first) - 1) % 2).astype(jnp.int32)
        end1 = end_of(tile_expert)
        exp1 = owner(end1)
        end2 = end_of(exp1)
        exp2 = owner(end2)
        has1 = end1 < n_used[0]
        nxt = jnp.stack([jnp.where(has1, exp1, -1),
                         jnp.where(has1 & (end2 < n_used[0]), exp2, -1)]).astype(jnp.int32)
        dest, wk = _dest_t(selr, wf, rank, pstart.astype(F32)[:, None])
        tt = TOK_TILE
        dest = dest.reshape(B, TOP_K, T // tt, tt).transpose(0, 2, 1, 3).reshape(N // tt, 1, TOP_K * tt)
        wk = wk.transpose(0, 2, 1).reshape(N, TOP_K)

        xs = _dispatch(h2p, dest, (pstart + cnt).astype(jnp.int32), (pcnt - cnt).astype(jnp.int32), R)
        y = _experts(xs, tile_expert, n_used, first, slot, nxt, w_gate[l], w_up[l], w_down[l])
        out = _final(h2p, w_sh_gate[l].astype(BF16), w_sh_up[l].astype(BF16), w_sh_down[l].astype(BF16),
                     x1.reshape(N, D), gate_f, norm_out, dest, wk, y, T)
        x = out.reshape(B, T, D)
    return x
```

```python
import functools

import jax
import jax.numpy as jnp
from jax import lax
from jax.experimental import pallas as pl
from jax.experimental.pallas import tpu as pltpu

MOBA_HEADS = 8
MOBA_HEAD_DIM = 128
MOBA_BLOCK = 256
MOBA_TOPK = 3
ROPE_THETA = 500000.0
ROPE_DIMS = 32
RET_HEADS = 4
RET_HEAD_DIM = 256
RET_ROPE_BASE = 10000.0
N_EXPERTS = 64
TOP_K = 8
N_GROUPS = 8
TOPK_GROUPS = 4
ROUTE_SCALE = 2.5
NORM_EPS = 1e-6
NEG = -1e30

LANES = 128
SUBLANES = 8
VMEM_LIMIT = 56 * 1024 * 1024

RET_CHUNK = 256
ROW_TILE = 256
TOK_TILE = 256

F32 = jnp.float32
BF16 = jnp.bfloat16
U32 = jnp.uint32


def _cparams(sem):
    return pltpu.CompilerParams(dimension_semantics=sem, vmem_limit_bytes=VMEM_LIMIT)


def _rms_mod(xf, g, shift, scale):
    ms = jnp.mean(xf * xf, axis=-1, keepdims=True)
    y = xf * lax.rsqrt(ms + NORM_EPS) * g
    return y * (1.0 + scale) + shift


def _pack_pair(lo, hi):
    lo_b = lax.bitcast_convert_type(lo.astype(BF16).astype(F32), U32)
    hi_b = lax.bitcast_convert_type(hi.astype(BF16).astype(F32), U32)
    return (lo_b >> 16) | hi_b


def _unpack_pair(p):
    lo = lax.bitcast_convert_type(p << 16, F32)
    hi = lax.bitcast_convert_type(p & jnp.uint32(0xFFFF0000), F32)
    return lo, hi


def _store_rows_as_tiles(ref, val, base=0):
    n = val.shape[0]
    for c in range(SUBLANES):
        ref[pl.ds(base * SUBLANES + c, n, stride=SUBLANES), :] = val[:, c * LANES:(c + 1) * LANES]


def _load_tiles_as_rows(ref, n, base=0):
    return jnp.concatenate([ref[pl.ds(base * SUBLANES + c, n, stride=SUBLANES), :] for c in range(SUBLANES)],
                           axis=1)


def _flat_tiles(ref):
    rows = 1
    for d in ref.shape[:-2]:
        rows *= d
    return ref.reshape(rows * SUBLANES, LANES)


def _tables_kernel(pos_ref, invm_ref, invr_ref, mc_ref, ms1_ref, ms2_ref, rc_ref, rs_ref):
    pos = pos_ref[0].astype(F32)
    angm = pos * invm_ref[...]
    lane = lax.broadcasted_iota(jnp.int32, angm.shape, 1)
    half = ROPE_DIMS // 2
    c = jnp.cos(angm)
    s = jnp.sin(angm)
    mc_ref[0] = c
    ms1_ref[0] = jnp.where(lane < half, -s, 0.0)
    ms2_ref[0] = jnp.where((lane >= half) & (lane < ROPE_DIMS), s, 0.0)
    angr = pos * invr_ref[...]
    rc_ref[0] = jnp.cos(angr)
    rs_ref[0] = jnp.sin(angr)


def _rope_tables(positions):
    B, T = positions.shape
    tm = 512
    half = ROPE_DIMS // 2
    moba_inv = ROPE_THETA ** (-(jnp.arange(half, dtype=F32) * 2.0 / ROPE_DIMS))
    invm = jnp.concatenate([moba_inv, moba_inv, jnp.zeros((LANES - ROPE_DIMS,), F32)])[None, :]
    invr = (RET_ROPE_BASE ** (-jnp.linspace(0.0, 1.0, RET_HEAD_DIM // 2, dtype=F32)))[None, :]
    pos3 = positions.reshape(B, T, 1)
    tab = jax.ShapeDtypeStruct((B, T, LANES), F32)
    spec = pl.BlockSpec((1, tm, LANES), lambda b, i: (b, i, 0))
    return pl.pallas_call(
        _tables_kernel,
        grid=(B, T // tm),
        in_specs=[pl.BlockSpec((1, tm, 1), lambda b, i: (b, i, 0)),
                  pl.BlockSpec((1, LANES), lambda b, i: (0, 0)),
                  pl.BlockSpec((1, LANES), lambda b, i: (0, 0))],
        out_specs=[spec] * 5,
        out_shape=[tab] * 5,
        compiler_params=_cparams(("parallel", "parallel")),
        name="rope_tables",
    )(pos3, invm, invr)


def _ada_kernel(ct_ref, w_ref, b_ref, o_ref):
    ct = ct_ref[...]
    sct = ct * jax.nn.sigmoid(ct)
    w = w_ref[...]
    for b in range(ct.shape[1]):
        o_ref[b:b + 1, :] = jnp.sum(w * sct[:, b:b + 1], axis=0, keepdims=True) + b_ref[...]


def _ada(c, w_ada, b_ada):
    B, D = c.shape
    n_out = w_ada.shape[1]
    tn = 1024
    return pl.pallas_call(
        _ada_kernel,
        grid=(n_out // tn,),
        in_specs=[pl.BlockSpec((D, B), lambda j: (0, 0)),
                  pl.BlockSpec((D, tn), lambda j: (0, j)),
                  pl.BlockSpec((1, tn), lambda j: (0, j))],
        out_specs=pl.BlockSpec((B, tn), lambda j: (0, j)),
        out_shape=jax.ShapeDtypeStruct((B, n_out), F32),
        compiler_params=_cparams(("parallel",)),
        name="adaln_mod",
    )(c.T, w_ada, b_ada[None, :])


def _inproj_kernel(x_ref, g_ref, sh_ref, sc_ref, w_ref, mc_ref, ms1_ref, ms2_ref, rc_ref, rs_ref,
                   o_ref, hn_ref, *, tn, moba_tiles, ret_lo, ret_k_lo, ret_hi):
    j = pl.program_id(2)

    @pl.when(j == 0)
    def _():
        h = _rms_mod(x_ref[0], g_ref[...], sh_ref[0], sc_ref[0])
        hn_ref[...] = h.astype(BF16)

    acc = jnp.dot(hn_ref[...], w_ref[...], preferred_element_type=F32)

    @pl.when(j < moba_tiles)
    def _():
        c, s1, s2 = mc_ref[0], ms1_ref[0], ms2_ref[0]
        half = ROPE_DIMS // 2
        for g in range(tn // LANES):
            a = acc[:, g * LANES:(g + 1) * LANES]
            r = a * c + pltpu.roll(a, LANES - half, 1) * s1 + pltpu.roll(a, half, 1) * s2
            o_ref[0, :, g * LANES:(g + 1) * LANES] = r.astype(o_ref.dtype)

    @pl.when((j >= ret_lo) & (j < ret_hi))
    def _():
        c, s = rc_ref[0], rs_ref[0]
        fac = jnp.where(j >= ret_k_lo, RET_HEAD_DIM ** -0.5, 1.0).astype(F32)
        hw = RET_HEAD_DIM // 2
        for g in range(tn // RET_HEAD_DIM):
            x1 = acc[:, g * RET_HEAD_DIM:g * RET_HEAD_DIM + hw]
            x2 = acc[:, g * RET_HEAD_DIM + hw:(g + 1) * RET_HEAD_DIM]
            o_ref[0, :, g * RET_HEAD_DIM:g * RET_HEAD_DIM + hw] = ((x1 * c - x2 * s) * fac).astype(o_ref.dtype)
            o_ref[0, :, g * RET_HEAD_DIM + hw:(g + 1) * RET_HEAD_DIM] = ((x2 * c + x1 * s) * fac).astype(o_ref.dtype)

    @pl.when(((j >= moba_tiles) & (j < ret_lo)) | (j >= ret_hi))
    def _():
        o_ref[0] = acc.astype(o_ref.dtype)


def _in_proj(x, g, shift, scale, w_bf, tabs, moba_w, ret_w):
    B, T, D = x.shape
    NC = w_bf.shape[1]
    tm, tn = 1024, 1024
    mc, ms1, ms2, rc, rs = tabs
    kern = functools.partial(
        _inproj_kernel, tn=tn,
        moba_tiles=2 * moba_w // tn,
        ret_lo=3 * moba_w // tn,
        ret_k_lo=(3 * moba_w + ret_w) // tn,
        ret_hi=(3 * moba_w + 2 * ret_w) // tn)
    tab_spec = pl.BlockSpec((1, tm, LANES), lambda b, i, j: (b, i, 0))
    vec_spec = pl.BlockSpec((1, 1, D), lambda b, i, j: (b, 0, 0))
    return pl.pallas_call(
        kern,
        grid=(B, T // tm, NC // tn),
        in_specs=[pl.BlockSpec((1, tm, D), lambda b, i, j: (b, i, 0)),
                  pl.BlockSpec((1, D), lambda b, i, j: (0, 0)),
                  vec_spec, vec_spec,
                  pl.BlockSpec((D, tn), lambda b, i, j: (0, j)),
                  tab_spec, tab_spec, tab_spec, tab_spec, tab_spec],
        out_specs=pl.BlockSpec((1, tm, tn), lambda b, i, j: (b, i, j)),
        out_shape=jax.ShapeDtypeStruct((B, T, NC), BF16),
        scratch_shapes=[pltpu.VMEM((tm, D), BF16)],
        compiler_params=_cparams(("parallel", "parallel", "arbitrary")),
        name="in_proj",
    )(x, g[None, :], shift[:, None, :], scale[:, None, :], w_bf, mc, ms1, ms2, rc, rs)


def _moba_kernel(q_ref, k_ref, v_ref, o_ref, km_ref, vt_ref, sel_ref, qs_ref, s_ref, m_ref, acc_ref,
                 *, nb, hp):
    qb = pl.program_id(2)
    BS, hd = MOBA_BLOCK, MOBA_HEAD_DIM

    @pl.when(qb == 0)
    def _():
        for h in range(hp):
            hs = slice(h * hd, (h + 1) * hd)
            for n in range(nb):
                kb = k_ref[0, n * BS:(n + 1) * BS, hs].astype(F32)
                km_ref[h, n:n + 1, :] = jnp.sum(kb, axis=0, keepdims=True) * (1.0 / BS)
                vt_ref[h, n, :hd, :] = v_ref[0, n * BS:(n + 1) * BS, hs].astype(F32).T.astype(BF16)
                vt_ref[h, n, hd:, :] = jnp.ones((vt_ref.shape[2] - hd, BS), BF16)

    scale = hd ** -0.5
    own = pl.multiple_of(qb * BS, BS)
    blk = lax.broadcasted_iota(jnp.int32, (nb, BS), 0)
    blk_f = blk.astype(F32)

    for h in range(hp):
        hs = slice(h * hd, (h + 1) * hd)
        qT = q_ref[0, :, hs].astype(F32).T
        gate = jnp.dot(km_ref[h], qT, preferred_element_type=F32,
                       precision=lax.Precision.HIGHEST)
        g = jnp.where(blk < qb, gate, NEG)
        sel = jnp.zeros((nb, BS), F32)
        for _ in range(MOBA_TOPK):
            m = jnp.max(g, axis=0, keepdims=True)
            idx = jnp.min(jnp.where(g == m, blk_f, float(nb)), axis=0, keepdims=True)
            pick = blk_f == idx
            sel = jnp.where(pick & (m > 0.5 * NEG), 1.0, sel)
            g = jnp.where(pick, -jnp.inf, g)
        sel_ref[h] = sel
        qs_ref[h] = (qT * scale).astype(BF16)
        m_ref[h] = jnp.full((1, BS), NEG, F32)

    n_pairs = lax.shift_right_logical(qb + 1, 1)

    def sweep_scores(j, _):
        for h in range(hp):
            mh = m_ref[h]
            for u in range(2):
                n = 2 * j + u
                off = pl.multiple_of(n * BS, BS)
                s = jnp.dot(k_ref[0, pl.ds(off, BS), h * hd:(h + 1) * hd], qs_ref[h],
                            preferred_element_type=F32)
                s = jnp.where(sel_ref[h, pl.ds(n, 1), :] > 0.0, s, NEG)
                s_ref[h, n] = s
                mh = jnp.maximum(mh, jnp.max(s, axis=0, keepdims=True))
            m_ref[h] = mh
        return 0

    lax.fori_loop(0, n_pairs, sweep_scores, 0)

    krow = lax.broadcasted_iota(jnp.int32, (BS, BS), 0)
    qcol = lax.broadcasted_iota(jnp.int32, (BS, BS), 1)
    for h in range(hp):
        hs = slice(h * hd, (h + 1) * hd)
        s = jnp.dot(k_ref[0, pl.ds(own, BS), hs], qs_ref[h], preferred_element_type=F32)
        s = jnp.where(krow <= qcol, s, NEG)
        m = jnp.maximum(m_ref[h], jnp.max(s, axis=0, keepdims=True))
        m_ref[h] = m
        p = jnp.exp((s - m).astype(BF16))
        acc_ref[h] = jnp.dot(vt_ref[h, qb], p, preferred_element_type=F32)

    def sweep_values(j, _):
        for h in range(hp):
            mh = m_ref[h]
            p0 = jnp.exp((s_ref[h, 2 * j] - mh).astype(BF16))
            p1 = jnp.exp((s_ref[h, 2 * j + 1] - mh).astype(BF16))
            acc_ref[h] = acc_ref[h] + (
                jnp.dot(vt_ref[h, 2 * j], p0, preferred_element_type=F32)
                + jnp.dot(vt_ref[h, 2 * j + 1], p1, preferred_element_type=F32))
        return 0

    lax.fori_loop(0, n_pairs, sweep_values, 0)

    for h in range(hp):
        acc = acc_ref[h]
        o_ref[0, :, h * hd:(h + 1) * hd] = (acc[:hd, :] / acc[hd:hd + 1, :]).T.astype(o_ref.dtype)


def _moba(proj, B, T, moba_w):
    H, hd, BS = MOBA_HEADS, MOBA_HEAD_DIM, MOBA_BLOCK
    nb = T // BS
    hp = 4
    ones_rows = 2 * SUBLANES
    gw = hp * hd
    gpw = moba_w // gw
    return pl.pallas_call(
        functools.partial(_moba_kernel, nb=nb, hp=hp),
        grid=(B, H // hp, nb),
        in_specs=[pl.BlockSpec((1, BS, gw), lambda b, h, i: (b, i, h)),
                  pl.BlockSpec((1, T, gw), lambda b, h, i: (b, 0, gpw + h)),
                  pl.BlockSpec((1, T, gw), lambda b, h, i: (b, 0, 2 * gpw + h))],
        out_specs=pl.BlockSpec((1, BS, gw), lambda b, h, i: (b, i, h)),
        out_shape=jax.ShapeDtypeStruct((B, T, moba_w), BF16),
        scratch_shapes=[pltpu.VMEM((hp, nb, hd), F32),
                        pltpu.VMEM((hp, nb, hd + ones_rows, BS), BF16),
                        pltpu.VMEM((hp, nb, BS), F32),
                        pltpu.VMEM((hp, hd, BS), BF16),
                        pltpu.VMEM((hp, nb, BS, BS), F32),
                        pltpu.VMEM((hp, 1, BS), F32),
                        pltpu.VMEM((hp, hd + ones_rows, BS), F32)],
        compiler_params=_cparams(("parallel", "parallel", "arbitrary")),
        name="moba_attn",
    )(proj, proj, proj)


def _ret_kernel(q_ref, k_ref, v_ref, g_ref, dm_ref, xi_ref, zeta_ref, cd_ref, o_ref, s_ref):
    c = pl.program_id(1)
    d = RET_HEAD_DIM

    @pl.when(c == 0)
    def _():
        s_ref[...] = jnp.zeros_like(s_ref)

    for h in range(RET_HEADS):
        sl = slice(h * d, (h + 1) * d)
        q = q_ref[0, :, sl]
        k = k_ref[0, :, sl]
        v = v_ref[0, :, sl]
        inner = lax.dot_general(q, k, (((1,), (1,)), ((), ())), preferred_element_type=F32) * dm_ref[h]
        S = s_ref[h]
        o = (jnp.dot(inner.astype(BF16), v, preferred_element_type=F32)
             + jnp.dot(q, S.astype(BF16), preferred_element_type=F32) * xi_ref[h])
        kz = (k.astype(F32) * zeta_ref[h]).astype(BF16)
        s_ref[h] = S * cd_ref[h] + lax.dot_general(kz, v, (((0,), (0,)), ((), ())),
                                                   preferred_element_type=F32)
        mu = jnp.mean(o, axis=-1, keepdims=True)
        dlt = o - mu
        var = jnp.mean(dlt * dlt, axis=-1, keepdims=True)
        on = dlt * lax.rsqrt(var + NORM_EPS)
        gg = g_ref[0, :, sl].astype(F32)
        o_ref[0, :, sl] = (on * (gg * jax.nn.sigmoid(gg))).astype(o_ref.dtype)


def _retention(proj, B, T, moba_w, ret_w):
    C, H = RET_CHUNK, RET_HEADS
    gamma = 1.0 - jnp.exp2(-5.0 - jnp.arange(H, dtype=F32))
    log_g = jnp.log(gamma)
    pos = jnp.arange(C, dtype=F32)
    diff = pos[:, None] - pos[None, :]
    dmask = jnp.where(diff >= 0, jnp.exp(jnp.maximum(diff, 0.0) * log_g[:, None, None]), 0.0)
    xi = jnp.exp((pos + 1.0) * log_g[:, None])[:, :, None]
    zeta = jnp.exp((C - 1.0 - pos) * log_g[:, None])[:, :, None]
    cd = jnp.exp(C * log_g)[:, None, None]
    base = 3 * moba_w // ret_w
    col = lambda off: pl.BlockSpec((1, C, ret_w), lambda b, c: (b, c, base + off))
    full = lambda shp: pl.BlockSpec(shp, lambda b, c: (0,) * len(shp))
    return pl.pallas_call(
        _ret_kernel,
        grid=(B, T // C),
        in_specs=[col(0), col(1), col(2), col(3),
                  full((H, C, C)), full((H, C, 1)), full((H, C, 1)), full((H, 1, 1))],
        out_specs=pl.BlockSpec((1, C, ret_w), lambda b, c: (b, c, 0)),
        out_shape=jax.ShapeDtypeStruct((B, T, ret_w), BF16),
        scratch_shapes=[pltpu.VMEM((H, RET_HEAD_DIM, RET_HEAD_DIM), F32)],
        compiler_params=_cparams(("parallel", "arbitrary")),
        name="retention",
    )(proj, proj, proj, proj, dmask, xi, zeta, cd)


def _outproj_kernel(oa_ref, or_ref, w_ref, x_ref, ga_ref, g_ref, sh_ref, sc_ref, wrh_ref, wrl_ref,
                    x1_ref, hp_ref, lg_ref, *, moba_w):
    mix = (jnp.dot(oa_ref[0], w_ref[:moba_w, :], preferred_element_type=F32)
           + jnp.dot(or_ref[0], w_ref[moba_w:, :], preferred_element_type=F32))
    x1 = x_ref[0] + ga_ref[0] * mix
    x1_ref[0] = x1
    h = _rms_mod(x1, g_ref[...], sh_ref[0], sc_ref[0])
    half = h.shape[-1] // 2
    _store_rows_as_tiles(_flat_tiles(hp_ref), _pack_pair(h[:, :half], h[:, half:]))
    h_hi = h.astype(BF16)
    h_lo = (h - h_hi.astype(F32)).astype(BF16)
    lg_ref[0] = (jnp.dot(h_hi, wrh_ref[...], preferred_element_type=F32)
                 + (jnp.dot(h_lo, wrh_ref[...], preferred_element_type=F32)
                    + jnp.dot(h_hi, wrl_ref[...], preferred_element_type=F32)))


def _out_proj(o_a, o_r, w_bf, x, gate_a, g, shift, scale, w_router):
    B, T, D = x.shape
    moba_w, ret_w = o_a.shape[-1], o_r.shape[-1]
    E = w_router.shape[1]
    tm = 256
    wr_hi = w_router.astype(BF16)
    wr_lo = (w_router - wr_hi.astype(F32)).astype(BF16)
    vec = pl.BlockSpec((1, 1, D), lambda b, i: (b, 0, 0))
    row = lambda w: pl.BlockSpec((1, tm, w), lambda b, i: (b, i, 0))
    return pl.pallas_call(
        functools.partial(_outproj_kernel, moba_w=moba_w),
        grid=(B, T // tm),
        in_specs=[row(moba_w), row(ret_w),
                  pl.BlockSpec((moba_w + ret_w, D), lambda b, i: (0, 0)),
                  row(D), vec,
                  pl.BlockSpec((1, D), lambda b, i: (0, 0)),
                  vec, vec,
                  pl.BlockSpec((D, E), lambda b, i: (0, 0)),
                  pl.BlockSpec((D, E), lambda b, i: (0, 0))],
        out_specs=[row(D), pl.BlockSpec((1, tm, SUBLANES, LANES), lambda b, i: (b, i, 0, 0)), row(E)],
        out_shape=[jax.ShapeDtypeStruct((B, T, D), F32),
                   jax.ShapeDtypeStruct((B, T, SUBLANES, LANES), U32),
                   jax.ShapeDtypeStruct((B, T, E), F32)],
        compiler_params=_cparams(("parallel", "parallel")),
        name="out_proj",
    )(o_a, o_r, w_bf, x, gate_a[:, None, :], g[None, :], shift[:, None, :], scale[:, None, :], wr_hi, wr_lo)


def _route_kernel_t(lg_ref, b_ref, selr_ref, wf_ref, rank_ref, cnt_ref, carry_ref):
    @pl.when((pl.program_id(0) == 0) & (pl.program_id(1) == 0))
    def _():
        carry_ref[...] = jnp.zeros_like(carry_ref)

    E = N_EXPERTS
    gsz = E // N_GROUPS
    assert gsz == SUBLANES and N_GROUPS == SUBLANES, "a routing group is one sublane tile of experts"
    s = jax.nn.sigmoid(lg_ref[0].T)
    biased = s + b_ref[...]
    tm = s.shape[1]
    sub = lax.broadcasted_iota(jnp.int32, (SUBLANES, tm), 0).astype(F32)
    eid = lax.broadcasted_iota(jnp.int32, (E, tm), 0).astype(F32)

    def first_argmax(v, ids, width):
        m = jnp.max(v, axis=0, keepdims=True)
        idx = jnp.min(jnp.where(v == m, ids, float(width)), axis=0, keepdims=True)
        return m, idx

    gscore = jnp.zeros((N_GROUPS, tm), F32)
    for gi in range(N_GROUPS):
        v = biased[gi * gsz:(gi + 1) * gsz, :]
        m1, i1 = first_argmax(v, sub, gsz)
        m2 = jnp.max(jnp.where(sub == i1, -jnp.inf, v), axis=0, keepdims=True)
        gscore = jnp.where(sub == float(gi), m1 + m2, gscore)

    gsel = jnp.zeros((N_GROUPS, tm), F32)
    for _ in range(TOPK_GROUPS):
        _, gi = first_argmax(gscore, sub, N_GROUPS)
        pick = sub == gi
        gsel = jnp.where(pick, 1.0, gsel)
        gscore = jnp.where(pick, -jnp.inf, gscore)

    cand = jnp.concatenate(
        [jnp.where(gsel[gi:gi + 1, :] > 0.0, biased[gi * gsz:(gi + 1) * gsz, :], NEG) for gi in range(N_GROUPS)],
        axis=0)
    selr = jnp.zeros((E, tm), F32)
    for r in range(TOP_K):
        _, ei = first_argmax(cand, eid, E)
        pick = eid == ei
        selr = jnp.where(pick, float(r + 1), selr)
        cand = jnp.where(pick, -jnp.inf, cand)

    chosen = selr > 0.0
    w = jnp.where(chosen, s, 0.0)
    wsum = jnp.sum(w, axis=0, keepdims=True)
    selr_ref[0] = selr
    wf_ref[0] = w / wsum * ROUTE_SCALE

    onehot = chosen.astype(BF16)
    c_i = lax.broadcasted_iota(jnp.int32, (tm, tm), 0)
    r_i = lax.broadcasted_iota(jnp.int32, (tm, tm), 1)
    tri = (c_i < r_i).astype(BF16)
    carry = carry_ref[...]
    rank_ref[0] = jnp.dot(onehot, tri, preferred_element_type=F32) + carry
    carry = carry + jnp.sum(chosen.astype(F32), axis=1, keepdims=True)
    carry_ref[...] = carry
    cnt_ref[...] = carry


def _route_t(logits_t, bias):
    B, T, E = logits_t.shape
    tm = 512
    blk = pl.BlockSpec((1, E, tm), lambda b, i: (b, 0, i))
    col = pl.BlockSpec((E, 1), lambda b, i: (0, 0))
    full = jax.ShapeDtypeStruct((B, E, T), F32)
    return pl.pallas_call(
        _route_kernel_t,
        grid=(B, T // tm),
        in_specs=[pl.BlockSpec((1, tm, E), lambda b, i: (b, i, 0)), col],
        out_specs=[blk, blk, blk, col],
        out_shape=[full, full, full, jax.ShapeDtypeStruct((E, 1), F32)],
        scratch_shapes=[pltpu.VMEM((E, 1), F32)],
        compiler_params=_cparams(("arbitrary", "arbitrary")),
        name="route_topk",
    )(logits_t, bias[:, None])


def _dest_kernel_t(selr_ref, wf_ref, rank_ref, ps_ref, dest_ref, wk_ref):
    selr = selr_ref[0]
    destfull = rank_ref[0] + ps_ref[...]
    wf = wf_ref[0]
    for r in range(TOP_K):
        hit = selr == float(r + 1)
        dest_ref[0, r:r + 1, :] = jnp.sum(jnp.where(hit, destfull, 0.0), axis=0, keepdims=True).astype(jnp.int32)
        wk_ref[0, r:r + 1, :] = jnp.sum(jnp.where(hit, wf, 0.0), axis=0, keepdims=True)


def _dest_t(selr, wf, rank, pstart_f):
    B, E, T = selr.shape
    tm = 512
    blk = pl.BlockSpec((1, E, tm), lambda b, i: (b, 0, i))
    outb = pl.BlockSpec((1, TOP_K, tm), lambda b, i: (b, 0, i))
    return pl.pallas_call(
        _dest_kernel_t,
        grid=(B, T // tm),
        in_specs=[blk, blk, blk, pl.BlockSpec((E, 1), lambda b, i: (0, 0))],
        out_specs=[outb, outb],
        out_shape=[jax.ShapeDtypeStruct((B, TOP_K, T), jnp.int32), jax.ShapeDtypeStruct((B, TOP_K, T), F32)],
        compiler_params=_cparams(("parallel", "parallel")),
        name="route_dest",
    )(selr, wf, rank, pstart_f)


def _row_copy(src, s_row, dst, d_row, n, sem):
    return pltpu.make_async_copy(src.at[pl.ds(s_row, n)], dst.at[pl.ds(d_row, n)], sem)


def _dispatch_kernel(padlo_ref, padn_ref, dest_ref, h_ref, z_ref, xs_ref, sem, zsem, *, tt, n_exp):
    i = pl.program_id(0)

    def issue(t, _):
        for k in range(TOP_K):
            _row_copy(h_ref, t, xs_ref, dest_ref[0, 0, k * tt + t], 1, sem).start(priority=k % 2)
        return 0

    lax.fori_loop(0, tt, issue, 0, unroll=2)

    def each_pad(fn):
        def per_expert(e, _):
            lo = padlo_ref[e]

            def one(r, _):
                fn(lo + r)
                return 0

            lax.fori_loop(0, padn_ref[e], one, 0)
            return 0

        lax.fori_loop(0, n_exp, per_expert, 0)

    @pl.when(i == 0)
    def _():
        each_pad(lambda r: _row_copy(z_ref, 0, xs_ref, r, 1, zsem).start())

    for k in range(TOP_K):
        _row_copy(h_ref, 0, xs_ref, 0, tt, sem).wait()

    @pl.when(i == 0)
    def _():
        each_pad(lambda r: _row_copy(z_ref, 0, xs_ref, 0, 1, zsem).wait())


def _dispatch(h2p, dest, pad_lo, pad_n, R):
    N = h2p.shape[0]
    tile = h2p.shape[1:]
    tt = TOK_TILE
    dest3 = dest
    zeros = jnp.zeros((SUBLANES,) + tile, h2p.dtype)
    grid_spec = pltpu.PrefetchScalarGridSpec(
        num_scalar_prefetch=2,
        grid=(N // tt,),
        in_specs=[pl.BlockSpec((1, 1, tt * TOP_K), lambda i, lo, n: (i, 0, 0), memory_space=pltpu.SMEM),
                  pl.BlockSpec((tt,) + tile, lambda i, lo, n: (i, 0, 0)),
                  pl.BlockSpec((SUBLANES,) + tile, lambda i, lo, n: (0, 0, 0))],
        out_specs=pl.BlockSpec(memory_space=pl.ANY),
        scratch_shapes=[pltpu.SemaphoreType.DMA(()), pltpu.SemaphoreType.DMA(())],
    )
    return pl.pallas_call(
        functools.partial(_dispatch_kernel, tt=tt, n_exp=N_EXPERTS),
        grid_spec=grid_spec,
        out_shape=jax.ShapeDtypeStruct((R,) + tile, h2p.dtype),
        compiler_params=_cparams(("arbitrary",)),
        name="moe_dispatch",
    )(pad_lo, pad_n, dest3, h2p, zeros)


def _expert_kernel(te_ref, nu_ref, first_ref, slot_ref, nxt_ref, xs_ref, wg_hbm, wu_hbm, wd_hbm, y_ref,
                   stg, stu, std, wgb, wub, wdb, sem):
    i = pl.program_id(0)
    n_chunks = 2

    def fetch(e, s):
        cps = []
        for m, (src, dst) in enumerate(((wg_hbm, stg), (wu_hbm, stu), (wd_hbm, std))):
            rows = dst.shape[1] // n_chunks
            for c in range(n_chunks):
                cps.append(pltpu.make_async_copy(src.at[e, pl.ds(c * rows, rows)],
                                                 dst.at[s, pl.ds(c * rows, rows)], sem.at[s, m]))
        return cps

    @pl.when(i == 0)
    def _():
        for cp in fetch(te_ref[0], 0):
            cp.start(priority=1)

        @pl.when(nxt_ref[0, 0] >= 0)
        def _():
            for cp in fetch(nxt_ref[0, 0], 1):
                cp.start(priority=1)

    active = i < nu_ref[0]

    def swiglu_tile(wg, wu, wd):
        m_rows = xs_ref.shape[0]
        lo, hi = _unpack_pair(_load_tiles_as_rows(_flat_tiles(xs_ref), m_rows))
        lo = lo.astype(BF16)
        hi = hi.astype(BF16)
        half = lo.shape[-1]
        a = (jnp.dot(lo, wg[:half, :], preferred_element_type=F32)
             + jnp.dot(hi, wg[half:, :], preferred_element_type=F32))
        u = (jnp.dot(lo, wu[:half, :], preferred_element_type=F32)
             + jnp.dot(hi, wu[half:, :], preferred_element_type=F32))
        hmid = (a * jax.nn.sigmoid(a) * u).astype(BF16)
        y = jnp.dot(hmid, wd, preferred_element_type=F32)
        _store_rows_as_tiles(_flat_tiles(y_ref), _pack_pair(y[:, :half], y[:, half:]))

    is_first = first_ref[i] == 1

    @pl.when(active & is_first)
    def _():
        s = slot_ref[i]
        for cp in fetch(0, s):
            cp.wait()
        wg = stg[s].astype(BF16)
        wu = stu[s].astype(BF16)
        wd = std[s].astype(BF16)
        wgb[...] = wg
        wub[...] = wu
        wdb[...] = wd
        swiglu_tile(wg, wu, wd)

        @pl.when(nxt_ref[1, i] >= 0)
        def _():
            for cp in fetch(nxt_ref[1, i], s):
                cp.start(priority=1)

    @pl.when(active & jnp.logical_not(is_first))
    def _():
        swiglu_tile(wgb[...], wub[...], wdb[...])


def _experts(xs, tile_expert, n_used, first, slot, nxt, wg, wu, wd):
    R = xs.shape[0]
    tile = xs.shape[1:]
    M = ROW_TILE
    _, D, F = wg.shape
    row = lambda i, te, nu, fi, sl, nx: (jnp.minimum(i, nu[0] - 1), 0, 0)
    grid_spec = pltpu.PrefetchScalarGridSpec(
        num_scalar_prefetch=5,
        grid=(R // M,),
        in_specs=[pl.BlockSpec((M,) + tile, row),
                  pl.BlockSpec(memory_space=pl.ANY),
                  pl.BlockSpec(memory_space=pl.ANY),
                  pl.BlockSpec(memory_space=pl.ANY)],
        out_specs=pl.BlockSpec((M,) + tile, row),
        scratch_shapes=[pltpu.VMEM((2, D, F), F32), pltpu.VMEM((2, D, F), F32), pltpu.VMEM((2, F, D), F32),
                        pltpu.VMEM((D, F), BF16), pltpu.VMEM((D, F), BF16), pltpu.VMEM((F, D), BF16),
                        pltpu.SemaphoreType.DMA((2, 3))],
    )
    return pl.pallas_call(
        _expert_kernel,
        grid_spec=grid_spec,
        out_shape=jax.ShapeDtypeStruct((R,) + tile, U32),
        compiler_params=_cparams(("arbitrary",)),
        name="moe_experts",
    )(tile_expert, n_used, first, slot, nxt, xs, wg, wu, wd)


def _final_kernel(dcur_ref, dnxt_ref, h_ref, wsg_ref, wsu_ref, wsd_ref, x1_ref, gf_ref, g_ref, wk_ref, y_ref,
                  o_ref, ybuf, sem, *, tt, n_tiles):
    i = pl.program_id(0)
    slot = lax.rem(i, 2)

    def gather(d_ref, s):
        def issue(t, _):
            for k in range(TOP_K):
                pltpu.make_async_copy(y_ref.at[pl.ds(d_ref[0, 0, k * tt + t], 1)],
                                      ybuf.at[s, k, pl.ds(t, 1)], sem.at[s]).start(priority=k % 2)
            return 0

        lax.fori_loop(0, tt, issue, 0, unroll=2)

    @pl.when(i == 0)
    def _():
        gather(dcur_ref, 0)

    for p in range(2):
        @pl.when((i + 1 < n_tiles) & (slot == p))
        def _(p=p):
            gather(dnxt_ref, 1 - p)

    lo, hi = _unpack_pair(_load_tiles_as_rows(_flat_tiles(h_ref), tt))
    lo = lo.astype(BF16)
    hi = hi.astype(BF16)
    half = lo.shape[-1]
    a = (jnp.dot(lo, wsg_ref[:half, :], preferred_element_type=F32)
         + jnp.dot(hi, wsg_ref[half:, :], preferred_element_type=F32))
    u = (jnp.dot(lo, wsu_ref[:half, :], preferred_element_type=F32)
         + jnp.dot(hi, wsu_ref[half:, :], preferred_element_type=F32))
    hmid = (a * jax.nn.sigmoid(a) * u).astype(BF16)
    shared = jnp.dot(hmid, wsd_ref[...], preferred_element_type=F32)

    for k in range(TOP_K):
        pltpu.make_async_copy(y_ref.at[pl.ds(0, tt)], ybuf.at[slot, k], sem.at[slot]).wait()

    wk = wk_ref[...]
    yflat = _flat_tiles(ybuf)

    for p in range(2):
        @pl.when(slot == p)
        def _(p=p):
            r_lo = jnp.zeros((tt, half), F32)
            r_hi = jnp.zeros((tt, half), F32)
            for k in range(TOP_K):
                ylo, yhi = _unpack_pair(_load_tiles_as_rows(yflat, tt, base=(p * TOP_K + k) * tt))
                wcol = wk[:, k:k + 1]
                r_lo = r_lo + wcol * ylo
                r_hi = r_hi + wcol * yhi
            total = shared + jnp.concatenate([r_lo, r_hi], axis=1)
            x2 = x1_ref[...] + gf_ref[0] * total
            ms = jnp.mean(x2 * x2, axis=-1, keepdims=True)
            o_ref[...] = x2 * lax.rsqrt(ms + NORM_EPS) * g_ref[...]


def _final(h2p, wsg, wsu, wsd, x1, gate_f, norm_out, dest, wk, y, T):
    N = h2p.shape[0]
    tile = h2p.shape[1:]
    D = x1.shape[-1]
    F = wsg.shape[1]
    tt = TOK_TILE
    per_b = T // tt
    n_tiles = N // tt
    dest3 = dest
    rowb = lambda w: pl.BlockSpec((tt, w), lambda i: (i, 0))
    const = lambda shp: pl.BlockSpec(shp, lambda i: (0,) * len(shp))
    dspec = lambda f: pl.BlockSpec((1, 1, tt * TOP_K), f, memory_space=pltpu.SMEM)
    return pl.pallas_call(
        functools.partial(_final_kernel, tt=tt, n_tiles=n_tiles),
        grid=(n_tiles,),
        in_specs=[dspec(lambda i: (i, 0, 0)),
                  dspec(lambda i: (jnp.minimum(i + 1, n_tiles - 1), 0, 0)),
                  pl.BlockSpec((tt,) + tile, lambda i: (i, 0, 0)),
                  const((D, F)), const((D, F)), const((F, D)), rowb(D),
                  pl.BlockSpec((1, 1, D), lambda i: (i // per_b, 0, 0)),
                  const((1, D)),
                  pl.BlockSpec((tt, TOP_K), lambda i: (i, 0)),
                  pl.BlockSpec(memory_space=pl.ANY)],
        out_specs=rowb(D),
        out_shape=jax.ShapeDtypeStruct((N, D), F32),
        scratch_shapes=[pltpu.VMEM((2, TOP_K, tt) + tile, U32), pltpu.SemaphoreType.DMA((2,))],
        compiler_params=_cparams(("arbitrary",)),
        name="moe_combine_final",
    )(dest3, dest3, h2p, wsg, wsu, wsd, x1, gate_f[:, None, :], norm_out[None, :], wk, y)


def kernel(x, c, positions, w_ada, b_ada, norm_mix, norm_ffn, norm_out, w_in, w_out, w_router, router_bias,
           w_gate, w_up, w_down, w_sh_gate, w_sh_up, w_sh_down):
    B, T, D = x.shape
    depth = w_ada.shape[0]
    assert depth == 1, "the final rmsnorm is fused into the layer's last kernel"
    moba_w = MOBA_HEADS * MOBA_HEAD_DIM
    ret_w = RET_HEADS * RET_HEAD_DIM
    N = B * T
    E, M = N_EXPERTS, ROW_TILE
    R = N * TOP_K + E * M
    n_tiles = R // M
    tabs = _rope_tables(positions)

    for l in range(depth):
        mod = _ada(c, w_ada[l], b_ada[l])
        shift_a, scale_a, gate_a, shift_f, scale_f, gate_f = jnp.split(mod, 6, axis=-1)

        proj = _in_proj(x, norm_mix[l], shift_a, scale_a, w_in[l].astype(BF16), tabs, moba_w, ret_w)
        o_a = _moba(proj, B, T, moba_w)
        o_r = _retention(proj, B, T, moba_w, ret_w)
        x1, h2p, logits = _out_proj(o_a, o_r, w_out[l].astype(BF16), x, gate_a, norm_ffn[l],
                                    shift_f, scale_f, w_router[l])

        assert D // 2 == SUBLANES * LANES, "a packed row must fill exactly one (SUBLANES, LANES) tile"
        h2p = h2p.reshape(N, SUBLANES, LANES)
        selr, wf, rank, counts = _route_t(logits, router_bias[l])
        cnt = counts[:, 0].astype(jnp.int32)
        pcnt = (cnt + M - 1) // M * M
        pend = jnp.cumsum(pcnt)
        pstart = pend - pcnt
        tidx = jnp.arange(n_tiles, dtype=jnp.int32)
        end_tile = pend // M
        eids = jnp.arange(E, dtype=jnp.int32)
        owner = lambda v: jnp.minimum(jnp.sum(end_tile[None, :] <= v[:, None], axis=1), E - 1).astype(jnp.int32)
        end_of = lambda e: jnp.sum(jnp.where(e[:, None] == eids[None, :], end_tile[None, :], 0), axis=1)
        tile_expert = owner(tidx)
        n_used = end_tile[-1:].astype(jnp.int32)
        first = (((tidx == 0) | (tile_expert != jnp.roll(tile_expert, 1))) & (tidx < n_used[0])).astype(jnp.int32)
        slot = ((jnp.cumsum(first) - 1) % 2).astype(jnp.int32)
        end1 = end_of(tile_expert)
        exp1 = owner(end1)
        end2 = end_of(exp1)
        exp2 = owner(end2)
        has1 = end1 < n_used[0]
        nxt = jnp.stack([jnp.where(has1, exp1, -1),
                         jnp.where(has1 & (end2 < n_used[0]), exp2, -1)]).astype(jnp.int32)
        dest, wk = _dest_t(selr, wf, rank, pstart.astype(F32)[:, None])
        tt = TOK_TILE
        dest = dest.reshape(B, TOP_K, T // tt, tt).transpose(0, 2, 1, 3).reshape(N // tt, 1, TOP_K * tt)
        wk = wk.transpose(0, 2, 1).reshape(N, TOP_K)

        xs = _dispatch(h2p, dest, (pstart + cnt).astype(jnp.int32), (pcnt - cnt).astype(jnp.int32), R)
        y = _experts(xs, tile_expert, n_used, first, slot, nxt, w_gate[l], w_up[l], w_down[l])
        out = _final(h2p, w_sh_gate[l].astype(BF16), w_sh_up[l].astype(BF16), w_sh_down[l].astype(BF16),
                     x1.reshape(N, D), gate_f, norm_out, dest, wk, y, T)
        x = out.reshape(B, T, D)
    return x
```

```python
import functools

import jax
import jax.numpy as jnp
from jax import lax
from jax.experimental import pallas as pl
from jax.experimental.pallas import tpu as pltpu

MOBA_HEADS = 8
MOBA_HEAD_DIM = 128
MOBA_BLOCK = 256
MOBA_TOPK = 3
ROPE_THETA = 500000.0
ROPE_DIMS = 32
RET_HEADS = 4
RET_HEAD_DIM = 256
RET_ROPE_BASE = 10000.0
N_EXPERTS = 64
TOP_K = 8
N_GROUPS = 8
TOPK_GROUPS = 4
ROUTE_SCALE = 2.5
NORM_EPS = 1e-6
NEG = -1e30

LANES = 128
SUBLANES = 8
VMEM_LIMIT = 56 * 1024 * 1024

RET_CHUNK = 256
ROW_TILE = 256
TOK_TILE = 256

F32 = jnp.float32
BF16 = jnp.bfloat16
U32 = jnp.uint32


def _cparams(sem):
    return pltpu.CompilerParams(dimension_semantics=sem, vmem_limit_bytes=VMEM_LIMIT)


def _rms_mod(xf, g, shift, scale):
    ms = jnp.mean(xf * xf, axis=-1, keepdims=True)
    y = xf * lax.rsqrt(ms + NORM_EPS) * g
    return y * (1.0 + scale) + shift


def _pack_pair(lo, hi):
    lo_b = lax.bitcast_convert_type(lo.astype(BF16).astype(F32), U32)
    hi_b = lax.bitcast_convert_type(hi.astype(BF16).astype(F32), U32)
    return (lo_b >> 16) | hi_b


def _unpack_pair(p):
    lo = lax.bitcast_convert_type(p << 16, F32)
    hi = lax.bitcast_convert_type(p & jnp.uint32(0xFFFF0000), F32)
    return lo, hi


def _store_rows_as_tiles(ref, val, base=0):
    n = val.shape[0]
    for c in range(SUBLANES):
        ref[pl.ds(base * SUBLANES + c, n, stride=SUBLANES), :] = val[:, c * LANES:(c + 1) * LANES]


def _load_tiles_as_rows(ref, n, base=0):
    return jnp.concatenate([ref[pl.ds(base * SUBLANES + c, n, stride=SUBLANES), :] for c in range(SUBLANES)],
                           axis=1)


def _flat_tiles(ref):
    rows = 1
    for d in ref.shape[:-2]:
        rows *= d
    return ref.reshape(rows * SUBLANES, LANES)


def _tables_kernel(pos_ref, invm_ref, invr_ref, mc_ref, ms1_ref, ms2_ref, rc_ref, rs_ref):
    pos = pos_ref[0].astype(F32)
    angm = pos * invm_ref[...]
    lane = lax.broadcasted_iota(jnp.int32, angm.shape, 1)
    half = ROPE_DIMS // 2
    c = jnp.cos(angm)
    s = jnp.sin(angm)
    mc_ref[0] = c
    ms1_ref[0] = jnp.where(lane < half, -s, 0.0)
    ms2_ref[0] = jnp.where((lane >= half) & (lane < ROPE_DIMS), s, 0.0)
    angr = pos * invr_ref[...]
    rc_ref[0] = jnp.cos(angr)
    rs_ref[0] = jnp.sin(angr)


def _rope_tables(positions):
    B, T = positions.shape
    tm = 512
    half = ROPE_DIMS // 2
    moba_inv = ROPE_THETA ** (-(jnp.arange(half, dtype=F32) * 2.0 / ROPE_DIMS))
    invm = jnp.concatenate([moba_inv, moba_inv, jnp.zeros((LANES - ROPE_DIMS,), F32)])[None, :]
    invr = (RET_ROPE_BASE ** (-jnp.linspace(0.0, 1.0, RET_HEAD_DIM // 2, dtype=F32)))[None, :]
    pos3 = positions.reshape(B, T, 1)
    tab = jax.ShapeDtypeStruct((B, T, LANES), F32)
    spec = pl.BlockSpec((1, tm, LANES), lambda b, i: (b, i, 0))
    return pl.pallas_call(
        _tables_kernel,
        grid=(B, T // tm),
        in_specs=[pl.BlockSpec((1, tm, 1), lambda b, i: (b, i, 0)),
                  pl.BlockSpec((1, LANES), lambda b, i: (0, 0)),
                  pl.BlockSpec((1, LANES), lambda b, i: (0, 0))],
        out_specs=[spec] * 5,
        out_shape=[tab] * 5,
        compiler_params=_cparams(("parallel", "parallel")),
        name="rope_tables",
    )(pos3, invm, invr)


def _ada_kernel(ct_ref, w_ref, b_ref, o_ref):
    ct = ct_ref[...]
    sct = ct * jax.nn.sigmoid(ct)
    w = w_ref[...]
    for b in range(ct.shape[1]):
        o_ref[b:b + 1, :] = jnp.sum(w * sct[:, b:b + 1], axis=0, keepdims=True) + b_ref[...]


def _ada(c, w_ada, b_ada):
    B, D = c.shape
    n_out = w_ada.shape[1]
    tn = 1024
    return pl.pallas_call(
        _ada_kernel,
        grid=(n_out // tn,),
        in_specs=[pl.BlockSpec((D, B), lambda j: (0, 0)),
                  pl.BlockSpec((D, tn), lambda j: (0, j)),
                  pl.BlockSpec((1, tn), lambda j: (0, j))],
        out_specs=pl.BlockSpec((B, tn), lambda j: (0, j)),
        out_shape=jax.ShapeDtypeStruct((B, n_out), F32),
        compiler_params=_cparams(("parallel",)),
        name="adaln_mod",
    )(c.T, w_ada, b_ada[None, :])


def _inproj_kernel(x_ref, g_ref, sh_ref, sc_ref, w_ref, mc_ref, ms1_ref, ms2_ref, rc_ref, rs_ref,
                   o_ref, hn_ref, *, tn, moba_tiles, ret_lo, ret_k_lo, ret_hi):
    j = pl.program_id(2)

    @pl.when(j == 0)
    def _():
        h = _rms_mod(x_ref[0], g_ref[...], sh_ref[0], sc_ref[0])
        hn_ref[...] = h.astype(BF16)

    acc = jnp.dot(hn_ref[...], w_ref[...], preferred_element_type=F32)

    @pl.when(j < moba_tiles)
    def _():
        c, s1, s2 = mc_ref[0], ms1_ref[0], ms2_ref[0]
        half = ROPE_DIMS // 2
        for g in range(tn // LANES):
            a = acc[:, g * LANES:(g + 1) * LANES]
            r = a * c + pltpu.roll(a, LANES - half, 1) * s1 + pltpu.roll(a, half, 1) * s2
            o_ref[0, :, g * LANES:(g + 1) * LANES] = r.astype(o_ref.dtype)

    @pl.when((j >= ret_lo) & (j < ret_hi))
    def _():
        c, s = rc_ref[0], rs_ref[0]
        fac = jnp.where(j >= ret_k_lo, RET_HEAD_DIM ** -0.5, 1.0).astype(F32)
        hw = RET_HEAD_DIM // 2
        for g in range(tn // RET_HEAD_DIM):
            x1 = acc[:, g * RET_HEAD_DIM:g * RET_HEAD_DIM + hw]
            x2 = acc[:, g * RET_HEAD_DIM + hw:(g + 1) * RET_HEAD_DIM]
            o_ref[0, :, g * RET_HEAD_DIM:g * RET_HEAD_DIM + hw] = ((x1 * c - x2 * s) * fac).astype(o_ref.dtype)
            o_ref[0, :, g * RET_HEAD_DIM + hw:(g + 1) * RET_HEAD_DIM] = ((x2 * c + x1 * s) * fac).astype(o_ref.dtype)

    @pl.when(((j >= moba_tiles) & (j < ret_lo)) | (j >= ret_hi))
    def _():
        o_ref[0] = acc.astype(o_ref.dtype)


def _in_proj(x, g, shift, scale, w_bf, tabs, moba_w, ret_w):
    B, T, D = x.shape
    NC = w_bf.shape[1]
    tm, tn = 1024, 1024
    mc, ms1, ms2, rc, rs = tabs
    kern = functools.partial(
        _inproj_kernel, tn=tn,
        moba_tiles=2 * moba_w // tn,
        ret_lo=3 * moba_w // tn,
        ret_k_lo=(3 * moba_w + ret_w) // tn,
        ret_hi=(3 * moba_w + 2 * ret_w) // tn)
    tab_spec = pl.BlockSpec((1, tm, LANES), lambda b, i, j: (b, i, 0))
    vec_spec = pl.BlockSpec((1, 1, D), lambda b, i, j: (b, 0, 0))
    return pl.pallas_call(
        kern,
        grid=(B, T // tm, NC // tn),
        in_specs=[pl.BlockSpec((1, tm, D), lambda b, i, j: (b, i, 0)),
                  pl.BlockSpec((1, D), lambda b, i, j: (0, 0)),
                  vec_spec, vec_spec,
                  pl.BlockSpec((D, tn), lambda b, i, j: (0, j)),
                  tab_spec, tab_spec, tab_spec, tab_spec, tab_spec],
        out_specs=pl.BlockSpec((1, tm, tn), lambda b, i, j: (b, i, j)),
        out_shape=jax.ShapeDtypeStruct((B, T, NC), BF16),
        scratch_shapes=[pltpu.VMEM((tm, D), BF16)],
        compiler_params=_cparams(("parallel", "parallel", "arbitrary")),
        name="in_proj",
    )(x, g[None, :], shift[:, None, :], scale[:, None, :], w_bf, mc, ms1, ms2, rc, rs)


def _moba_kernel(q_ref, k_ref, v_ref, o_ref, km_ref, vt_ref, sel_ref, qs_ref, s_ref, m_ref, acc_ref,
                 *, nb, hp):
    qb = pl.program_id(2)
    BS, hd = MOBA_BLOCK, MOBA_HEAD_DIM

    @pl.when(qb == 0)
    def _():
        for h in range(hp):
            hs = slice(h * hd, (h + 1) * hd)
            for n in range(nb):
                kb = k_ref[0, n * BS:(n + 1) * BS, hs].astype(F32)
                km_ref[h, n:n + 1, :] = jnp.sum(kb, axis=0, keepdims=True) * (1.0 / BS)
                vt_ref[h, n, :hd, :] = v_ref[0, n * BS:(n + 1) * BS, hs].astype(F32).T.astype(BF16)
                vt_ref[h, n, hd:, :] = jnp.ones((vt_ref.shape[2] - hd, BS), BF16)

    scale = hd ** -0.5
    own = pl.multiple_of(qb * BS, BS)
    blk = lax.broadcasted_iota(jnp.int32, (nb, BS), 0)
    blk_f = blk.astype(F32)

    for h in range(hp):
        hs = slice(h * hd, (h + 1) * hd)
        qT = q_ref[0, :, hs].astype(F32).T
        gate = jnp.dot(km_ref[h], qT, preferred_element_type=F32,
                       precision=lax.Precision.HIGHEST)
        g = jnp.where(blk < qb, gate, NEG)
        sel = jnp.zeros((nb, BS), F32)
        for _ in range(MOBA_TOPK):
            m = jnp.max(g, axis=0, keepdims=True)
            idx = jnp.min(jnp.where(g == m, blk_f, float(nb)), axis=0, keepdims=True)
            pick = blk_f == idx
            sel = jnp.where(pick & (m > 0.5 * NEG), 1.0, sel)
            g = jnp.where(pick, -jnp.inf, g)
        sel_ref[h] = sel
        qs_ref[h] = (qT * scale).astype(BF16)
        m_ref[h] = jnp.full((1, BS), NEG, F32)

    n_pairs = lax.shift_right_logical(qb + 1, 1)

    def sweep_scores(j, _):
        for h in range(hp):
            mh = m_ref[h]
            for u in range(2):
                n = 2 * j + u
                off = pl.multiple_of(n * BS, BS)
                s = jnp.dot(k_ref[0, pl.ds(off, BS), h * hd:(h + 1) * hd], qs_ref[h],
                            preferred_element_type=F32)
                s = jnp.where(sel_ref[h, pl.ds(n, 1), :] > 0.0, s, NEG)
                s_ref[h, n] = s
                mh = jnp.maximum(mh, jnp.max(s, axis=0, keepdims=True))
            m_ref[h] = mh
        return 0

    lax.fori_loop(0, n_pairs, sweep_scores, 0)

    krow = lax.broadcasted_iota(jnp.int32, (BS, BS), 0)
    qcol = lax.broadcasted_iota(jnp.int32, (BS, BS), 1)
    for h in range(hp):
        hs = slice(h * hd, (h + 1) * hd)
        s = jnp.dot(k_ref[0, pl.ds(own, BS), hs], qs_ref[h], preferred_element_type=F32)
        s = jnp.where(krow <= qcol, s, NEG)
        m = jnp.maximum(m_ref[h], jnp.max(s, axis=0, keepdims=True))
        m_ref[h] = m
        p = jnp.exp((s - m).astype(BF16))
        acc_ref[h] = jnp.dot(vt_ref[h, qb], p, preferred_element_type=F32)

    def sweep_values(j, _):
        for h in range(hp):
            mh = m_ref[h]
            p0 = jnp.exp((s_ref[h, 2 * j] - mh).astype(BF16))
            p1 = jnp.exp((s_ref[h, 2 * j + 1] - mh).astype(BF16))
            acc_ref[h] = acc_ref[h] + (
                jnp.dot(vt_ref[h, 2 * j], p0, preferred_element_type=F32)
                + jnp.dot(vt_ref[h, 2 * j + 1], p1, preferred_element_type=F32))
        return 0

    lax.fori_loop(0, n_pairs, sweep_values, 0)

    for h in range(hp):
        acc = acc_ref[h]
        o_ref[0, :, h * hd:(h + 1) * hd] = (acc[:hd, :] / acc[hd:hd + 1, :]).T.astype(o_ref.dtype)


def _moba(proj, B, T, moba_w):
    H, hd, BS = MOBA_HEADS, MOBA_HEAD_DIM, MOBA_BLOCK
    nb = T // BS
    hp = 4
    ones_rows = 2 * SUBLANES
    gw = hp * hd
    gpw = moba_w // gw
    return pl.pallas_call(
        functools.partial(_moba_kernel, nb=nb, hp=hp),
        grid=(B, H // hp, nb),
        in_specs=[pl.BlockSpec((1, BS, gw), lambda b, h, i: (b, i, h)),
                  pl.BlockSpec((1, T, gw), lambda b, h, i: (b, 0, gpw + h)),
                  pl.BlockSpec((1, T, gw), lambda b, h, i: (b, 0, 2 * gpw + h))],
        out_specs=pl.BlockSpec((1, BS, gw), lambda b, h, i: (b, i, h)),
        out_shape=jax.ShapeDtypeStruct((B, T, moba_w), BF16),
        scratch_shapes=[pltpu.VMEM((hp, nb, hd), F32),
                        pltpu.VMEM((hp, nb, hd + ones_rows, BS), BF16),
                        pltpu.VMEM((hp, nb, BS), F32),
                        pltpu.VMEM((hp, hd, BS), BF16),
                        pltpu.VMEM((hp, nb, BS, BS), F32),
                        pltpu.VMEM((hp, 1, BS), F32),
                        pltpu.VMEM((hp, hd + ones_rows, BS), F32)],
        compiler_params=_cparams(("parallel", "parallel", "arbitrary")),
        name="moba_attn",
    )(proj, proj, proj)


def _ret_kernel(q_ref, k_ref, v_ref, g_ref, dm_ref, xi_ref, zeta_ref, cd_ref, o_ref, s_ref):
    c = pl.program_id(1)
    d = RET_HEAD_DIM

    @pl.when(c == 0)
    def _():
        s_ref[...] = jnp.zeros_like(s_ref)

    for h in range(RET_HEADS):
        sl = slice(h * d, (h + 1) * d)
        q = q_ref[0, :, sl]
        k = k_ref[0, :, sl]
        v = v_ref[0, :, sl]
        inner = lax.dot_general(q, k, (((1,), (1,)), ((), ())), preferred_element_type=F32) * dm_ref[h]
        S = s_ref[h]
        o = (jnp.dot(inner.astype(BF16), v, preferred_element_type=F32)
             + jnp.dot(q, S.astype(BF16), preferred_element_type=F32) * xi_ref[h])
        kz = (k.astype(F32) * zeta_ref[h]).astype(BF16)
        s_ref[h] = S * cd_ref[h] + lax.dot_general(kz, v, (((0,), (0,)), ((), ())),
                                                   preferred_element_type=F32)
        mu = jnp.mean(o, axis=-1, keepdims=True)
        dlt = o - mu
        var = jnp.mean(dlt * dlt, axis=-1, keepdims=True)
        on = dlt * lax.rsqrt(var + NORM_EPS)
        gg = g_ref[0, :, sl].astype(F32)
        o_ref[0, :, sl] = (on * (gg * jax.nn.sigmoid(gg))).astype(o_ref.dtype)


def _retention(proj, B, T, moba_w, ret_w):
    C, H = RET_CHUNK, RET_HEADS
    gamma = 1.0 - jnp.exp2(-5.0 - jnp.arange(H, dtype=F32))
    log_g = jnp.log(gamma)
    pos = jnp.arange(C, dtype=F32)
    diff = pos[:, None] - pos[None, :]
    dmask = jnp.where(diff >= 0, jnp.exp(jnp.maximum(diff, 0.0) * log_g[:, None, None]), 0.0)
    xi = jnp.exp((pos + 1.0) * log_g[:, None])[:, :, None]
    zeta = jnp.exp((C - 1.0 - pos) * log_g[:, None])[:, :, None]
    cd = jnp.exp(C * log_g)[:, None, None]
    base = 3 * moba_w // ret_w
    col = lambda off: pl.BlockSpec((1, C, ret_w), lambda b, c: (b, c, base + off))
    full = lambda shp: pl.BlockSpec(shp, lambda b, c: (0,) * len(shp))
    return pl.pallas_call(
        _ret_kernel,
        grid=(B, T // C),
        in_specs=[col(0), col(1), col(2), col(3),
                  full((H, C, C)), full((H, C, 1)), full((H, C, 1)), full((H, 1, 1))],
        out_specs=pl.BlockSpec((1, C, ret_w), lambda b, c: (b, c, 0)),
        out_shape=jax.ShapeDtypeStruct((B, T, ret_w), BF16),
        scratch_shapes=[pltpu.VMEM((H, RET_HEAD_DIM, RET_HEAD_DIM), F32)],
        compiler_params=_cparams(("parallel", "arbitrary")),
        name="retention",
    )(proj, proj, proj, proj, dmask, xi, zeta, cd)


def _outproj_kernel(oa_ref, or_ref, w_ref, x_ref, ga_ref, g_ref, sh_ref, sc_ref, wrh_ref, wrl_ref,
                    x1_ref, hp_ref, lg_ref, *, moba_w):
    mix = (jnp.dot(oa_ref[0], w_ref[:moba_w, :], preferred_element_type=F32)
           + jnp.dot(or_ref[0], w_ref[moba_w:, :], preferred_element_type=F32))
    x1 = x_ref[0] + ga_ref[0] * mix
    x1_ref[0] = x1
    h = _rms_mod(x1, g_ref[...], sh_ref[0], sc_ref[0])
    half = h.shape[-1] // 2
    _store_rows_as_tiles(_flat_tiles(hp_ref), _pack_pair(h[:, :half], h[:, half:]))
    h_hi = h.astype(BF16)
    h_lo = (h - h_hi.astype(F32)).astype(BF16)
    lg_ref[0] = (jnp.dot(h_hi, wrh_ref[...], preferred_element_type=F32)
                 + (jnp.dot(h_lo, wrh_ref[...], preferred_element_type=F32)
                    + jnp.dot(h_hi, wrl_ref[...], preferred_element_type=F32)))


def _out_proj(o_a, o_r, w_bf, x, gate_a, g, shift, scale, w_router):
    B, T, D = x.shape
    moba_w, ret_w = o_a.shape[-1], o_r.shape[-1]
    E = w_router.shape[1]
    tm = 256
    wr_hi = w_router.astype(BF16)
    wr_lo = (w_router - wr_hi.astype(F32)).astype(BF16)
    vec = pl.BlockSpec((1, 1, D), lambda b, i: (b, 0, 0))
    row = lambda w: pl.BlockSpec((1, tm, w), lambda b, i: (b, i, 0))
    return pl.pallas_call(
        functools.partial(_outproj_kernel, moba_w=moba_w),
        grid=(B, T // tm),
        in_specs=[row(moba_w), row(ret_w),
                  pl.BlockSpec((moba_w + ret_w, D), lambda b, i: (0, 0)),
                  row(D), vec,
                  pl.BlockSpec((1, D), lambda b, i: (0, 0)),
                  vec, vec,
                  pl.BlockSpec((D, E), lambda b, i: (0, 0)),
                  pl.BlockSpec((D, E), lambda b, i: (0, 0))],
        out_specs=[row(D), pl.BlockSpec((1, tm, SUBLANES, LANES), lambda b, i: (b, i, 0, 0)), row(E)],
        out_shape=[jax.ShapeDtypeStruct((B, T, D), F32),
                   jax.ShapeDtypeStruct((B, T, SUBLANES, LANES), U32),
                   jax.ShapeDtypeStruct((B, T, E), F32)],
        compiler_params=_cparams(("parallel", "parallel")),
        name="out_proj",
    )(o_a, o_r, w_bf, x, gate_a[:, None, :], g[None, :], shift[:, None, :], scale[:, None, :], wr_hi, wr_lo)


def _route_kernel_t(lg_ref, b_ref, selr_ref, wf_ref, rank_ref, cnt_ref, carry_ref):
    @pl.when((pl.program_id(0) == 0) & (pl.program_id(1) == 0))
    def _():
        carry_ref[...] = jnp.zeros_like(carry_ref)

    E = N_EXPERTS
    gsz = E // N_GROUPS
    assert gsz == SUBLANES and N_GROUPS == SUBLANES, "a routing group is one sublane tile of experts"
    s = jax.nn.sigmoid(lg_ref[0].T)
    biased = s + b_ref[...]
    tm = s.shape[1]
    sub = lax.broadcasted_iota(jnp.int32, (SUBLANES, tm), 0).astype(F32)
    eid = lax.broadcasted_iota(jnp.int32, (E, tm), 0).astype(F32)

    def first_argmax(v, ids, width):
        m = jnp.max(v, axis=0, keepdims=True)
        idx = jnp.min(jnp.where(v == m, ids, float(width)), axis=0, keepdims=True)
        return m, idx

    gscore = jnp.zeros((N_GROUPS, tm), F32)
    for gi in range(N_GROUPS):
        v = biased[gi * gsz:(gi + 1) * gsz, :]
        m1, i1 = first_argmax(v, sub, gsz)
        m2 = jnp.max(jnp.where(sub == i1, -jnp.inf, v), axis=0, keepdims=True)
        gscore = jnp.where(sub == float(gi), m1 + m2, gscore)

    gsel = jnp.zeros((N_GROUPS, tm), F32)
    for _ in range(TOPK_GROUPS):
        _, gi = first_argmax(gscore, sub, N_GROUPS)
        pick = sub == gi
        gsel = jnp.where(pick, 1.0, gsel)
        gscore = jnp.where(pick, -jnp.inf, gscore)

    cand = jnp.concatenate(
        [jnp.where(gsel[gi:gi + 1, :] > 0.0, biased[gi * gsz:(gi + 1) * gsz, :], NEG) for gi in range(N_GROUPS)],
        axis=0)
    selr = jnp.zeros((E, tm), F32)
    for r in range(TOP_K):
        _, ei = first_argmax(cand, eid, E)
        pick = eid == ei
        selr = jnp.where(pick, float(r + 1), selr)
        cand = jnp.where(pick, -jnp.inf, cand)

    chosen = selr > 0.0
    w = jnp.where(chosen, s, 0.0)
    wsum = jnp.sum(w, axis=0, keepdims=True)
    selr_ref[0] = selr
    wf_ref[0] = w / wsum * ROUTE_SCALE

    onehot = chosen.astype(BF16)
    c_i = lax.broadcasted_iota(jnp.int32, (tm, tm), 0)
    r_i = lax.broadcasted_iota(jnp.int32, (tm, tm), 1)
    tri = (c_i < r_i).astype(BF16)
    carry = carry_ref[...]
    rank_ref[0] = jnp.dot(onehot, tri, preferred_element_type=F32) + carry
    carry = carry + jnp.sum(chosen.astype(F32), axis=1, keepdims=True)
    carry_ref[...] = carry
    cnt_ref[...] = carry


def _route_t(logits_t, bias):
    B, T, E = logits_t.shape
    tm = 512
    blk = pl.BlockSpec((1, E, tm), lambda b, i: (b, 0, i))
    col = pl.BlockSpec((E, 1), lambda b, i: (0, 0))
    full = jax.ShapeDtypeStruct((B, E, T), F32)
    return pl.pallas_call(
        _route_kernel_t,
        grid=(B, T // tm),
        in_specs=[pl.BlockSpec((1, tm, E), lambda b, i: (b, i, 0)), col],
        out_specs=[blk, blk, blk, col],
        out_shape=[full, full, full, jax.ShapeDtypeStruct((E, 1), F32)],
        scratch_shapes=[pltpu.VMEM((E, 1), F32)],
        compiler_params=_cparams(("arbitrary", "arbitrary")),
        name="route_topk",
    )(logits_t, bias[:, None])


def _dest_kernel_t(selr_ref, wf_ref, rank_ref, ps_ref, dest_ref, wk_ref):
    selr = selr_ref[0]
    destfull = rank_ref[0] + ps_ref[...]
    wf = wf_ref[0]
    for r in range(TOP_K):
        hit = selr == float(r + 1)
        dest_ref[0, r:r + 1, :] = jnp.sum(jnp.where(hit, destfull, 0.0), axis=0, keepdims=True).astype(jnp.int32)
        wk_ref[0, r:r + 1, :] = jnp.sum(jnp.where(hit, wf, 0.0), axis=0, keepdims=True)


def _dest_t(selr, wf, rank, pstart_f):
    B, E, T = selr.shape
    tm = 512
    blk = pl.BlockSpec((1, E, tm), lambda b, i: (b, 0, i))
    outb = pl.BlockSpec((1, TOP_K, tm), lambda b, i: (b, 0, i))
    return pl.pallas_call(
        _dest_kernel_t,
        grid=(B, T // tm),
        in_specs=[blk, blk, blk, pl.BlockSpec((E, 1), lambda b, i: (0, 0))],
        out_specs=[outb, outb],
        out_shape=[jax.ShapeDtypeStruct((B, TOP_K, T), jnp.int32), jax.ShapeDtypeStruct((B, TOP_K, T), F32)],
        compiler_params=_cparams(("parallel", "parallel")),
        name="route_dest",
    )(selr, wf, rank, pstart_f)


def _row_copy(src, s_row, dst, d_row, n, sem):
    return pltpu.make_async_copy(src.at[pl.ds(s_row, n)], dst.at[pl.ds(d_row, n)], sem)


def _shared_swiglu(h_ref, wsg_ref, wsu_ref, wsd_ref, n):
    lo, hi = _unpack_pair(_load_tiles_as_rows(_flat_tiles(h_ref), n))
    lo = lo.astype(BF16)
    hi = hi.astype(BF16)
    half = lo.shape[-1]
    a = (jnp.dot(lo, wsg_ref[:half, :], preferred_element_type=F32)
         + jnp.dot(hi, wsg_ref[half:, :], preferred_element_type=F32))
    u = (jnp.dot(lo, wsu_ref[:half, :], preferred_element_type=F32)
         + jnp.dot(hi, wsu_ref[half:, :], preferred_element_type=F32))
    hmid = (a * jax.nn.sigmoid(a) * u).astype(BF16)
    return jnp.dot(hmid, wsd_ref[...], preferred_element_type=F32)


def _dispatch_kernel(padlo_ref, padn_ref, dest_ref, h_ref, z_ref, wsg_ref, wsu_ref, wsd_ref,
                     xs_ref, sh_ref, sem, zsem, *, tt, n_exp):
    i = pl.program_id(0)

    def issue(t, _):
        for k in range(TOP_K):
            _row_copy(h_ref, t, xs_ref, dest_ref[0, 0, k * tt + t], 1, sem).start(priority=k % 2)
        return 0

    lax.fori_loop(0, tt, issue, 0, unroll=2)

    def each_pad(fn):
        def per_expert(e, _):
            lo = padlo_ref[e]

            def one(r, _):
                fn(lo + r)
                return 0

            lax.fori_loop(0, padn_ref[e], one, 0)
            return 0

        lax.fori_loop(0, n_exp, per_expert, 0)

    @pl.when(i == 0)
    def _():
        each_pad(lambda r: _row_copy(z_ref, 0, xs_ref, r, 1, zsem).start())

    sh_ref[...] = _shared_swiglu(h_ref, wsg_ref, wsu_ref, wsd_ref, tt)

    for k in range(TOP_K):
        _row_copy(h_ref, 0, xs_ref, 0, tt, sem).wait()

    @pl.when(i == 0)
    def _():
        each_pad(lambda r: _row_copy(z_ref, 0, xs_ref, 0, 1, zsem).wait())


def _dispatch(h2p, dest, pad_lo, pad_n, R, wsg, wsu, wsd):
    N = h2p.shape[0]
    tile = h2p.shape[1:]
    D, F = wsg.shape
    tt = TOK_TILE
    dest3 = dest
    zeros = jnp.zeros((SUBLANES,) + tile, h2p.dtype)
    const = lambda shp: pl.BlockSpec(shp, lambda i, lo, n: (0,) * len(shp))
    grid_spec = pltpu.PrefetchScalarGridSpec(
        num_scalar_prefetch=2,
        grid=(N // tt,),
        in_specs=[pl.BlockSpec((1, 1, tt * TOP_K), lambda i, lo, n: (i, 0, 0), memory_space=pltpu.SMEM),
                  pl.BlockSpec((tt,) + tile, lambda i, lo, n: (i, 0, 0)),
                  const((SUBLANES,) + tile), const((D, F)), const((D, F)), const((F, D))],
        out_specs=[pl.BlockSpec(memory_space=pl.ANY), pl.BlockSpec((tt, D), lambda i, lo, n: (i, 0))],
        scratch_shapes=[pltpu.SemaphoreType.DMA(()), pltpu.SemaphoreType.DMA(())],
    )
    return pl.pallas_call(
        functools.partial(_dispatch_kernel, tt=tt, n_exp=N_EXPERTS),
        grid_spec=grid_spec,
        out_shape=[jax.ShapeDtypeStruct((R,) + tile, h2p.dtype), jax.ShapeDtypeStruct((N, D), F32)],
        compiler_params=_cparams(("arbitrary",)),
        name="moe_dispatch",
    )(pad_lo, pad_n, dest3, h2p, zeros, wsg, wsu, wsd)


def _expert_kernel(te_ref, nu_ref, first_ref, slot_ref, nxt_ref, xs_ref, wg_hbm, wu_hbm, wd_hbm, y_ref,
                   stg, stu, std, wgb, wub, wdb, sem):
    i = pl.program_id(0)
    n_chunks = 2

    def fetch(e, s):
        cps = []
        for m, (src, dst) in enumerate(((wg_hbm, stg), (wu_hbm, stu), (wd_hbm, std))):
            rows = dst.shape[1] // n_chunks
            for c in range(n_chunks):
                cps.append(pltpu.make_async_copy(src.at[e, pl.ds(c * rows, rows)],
                                                 dst.at[s, pl.ds(c * rows, rows)], sem.at[s, m]))
        return cps

    @pl.when(i == 0)
    def _():
        for cp in fetch(te_ref[0], 0):
            cp.start(priority=1)

        @pl.when(nxt_ref[0, 0] >= 0)
        def _():
            for cp in fetch(nxt_ref[0, 0], 1):
                cp.start(priority=1)

    active = i < nu_ref[0]

    def swiglu_tile(wg, wu, wd):
        m_rows = xs_ref.shape[0]
        lo, hi = _unpack_pair(_load_tiles_as_rows(_flat_tiles(xs_ref), m_rows))
        lo = lo.astype(BF16)
        hi = hi.astype(BF16)
        half = lo.shape[-1]
        a = (jnp.dot(lo, wg[:half, :], preferred_element_type=F32)
             + jnp.dot(hi, wg[half:, :], preferred_element_type=F32))
        u = (jnp.dot(lo, wu[:half, :], preferred_element_type=F32)
             + jnp.dot(hi, wu[half:, :], preferred_element_type=F32))
        hmid = (a * jax.nn.sigmoid(a) * u).astype(BF16)
        y = jnp.dot(hmid, wd, preferred_element_type=F32)
        _store_rows_as_tiles(_flat_tiles(y_ref), _pack_pair(y[:, :half], y[:, half:]))

    is_first = first_ref[i] == 1

    @pl.when(active & is_first)
    def _():
        s = slot_ref[i]
        for cp in fetch(0, s):
            cp.wait()
        wg = stg[s].astype(BF16)
        wu = stu[s].astype(BF16)
        wd = std[s].astype(BF16)
        wgb[...] = wg
        wub[...] = wu
        wdb[...] = wd
        swiglu_tile(wg, wu, wd)

        @pl.when(nxt_ref[1, i] >= 0)
        def _():
            for cp in fetch(nxt_ref[1, i], s):
                cp.start(priority=1)

    @pl.when(active & jnp.logical_not(is_first))
    def _():
        swiglu_tile(wgb[...], wub[...], wdb[...])


def _experts(xs, tile_expert, n_used, first, slot, nxt, wg, wu, wd):
    R = xs.shape[0]
    tile = xs.shape[1:]
    M = ROW_TILE
    _, D, F = wg.shape
    row = lambda i, te, nu, fi, sl, nx: (jnp.minimum(i, nu[0] - 1), 0, 0)
    grid_spec = pltpu.PrefetchScalarGridSpec(
        num_scalar_prefetch=5,
        grid=(R // M,),
        in_specs=[pl.BlockSpec((M,) + tile, row),
                  pl.BlockSpec(memory_space=pl.ANY),
                  pl.BlockSpec(memory_space=pl.ANY),
                  pl.BlockSpec(memory_space=pl.ANY)],
        out_specs=pl.BlockSpec((M,) + tile, row),
        scratch_shapes=[pltpu.VMEM((2, D, F), F32), pltpu.VMEM((2, D, F), F32), pltpu.VMEM((2, F, D), F32),
                        pltpu.VMEM((D, F), BF16), pltpu.VMEM((D, F), BF16), pltpu.VMEM((F, D), BF16),
                        pltpu.SemaphoreType.DMA((2, 3))],
    )
    return pl.pallas_call(
        _expert_kernel,
        grid_spec=grid_spec,
        out_shape=jax.ShapeDtypeStruct((R,) + tile, U32),
        compiler_params=_cparams(("arbitrary",)),
        name="moe_experts",
    )(tile_expert, n_used, first, slot, nxt, xs, wg, wu, wd)


def _final_kernel(dcur_ref, dnxt_ref, sh_ref, x1_ref, gf_ref, g_ref, wk_ref, y_ref,
                  o_ref, ybuf, sem, *, tt, n_tiles):
    i = pl.program_id(0)
    slot = lax.rem(i, 2)

    def gather(d_ref, s):
        def issue(t, _):
            for k in range(TOP_K):
                pltpu.make_async_copy(y_ref.at[pl.ds(d_ref[0, 0, k * tt + t], 1)],
                                      ybuf.at[s, k, pl.ds(t, 1)], sem.at[s]).start(priority=k % 2)
            return 0

        lax.fori_loop(0, tt, issue, 0, unroll=2)

    @pl.when(i == 0)
    def _():
        gather(dcur_ref, 0)

    for p in range(2):
        @pl.when((i + 1 < n_tiles) & (slot == p))
        def _(p=p):
            gather(dnxt_ref, 1 - p)

    half = SUBLANES * LANES

    for k in range(TOP_K):
        pltpu.make_async_copy(y_ref.at[pl.ds(0, tt)], ybuf.at[slot, k], sem.at[slot]).wait()

    wk = wk_ref[...]
    yflat = _flat_tiles(ybuf)

    for p in range(2):
        @pl.when(slot == p)
        def _(p=p):
            r_lo = jnp.zeros((tt, half), F32)
            r_hi = jnp.zeros((tt, half), F32)
            for k in range(TOP_K):
                ylo, yhi = _unpack_pair(_load_tiles_as_rows(yflat, tt, base=(p * TOP_K + k) * tt))
                wcol = wk[:, k:k + 1]
                r_lo = r_lo + wcol * ylo
                r_hi = r_hi + wcol * yhi
            total = sh_ref[...] + jnp.concatenate([r_lo, r_hi], axis=1)
            x2 = x1_ref[...] + gf_ref[0] * total
            ms = jnp.mean(x2 * x2, axis=-1, keepdims=True)
            o_ref[...] = x2 * lax.rsqrt(ms + NORM_EPS) * g_ref[...]


def _final(shared, x1, gate_f, norm_out, dest, wk, y, T):
    N, D = x1.shape
    tile = y.shape[1:]
    tt = TOK_TILE
    per_b = T // tt
    n_tiles = N // tt
    dest3 = dest
    rowb = lambda w: pl.BlockSpec((tt, w), lambda i: (i, 0))
    const = lambda shp: pl.BlockSpec(shp, lambda i: (0,) * len(shp))
    dspec = lambda f: pl.BlockSpec((1, 1, tt * TOP_K), f, memory_space=pltpu.SMEM)
    return pl.pallas_call(
        functools.partial(_final_kernel, tt=tt, n_tiles=n_tiles),
        grid=(n_tiles,),
        in_specs=[dspec(lambda i: (i, 0, 0)),
                  dspec(lambda i: (jnp.minimum(i + 1, n_tiles - 1), 0, 0)),
                  rowb(D), rowb(D),
                  pl.BlockSpec((1, 1, D), lambda i: (i // per_b, 0, 0)),
                  const((1, D)),
                  pl.BlockSpec((tt, TOP_K), lambda i: (i, 0)),
                  pl.BlockSpec(memory_space=pl.ANY)],
        out_specs=rowb(D),
        out_shape=jax.ShapeDtypeStruct((N, D), F32),
        scratch_shapes=[pltpu.VMEM((2, TOP_K, tt) + tile, U32), pltpu.SemaphoreType.DMA((2,))],
        compiler_params=_cparams(("arbitrary",)),
        name="moe_combine_final",
    )(dest3, dest3, shared, x1, gate_f[:, None, :], norm_out[None, :], wk, y)


def kernel(x, c, positions, w_ada, b_ada, norm_mix, norm_ffn, norm_out, w_in, w_out, w_router, router_bias,
           w_gate, w_up, w_down, w_sh_gate, w_sh_up, w_sh_down):
    B, T, D = x.shape
    depth = w_ada.shape[0]
    assert depth == 1, "the final rmsnorm is fused into the layer's last kernel"
    moba_w = MOBA_HEADS * MOBA_HEAD_DIM
    ret_w = RET_HEADS * RET_HEAD_DIM
    N = B * T
    E, M = N_EXPERTS, ROW_TILE
    R = N * TOP_K + E * M
    n_tiles = R // M
    tabs = _rope_tables(positions)

    for l in range(depth):
        mod = _ada(c, w_ada[l], b_ada[l])
        shift_a, scale_a, gate_a, shift_f, scale_f, gate_f = jnp.split(mod, 6, axis=-1)

        proj = _in_proj(x, norm_mix[l], shift_a, scale_a, w_in[l].astype(BF16), tabs, moba_w, ret_w)
        o_a = _moba(proj, B, T, moba_w)
        o_r = _retention(proj, B, T, moba_w, ret_w)
        x1, h2p, logits = _out_proj(o_a, o_r, w_out[l].astype(BF16), x, gate_a, norm_ffn[l],
                                    shift_f, scale_f, w_router[l])

        assert D // 2 == SUBLANES * LANES, "a packed row must fill exactly one (SUBLANES, LANES) tile"
        h2p = h2p.reshape(N, SUBLANES, LANES)
        selr, wf, rank, counts = _route_t(logits, router_bias[l])
        cnt = counts[:, 0].astype(jnp.int32)
        pcnt = (cnt + M - 1) // M * M
        pend = jnp.cumsum(pcnt)
        pstart = pend - pcnt
        tidx = jnp.arange(n_tiles, dtype=jnp.int32)
        end_tile = pend // M
        eids = jnp.arange(E, dtype=jnp.int32)
        owner = lambda v: jnp.minimum(jnp.sum(end_tile[None, :] <= v[:, None], axis=1), E - 1).astype(jnp.int32)
        end_of = lambda e: jnp.sum(jnp.where(e[:, None] == eids[None, :], end_tile[None, :], 0), axis=1)
        tile_expert = owner(tidx)
        n_used = end_tile[-1:].astype(jnp.int32)
        first = (((tidx == 0) | (tile_expert != jnp.roll(tile_expert, 1))) & (tidx < n_used[0])).astype(jnp.int32)
        slot = ((jnp.cumsum(first) - 1) % 2).astype(jnp.int32)
        end1 = end_of(tile_expert)
        exp1 = owner(end1)
        end2 = end_of(exp1)
        exp2 = owner(end2)
        has1 = end1 < n_used[0]
        nxt = jnp.stack([jnp.where(has1, exp1, -1),
                         jnp.where(has1 & (end2 < n_used[0]), exp2, -1)]).astype(jnp.int32)
        dest, wk = _dest_t(selr, wf, rank, pstart.astype(F32)[:, None])
        tt = TOK_TILE
        dest = dest.reshape(B, TOP_K, T // tt, tt).transpose(0, 2, 1, 3).reshape(N // tt, 1, TOP_K * tt)
        wk = wk.transpose(0, 2, 1).reshape(N, TOP_K)

        xs, shared = _dispatch(h2p, dest, (pstart + cnt).astype(jnp.int32), (pcnt - cnt).astype(jnp.int32), R,
                               w_sh_gate[l].astype(BF16), w_sh_up[l].astype(BF16), w_sh_down[l].astype(BF16))
        y = _experts(xs, tile_expert, n_used, first, slot, nxt, w_gate[l], w_up[l], w_down[l])
        out = _final(shared, x1.reshape(N, D), gate_f, norm_out, dest, wk, y, T)
        x = out.reshape(B, T, D)
    return x
```

```python
import functools

import jax
import jax.numpy as jnp
from jax import lax
from jax.experimental import pallas as pl
from jax.experimental.pallas import tpu as pltpu

MOBA_HEADS = 8
MOBA_HEAD_DIM = 128
MOBA_BLOCK = 256
MOBA_TOPK = 3
ROPE_THETA = 500000.0
ROPE_DIMS = 32
RET_HEADS = 4
RET_HEAD_DIM = 256
RET_ROPE_BASE = 10000.0
N_EXPERTS = 64
TOP_K = 8
N_GROUPS = 8
TOPK_GROUPS = 4
ROUTE_SCALE = 2.5
NORM_EPS = 1e-6
NEG = -1e30

LANES = 128
SUBLANES = 8
VMEM_LIMIT = 56 * 1024 * 1024

RET_CHUNK = 256
ROW_TILE = 256
TOK_TILE = 256

F32 = jnp.float32
BF16 = jnp.bfloat16
U32 = jnp.uint32


def _cparams(sem):
    return pltpu.CompilerParams(dimension_semantics=sem, vmem_limit_bytes=VMEM_LIMIT)


def _rms_mod(xf, g, shift, scale):
    ms = jnp.mean(xf * xf, axis=-1, keepdims=True)
    y = xf * lax.rsqrt(ms + NORM_EPS) * g
    return y * (1.0 + scale) + shift


def _pack_pair(lo, hi):
    lo_b = lax.bitcast_convert_type(lo.astype(BF16).astype(F32), U32)
    hi_b = lax.bitcast_convert_type(hi.astype(BF16).astype(F32), U32)
    return (lo_b >> 16) | hi_b


def _unpack_pair(p):
    lo = lax.bitcast_convert_type(p << 16, F32)
    hi = lax.bitcast_convert_type(p & jnp.uint32(0xFFFF0000), F32)
    return lo, hi


def _store_rows_as_tiles(ref, val, base=0):
    n = val.shape[0]
    for c in range(SUBLANES):
        ref[pl.ds(base * SUBLANES + c, n, stride=SUBLANES), :] = val[:, c * LANES:(c + 1) * LANES]


def _load_tiles_as_rows(ref, n, base=0):
    return jnp.concatenate([ref[pl.ds(base * SUBLANES + c, n, stride=SUBLANES), :] for c in range(SUBLANES)],
                           axis=1)


def _flat_tiles(ref):
    rows = 1
    for d in ref.shape[:-2]:
        rows *= d
    return ref.reshape(rows * SUBLANES, LANES)


def _tables_kernel(pos_ref, invm_ref, invr_ref, mc_ref, ms1_ref, ms2_ref, rc_ref, rs_ref):
    pos = pos_ref[0].astype(F32)
    angm = pos * invm_ref[...]
    lane = lax.broadcasted_iota(jnp.int32, angm.shape, 1)
    half = ROPE_DIMS // 2
    c = jnp.cos(angm)
    s = jnp.sin(angm)
    mc_ref[0] = c
    ms1_ref[0] = jnp.where(lane < half, -s, 0.0)
    ms2_ref[0] = jnp.where((lane >= half) & (lane < ROPE_DIMS), s, 0.0)
    angr = pos * invr_ref[...]
    rc_ref[0] = jnp.cos(angr)
    rs_ref[0] = jnp.sin(angr)


def _rope_tables(positions):
    B, T = positions.shape
    tm = 512
    half = ROPE_DIMS // 2
    moba_inv = ROPE_THETA ** (-(jnp.arange(half, dtype=F32) * 2.0 / ROPE_DIMS))
    invm = jnp.concatenate([moba_inv, moba_inv, jnp.zeros((LANES - ROPE_DIMS,), F32)])[None, :]
    invr = (RET_ROPE_BASE ** (-jnp.linspace(0.0, 1.0, RET_HEAD_DIM // 2, dtype=F32)))[None, :]
    pos3 = positions.reshape(B, T, 1)
    tab = jax.ShapeDtypeStruct((B, T, LANES), F32)
    spec = pl.BlockSpec((1, tm, LANES), lambda b, i: (b, i, 0))
    return pl.pallas_call(
        _tables_kernel,
        grid=(B, T // tm),
        in_specs=[pl.BlockSpec((1, tm, 1), lambda b, i: (b, i, 0)),
                  pl.BlockSpec((1, LANES), lambda b, i: (0, 0)),
                  pl.BlockSpec((1, LANES), lambda b, i: (0, 0))],
        out_specs=[spec] * 5,
        out_shape=[tab] * 5,
        compiler_params=_cparams(("parallel", "parallel")),
        name="rope_tables",
    )(pos3, invm, invr)


def _ada_kernel(ct_ref, w_ref, b_ref, o_ref):
    ct = ct_ref[...]
    sct = ct * jax.nn.sigmoid(ct)
    w = w_ref[...]
    for b in range(ct.shape[1]):
        o_ref[b:b + 1, :] = jnp.sum(w * sct[:, b:b + 1], axis=0, keepdims=True) + b_ref[...]


def _ada(c, w_ada, b_ada):
    B, D = c.shape
    n_out = w_ada.shape[1]
    tn = 1024
    return pl.pallas_call(
        _ada_kernel,
        grid=(n_out // tn,),
        in_specs=[pl.BlockSpec((D, B), lambda j: (0, 0)),
                  pl.BlockSpec((D, tn), lambda j: (0, j)),
                  pl.BlockSpec((1, tn), lambda j: (0, j))],
        out_specs=pl.BlockSpec((B, tn), lambda j: (0, j)),
        out_shape=jax.ShapeDtypeStruct((B, n_out), F32),
        compiler_params=_cparams(("parallel",)),
        name="adaln_mod",
    )(c.T, w_ada, b_ada[None, :])


def _inproj_kernel(x_ref, g_ref, sh_ref, sc_ref, w_ref, mc_ref, ms1_ref, ms2_ref, rc_ref, rs_ref,
                   o_ref, hn_ref, *, tn, moba_tiles, ret_lo, ret_k_lo, ret_hi):
    j = pl.program_id(2)

    @pl.when(j == 0)
    def _():
        h = _rms_mod(x_ref[0], g_ref[...], sh_ref[0], sc_ref[0])
        hn_ref[...] = h.astype(BF16)

    acc = jnp.dot(hn_ref[...], w_ref[...], preferred_element_type=F32)

    @pl.when(j < moba_tiles)
    def _():
        c, s1, s2 = mc_ref[0], ms1_ref[0], ms2_ref[0]
        half = ROPE_DIMS // 2
        for g in range(tn // LANES):
            a = acc[:, g * LANES:(g + 1) * LANES]
            r = a * c + pltpu.roll(a, LANES - half, 1) * s1 + pltpu.roll(a, half, 1) * s2
            o_ref[0, :, g * LANES:(g + 1) * LANES] = r.astype(o_ref.dtype)

    @pl.when((j >= ret_lo) & (j < ret_hi))
    def _():
        c, s = rc_ref[0], rs_ref[0]
        fac = jnp.where(j >= ret_k_lo, RET_HEAD_DIM ** -0.5, 1.0).astype(F32)
        hw = RET_HEAD_DIM // 2
        for g in range(tn // RET_HEAD_DIM):
            x1 = acc[:, g * RET_HEAD_DIM:g * RET_HEAD_DIM + hw]
            x2 = acc[:, g * RET_HEAD_DIM + hw:(g + 1) * RET_HEAD_DIM]
            o_ref[0, :, g * RET_HEAD_DIM:g * RET_HEAD_DIM + hw] = ((x1 * c - x2 * s) * fac).astype(o_ref.dtype)
            o_ref[0, :, g * RET_HEAD_DIM + hw:(g + 1) * RET_HEAD_DIM] = ((x2 * c + x1 * s) * fac).astype(o_ref.dtype)

    @pl.when(((j >= moba_tiles) & (j < ret_lo)) | (j >= ret_hi))
    def _():
        o_ref[0] = acc.astype(o_ref.dtype)


def _in_proj(x, g, shift, scale, w_bf, tabs, moba_w, ret_w):
    B, T, D = x.shape
    NC = w_bf.shape[1]
    tm, tn = 1024, 1024
    mc, ms1, ms2, rc, rs = tabs
    kern = functools.partial(
        _inproj_kernel, tn=tn,
        moba_tiles=2 * moba_w // tn,
        ret_lo=3 * moba_w // tn,
        ret_k_lo=(3 * moba_w + ret_w) // tn,
        ret_hi=(3 * moba_w + 2 * ret_w) // tn)
    tab_spec = pl.BlockSpec((1, tm, LANES), lambda b, i, j: (b, i, 0))
    vec_spec = pl.BlockSpec((1, 1, D), lambda b, i, j: (b, 0, 0))
    return pl.pallas_call(
        kern,
        grid=(B, T // tm, NC // tn),
        in_specs=[pl.BlockSpec((1, tm, D), lambda b, i, j: (b, i, 0)),
                  pl.BlockSpec((1, D), lambda b, i, j: (0, 0)),
                  vec_spec, vec_spec,
                  pl.BlockSpec((D, tn), lambda b, i, j: (0, j)),
                  tab_spec, tab_spec, tab_spec, tab_spec, tab_spec],
        out_specs=pl.BlockSpec((1, tm, tn), lambda b, i, j: (b, i, j)),
        out_shape=jax.ShapeDtypeStruct((B, T, NC), BF16),
        scratch_shapes=[pltpu.VMEM((tm, D), BF16)],
        compiler_params=_cparams(("parallel", "parallel", "arbitrary")),
        name="in_proj",
    )(x, g[None, :], shift[:, None, :], scale[:, None, :], w_bf, mc, ms1, ms2, rc, rs)


def _moba_kernel(q_ref, k_ref, v_ref, o_ref, km_ref, vt_ref, sel_ref, qs_ref, s_ref, m_ref, acc_ref,
                 *, nb, hp):
    qb = pl.program_id(2)
    BS, hd = MOBA_BLOCK, MOBA_HEAD_DIM

    @pl.when(qb == 0)
    def _():
        for h in range(hp):
            hs = slice(h * hd, (h + 1) * hd)
            for n in range(nb):
                kb = k_ref[0, n * BS:(n + 1) * BS, hs].astype(F32)
                km_ref[h, n:n + 1, :] = jnp.sum(kb, axis=0, keepdims=True) * (1.0 / BS)
                vt_ref[h, n, :hd, :] = v_ref[0, n * BS:(n + 1) * BS, hs].astype(F32).T.astype(BF16)
                vt_ref[h, n, hd:, :] = jnp.ones((vt_ref.shape[2] - hd, BS), BF16)

    scale = hd ** -0.5
    own = pl.multiple_of(qb * BS, BS)
    blk = lax.broadcasted_iota(jnp.int32, (nb, BS), 0)
    blk_f = blk.astype(F32)

    for h in range(hp):
        hs = slice(h * hd, (h + 1) * hd)
        qT = q_ref[0, :, hs].astype(F32).T
        gate = jnp.dot(km_ref[h], qT, preferred_element_type=F32,
                       precision=lax.Precision.HIGHEST)
        g = jnp.where(blk < qb, gate, NEG)
        sel = jnp.zeros((nb, BS), F32)
        for _ in range(MOBA_TOPK):
            m = jnp.max(g, axis=0, keepdims=True)
            idx = jnp.min(jnp.where(g == m, blk_f, float(nb)), axis=0, keepdims=True)
            pick = blk_f == idx
            sel = jnp.where(pick & (m > 0.5 * NEG), 1.0, sel)
            g = jnp.where(pick, -jnp.inf, g)
        sel_ref[h] = sel
        qs_ref[h] = (qT * scale).astype(BF16)
        m_ref[h] = jnp.full((1, BS), NEG, F32)

    n_pairs = lax.shift_right_logical(qb + 1, 1)

    def sweep_scores(j, _):
        for h in range(hp):
            mh = m_ref[h]
            for u in range(2):
                n = 2 * j + u
                off = pl.multiple_of(n * BS, BS)
                s = jnp.dot(k_ref[0, pl.ds(off, BS), h * hd:(h + 1) * hd], qs_ref[h],
                            preferred_element_type=F32)
                s = jnp.where(sel_ref[h, pl.ds(n, 1), :] > 0.0, s, NEG)
                s_ref[h, n] = s
                mh = jnp.maximum(mh, jnp.max(s, axis=0, keepdims=True))
            m_ref[h] = mh
        return 0

    lax.fori_loop(0, n_pairs, sweep_scores, 0)

    krow = lax.broadcasted_iota(jnp.int32, (BS, BS), 0)
    qcol = lax.broadcasted_iota(jnp.int32, (BS, BS), 1)
    for h in range(hp):
        hs = slice(h * hd, (h + 1) * hd)
        s = jnp.dot(k_ref[0, pl.ds(own, BS), hs], qs_ref[h], preferred_element_type=F32)
        s = jnp.where(krow <= qcol, s, NEG)
        m = jnp.maximum(m_ref[h], jnp.max(s, axis=0, keepdims=True))
        m_ref[h] = m
        p = jnp.exp((s - m).astype(BF16))
        acc_ref[h] = jnp.dot(vt_ref[h, qb], p, preferred_element_type=F32)

    def sweep_values(j, _):
        for h in range(hp):
            mh = m_ref[h]
            p0 = jnp.exp((s_ref[h, 2 * j] - mh).astype(BF16))
            p1 = jnp.exp((s_ref[h, 2 * j + 1] - mh).astype(BF16))
            acc_ref[h] = acc_ref[h] + (
                jnp.dot(vt_ref[h, 2 * j], p0, preferred_element_type=F32)
                + jnp.dot(vt_ref[h, 2 * j + 1], p1, preferred_element_type=F32))
        return 0

    lax.fori_loop(0, n_pairs, sweep_values, 0)

    for h in range(hp):
        acc = acc_ref[h]
        o_ref[0, :, h * hd:(h + 1) * hd] = (acc[:hd, :] / acc[hd:hd + 1, :]).T.astype(o_ref.dtype)


def _moba(proj, B, T, moba_w):
    H, hd, BS = MOBA_HEADS, MOBA_HEAD_DIM, MOBA_BLOCK
    nb = T // BS
    hp = 4
    ones_rows = 2 * SUBLANES
    gw = hp * hd
    gpw = moba_w // gw
    return pl.pallas_call(
        functools.partial(_moba_kernel, nb=nb, hp=hp),
        grid=(B, H // hp, nb),
        in_specs=[pl.BlockSpec((1, BS, gw), lambda b, h, i: (b, i, h)),
                  pl.BlockSpec((1, T, gw), lambda b, h, i: (b, 0, gpw + h)),
                  pl.BlockSpec((1, T, gw), lambda b, h, i: (b, 0, 2 * gpw + h))],
        out_specs=pl.BlockSpec((1, BS, gw), lambda b, h, i: (b, i, h)),
        out_shape=jax.ShapeDtypeStruct((B, T, moba_w), BF16),
        scratch_shapes=[pltpu.VMEM((hp, nb, hd), F32),
                        pltpu.VMEM((hp, nb, hd + ones_rows, BS), BF16),
                        pltpu.VMEM((hp, nb, BS), F32),
                        pltpu.VMEM((hp, hd, BS), BF16),
                        pltpu.VMEM((hp, nb, BS, BS), F32),
                        pltpu.VMEM((hp, 1, BS), F32),
                        pltpu.VMEM((hp, hd + ones_rows, BS), F32)],
        compiler_params=_cparams(("parallel", "parallel", "arbitrary")),
        name="moba_attn",
    )(proj, proj, proj)


def _ret_kernel(q_ref, k_ref, v_ref, g_ref, dm_ref, xi_ref, zeta_ref, cd_ref, o_ref, s_ref):
    c = pl.program_id(1)
    d = RET_HEAD_DIM

    @pl.when(c == 0)
    def _():
        s_ref[...] = jnp.zeros_like(s_ref)

    for h in range(RET_HEADS):
        sl = slice(h * d, (h + 1) * d)
        q = q_ref[0, :, sl]
        k = k_ref[0, :, sl]
        v = v_ref[0, :, sl]
        inner = lax.dot_general(q, k, (((1,), (1,)), ((), ())), preferred_element_type=F32) * dm_ref[h]
        S = s_ref[h]
        o = (jnp.dot(inner.astype(BF16), v, preferred_element_type=F32)
             + jnp.dot(q, S.astype(BF16), preferred_element_type=F32) * xi_ref[h])
        kz = (k.astype(F32) * zeta_ref[h]).astype(BF16)
        s_ref[h] = S * cd_ref[h] + lax.dot_general(kz, v, (((0,), (0,)), ((), ())),
                                                   preferred_element_type=F32)
        mu = jnp.mean(o, axis=-1, keepdims=True)
        dlt = o - mu
        var = jnp.mean(dlt * dlt, axis=-1, keepdims=True)
        on = dlt * lax.rsqrt(var + NORM_EPS)
        gg = g_ref[0, :, sl].astype(F32)
        o_ref[0, :, sl] = (on * (gg * jax.nn.sigmoid(gg))).astype(o_ref.dtype)


def _retention(proj, B, T, moba_w, ret_w):
    C, H = RET_CHUNK, RET_HEADS
    gamma = 1.0 - jnp.exp2(-5.0 - jnp.arange(H, dtype=F32))
    log_g = jnp.log(gamma)
    pos = jnp.arange(C, dtype=F32)
    diff = pos[:, None] - pos[None, :]
    dmask = jnp.where(diff >= 0, jnp.exp(jnp.maximum(diff, 0.0) * log_g[:, None, None]), 0.0)
    xi = jnp.exp((pos + 1.0) * log_g[:, None])[:, :, None]
    zeta = jnp.exp((C - 1.0 - pos) * log_g[:, None])[:, :, None]
    cd = jnp.exp(C * log_g)[:, None, None]
    base = 3 * moba_w // ret_w
    col = lambda off: pl.BlockSpec((1, C, ret_w), lambda b, c: (b, c, base + off))
    full = lambda shp: pl.BlockSpec(shp, lambda b, c: (0,) * len(shp))
    return pl.pallas_call(
        _ret_kernel,
        grid=(B, T // C),
        in_specs=[col(0), col(1), col(2), col(3),
                  full((H, C, C)), full((H, C, 1)), full((H, C, 1)), full((H, 1, 1))],
        out_specs=pl.BlockSpec((1, C, ret_w), lambda b, c: (b, c, 0)),
        out_shape=jax.ShapeDtypeStruct((B, T, ret_w), BF16),
        scratch_shapes=[pltpu.VMEM((H, RET_HEAD_DIM, RET_HEAD_DIM), F32)],
        compiler_params=_cparams(("parallel", "arbitrary")),
        name="retention",
    )(proj, proj, proj, proj, dmask, xi, zeta, cd)


def _outproj_kernel(oa_ref, or_ref, w_ref, x_ref, ga_ref, g_ref, sh_ref, sc_ref, wrh_ref, wrl_ref,
                    x1_ref, hp_ref, lg_ref, *, moba_w):
    mix = (jnp.dot(oa_ref[0], w_ref[:moba_w, :], preferred_element_type=F32)
           + jnp.dot(or_ref[0], w_ref[moba_w:, :], preferred_element_type=F32))
    x1 = x_ref[0] + ga_ref[0] * mix
    x1_ref[0] = x1
    h = _rms_mod(x1, g_ref[...], sh_ref[0], sc_ref[0])
    half = h.shape[-1] // 2
    _store_rows_as_tiles(_flat_tiles(hp_ref), _pack_pair(h[:, :half], h[:, half:]))
    h_hi = h.astype(BF16)
    h_lo = (h - h_hi.astype(F32)).astype(BF16)
    lg_ref[0] = (jnp.dot(h_hi, wrh_ref[...], preferred_element_type=F32)
                 + (jnp.dot(h_lo, wrh_ref[...], preferred_element_type=F32)
                    + jnp.dot(h_hi, wrl_ref[...], preferred_element_type=F32)))


def _out_proj(o_a, o_r, w_bf, x, gate_a, g, shift, scale, w_router):
    B, T, D = x.shape
    moba_w, ret_w = o_a.shape[-1], o_r.shape[-1]
    E = w_router.shape[1]
    tm = 256
    wr_hi = w_router.astype(BF16)
    wr_lo = (w_router - wr_hi.astype(F32)).astype(BF16)
    vec = pl.BlockSpec((1, 1, D), lambda b, i: (b, 0, 0))
    row = lambda w: pl.BlockSpec((1, tm, w), lambda b, i: (b, i, 0))
    return pl.pallas_call(
        functools.partial(_outproj_kernel, moba_w=moba_w),
        grid=(B, T // tm),
        in_specs=[row(moba_w), row(ret_w),
                  pl.BlockSpec((moba_w + ret_w, D), lambda b, i: (0, 0)),
                  row(D), vec,
                  pl.BlockSpec((1, D), lambda b, i: (0, 0)),
                  vec, vec,
                  pl.BlockSpec((D, E), lambda b, i: (0, 0)),
                  pl.BlockSpec((D, E), lambda b, i: (0, 0))],
        out_specs=[row(D), pl.BlockSpec((1, tm, SUBLANES, LANES), lambda b, i: (b, i, 0, 0)), row(E)],
        out_shape=[jax.ShapeDtypeStruct((B, T, D), F32),
                   jax.ShapeDtypeStruct((B, T, SUBLANES, LANES), U32),
                   jax.ShapeDtypeStruct((B, T, E), F32)],
        compiler_params=_cparams(("parallel", "parallel")),
        name="out_proj",
    )(o_a, o_r, w_bf, x, gate_a[:, None, :], g[None, :], shift[:, None, :], scale[:, None, :], wr_hi, wr_lo)


def _route_kernel_t(lg_ref, b_ref, selr_ref, wf_ref, rank_ref, cnt_ref, carry_ref):
    @pl.when((pl.program_id(0) == 0) & (pl.program_id(1) == 0))
    def _():
        carry_ref[...] = jnp.zeros_like(carry_ref)

    E = N_EXPERTS
    gsz = E // N_GROUPS
    assert gsz == SUBLANES and N_GROUPS == SUBLANES, "a routing group is one sublane tile of experts"
    s = jax.nn.sigmoid(lg_ref[0].T)
    biased = s + b_ref[...]
    tm = s.shape[1]
    sub = lax.broadcasted_iota(jnp.int32, (SUBLANES, tm), 0).astype(F32)
    eid = lax.broadcasted_iota(jnp.int32, (E, tm), 0).astype(F32)

    def first_argmax(v, ids, width):
        m = jnp.max(v, axis=0, keepdims=True)
        idx = jnp.min(jnp.where(v == m, ids, float(width)), axis=0, keepdims=True)
        return m, idx

    gscore = jnp.zeros((N_GROUPS, tm), F32)
    for gi in range(N_GROUPS):
        v = biased[gi * gsz:(gi + 1) * gsz, :]
        m1, i1 = first_argmax(v, sub, gsz)
        m2 = jnp.max(jnp.where(sub == i1, -jnp.inf, v), axis=0, keepdims=True)
        gscore = jnp.where(sub == float(gi), m1 + m2, gscore)

    gsel = jnp.zeros((N_GROUPS, tm), F32)
    for _ in range(TOPK_GROUPS):
        _, gi = first_argmax(gscore, sub, N_GROUPS)
        pick = sub == gi
        gsel = jnp.where(pick, 1.0, gsel)
        gscore = jnp.where(pick, -jnp.inf, gscore)

    cand = jnp.concatenate(
        [jnp.where(gsel[gi:gi + 1, :] > 0.0, biased[gi * gsz:(gi + 1) * gsz, :], NEG) for gi in range(N_GROUPS)],
        axis=0)
    selr = jnp.zeros((E, tm), F32)
    for r in range(TOP_K):
        _, ei = first_argmax(cand, eid, E)
        pick = eid == ei
        selr = jnp.where(pick, float(r + 1), selr)
        cand = jnp.where(pick, -jnp.inf, cand)

    chosen = selr > 0.0
    w = jnp.where(chosen, s, 0.0)
    wsum = jnp.sum(w, axis=0, keepdims=True)
    selr_ref[0] = selr
    wf_ref[0] = w / wsum * ROUTE_SCALE

    onehot = chosen.astype(BF16)
    c_i = lax.broadcasted_iota(jnp.int32, (tm, tm), 0)
    r_i = lax.broadcasted_iota(jnp.int32, (tm, tm), 1)
    tri = (c_i < r_i).astype(BF16)
    carry = carry_ref[...]
    rank_ref[0] = jnp.dot(onehot, tri, preferred_element_type=F32) + carry
    carry = carry + jnp.sum(chosen.astype(F32), axis=1, keepdims=True)
    carry_ref[...] = carry
    cnt_ref[...] = carry


def _route_t(logits_t, bias):
    B, T, E = logits_t.shape
    tm = 512
    blk = pl.BlockSpec((1, E, tm), lambda b, i: (b, 0, i))
    col = pl.BlockSpec((E, 1), lambda b, i: (0, 0))
    full = jax.ShapeDtypeStruct((B, E, T), F32)
    return pl.pallas_call(
        _route_kernel_t,
        grid=(B, T // tm),
        in_specs=[pl.BlockSpec((1, tm, E), lambda b, i: (b, i, 0)), col],
        out_specs=[blk, blk, blk, col],
        out_shape=[full, full, full, jax.ShapeDtypeStruct((E, 1), F32)],
        scratch_shapes=[pltpu.VMEM((E, 1), F32)],
        compiler_params=_cparams(("arbitrary", "arbitrary")),
        name="route_topk",
    )(logits_t, bias[:, None])


def _dest_kernel_t(selr_ref, wf_ref, rank_ref, ps_ref, dest_ref, wk_ref):
    selr = selr_ref[0]
    destfull = rank_ref[0] + ps_ref[...]
    wf = wf_ref[0]
    for r in range(TOP_K):
        hit = selr == float(r + 1)
        dest_ref[0, r:r + 1, :] = jnp.sum(jnp.where(hit, destfull, 0.0), axis=0, keepdims=True).astype(jnp.int32)
        wk_ref[0, r:r + 1, :] = jnp.sum(jnp.where(hit, wf, 0.0), axis=0, keepdims=True)


def _dest_t(selr, wf, rank, pstart_f):
    B, E, T = selr.shape
    tm = 512
    blk = pl.BlockSpec((1, E, tm), lambda b, i: (b, 0, i))
    outb = pl.BlockSpec((1, TOP_K, tm), lambda b, i: (b, 0, i))
    return pl.pallas_call(
        _dest_kernel_t,
        grid=(B, T // tm),
        in_specs=[blk, blk, blk, pl.BlockSpec((E, 1), lambda b, i: (0, 0))],
        out_specs=[outb, outb],
        out_shape=[jax.ShapeDtypeStruct((B, TOP_K, T), jnp.int32), jax.ShapeDtypeStruct((B, TOP_K, T), F32)],
        compiler_params=_cparams(("parallel", "parallel")),
        name="route_dest",
    )(selr, wf, rank, pstart_f)


def _row_copy(src, s_row, dst, d_row, n, sem):
    return pltpu.make_async_copy(src.at[pl.ds(s_row, n)], dst.at[pl.ds(d_row, n)], sem)


def _shared_swiglu(h_ref, wsg_ref, wsu_ref, wsd_ref, n):
    lo, hi = _unpack_pair(_load_tiles_as_rows(_flat_tiles(h_ref), n))
    lo = lo.astype(BF16)
    hi = hi.astype(BF16)
    half = lo.shape[-1]
    a = (jnp.dot(lo, wsg_ref[:half, :], preferred_element_type=F32)
         + jnp.dot(hi, wsg_ref[half:, :], preferred_element_type=F32))
    u = (jnp.dot(lo, wsu_ref[:half, :], preferred_element_type=F32)
         + jnp.dot(hi, wsu_ref[half:, :], preferred_element_type=F32))
    hmid = (a * jax.nn.sigmoid(a) * u).astype(BF16)
    return jnp.dot(hmid, wsd_ref[...], preferred_element_type=F32)


def _dispatch_kernel(padlo_ref, padn_ref, dest_ref, h_ref, z_ref, wsg_ref, wsu_ref, wsd_ref,
                     xs_ref, sh_ref, sem, zsem, *, tt, n_exp):
    i = pl.program_id(0)

    def issue(t, _):
        for k in range(TOP_K):
            _row_copy(h_ref, t, xs_ref, dest_ref[0, 0, k * tt + t], 1, sem).start(priority=1)
        return 0

    lax.fori_loop(0, tt, issue, 0, unroll=2)

    def each_pad(fn):
        def per_expert(e, _):
            lo = padlo_ref[e]

            def one(r, _):
                fn(lo + r)
                return 0

            lax.fori_loop(0, padn_ref[e], one, 0)
            return 0

        lax.fori_loop(0, n_exp, per_expert, 0)

    @pl.when(i == 0)
    def _():
        each_pad(lambda r: _row_copy(z_ref, 0, xs_ref, r, 1, zsem).start())

    sh_ref[...] = _shared_swiglu(h_ref, wsg_ref, wsu_ref, wsd_ref, tt)

    for k in range(TOP_K):
        _row_copy(h_ref, 0, xs_ref, 0, tt, sem).wait()

    @pl.when(i == 0)
    def _():
        each_pad(lambda r: _row_copy(z_ref, 0, xs_ref, 0, 1, zsem).wait())


def _dispatch(h2p, dest, pad_lo, pad_n, R, wsg, wsu, wsd):
    N = h2p.shape[0]
    tile = h2p.shape[1:]
    D, F = wsg.shape
    tt = TOK_TILE
    dest3 = dest
    zeros = jnp.zeros((SUBLANES,) + tile, h2p.dtype)
    const = lambda shp: pl.BlockSpec(shp, lambda i, lo, n: (0,) * len(shp))
    grid_spec = pltpu.PrefetchScalarGridSpec(
        num_scalar_prefetch=2,
        grid=(N // tt,),
        in_specs=[pl.BlockSpec((1, 1, tt * TOP_K), lambda i, lo, n: (i, 0, 0), memory_space=pltpu.SMEM),
                  pl.BlockSpec((tt,) + tile, lambda i, lo, n: (i, 0, 0)),
                  const((SUBLANES,) + tile), const((D, F)), const((D, F)), const((F, D))],
        out_specs=[pl.BlockSpec(memory_space=pl.ANY), pl.BlockSpec((tt, D), lambda i, lo, n: (i, 0))],
        scratch_shapes=[pltpu.SemaphoreType.DMA(()), pltpu.SemaphoreType.DMA(())],
    )
    return pl.pallas_call(
        functools.partial(_dispatch_kernel, tt=tt, n_exp=N_EXPERTS),
        grid_spec=grid_spec,
        out_shape=[jax.ShapeDtypeStruct((R,) + tile, h2p.dtype), jax.ShapeDtypeStruct((N, D), F32)],
        compiler_params=_cparams(("arbitrary",)),
        name="moe_dispatch",
    )(pad_lo, pad_n, dest3, h2p, zeros, wsg, wsu, wsd)


def _expert_kernel(te_ref, nu_ref, first_ref, slot_ref, nxt_ref, xs_ref, wg_hbm, wu_hbm, wd_hbm, y_ref,
                   stg, stu, std, wgb, wub, wdb, sem):
    i = pl.program_id(0)
    n_chunks = 2

    def fetch(e, s):
        cps = []
        for m, (src, dst) in enumerate(((wg_hbm, stg), (wu_hbm, stu), (wd_hbm, std))):
            rows = dst.shape[1] // n_chunks
            for c in range(n_chunks):
                cps.append(pltpu.make_async_copy(src.at[e, pl.ds(c * rows, rows)],
                                                 dst.at[s, pl.ds(c * rows, rows)], sem.at[s, m]))
        return cps

    @pl.when(i == 0)
    def _():
        for cp in fetch(te_ref[0], 0):
            cp.start(priority=1)

        @pl.when(nxt_ref[0, 0] >= 0)
        def _():
            for cp in fetch(nxt_ref[0, 0], 1):
                cp.start(priority=1)

    active = i < nu_ref[0]

    def swiglu_tile(wg, wu, wd):
        m_rows = xs_ref.shape[0]
        lo, hi = _unpack_pair(_load_tiles_as_rows(_flat_tiles(xs_ref), m_rows))
        lo = lo.astype(BF16)
        hi = hi.astype(BF16)
        half = lo.shape[-1]
        a = (jnp.dot(lo, wg[:half, :], preferred_element_type=F32)
             + jnp.dot(hi, wg[half:, :], preferred_element_type=F32))
        u = (jnp.dot(lo, wu[:half, :], preferred_element_type=F32)
             + jnp.dot(hi, wu[half:, :], preferred_element_type=F32))
        hmid = (a * jax.nn.sigmoid(a) * u).astype(BF16)
        y = jnp.dot(hmid, wd, preferred_element_type=F32)
        _store_rows_as_tiles(_flat_tiles(y_ref), _pack_pair(y[:, :half], y[:, half:]))

    is_first = first_ref[i] == 1

    @pl.when(active & is_first)
    def _():
        s = slot_ref[i]
        for cp in fetch(0, s):
            cp.wait()
        wg = stg[s].astype(BF16)
        wu = stu[s].astype(BF16)
        wd = std[s].astype(BF16)
        wgb[...] = wg
        wub[...] = wu
        wdb[...] = wd
        swiglu_tile(wg, wu, wd)

        @pl.when(nxt_ref[1, i] >= 0)
        def _():
            for cp in fetch(nxt_ref[1, i], s):
                cp.start(priority=1)

    @pl.when(active & jnp.logical_not(is_first))
    def _():
        swiglu_tile(wgb[...], wub[...], wdb[...])


def _experts(xs, tile_expert, n_used, first, slot, nxt, wg, wu, wd):
    R = xs.shape[0]
    tile = xs.shape[1:]
    M = ROW_TILE
    _, D, F = wg.shape
    row = lambda i, te, nu, fi, sl, nx: (jnp.minimum(i, nu[0] - 1), 0, 0)
    grid_spec = pltpu.PrefetchScalarGridSpec(
        num_scalar_prefetch=5,
        grid=(R // M,),
        in_specs=[pl.BlockSpec((M,) + tile, row),
                  pl.BlockSpec(memory_space=pl.ANY),
                  pl.BlockSpec(memory_space=pl.ANY),
                  pl.BlockSpec(memory_space=pl.ANY)],
        out_specs=pl.BlockSpec((M,) + tile, row),
        scratch_shapes=[pltpu.VMEM((2, D, F), F32), pltpu.VMEM((2, D, F), F32), pltpu.VMEM((2, F, D), F32),
                        pltpu.VMEM((D, F), BF16), pltpu.VMEM((D, F), BF16), pltpu.VMEM((F, D), BF16),
                        pltpu.SemaphoreType.DMA((2, 3))],
    )
    return pl.pallas_call(
        _expert_kernel,
        grid_spec=grid_spec,
        out_shape=jax.ShapeDtypeStruct((R,) + tile, U32),
        compiler_params=_cparams(("arbitrary",)),
        name="moe_experts",
    )(tile_expert, n_used, first, slot, nxt, xs, wg, wu, wd)


def _final_kernel(dcur_ref, dnxt_ref, sh_ref, x1_ref, gf_ref, g_ref, wk_ref, y_ref,
                  o_ref, ybuf, sem, *, tt, n_tiles):
    i = pl.program_id(0)
    slot = lax.rem(i, 2)

    def gather(d_ref, s):
        def issue(t, _):
            for k in range(TOP_K):
                pltpu.make_async_copy(y_ref.at[pl.ds(d_ref[0, 0, k * tt + t], 1)],
                                      ybuf.at[s, k, pl.ds(t, 1)], sem.at[s]).start(priority=1)
            return 0

        lax.fori_loop(0, tt, issue, 0, unroll=2)

    @pl.when(i == 0)
    def _():
        gather(dcur_ref, 0)

    for p in range(2):
        @pl.when((i + 1 < n_tiles) & (slot == p))
        def _(p=p):
            gather(dnxt_ref, 1 - p)

    half = SUBLANES * LANES

    for k in range(TOP_K):
        pltpu.make_async_copy(y_ref.at[pl.ds(0, tt)], ybuf.at[slot, k], sem.at[slot]).wait()

    wk = wk_ref[...]
    yflat = _flat_tiles(ybuf)

    for p in range(2):
        @pl.when(slot == p)
        def _(p=p):
            r_lo = jnp.zeros((tt, half), F32)
            r_hi = jnp.zeros((tt, half), F32)
            for k in range(TOP_K):
                ylo, yhi = _unpack_pair(_load_tiles_as_rows(yflat, tt, base=(p * TOP_K + k) * tt))
                wcol = wk[:, k:k + 1]
                r_lo = r_lo + wcol * ylo
                r_hi = r_hi + wcol * yhi
            total = sh_ref[...] + jnp.concatenate([r_lo, r_hi], axis=1)
            x2 = x1_ref[...] + gf_ref[0] * total
            ms = jnp.mean(x2 * x2, axis=-1, keepdims=True)
            o_ref[...] = x2 * lax.rsqrt(ms + NORM_EPS) * g_ref[...]


def _final(shared, x1, gate_f, norm_out, dest, wk, y, T):
    N, D = x1.shape
    tile = y.shape[1:]
    tt = TOK_TILE
    per_b = T // tt
    n_tiles = N // tt
    dest3 = dest
    rowb = lambda w: pl.BlockSpec((tt, w), lambda i: (i, 0))
    const = lambda shp: pl.BlockSpec(shp, lambda i: (0,) * len(shp))
    dspec = lambda f: pl.BlockSpec((1, 1, tt * TOP_K), f, memory_space=pltpu.SMEM)
    return pl.pallas_call(
        functools.partial(_final_kernel, tt=tt, n_tiles=n_tiles),
        grid=(n_tiles,),
        in_specs=[dspec(lambda i: (i, 0, 0)),
                  dspec(lambda i: (jnp.minimum(i + 1, n_tiles - 1), 0, 0)),
                  rowb(D), rowb(D),
                  pl.BlockSpec((1, 1, D), lambda i: (i // per_b, 0, 0)),
                  const((1, D)),
                  pl.BlockSpec((tt, TOP_K), lambda i: (i, 0)),
                  pl.BlockSpec(memory_space=pl.ANY)],
        out_specs=rowb(D),
        out_shape=jax.ShapeDtypeStruct((N, D), F32),
        scratch_shapes=[pltpu.VMEM((2, TOP_K, tt) + tile, U32), pltpu.SemaphoreType.DMA((2,))],
        compiler_params=_cparams(("arbitrary",)),
        name="moe_combine_final",
    )(dest3, dest3, shared, x1, gate_f[:, None, :], norm_out[None, :], wk, y)


def kernel(x, c, positions, w_ada, b_ada, norm_mix, norm_ffn, norm_out, w_in, w_out, w_router, router_bias,
           w_gate, w_up, w_down, w_sh_gate, w_sh_up, w_sh_down):
    B, T, D = x.shape
    depth = w_ada.shape[0]
    assert depth == 1, "the final rmsnorm is fused into the layer's last kernel"
    moba_w = MOBA_HEADS * MOBA_HEAD_DIM
    ret_w = RET_HEADS * RET_HEAD_DIM
    N = B * T
    E, M = N_EXPERTS, ROW_TILE
    R = N * TOP_K + E * M
    n_tiles = R // M
    tabs = _rope_tables(positions)

    for l in range(depth):
        mod = _ada(c, w_ada[l], b_ada[l])
        shift_a, scale_a, gate_a, shift_f, scale_f, gate_f = jnp.split(mod, 6, axis=-1)

        proj = _in_proj(x, norm_mix[l], shift_a, scale_a, w_in[l].astype(BF16), tabs, moba_w, ret_w)
        o_a = _moba(proj, B, T, moba_w)
        o_r = _retention(proj, B, T, moba_w, ret_w)
        x1, h2p, logits = _out_proj(o_a, o_r, w_out[l].astype(BF16), x, gate_a, norm_ffn[l],
                                    shift_f, scale_f, w_router[l])

        assert D // 2 == SUBLANES * LANES, "a packed row must fill exactly one (SUBLANES, LANES) tile"
        h2p = h2p.reshape(N, SUBLANES, LANES)
        selr, wf, rank, counts = _route_t(logits, router_bias[l])
        cnt = counts[:, 0].astype(jnp.int32)
        pcnt = (cnt + M - 1) // M * M
        pend = jnp.cumsum(pcnt)
        pstart = pend - pcnt
        tidx = jnp.arange(n_tiles, dtype=jnp.int32)
        end_tile = pend // M
        eids = jnp.arange(E, dtype=jnp.int32)
        owner = lambda v: jnp.minimum(jnp.sum(end_tile[None, :] <= v[:, None], axis=1), E - 1).astype(jnp.int32)
        end_of = lambda e: jnp.sum(jnp.where(e[:, None] == eids[None, :], end_tile[None, :], 0), axis=1)
        tile_expert = owner(tidx)
        n_used = end_tile[-1:].astype(jnp.int32)
        first = (((tidx == 0) | (tile_expert != jnp.roll(tile_expert, 1))) & (tidx < n_used[0])).astype(jnp.int32)
        slot = ((jnp.cumsum(first) - 1) % 2).astype(jnp.int32)
        end1 = end_of(tile_expert)
        exp1 = owner(end1)
        end2 = end_of(exp1)
        exp2 = owner(end2)
        has1 = end1 < n_used[0]
        nxt = jnp.stack([jnp.where(has1, exp1, -1),
                         jnp.where(has1 & (end2 < n_used[0]), exp2, -1)]).astype(jnp.int32)
        dest, wk = _dest_t(selr, wf, rank, pstart.astype(F32)[:, None])
        tt = TOK_TILE
        dest = dest.reshape(B, TOP_K, T // tt, tt).transpose(0, 2, 1, 3).reshape(N // tt, 1, TOP_K * tt)
        wk = wk.transpose(0, 2, 1).reshape(N, TOP_K)

        xs, shared = _dispatch(h2p, dest, (pstart + cnt).astype(jnp.int32), (pcnt - cnt).astype(jnp.int32), R,
                               w_sh_gate[l].astype(BF16), w_sh_up[l].astype(BF16), w_sh_down[l].astype(BF16))
        y = _experts(xs, tile_expert, n_used, first, slot, nxt, w_gate[l], w_up[l], w_down[l])
        out = _final(shared, x1.reshape(N, D), gate_f, norm_out, dest, wk, y, T)
        x = out.reshape(B, T, D)
    return x
```

```python
import functools

import jax
import jax.numpy as jnp
from jax import lax
from jax.experimental import pallas as pl
from jax.experimental.pallas import tpu as pltpu

MOBA_HEADS = 8
MOBA_HEAD_DIM = 128
MOBA_BLOCK = 256
MOBA_TOPK = 3
ROPE_THETA = 500000.0
ROPE_DIMS = 32
RET_HEADS = 4
RET_HEAD_DIM = 256
RET_ROPE_BASE = 10000.0
N_EXPERTS = 64
TOP_K = 8
N_GROUPS = 8
TOPK_GROUPS = 4
ROUTE_SCALE = 2.5
NORM_EPS = 1e-6
NEG = -1e30

LANES = 128
SUBLANES = 8
VMEM_LIMIT = 56 * 1024 * 1024

RET_CHUNK = 256
ROW_TILE = 256
TOK_TILE = 256

F32 = jnp.float32
BF16 = jnp.bfloat16
U32 = jnp.uint32


def _cparams(sem):
    return pltpu.CompilerParams(dimension_semantics=sem, vmem_limit_bytes=VMEM_LIMIT)


def _rms_mod(xf, g, shift, scale):
    ms = jnp.mean(xf * xf, axis=-1, keepdims=True)
    y = xf * lax.rsqrt(ms + NORM_EPS) * g
    return y * (1.0 + scale) + shift


def _pack_pair(lo, hi):
    lo_b = lax.bitcast_convert_type(lo.astype(BF16).astype(F32), U32)
    hi_b = lax.bitcast_convert_type(hi.astype(BF16).astype(F32), U32)
    return (lo_b >> 16) | hi_b


def _unpack_pair(p):
    lo = lax.bitcast_convert_type(p << 16, F32)
    hi = lax.bitcast_convert_type(p & jnp.uint32(0xFFFF0000), F32)
    return lo, hi


def _store_rows_as_tiles(ref, val, base=0):
    n = val.shape[0]
    for c in range(SUBLANES):
        ref[pl.ds(base * SUBLANES + c, n, stride=SUBLANES), :] = val[:, c * LANES:(c + 1) * LANES]


def _load_tiles_as_rows(ref, n, base=0):
    return jnp.concatenate([ref[pl.ds(base * SUBLANES + c, n, stride=SUBLANES), :] for c in range(SUBLANES)],
                           axis=1)


def _flat_tiles(ref):
    rows = 1
    for d in ref.shape[:-2]:
        rows *= d
    return ref.reshape(rows * SUBLANES, LANES)


def _tables_kernel(pos_ref, invm_ref, invr_ref, mc_ref, ms1_ref, ms2_ref, rc_ref, rs_ref):
    pos = pos_ref[0].astype(F32)
    angm = pos * invm_ref[...]
    lane = lax.broadcasted_iota(jnp.int32, angm.shape, 1)
    half = ROPE_DIMS // 2
    c = jnp.cos(angm)
    s = jnp.sin(angm)
    mc_ref[0] = c
    ms1_ref[0] = jnp.where(lane < half, -s, 0.0)
    ms2_ref[0] = jnp.where((lane >= half) & (lane < ROPE_DIMS), s, 0.0)
    angr = pos * invr_ref[...]
    rc_ref[0] = jnp.cos(angr)
    rs_ref[0] = jnp.sin(angr)


def _rope_tables(positions):
    B, T = positions.shape
    tm = 512
    half = ROPE_DIMS // 2
    moba_inv = ROPE_THETA ** (-(jnp.arange(half, dtype=F32) * 2.0 / ROPE_DIMS))
    invm = jnp.concatenate([moba_inv, moba_inv, jnp.zeros((LANES - ROPE_DIMS,), F32)])[None, :]
    invr = (RET_ROPE_BASE ** (-jnp.linspace(0.0, 1.0, RET_HEAD_DIM // 2, dtype=F32)))[None, :]
    pos3 = positions.reshape(B, T, 1)
    tab = jax.ShapeDtypeStruct((B, T, LANES), F32)
    spec = pl.BlockSpec((1, tm, LANES), lambda b, i: (b, i, 0))
    return pl.pallas_call(
        _tables_kernel,
        grid=(B, T // tm),
        in_specs=[pl.BlockSpec((1, tm, 1), lambda b, i: (b, i, 0)),
                  pl.BlockSpec((1, LANES), lambda b, i: (0, 0)),
                  pl.BlockSpec((1, LANES), lambda b, i: (0, 0))],
        out_specs=[spec] * 5,
        out_shape=[tab] * 5,
        compiler_params=_cparams(("parallel", "parallel")),
        name="rope_tables",
    )(pos3, invm, invr)


def _ada_kernel(ct_ref, w_ref, b_ref, o_ref):
    ct = ct_ref[...]
    sct = ct * jax.nn.sigmoid(ct)
    w = w_ref[...]
    for b in range(ct.shape[1]):
        o_ref[b:b + 1, :] = jnp.sum(w * sct[:, b:b + 1], axis=0, keepdims=True) + b_ref[...]


def _ada(c, w_ada, b_ada):
    B, D = c.shape
    n_out = w_ada.shape[1]
    tn = 1024
    return pl.pallas_call(
        _ada_kernel,
        grid=(n_out // tn,),
        in_specs=[pl.BlockSpec((D, B), lambda j: (0, 0)),
                  pl.BlockSpec((D, tn), lambda j: (0, j)),
                  pl.BlockSpec((1, tn), lambda j: (0, j))],
        out_specs=pl.BlockSpec((B, tn), lambda j: (0, j)),
        out_shape=jax.ShapeDtypeStruct((B, n_out), F32),
        compiler_params=_cparams(("parallel",)),
        name="adaln_mod",
    )(c.T, w_ada, b_ada[None, :])


def _inproj_kernel(x_ref, g_ref, sh_ref, sc_ref, w_ref, mc_ref, ms1_ref, ms2_ref, rc_ref, rs_ref,
                   o_ref, hn_ref, *, tn, moba_tiles, ret_lo, ret_k_lo, ret_hi):
    j = pl.program_id(2)

    @pl.when(j == 0)
    def _():
        h = _rms_mod(x_ref[0], g_ref[...], sh_ref[0], sc_ref[0])
        hn_ref[...] = h.astype(BF16)

    acc = jnp.dot(hn_ref[...], w_ref[...], preferred_element_type=F32)

    @pl.when(j < moba_tiles)
    def _():
        c, s1, s2 = mc_ref[0], ms1_ref[0], ms2_ref[0]
        half = ROPE_DIMS // 2
        for g in range(tn // LANES):
            a = acc[:, g * LANES:(g + 1) * LANES]
            r = a * c + pltpu.roll(a, LANES - half, 1) * s1 + pltpu.roll(a, half, 1) * s2
            o_ref[0, :, g * LANES:(g + 1) * LANES] = r.astype(o_ref.dtype)

    @pl.when((j >= ret_lo) & (j < ret_hi))
    def _():
        c, s = rc_ref[0], rs_ref[0]
        fac = jnp.where(j >= ret_k_lo, RET_HEAD_DIM ** -0.5, 1.0).astype(F32)
        hw = RET_HEAD_DIM // 2
        for g in range(tn // RET_HEAD_DIM):
            x1 = acc[:, g * RET_HEAD_DIM:g * RET_HEAD_DIM + hw]
            x2 = acc[:, g * RET_HEAD_DIM + hw:(g + 1) * RET_HEAD_DIM]
            o_ref[0, :, g * RET_HEAD_DIM:g * RET_HEAD_DIM + hw] = ((x1 * c - x2 * s) * fac).astype(o_ref.dtype)
            o_ref[0, :, g * RET_HEAD_DIM + hw:(g + 1) * RET_HEAD_DIM] = ((x2 * c + x1 * s) * fac).astype(o_ref.dtype)

    @pl.when(((j >= moba_tiles) & (j < ret_lo)) | (j >= ret_hi))
    def _():
        o_ref[0] = acc.astype(o_ref.dtype)


def _in_proj(x, g, shift, scale, w_bf, tabs, moba_w, ret_w):
    B, T, D = x.shape
    NC = w_bf.shape[1]
    tm, tn = 1024, 1024
    mc, ms1, ms2, rc, rs = tabs
    kern = functools.partial(
        _inproj_kernel, tn=tn,
        moba_tiles=2 * moba_w // tn,
        ret_lo=3 * moba_w // tn,
        ret_k_lo=(3 * moba_w + ret_w) // tn,
        ret_hi=(3 * moba_w + 2 * ret_w) // tn)
    tab_spec = pl.BlockSpec((1, tm, LANES), lambda b, i, j: (b, i, 0))
    vec_spec = pl.BlockSpec((1, 1, D), lambda b, i, j: (b, 0, 0))
    return pl.pallas_call(
        kern,
        grid=(B, T // tm, NC // tn),
        in_specs=[pl.BlockSpec((1, tm, D), lambda b, i, j: (b, i, 0)),
                  pl.BlockSpec((1, D), lambda b, i, j: (0, 0)),
                  vec_spec, vec_spec,
                  pl.BlockSpec((D, tn), lambda b, i, j: (0, j)),
                  tab_spec, tab_spec, tab_spec, tab_spec, tab_spec],
        out_specs=pl.BlockSpec((1, tm, tn), lambda b, i, j: (b, i, j)),
        out_shape=jax.ShapeDtypeStruct((B, T, NC), BF16),
        scratch_shapes=[pltpu.VMEM((tm, D), BF16)],
        compiler_params=_cparams(("parallel", "parallel", "arbitrary")),
        name="in_proj",
    )(x, g[None, :], shift[:, None, :], scale[:, None, :], w_bf, mc, ms1, ms2, rc, rs)


def _moba_kernel(q_ref, k_ref, v_ref, o_ref, km_ref, kms_ref, vt_ref, sel_ref, qs_ref, s_ref, m_ref, acc_ref,
                 *, nb, hp):
    qb = pl.program_id(2)
    BS, hd = MOBA_BLOCK, MOBA_HEAD_DIM

    @pl.when(qb == 0)
    def _():
        for h in range(hp):
            hs = slice(h * hd, (h + 1) * hd)
            for n in range(nb):
                kb = k_ref[0, n * BS:(n + 1) * BS, hs].astype(F32)
                km_ref[h, n:n + 1, :] = jnp.sum(kb, axis=0, keepdims=True) * (1.0 / BS)
                vt_ref[h, n, :hd, :] = v_ref[0, n * BS:(n + 1) * BS, hs].astype(F32).T.astype(BF16)
                vt_ref[h, n, hd:, :] = jnp.ones((vt_ref.shape[2] - hd, BS), BF16)
            km = km_ref[h]
            k1 = km.astype(BF16)
            r1 = km - k1.astype(F32)
            k2 = r1.astype(BF16)
            kms_ref[h, 0] = k1
            kms_ref[h, 1] = k2
            kms_ref[h, 2] = (r1 - k2.astype(F32)).astype(BF16)

    scale = hd ** -0.5
    own = pl.multiple_of(qb * BS, BS)
    blk = lax.broadcasted_iota(jnp.int32, (nb, BS), 0)
    blk_f = blk.astype(F32)

    for h in range(hp):
        hs = slice(h * hd, (h + 1) * hd)
        qT = q_ref[0, :, hs].astype(F32).T
        qTb = qT.astype(BF16)
        gate = (jnp.dot(kms_ref[h, 0], qTb, preferred_element_type=F32)
                + (jnp.dot(kms_ref[h, 1], qTb, preferred_element_type=F32)
                   + jnp.dot(kms_ref[h, 2], qTb, preferred_element_type=F32)))
        g = jnp.where(blk < qb, gate, NEG)
        sel = jnp.zeros((nb, BS), F32)
        for _ in range(MOBA_TOPK):
            m = jnp.max(g, axis=0, keepdims=True)
            idx = jnp.min(jnp.where(g == m, blk_f, float(nb)), axis=0, keepdims=True)
            pick = blk_f == idx
            sel = jnp.where(pick & (m > 0.5 * NEG), 1.0, sel)
            g = jnp.where(pick, -jnp.inf, g)
        sel_ref[h] = sel
        qs_ref[h] = (qT * scale).astype(BF16)
        m_ref[h] = jnp.full((1, BS), NEG, F32)

    n_pairs = lax.shift_right_logical(qb + 1, 1)

    def sweep_scores(j, _):
        for h in range(hp):
            mh = m_ref[h]
            for u in range(2):
                n = 2 * j + u
                off = pl.multiple_of(n * BS, BS)
                s = jnp.dot(k_ref[0, pl.ds(off, BS), h * hd:(h + 1) * hd], qs_ref[h],
                            preferred_element_type=F32)
                s = jnp.where(sel_ref[h, pl.ds(n, 1), :] > 0.0, s, NEG)
                s_ref[h, n] = s
                mh = jnp.maximum(mh, jnp.max(s, axis=0, keepdims=True))
            m_ref[h] = mh
        return 0

    lax.fori_loop(0, n_pairs, sweep_scores, 0)

    krow = lax.broadcasted_iota(jnp.int32, (BS, BS), 0)
    qcol = lax.broadcasted_iota(jnp.int32, (BS, BS), 1)
    for h in range(hp):
        hs = slice(h * hd, (h + 1) * hd)
        s = jnp.dot(k_ref[0, pl.ds(own, BS), hs], qs_ref[h], preferred_element_type=F32)
        s = jnp.where(krow <= qcol, s, NEG)
        m = jnp.maximum(m_ref[h], jnp.max(s, axis=0, keepdims=True))
        m_ref[h] = m
        p = jnp.exp((s - m).astype(BF16))
        acc_ref[h] = jnp.dot(vt_ref[h, qb], p, preferred_element_type=F32)

    def sweep_values(j, _):
        for h in range(hp):
            mh = m_ref[h]
            p0 = jnp.exp((s_ref[h, 2 * j] - mh).astype(BF16))
            p1 = jnp.exp((s_ref[h, 2 * j + 1] - mh).astype(BF16))
            acc_ref[h] = acc_ref[h] + (
                jnp.dot(vt_ref[h, 2 * j], p0, preferred_element_type=F32)
                + jnp.dot(vt_ref[h, 2 * j + 1], p1, preferred_element_type=F32))
        return 0

    lax.fori_loop(0, n_pairs, sweep_values, 0)

    for h in range(hp):
        acc = acc_ref[h]
        o_ref[0, :, h * hd:(h + 1) * hd] = (acc[:hd, :] / acc[hd:hd + 1, :]).T.astype(o_ref.dtype)


def _moba(proj, B, T, moba_w):
    H, hd, BS = MOBA_HEADS, MOBA_HEAD_DIM, MOBA_BLOCK
    nb = T // BS
    hp = 4
    ones_rows = 2 * SUBLANES
    gw = hp * hd
    gpw = moba_w // gw
    return pl.pallas_call(
        functools.partial(_moba_kernel, nb=nb, hp=hp),
        grid=(B, H // hp, nb),
        in_specs=[pl.BlockSpec((1, BS, gw), lambda b, h, i: (b, i, h)),
                  pl.BlockSpec((1, T, gw), lambda b, h, i: (b, 0, gpw + h)),
                  pl.BlockSpec((1, T, gw), lambda b, h, i: (b, 0, 2 * gpw + h))],
        out_specs=pl.BlockSpec((1, BS, gw), lambda b, h, i: (b, i, h)),
        out_shape=jax.ShapeDtypeStruct((B, T, moba_w), BF16),
        scratch_shapes=[pltpu.VMEM((hp, nb, hd), F32),
                        pltpu.VMEM((hp, 3, nb, hd), BF16),
                        pltpu.VMEM((hp, nb, hd + ones_rows, BS), BF16),
                        pltpu.VMEM((hp, nb, BS), F32),
                        pltpu.VMEM((hp, hd, BS), BF16),
                        pltpu.VMEM((hp, nb, BS, BS), F32),
                        pltpu.VMEM((hp, 1, BS), F32),
                        pltpu.VMEM((hp, hd + ones_rows, BS), F32)],
        compiler_params=_cparams(("parallel", "parallel", "arbitrary")),
        name="moba_attn",
    )(proj, proj, proj)


def _ret_kernel(q_ref, k_ref, v_ref, g_ref, dm_ref, xi_ref, zeta_ref, cd_ref, o_ref, s_ref):
    c = pl.program_id(1)
    d = RET_HEAD_DIM

    @pl.when(c == 0)
    def _():
        s_ref[...] = jnp.zeros_like(s_ref)

    for h in range(RET_HEADS):
        sl = slice(h * d, (h + 1) * d)
        q = q_ref[0, :, sl]
        k = k_ref[0, :, sl]
        v = v_ref[0, :, sl]
        inner = lax.dot_general(q, k, (((1,), (1,)), ((), ())), preferred_element_type=F32) * dm_ref[h]
        S = s_ref[h]
        o = (jnp.dot(inner.astype(BF16), v, preferred_element_type=F32)
             + jnp.dot(q, S.astype(BF16), preferred_element_type=F32) * xi_ref[h])
        kz = (k.astype(F32) * zeta_ref[h]).astype(BF16)
        s_ref[h] = S * cd_ref[h] + lax.dot_general(kz, v, (((0,), (0,)), ((), ())),
                                                   preferred_element_type=F32)
        mu = jnp.mean(o, axis=-1, keepdims=True)
        dlt = o - mu
        var = jnp.mean(dlt * dlt, axis=-1, keepdims=True)
        on = dlt * lax.rsqrt(var + NORM_EPS)
        gg = g_ref[0, :, sl].astype(F32)
        o_ref[0, :, sl] = (on * (gg * jax.nn.sigmoid(gg))).astype(o_ref.dtype)


def _retention(proj, B, T, moba_w, ret_w):
    C, H = RET_CHUNK, RET_HEADS
    gamma = 1.0 - jnp.exp2(-5.0 - jnp.arange(H, dtype=F32))
    log_g = jnp.log(gamma)
    pos = jnp.arange(C, dtype=F32)
    diff = pos[:, None] - pos[None, :]
    dmask = jnp.where(diff >= 0, jnp.exp(jnp.maximum(diff, 0.0) * log_g[:, None, None]), 0.0)
    xi = jnp.exp((pos + 1.0) * log_g[:, None])[:, :, None]
    zeta = jnp.exp((C - 1.0 - pos) * log_g[:, None])[:, :, None]
    cd = jnp.exp(C * log_g)[:, None, None]
    base = 3 * moba_w // ret_w
    col = lambda off: pl.BlockSpec((1, C, ret_w), lambda b, c: (b, c, base + off))
    full = lambda shp: pl.BlockSpec(shp, lambda b, c: (0,) * len(shp))
    return pl.pallas_call(
        _ret_kernel,
        grid=(B, T // C),
        in_specs=[col(0), col(1), col(2), col(3),
                  full((H, C, C)), full((H, C, 1)), full((H, C, 1)), full((H, 1, 1))],
        out_specs=pl.BlockSpec((1, C, ret_w), lambda b, c: (b, c, 0)),
        out_shape=jax.ShapeDtypeStruct((B, T, ret_w), BF16),
        scratch_shapes=[pltpu.VMEM((H, RET_HEAD_DIM, RET_HEAD_DIM), F32)],
        compiler_params=_cparams(("parallel", "arbitrary")),
        name="retention",
    )(proj, proj, proj, proj, dmask, xi, zeta, cd)


def _outproj_kernel(oa_ref, or_ref, w_ref, x_ref, ga_ref, g_ref, sh_ref, sc_ref, wrh_ref, wrl_ref,
                    x1_ref, hp_ref, lg_ref, *, moba_w):
    mix = (jnp.dot(oa_ref[0], w_ref[:moba_w, :], preferred_element_type=F32)
           + jnp.dot(or_ref[0], w_ref[moba_w:, :], preferred_element_type=F32))
    x1 = x_ref[0] + ga_ref[0] * mix
    x1_ref[0] = x1
    h = _rms_mod(x1, g_ref[...], sh_ref[0], sc_ref[0])
    half = h.shape[-1] // 2
    _store_rows_as_tiles(_flat_tiles(hp_ref), _pack_pair(h[:, :half], h[:, half:]))
    h_hi = h.astype(BF16)
    h_lo = (h - h_hi.astype(F32)).astype(BF16)
    lg_ref[0] = (jnp.dot(h_hi, wrh_ref[...], preferred_element_type=F32)
                 + (jnp.dot(h_lo, wrh_ref[...], preferred_element_type=F32)
                    + jnp.dot(h_hi, wrl_ref[...], preferred_element_type=F32)))


def _out_proj(o_a, o_r, w_bf, x, gate_a, g, shift, scale, w_router):
    B, T, D = x.shape
    moba_w, ret_w = o_a.shape[-1], o_r.shape[-1]
    E = w_router.shape[1]
    tm = 256
    wr_hi = w_router.astype(BF16)
    wr_lo = (w_router - wr_hi.astype(F32)).astype(BF16)
    vec = pl.BlockSpec((1, 1, D), lambda b, i: (b, 0, 0))
    row = lambda w: pl.BlockSpec((1, tm, w), lambda b, i: (b, i, 0))
    return pl.pallas_call(
        functools.partial(_outproj_kernel, moba_w=moba_w),
        grid=(B, T // tm),
        in_specs=[row(moba_w), row(ret_w),
                  pl.BlockSpec((moba_w + ret_w, D), lambda b, i: (0, 0)),
                  row(D), vec,
                  pl.BlockSpec((1, D), lambda b, i: (0, 0)),
                  vec, vec,
                  pl.BlockSpec((D, E), lambda b, i: (0, 0)),
                  pl.BlockSpec((D, E), lambda b, i: (0, 0))],
        out_specs=[row(D), pl.BlockSpec((1, tm, SUBLANES, LANES), lambda b, i: (b, i, 0, 0)), row(E)],
        out_shape=[jax.ShapeDtypeStruct((B, T, D), F32),
                   jax.ShapeDtypeStruct((B, T, SUBLANES, LANES), U32),
                   jax.ShapeDtypeStruct((B, T, E), F32)],
        compiler_params=_cparams(("parallel", "parallel")),
        name="out_proj",
    )(o_a, o_r, w_bf, x, gate_a[:, None, :], g[None, :], shift[:, None, :], scale[:, None, :], wr_hi, wr_lo)


def _route_kernel_t(lg_ref, b_ref, selr_ref, wf_ref, rank_ref, cnt_ref, carry_ref):
    @pl.when((pl.program_id(0) == 0) & (pl.program_id(1) == 0))
    def _():
        carry_ref[...] = jnp.zeros_like(carry_ref)

    E = N_EXPERTS
    gsz = E // N_GROUPS
    assert gsz == SUBLANES and N_GROUPS == SUBLANES, "a routing group is one sublane tile of experts"
    s = jax.nn.sigmoid(lg_ref[0].T)
    biased = s + b_ref[...]
    tm = s.shape[1]
    sub = lax.broadcasted_iota(jnp.int32, (SUBLANES, tm), 0).astype(F32)
    eid = lax.broadcasted_iota(jnp.int32, (E, tm), 0).astype(F32)

    def first_argmax(v, ids, width):
        m = jnp.max(v, axis=0, keepdims=True)
        idx = jnp.min(jnp.where(v == m, ids, float(width)), axis=0, keepdims=True)
        return m, idx

    gscore = jnp.zeros((N_GROUPS, tm), F32)
    for gi in range(N_GROUPS):
        v = biased[gi * gsz:(gi + 1) * gsz, :]
        m1, i1 = first_argmax(v, sub, gsz)
        m2 = jnp.max(jnp.where(sub == i1, -jnp.inf, v), axis=0, keepdims=True)
        gscore = jnp.where(sub == float(gi), m1 + m2, gscore)

    gsel = jnp.zeros((N_GROUPS, tm), F32)
    for _ in range(TOPK_GROUPS):
        _, gi = first_argmax(gscore, sub, N_GROUPS)
        pick = sub == gi
        gsel = jnp.where(pick, 1.0, gsel)
        gscore = jnp.where(pick, -jnp.inf, gscore)

    cand = jnp.concatenate(
        [jnp.where(gsel[gi:gi + 1, :] > 0.0, biased[gi * gsz:(gi + 1) * gsz, :], NEG) for gi in range(N_GROUPS)],
        axis=0)
    selr = jnp.zeros((E, tm), F32)
    for r in range(TOP_K):
        _, ei = first_argmax(cand, eid, E)
        pick = eid == ei
        selr = jnp.where(pick, float(r + 1), selr)
        cand = jnp.where(pick, -jnp.inf, cand)

    chosen = selr > 0.0
    w = jnp.where(chosen, s, 0.0)
    wsum = jnp.sum(w, axis=0, keepdims=True)
    selr_ref[0] = selr
    wf_ref[0] = w / wsum * ROUTE_SCALE

    onehot = chosen.astype(BF16)
    c_i = lax.broadcasted_iota(jnp.int32, (tm, tm), 0)
    r_i = lax.broadcasted_iota(jnp.int32, (tm, tm), 1)
    tri = (c_i < r_i).astype(BF16)
    carry = carry_ref[...]
    rank_ref[0] = jnp.dot(onehot, tri, preferred_element_type=F32) + carry
    carry = carry + jnp.sum(chosen.astype(F32), axis=1, keepdims=True)
    carry_ref[...] = carry
    cnt_ref[...] = carry


def _route_t(logits_t, bias):
    B, T, E = logits_t.shape
    tm = 512
    blk = pl.BlockSpec((1, E, tm), lambda b, i: (b, 0, i))
    col = pl.BlockSpec((E, 1), lambda b, i: (0, 0))
    full = jax.ShapeDtypeStruct((B, E, T), F32)
    return pl.pallas_call(
        _route_kernel_t,
        grid=(B, T // tm),
        in_specs=[pl.BlockSpec((1, tm, E), lambda b, i: (b, i, 0)), col],
        out_specs=[blk, blk, blk, col],
        out_shape=[full, full, full, jax.ShapeDtypeStruct((E, 1), F32)],
        scratch_shapes=[pltpu.VMEM((E, 1), F32)],
        compiler_params=_cparams(("arbitrary", "arbitrary")),
        name="route_topk",
    )(logits_t, bias[:, None])


def _dest_kernel_t(selr_ref, wf_ref, rank_ref, ps_ref, dest_ref, wk_ref):
    selr = selr_ref[0]
    destfull = rank_ref[0] + ps_ref[...]
    wf = wf_ref[0]
    for r in range(TOP_K):
        hit = selr == float(r + 1)
        dest_ref[0, r:r + 1, :] = jnp.sum(jnp.where(hit, destfull, 0.0), axis=0, keepdims=True).astype(jnp.int32)
        wk_ref[0, r:r + 1, :] = jnp.sum(jnp.where(hit, wf, 0.0), axis=0, keepdims=True)


def _dest_t(selr, wf, rank, pstart_f):
    B, E, T = selr.shape
    tm = 512
    blk = pl.BlockSpec((1, E, tm), lambda b, i: (b, 0, i))
    outb = pl.BlockSpec((1, TOP_K, tm), lambda b, i: (b, 0, i))
    return pl.pallas_call(
        _dest_kernel_t,
        grid=(B, T // tm),
        in_specs=[blk, blk, blk, pl.BlockSpec((E, 1), lambda b, i: (0, 0))],
        out_specs=[outb, outb],
        out_shape=[jax.ShapeDtypeStruct((B, TOP_K, T), jnp.int32), jax.ShapeDtypeStruct((B, TOP_K, T), F32)],
        compiler_params=_cparams(("parallel", "parallel")),
        name="route_dest",
    )(selr, wf, rank, pstart_f)


def _row_copy(src, s_row, dst, d_row, n, sem):
    return pltpu.make_async_copy(src.at[pl.ds(s_row, n)], dst.at[pl.ds(d_row, n)], sem)


def _shared_swiglu(h_ref, wsg_ref, wsu_ref, wsd_ref, n):
    lo, hi = _unpack_pair(_load_tiles_as_rows(_flat_tiles(h_ref), n))
    lo = lo.astype(BF16)
    hi = hi.astype(BF16)
    half = lo.shape[-1]
    a = (jnp.dot(lo, wsg_ref[:half, :], preferred_element_type=F32)
         + jnp.dot(hi, wsg_ref[half:, :], preferred_element_type=F32))
    u = (jnp.dot(lo, wsu_ref[:half, :], preferred_element_type=F32)
         + jnp.dot(hi, wsu_ref[half:, :], preferred_element_type=F32))
    hmid = (a * jax.nn.sigmoid(a) * u).astype(BF16)
    return jnp.dot(hmid, wsd_ref[...], preferred_element_type=F32)


def _dispatch_kernel(padlo_ref, padn_ref, dest_ref, h_ref, z_ref, wsg_ref, wsu_ref, wsd_ref,
                     xs_ref, sh_ref, sem, zsem, *, tt, n_exp):
    i = pl.program_id(0)

    def issue(t, _):
        for k in range(TOP_K):
            _row_copy(h_ref, t, xs_ref, dest_ref[0, 0, k * tt + t], 1, sem).start(priority=k % 2)
        return 0

    lax.fori_loop(0, tt, issue, 0, unroll=2)

    def each_pad(fn):
        def per_expert(e, _):
            lo = padlo_ref[e]

            def one(r, _):
                fn(lo + r)
                return 0

            lax.fori_loop(0, padn_ref[e], one, 0)
            return 0

        lax.fori_loop(0, n_exp, per_expert, 0)

    @pl.when(i == 0)
    def _():
        each_pad(lambda r: _row_copy(z_ref, 0, xs_ref, r, 1, zsem).start())

    sh_ref[...] = _shared_swiglu(h_ref, wsg_ref, wsu_ref, wsd_ref, tt)

    for k in range(TOP_K):
        _row_copy(h_ref, 0, xs_ref, 0, tt, sem).wait()

    @pl.when(i == 0)
    def _():
        each_pad(lambda r: _row_copy(z_ref, 0, xs_ref, 0, 1, zsem).wait())


def _dispatch(h2p, dest, pad_lo, pad_n, R, wsg, wsu, wsd):
    N = h2p.shape[0]
    tile = h2p.shape[1:]
    D, F = wsg.shape
    tt = TOK_TILE
    dest3 = dest
    zeros = jnp.zeros((SUBLANES,) + tile, h2p.dtype)
    const = lambda shp: pl.BlockSpec(shp, lambda i, lo, n: (0,) * len(shp))
    grid_spec = pltpu.PrefetchScalarGridSpec(
        num_scalar_prefetch=2,
        grid=(N // tt,),
        in_specs=[pl.BlockSpec((1, 1, tt * TOP_K), lambda i, lo, n: (i, 0, 0), memory_space=pltpu.SMEM),
                  pl.BlockSpec((tt,) + tile, lambda i, lo, n: (i, 0, 0)),
                  const((SUBLANES,) + tile), const((D, F)), const((D, F)), const((F, D))],
        out_specs=[pl.BlockSpec(memory_space=pl.ANY), pl.BlockSpec((tt, D), lambda i, lo, n: (i, 0))],
        scratch_shapes=[pltpu.SemaphoreType.DMA(()), pltpu.SemaphoreType.DMA(())],
    )
    return pl.pallas_call(
        functools.partial(_dispatch_kernel, tt=tt, n_exp=N_EXPERTS),
        grid_spec=grid_spec,
        out_shape=[jax.ShapeDtypeStruct((R,) + tile, h2p.dtype), jax.ShapeDtypeStruct((N, D), F32)],
        compiler_params=_cparams(("arbitrary",)),
        name="moe_dispatch",
    )(pad_lo, pad_n, dest3, h2p, zeros, wsg, wsu, wsd)


def _expert_kernel(te_ref, nu_ref, first_ref, slot_ref, nxt_ref, nv_ref, xs_ref, wg_hbm, wu_hbm, wd_hbm, y_ref,
                   stg, stu, std, wgb, wub, wdb, sem):
    i = pl.program_id(0)
    n_chunks = 2

    def fetch(e, s):
        cps = []
        for m, (src, dst) in enumerate(((wg_hbm, stg), (wu_hbm, stu), (wd_hbm, std))):
            rows = dst.shape[1] // n_chunks
            for c in range(n_chunks):
                cps.append(pltpu.make_async_copy(src.at[e, pl.ds(c * rows, rows)],
                                                 dst.at[s, pl.ds(c * rows, rows)], sem.at[s, m]))
        return cps

    @pl.when(i == 0)
    def _():
        for cp in fetch(te_ref[0], 0):
            cp.start(priority=1)

        @pl.when(nxt_ref[0, 0] >= 0)
        def _():
            for cp in fetch(nxt_ref[0, 0], 1):
                cp.start(priority=1)

    active = i < nu_ref[0]

    def swiglu_tile(wg, wu, wd, m_rows=xs_ref.shape[0]):
        lo, hi = _unpack_pair(_load_tiles_as_rows(_flat_tiles(xs_ref), m_rows))
        lo = lo.astype(BF16)
        hi = hi.astype(BF16)
        half = lo.shape[-1]
        a = (jnp.dot(lo, wg[:half, :], preferred_element_type=F32)
             + jnp.dot(hi, wg[half:, :], preferred_element_type=F32))
        u = (jnp.dot(lo, wu[:half, :], preferred_element_type=F32)
             + jnp.dot(hi, wu[half:, :], preferred_element_type=F32))
        hmid = (a * jax.nn.sigmoid(a) * u).astype(BF16)
        y = jnp.dot(hmid, wd, preferred_element_type=F32)
        _store_rows_as_tiles(_flat_tiles(y_ref), _pack_pair(y[:, :half], y[:, half:]))
        rest = xs_ref.shape[0] - m_rows
        if rest:
            _store_rows_as_tiles(_flat_tiles(y_ref), jnp.zeros((rest, half), U32), base=m_rows)

    is_first = first_ref[i] == 1
    half_rows = xs_ref.shape[0] // 2
    short = nv_ref[i] <= half_rows

    @pl.when(active & is_first)
    def _():
        s = slot_ref[i]
        for cp in fetch(0, s):
            cp.wait()
        wg = stg[s].astype(BF16)
        wu = stu[s].astype(BF16)
        wd = std[s].astype(BF16)
        wgb[...] = wg
        wub[...] = wu
        wdb[...] = wd
        swiglu_tile(wg, wu, wd)

        @pl.when(nxt_ref[1, i] >= 0)
        def _():
            for cp in fetch(nxt_ref[1, i], s):
                cp.start(priority=1)

    @pl.when(active & jnp.logical_not(is_first) & jnp.logical_not(short))
    def _():
        swiglu_tile(wgb[...], wub[...], wdb[...])

    @pl.when(active & jnp.logical_not(is_first) & short)
    def _():
        swiglu_tile(wgb[...], wub[...], wdb[...], m_rows=half_rows)


def _experts(xs, tile_expert, n_used, first, slot, nxt, nvalid, wg, wu, wd):
    R = xs.shape[0]
    tile = xs.shape[1:]
    M = ROW_TILE
    _, D, F = wg.shape
    row = lambda i, te, nu, fi, sl, nx, nv: (jnp.minimum(i, nu[0] - 1), 0, 0)
    grid_spec = pltpu.PrefetchScalarGridSpec(
        num_scalar_prefetch=6,
        grid=(R // M,),
        in_specs=[pl.BlockSpec((M,) + tile, row),
                  pl.BlockSpec(memory_space=pl.ANY),
                  pl.BlockSpec(memory_space=pl.ANY),
                  pl.BlockSpec(memory_space=pl.ANY)],
        out_specs=pl.BlockSpec((M,) + tile, row),
        scratch_shapes=[pltpu.VMEM((2, D, F), F32), pltpu.VMEM((2, D, F), F32), pltpu.VMEM((2, F, D), F32),
                        pltpu.VMEM((D, F), BF16), pltpu.VMEM((D, F), BF16), pltpu.VMEM((F, D), BF16),
                        pltpu.SemaphoreType.DMA((2, 3))],
    )
    return pl.pallas_call(
        _expert_kernel,
        grid_spec=grid_spec,
        out_shape=jax.ShapeDtypeStruct((R,) + tile, U32),
        compiler_params=_cparams(("arbitrary",)),
        name="moe_experts",
    )(tile_expert, n_used, first, slot, nxt, nvalid, xs, wg, wu, wd)


def _final_kernel(dcur_ref, dnxt_ref, sh_ref, x1_ref, gf_ref, g_ref, wk_ref, y_ref,
                  o_ref, ybuf, sem, *, tt, n_tiles):
    i = pl.program_id(0)
    slot = lax.rem(i, 2)

    def gather(d_ref, s):
        def issue(t, _):
            for k in range(TOP_K):
                pltpu.make_async_copy(y_ref.at[pl.ds(d_ref[0, 0, k * tt + t], 1)],
                                      ybuf.at[s, k, pl.ds(t, 1)], sem.at[s]).start(priority=k % 2)
            return 0

        lax.fori_loop(0, tt, issue, 0, unroll=2)

    @pl.when(i == 0)
    def _():
        gather(dcur_ref, 0)

    for p in range(2):
        @pl.when((i + 1 < n_tiles) & (slot == p))
        def _(p=p):
            gather(dnxt_ref, 1 - p)

    half = SUBLANES * LANES

    for k in range(TOP_K):
        pltpu.make_async_copy(y_ref.at[pl.ds(0, tt)], ybuf.at[slot, k], sem.at[slot]).wait()

    wk = wk_ref[...]
    yflat = _flat_tiles(ybuf)

    for p in range(2):
        @pl.when(slot == p)
        def _(p=p):
            r_lo = jnp.zeros((tt, half), F32)
            r_hi = jnp.zeros((tt, half), F32)
            for k in range(TOP_K):
                ylo, yhi = _unpack_pair(_load_tiles_as_rows(yflat, tt, base=(p * TOP_K + k) * tt))
                wcol = wk[:, k:k + 1]
                r_lo = r_lo + wcol * ylo
                r_hi = r_hi + wcol * yhi
            total = sh_ref[...] + jnp.concatenate([r_lo, r_hi], axis=1)
            x2 = x1_ref[...] + gf_ref[0] * total
            ms = jnp.mean(x2 * x2, axis=-1, keepdims=True)
            o_ref[...] = x2 * lax.rsqrt(ms + NORM_EPS) * g_ref[...]


def _final(shared, x1, gate_f, norm_out, dest, wk, y, T):
    N, D = x1.shape
    tile = y.shape[1:]
    tt = TOK_TILE
    per_b = T // tt
    n_tiles = N // tt
    dest3 = dest
    rowb = lambda w: pl.BlockSpec((tt, w), lambda i: (i, 0))
    const = lambda shp: pl.BlockSpec(shp, lambda i: (0,) * len(shp))
    dspec = lambda f: pl.BlockSpec((1, 1, tt * TOP_K), f, memory_space=pltpu.SMEM)
    return pl.pallas_call(
        functools.partial(_final_kernel, tt=tt, n_tiles=n_tiles),
        grid=(n_tiles,),
        in_specs=[dspec(lambda i: (i, 0, 0)),
                  dspec(lambda i: (jnp.minimum(i + 1, n_tiles - 1), 0, 0)),
                  rowb(D), rowb(D),
                  pl.BlockSpec((1, 1, D), lambda i: (i // per_b, 0, 0)),
                  const((1, D)),
                  pl.BlockSpec((tt, TOP_K), lambda i: (i, 0)),
                  pl.BlockSpec(memory_space=pl.ANY)],
        out_specs=rowb(D),
        out_shape=jax.ShapeDtypeStruct((N, D), F32),
        scratch_shapes=[pltpu.VMEM((2, TOP_K, tt) + tile, U32), pltpu.SemaphoreType.DMA((2,))],
        compiler_params=_cparams(("arbitrary",)),
        name="moe_combine_final",
    )(dest3, dest3, shared, x1, gate_f[:, None, :], norm_out[None, :], wk, y)


def kernel(x, c, positions, w_ada, b_ada, norm_mix, norm_ffn, norm_out, w_in, w_out, w_router, router_bias,
           w_gate, w_up, w_down, w_sh_gate, w_sh_up, w_sh_down):
    B, T, D = x.shape
    depth = w_ada.shape[0]
    assert depth == 1, "the final rmsnorm is fused into the layer's last kernel"
    moba_w = MOBA_HEADS * MOBA_HEAD_DIM
    ret_w = RET_HEADS * RET_HEAD_DIM
    N = B * T
    E, M = N_EXPERTS, ROW_TILE
    R = N * TOP_K + E * M
    n_tiles = R // M
    tabs = _rope_tables(positions)

    for l in range(depth):
        mod = _ada(c, w_ada[l], b_ada[l])
        shift_a, scale_a, gate_a, shift_f, scale_f, gate_f = jnp.split(mod, 6, axis=-1)

        proj = _in_proj(x, norm_mix[l], shift_a, scale_a, w_in[l].astype(BF16), tabs, moba_w, ret_w)
        o_a = _moba(proj, B, T, moba_w)
        o_r = _retention(proj, B, T, moba_w, ret_w)
        x1, h2p, logits = _out_proj(o_a, o_r, w_out[l].astype(BF16), x, gate_a, norm_ffn[l],
                                    shift_f, scale_f, w_router[l])

        assert D // 2 == SUBLANES * LANES, "a packed row must fill exactly one (SUBLANES, LANES) tile"
        h2p = h2p.reshape(N, SUBLANES, LANES)
        selr, wf, rank, counts = _route_t(logits, router_bias[l])
        cnt = counts[:, 0].astype(jnp.int32)
        pcnt = (cnt + M - 1) // M * M
        pend = jnp.cumsum(pcnt)
        pstart = pend - pcnt
        tidx = jnp.arange(n_tiles, dtype=jnp.int32)
        end_tile = pend // M
        eids = jnp.arange(E, dtype=jnp.int32)
        owner = lambda v: jnp.minimum(jnp.sum(end_tile[None, :] <= v[:, None], axis=1), E - 1).astype(jnp.int32)
        end_of = lambda e: jnp.sum(jnp.where(e[:, None] == eids[None, :], end_tile[None, :], 0), axis=1)
        tile_expert = owner(tidx)
        n_used = end_tile[-1:].astype(jnp.int32)
        first = (((tidx == 0) | (tile_expert != jnp.roll(tile_expert, 1))) & (tidx < n_used[0])).astype(jnp.int32)
        slot = ((jnp.cumsum(first) - 1) % 2).astype(jnp.int32)
        end1 = end_of(tile_expert)
        exp1 = owner(end1)
        end2 = end_of(exp1)
        exp2 = owner(end2)
        has1 = end1 < n_used[0]
        of_tile = lambda v: jnp.sum(jnp.where(tile_expert[:, None] == eids[None, :], v[None, :], 0), axis=1)
        nvalid = jnp.clip(of_tile(cnt) - (tidx - of_tile(pstart // M)) * M, 0, M).astype(jnp.int32)
        nxt = jnp.stack([jnp.where(has1, exp1, -1),
                         jnp.where(has1 & (end2 < n_used[0]), exp2, -1)]).astype(jnp.int32)
        dest, wk = _dest_t(selr, wf, rank, pstart.astype(F32)[:, None])
        tt = TOK_TILE
        dest = dest.reshape(B, TOP_K, T // tt, tt).transpose(0, 2, 1, 3).reshape(N // tt, 1, TOP_K * tt)
        wk = wk.transpose(0, 2, 1).reshape(N, TOP_K)

        xs, shared = _dispatch(h2p, dest, (pstart + cnt).astype(jnp.int32), (pcnt - cnt).astype(jnp.int32), R,
                               w_sh_gate[l].astype(BF16), w_sh_up[l].astype(BF16), w_sh_down[l].astype(BF16))
        y = _experts(xs, tile_expert, n_used, first, slot, nxt, nvalid, w_gate[l], w_up[l], w_down[l])
        out = _final(shared, x1.reshape(N, D), gate_f, norm_out, dest, wk, y, T)
        x = out.reshape(B, T, D)
    return x
```

```python
import functools

import jax
import jax.numpy as jnp
from jax import lax
from jax.experimental import pallas as pl
from jax.experimental.pallas import tpu as pltpu

MOBA_HEADS = 8
MOBA_HEAD_DIM = 128
MOBA_BLOCK = 256
MOBA_TOPK = 3
ROPE_THETA = 500000.0
ROPE_DIMS = 32
RET_HEADS = 4
RET_HEAD_DIM = 256
RET_ROPE_BASE = 10000.0
N_EXPERTS = 64
TOP_K = 8
N_GROUPS = 8
TOPK_GROUPS = 4
ROUTE_SCALE = 2.5
NORM_EPS = 1e-6
NEG = -1e30

LANES = 128
SUBLANES = 8
VMEM_LIMIT = 56 * 1024 * 1024
INPROJ_VMEM = 60 * 1024 * 1024

RET_CHUNK = 256
ROW_TILE = 256
TOK_TILE = 256
TABLE_ROWS = 512
ADA_COLS = 1024
INPROJ_ROWS, INPROJ_COLS = 1024, 1024
MOBA_HEADS_PER_STEP = 4
OUTPROJ_ROWS = 256
ROUTE_TOKENS = 512

F32 = jnp.float32
BF16 = jnp.bfloat16
U32 = jnp.uint32


def _cparams(sem, vmem=VMEM_LIMIT):
    return pltpu.CompilerParams(dimension_semantics=sem, vmem_limit_bytes=vmem)


def _rms_mod(xf, g, shift, scale):
    ms = jnp.mean(xf * xf, axis=-1, keepdims=True)
    y = xf * lax.rsqrt(ms + NORM_EPS) * g
    return y * (1.0 + scale) + shift


def _pack_pair(lo, hi):
    lo_b = lax.bitcast_convert_type(lo.astype(BF16).astype(F32), U32)
    hi_b = lax.bitcast_convert_type(hi.astype(BF16).astype(F32), U32)
    return (lo_b >> 16) | hi_b


def _unpack_pair(p):
    lo = lax.bitcast_convert_type(p << 16, F32)
    hi = lax.bitcast_convert_type(p & jnp.uint32(0xFFFF0000), F32)
    return lo, hi


def _store_rows_as_tiles(ref, val, base=0):
    n = val.shape[0]
    for c in range(SUBLANES):
        ref[pl.ds(base * SUBLANES + c, n, stride=SUBLANES), :] = val[:, c * LANES:(c + 1) * LANES]


def _load_tiles_as_rows(ref, n, base=0):
    return jnp.concatenate([ref[pl.ds(base * SUBLANES + c, n, stride=SUBLANES), :] for c in range(SUBLANES)],
                           axis=1)


def _flat_tiles(ref):
    rows = 1
    for d in ref.shape[:-2]:
        rows *= d
    return ref.reshape(rows * SUBLANES, LANES)


def _tables_kernel(pos_ref, invm_ref, invr_ref, mc_ref, ms1_ref, ms2_ref, rc_ref, rs_ref):
    pos = pos_ref[0].astype(F32)
    angm = pos * invm_ref[...]
    lane = lax.broadcasted_iota(jnp.int32, angm.shape, 1)
    half = ROPE_DIMS // 2
    c = jnp.cos(angm)
    s = jnp.sin(angm)
    mc_ref[0] = c
    ms1_ref[0] = jnp.where(lane < half, -s, 0.0)
    ms2_ref[0] = jnp.where((lane >= half) & (lane < ROPE_DIMS), s, 0.0)
    angr = pos * invr_ref[...]
    rc_ref[0] = jnp.cos(angr)
    rs_ref[0] = jnp.sin(angr)


def _rope_tables(positions):
    B, T = positions.shape
    tm = TABLE_ROWS
    half = ROPE_DIMS // 2
    moba_inv = ROPE_THETA ** (-(jnp.arange(half, dtype=F32) * 2.0 / ROPE_DIMS))
    invm = jnp.concatenate([moba_inv, moba_inv, jnp.zeros((LANES - ROPE_DIMS,), F32)])[None, :]
    invr = (RET_ROPE_BASE ** (-jnp.linspace(0.0, 1.0, RET_HEAD_DIM // 2, dtype=F32)))[None, :]
    pos3 = positions.reshape(B, T, 1)
    tab = jax.ShapeDtypeStruct((B, T, LANES), F32)
    spec = pl.BlockSpec((1, tm, LANES), lambda b, i: (b, i, 0))
    return pl.pallas_call(
        _tables_kernel,
        grid=(B, T // tm),
        in_specs=[pl.BlockSpec((1, tm, 1), lambda b, i: (b, i, 0)),
                  pl.BlockSpec((1, LANES), lambda b, i: (0, 0)),
                  pl.BlockSpec((1, LANES), lambda b, i: (0, 0))],
        out_specs=[spec] * 5,
        out_shape=[tab] * 5,
        compiler_params=_cparams(("parallel", "parallel")),
        name="rope_tables",
    )(pos3, invm, invr)


def _ada_kernel(ct_ref, w_ref, b_ref, o_ref):
    ct = ct_ref[...]
    sct = ct * jax.nn.sigmoid(ct)
    w = w_ref[...]
    for b in range(ct.shape[1]):
        o_ref[b:b + 1, :] = jnp.sum(w * sct[:, b:b + 1], axis=0, keepdims=True) + b_ref[...]


def _ada(c, w_ada, b_ada):
    B, D = c.shape
    n_out = w_ada.shape[1]
    tn = ADA_COLS
    return pl.pallas_call(
        _ada_kernel,
        grid=(n_out // tn,),
        in_specs=[pl.BlockSpec((D, B), lambda j: (0, 0)),
                  pl.BlockSpec((D, tn), lambda j: (0, j)),
                  pl.BlockSpec((1, tn), lambda j: (0, j))],
        out_specs=pl.BlockSpec((B, tn), lambda j: (0, j)),
        out_shape=jax.ShapeDtypeStruct((B, n_out), F32),
        compiler_params=_cparams(("parallel",)),
        name="adaln_mod",
    )(c.T, w_ada, b_ada[None, :])


def _inproj_kernel(x_ref, g_ref, sh_ref, sc_ref, w_ref, mc_ref, ms1_ref, ms2_ref, rc_ref, rs_ref,
                   o_ref, hn_ref, *, tn, moba_tiles, ret_lo, ret_k_lo, ret_hi):
    j = pl.program_id(2)

    @pl.when(j == 0)
    def _():
        h = _rms_mod(x_ref[0], g_ref[...], sh_ref[0], sc_ref[0])
        hn_ref[...] = h.astype(BF16)

    acc = jnp.dot(hn_ref[...], w_ref[...].astype(BF16), preferred_element_type=F32)

    @pl.when(j < moba_tiles)
    def _():
        c, s1, s2 = mc_ref[0], ms1_ref[0], ms2_ref[0]
        half = ROPE_DIMS // 2
        for g in range(tn // LANES):
            a = acc[:, g * LANES:(g + 1) * LANES]
            r = a * c + pltpu.roll(a, LANES - half, 1) * s1 + pltpu.roll(a, half, 1) * s2
            o_ref[0, :, g * LANES:(g + 1) * LANES] = r.astype(o_ref.dtype)

    @pl.when((j >= ret_lo) & (j < ret_hi))
    def _():
        c, s = rc_ref[0], rs_ref[0]
        fac = jnp.where(j >= ret_k_lo, RET_HEAD_DIM ** -0.5, 1.0).astype(F32)
        hw = RET_HEAD_DIM // 2
        for g in range(tn // RET_HEAD_DIM):
            x1 = acc[:, g * RET_HEAD_DIM:g * RET_HEAD_DIM + hw]
            x2 = acc[:, g * RET_HEAD_DIM + hw:(g + 1) * RET_HEAD_DIM]
            o_ref[0, :, g * RET_HEAD_DIM:g * RET_HEAD_DIM + hw] = ((x1 * c - x2 * s) * fac).astype(o_ref.dtype)
            o_ref[0, :, g * RET_HEAD_DIM + hw:(g + 1) * RET_HEAD_DIM] = ((x2 * c + x1 * s) * fac).astype(o_ref.dtype)

    @pl.when(((j >= moba_tiles) & (j < ret_lo)) | (j >= ret_hi))
    def _():
        o_ref[0] = acc.astype(o_ref.dtype)


def _in_proj(x, g, shift, scale, w_bf, tabs, moba_w, ret_w):
    B, T, D = x.shape
    NC = w_bf.shape[1]
    tm, tn = INPROJ_ROWS, INPROJ_COLS
    mc, ms1, ms2, rc, rs = tabs
    kern = functools.partial(
        _inproj_kernel, tn=tn,
        moba_tiles=2 * moba_w // tn,
        ret_lo=3 * moba_w // tn,
        ret_k_lo=(3 * moba_w + ret_w) // tn,
        ret_hi=(3 * moba_w + 2 * ret_w) // tn)
    tab_spec = pl.BlockSpec((1, tm, LANES), lambda b, i, j: (b, i, 0))
    vec_spec = pl.BlockSpec((1, 1, D), lambda b, i, j: (b, 0, 0))
    return pl.pallas_call(
        kern,
        grid=(B, T // tm, NC // tn),
        in_specs=[pl.BlockSpec((1, tm, D), lambda b, i, j: (b, i, 0)),
                  pl.BlockSpec((1, D), lambda b, i, j: (0, 0)),
                  vec_spec, vec_spec,
                  pl.BlockSpec((D, tn), lambda b, i, j: (0, j)),
                  tab_spec, tab_spec, tab_spec, tab_spec, tab_spec],
        out_specs=pl.BlockSpec((1, tm, tn), lambda b, i, j: (b, i, j)),
        out_shape=jax.ShapeDtypeStruct((B, T, NC), BF16),
        scratch_shapes=[pltpu.VMEM((tm, D), BF16)],
        compiler_params=_cparams(("parallel", "parallel", "arbitrary"), vmem=INPROJ_VMEM),
        name="in_proj",
    )(x, g[None, :], shift[:, None, :], scale[:, None, :], w_bf, mc, ms1, ms2, rc, rs)


def _moba_kernel(q_ref, k_ref, v_ref, o_ref, km_ref, kms_ref, vt_ref, sel_ref, qs_ref, s_ref, m_ref, acc_ref,
                 *, nb, hp):
    qb = pl.program_id(2)
    BS, hd = MOBA_BLOCK, MOBA_HEAD_DIM

    @pl.when(qb == 0)
    def _():
        for h in range(hp):
            hs = slice(h * hd, (h + 1) * hd)
            for n in range(nb):
                kb = k_ref[0, n * BS:(n + 1) * BS, hs].astype(F32)
                km_ref[h, n:n + 1, :] = jnp.sum(kb, axis=0, keepdims=True) * (1.0 / BS)
                vt_ref[h, n, :hd, :] = v_ref[0, n * BS:(n + 1) * BS, hs].astype(F32).T.astype(BF16)
                vt_ref[h, n, hd:, :] = jnp.ones((vt_ref.shape[2] - hd, BS), BF16)
            km = km_ref[h]
            k1 = km.astype(BF16)
            r1 = km - k1.astype(F32)
            k2 = r1.astype(BF16)
            kms_ref[h, 0] = k1
            kms_ref[h, 1] = k2
            kms_ref[h, 2] = (r1 - k2.astype(F32)).astype(BF16)

    scale = hd ** -0.5
    own = pl.multiple_of(qb * BS, BS)
    blk = lax.broadcasted_iota(jnp.int32, (nb, BS), 0)
    blk_f = blk.astype(F32)

    for h in range(hp):
        hs = slice(h * hd, (h + 1) * hd)
        qT = q_ref[0, :, hs].astype(F32).T
        qTb = qT.astype(BF16)
        gate = (jnp.dot(kms_ref[h, 0], qTb, preferred_element_type=F32)
                + (jnp.dot(kms_ref[h, 1], qTb, preferred_element_type=F32)
                   + jnp.dot(kms_ref[h, 2], qTb, preferred_element_type=F32)))
        g = jnp.where(blk < qb, gate, NEG)
        sel = jnp.zeros((nb, BS), F32)
        for _ in range(MOBA_TOPK):
            m = jnp.max(g, axis=0, keepdims=True)
            idx = jnp.min(jnp.where(g == m, blk_f, float(nb)), axis=0, keepdims=True)
            pick = blk_f == idx
            sel = jnp.where(pick & (m > 0.5 * NEG), 1.0, sel)
            g = jnp.where(pick, -jnp.inf, g)
        sel_ref[h] = sel
        qs_ref[h] = (qT * scale).astype(BF16)
        m_ref[h] = jnp.full((1, BS), NEG, F32)

    n_pairs = lax.shift_right_logical(qb + 1, 1)

    def sweep_scores(j, _):
        for h in range(hp):
            mh = m_ref[h]
            for u in range(2):
                n = 2 * j + u
                off = pl.multiple_of(n * BS, BS)
                s = jnp.dot(k_ref[0, pl.ds(off, BS), h * hd:(h + 1) * hd], qs_ref[h],
                            preferred_element_type=F32)
                s = jnp.where(sel_ref[h, pl.ds(n, 1), :] > 0.0, s, NEG)
                s_ref[h, n] = s
                mh = jnp.maximum(mh, jnp.max(s, axis=0, keepdims=True))
            m_ref[h] = mh
        return 0

    lax.fori_loop(0, n_pairs, sweep_scores, 0)

    krow = lax.broadcasted_iota(jnp.int32, (BS, BS), 0)
    qcol = lax.broadcasted_iota(jnp.int32, (BS, BS), 1)
    for h in range(hp):
        hs = slice(h * hd, (h + 1) * hd)
        s = jnp.dot(k_ref[0, pl.ds(own, BS), hs], qs_ref[h], preferred_element_type=F32)
        s = jnp.where(krow <= qcol, s, NEG)
        m = jnp.maximum(m_ref[h], jnp.max(s, axis=0, keepdims=True))
        m_ref[h] = m
        p = jnp.exp((s - m).astype(BF16))
        acc_ref[h] = jnp.dot(vt_ref[h, qb], p, preferred_element_type=F32)

    def sweep_values(j, _):
        for h in range(hp):
            mh = m_ref[h]
            p0 = jnp.exp((s_ref[h, 2 * j] - mh).astype(BF16))
            p1 = jnp.exp((s_ref[h, 2 * j + 1] - mh).astype(BF16))
            acc_ref[h] = acc_ref[h] + (
                jnp.dot(vt_ref[h, 2 * j], p0, preferred_element_type=F32)
                + jnp.dot(vt_ref[h, 2 * j + 1], p1, preferred_element_type=F32))
        return 0

    lax.fori_loop(0, n_pairs, sweep_values, 0)

    for h in range(hp):
        acc = acc_ref[h]
        o_ref[0, :, h * hd:(h + 1) * hd] = (acc[:hd, :] / acc[hd:hd + 1, :]).T.astype(o_ref.dtype)


def _moba(proj, B, T, moba_w):
    H, hd, BS = MOBA_HEADS, MOBA_HEAD_DIM, MOBA_BLOCK
    nb = T // BS
    hp = MOBA_HEADS_PER_STEP
    ones_rows = 2 * SUBLANES
    gw = hp * hd
    gpw = moba_w // gw
    return pl.pallas_call(
        functools.partial(_moba_kernel, nb=nb, hp=hp),
        grid=(B, H // hp, nb),
        in_specs=[pl.BlockSpec((1, BS, gw), lambda b, h, i: (b, i, h)),
                  pl.BlockSpec((1, T, gw), lambda b, h, i: (b, 0, gpw + h)),
                  pl.BlockSpec((1, T, gw), lambda b, h, i: (b, 0, 2 * gpw + h))],
        out_specs=pl.BlockSpec((1, BS, gw), lambda b, h, i: (b, i, h)),
        out_shape=jax.ShapeDtypeStruct((B, T, moba_w), BF16),
        scratch_shapes=[pltpu.VMEM((hp, nb, hd), F32),
                        pltpu.VMEM((hp, 3, nb, hd), BF16),
                        pltpu.VMEM((hp, nb, hd + ones_rows, BS), BF16),
                        pltpu.VMEM((hp, nb, BS), F32),
                        pltpu.VMEM((hp, hd, BS), BF16),
                        pltpu.VMEM((hp, nb, BS, BS), F32),
                        pltpu.VMEM((hp, 1, BS), F32),
                        pltpu.VMEM((hp, hd + ones_rows, BS), F32)],
        compiler_params=_cparams(("parallel", "parallel", "arbitrary")),
        name="moba_attn",
    )(proj, proj, proj)


def _ret_kernel(q_ref, k_ref, v_ref, g_ref, dm_ref, xi_ref, zeta_ref, cd_ref, o_ref, s_ref):
    c = pl.program_id(1)
    d = RET_HEAD_DIM

    @pl.when(c == 0)
    def _():
        s_ref[...] = jnp.zeros_like(s_ref)

    for h in range(RET_HEADS):
        sl = slice(h * d, (h + 1) * d)
        q = q_ref[0, :, sl]
        k = k_ref[0, :, sl]
        v = v_ref[0, :, sl]
        inner = lax.dot_general(q, k, (((1,), (1,)), ((), ())), preferred_element_type=F32) * dm_ref[h]
        S = s_ref[h]
        o = (jnp.dot(inner.astype(BF16), v, preferred_element_type=F32)
             + jnp.dot(q, S.astype(BF16), preferred_element_type=F32) * xi_ref[h])
        kz = (k.astype(F32) * zeta_ref[h]).astype(BF16)
        s_ref[h] = S * cd_ref[h] + lax.dot_general(kz, v, (((0,), (0,)), ((), ())),
                                                   preferred_element_type=F32)
        mu = jnp.mean(o, axis=-1, keepdims=True)
        dlt = o - mu
        var = jnp.mean(dlt * dlt, axis=-1, keepdims=True)
        on = dlt * lax.rsqrt(var + NORM_EPS)
        gg = g_ref[0, :, sl].astype(F32)
        o_ref[0, :, sl] = (on * (gg * jax.nn.sigmoid(gg))).astype(o_ref.dtype)


def _retention(proj, B, T, moba_w, ret_w):
    C, H = RET_CHUNK, RET_HEADS
    gamma = 1.0 - jnp.exp2(-5.0 - jnp.arange(H, dtype=F32))
    log_g = jnp.log(gamma)
    pos = jnp.arange(C, dtype=F32)
    diff = pos[:, None] - pos[None, :]
    dmask = jnp.where(diff >= 0, jnp.exp(jnp.maximum(diff, 0.0) * log_g[:, None, None]), 0.0)
    xi = jnp.exp((pos + 1.0) * log_g[:, None])[:, :, None]
    zeta = jnp.exp((C - 1.0 - pos) * log_g[:, None])[:, :, None]
    cd = jnp.exp(C * log_g)[:, None, None]
    base = 3 * moba_w // ret_w
    col = lambda off: pl.BlockSpec((1, C, ret_w), lambda b, c: (b, c, base + off))
    full = lambda shp: pl.BlockSpec(shp, lambda b, c: (0,) * len(shp))
    return pl.pallas_call(
        _ret_kernel,
        grid=(B, T // C),
        in_specs=[col(0), col(1), col(2), col(3),
                  full((H, C, C)), full((H, C, 1)), full((H, C, 1)), full((H, 1, 1))],
        out_specs=pl.BlockSpec((1, C, ret_w), lambda b, c: (b, c, 0)),
        out_shape=jax.ShapeDtypeStruct((B, T, ret_w), BF16),
        scratch_shapes=[pltpu.VMEM((H, RET_HEAD_DIM, RET_HEAD_DIM), F32)],
        compiler_params=_cparams(("parallel", "arbitrary")),
        name="retention",
    )(proj, proj, proj, proj, dmask, xi, zeta, cd)


def _outproj_kernel(oa_ref, or_ref, w_ref, x_ref, ga_ref, g_ref, sh_ref, sc_ref, wrh_ref, wrl_ref,
                    x1_ref, hp_ref, lg_ref, *, moba_w):
    mix = (jnp.dot(oa_ref[0], w_ref[:moba_w, :], preferred_element_type=F32)
           + jnp.dot(or_ref[0], w_ref[moba_w:, :], preferred_element_type=F32))
    x1 = x_ref[0] + ga_ref[0] * mix
    x1_ref[0] = x1
    h = _rms_mod(x1, g_ref[...], sh_ref[0], sc_ref[0])
    half = h.shape[-1] // 2
    _store_rows_as_tiles(_flat_tiles(hp_ref), _pack_pair(h[:, :half], h[:, half:]))
    h_hi = h.astype(BF16)
    h_lo = (h - h_hi.astype(F32)).astype(BF16)
    lg_ref[0] = (jnp.dot(h_hi, wrh_ref[...], preferred_element_type=F32)
                 + (jnp.dot(h_lo, wrh_ref[...], preferred_element_type=F32)
                    + jnp.dot(h_hi, wrl_ref[...], preferred_element_type=F32)))


def _out_proj(o_a, o_r, w_bf, x, gate_a, g, shift, scale, w_router):
    B, T, D = x.shape
    moba_w, ret_w = o_a.shape[-1], o_r.shape[-1]
    E = w_router.shape[1]
    tm = OUTPROJ_ROWS
    wr_hi = w_router.astype(BF16)
    wr_lo = (w_router - wr_hi.astype(F32)).astype(BF16)
    vec = pl.BlockSpec((1, 1, D), lambda b, i: (b, 0, 0))
    row = lambda w: pl.BlockSpec((1, tm, w), lambda b, i: (b, i, 0))
    return pl.pallas_call(
        functools.partial(_outproj_kernel, moba_w=moba_w),
        grid=(B, T // tm),
        in_specs=[row(moba_w), row(ret_w),
                  pl.BlockSpec((moba_w + ret_w, D), lambda b, i: (0, 0)),
                  row(D), vec,
                  pl.BlockSpec((1, D), lambda b, i: (0, 0)),
                  vec, vec,
                  pl.BlockSpec((D, E), lambda b, i: (0, 0)),
                  pl.BlockSpec((D, E), lambda b, i: (0, 0))],
        out_specs=[row(D), pl.BlockSpec((1, tm, SUBLANES, LANES), lambda b, i: (b, i, 0, 0)), row(E)],
        out_shape=[jax.ShapeDtypeStruct((B, T, D), F32),
                   jax.ShapeDtypeStruct((B, T, SUBLANES, LANES), U32),
                   jax.ShapeDtypeStruct((B, T, E), F32)],
        compiler_params=_cparams(("parallel", "parallel")),
        name="out_proj",
    )(o_a, o_r, w_bf, x, gate_a[:, None, :], g[None, :], shift[:, None, :], scale[:, None, :], wr_hi, wr_lo)


def _route_kernel_t(lg_ref, b_ref, selr_ref, wf_ref, rank_ref, cnt_ref, carry_ref):
    @pl.when((pl.program_id(0) == 0) & (pl.program_id(1) == 0))
    def _():
        carry_ref[...] = jnp.zeros_like(carry_ref)

    E = N_EXPERTS
    gsz = E // N_GROUPS
    assert gsz == SUBLANES and N_GROUPS == SUBLANES, "a routing group is one sublane tile of experts"
    s = jax.nn.sigmoid(lg_ref[0].T)
    biased = s + b_ref[...]
    tm = s.shape[1]
    sub = lax.broadcasted_iota(jnp.int32, (SUBLANES, tm), 0).astype(F32)
    eid = lax.broadcasted_iota(jnp.int32, (E, tm), 0).astype(F32)

    def first_argmax(v, ids, width):
        m = jnp.max(v, axis=0, keepdims=True)
        idx = jnp.min(jnp.where(v == m, ids, float(width)), axis=0, keepdims=True)
        return m, idx

    gscore = jnp.zeros((N_GROUPS, tm), F32)
    for gi in range(N_GROUPS):
        v = biased[gi * gsz:(gi + 1) * gsz, :]
        m1, i1 = first_argmax(v, sub, gsz)
        m2 = jnp.max(jnp.where(sub == i1, -jnp.inf, v), axis=0, keepdims=True)
        gscore = jnp.where(sub == float(gi), m1 + m2, gscore)

    gsel = jnp.zeros((N_GROUPS, tm), F32)
    for _ in range(TOPK_GROUPS):
        _, gi = first_argmax(gscore, sub, N_GROUPS)
        pick = sub == gi
        gsel = jnp.where(pick, 1.0, gsel)
        gscore = jnp.where(pick, -jnp.inf, gscore)

    cand = jnp.concatenate(
        [jnp.where(gsel[gi:gi + 1, :] > 0.0, biased[gi * gsz:(gi + 1) * gsz, :], NEG) for gi in range(N_GROUPS)],
        axis=0)
    selr = jnp.zeros((E, tm), F32)
    for r in range(TOP_K):
        _, ei = first_argmax(cand, eid, E)
        pick = eid == ei
        selr = jnp.where(pick, float(r + 1), selr)
        cand = jnp.where(pick, -jnp.inf, cand)

    chosen = selr > 0.0
    w = jnp.where(chosen, s, 0.0)
    wsum = jnp.sum(w, axis=0, keepdims=True)
    selr_ref[0] = selr
    wf_ref[0] = w / wsum * ROUTE_SCALE

    onehot = chosen.astype(BF16)
    c_i = lax.broadcasted_iota(jnp.int32, (tm, tm), 0)
    r_i = lax.broadcasted_iota(jnp.int32, (tm, tm), 1)
    tri = (c_i < r_i).astype(BF16)
    carry = carry_ref[...]
    rank_ref[0] = jnp.dot(onehot, tri, preferred_element_type=F32) + carry
    carry = carry + jnp.sum(chosen.astype(F32), axis=1, keepdims=True)
    carry_ref[...] = carry
    cnt_ref[...] = carry


def _route_t(logits_t, bias):
    B, T, E = logits_t.shape
    tm = ROUTE_TOKENS
    blk = pl.BlockSpec((1, E, tm), lambda b, i: (b, 0, i))
    col = pl.BlockSpec((E, 1), lambda b, i: (0, 0))
    full = jax.ShapeDtypeStruct((B, E, T), F32)
    return pl.pallas_call(
        _route_kernel_t,
        grid=(B, T // tm),
        in_specs=[pl.BlockSpec((1, tm, E), lambda b, i: (b, i, 0)), col],
        out_specs=[blk, blk, blk, col],
        out_shape=[full, full, full, jax.ShapeDtypeStruct((E, 1), F32)],
        scratch_shapes=[pltpu.VMEM((E, 1), F32)],
        compiler_params=_cparams(("arbitrary", "arbitrary")),
        name="route_topk",
    )(logits_t, bias[:, None])


def _dest_kernel_t(selr_ref, wf_ref, rank_ref, ps_ref, dest_ref, wk_ref):
    selr = selr_ref[0]
    destfull = rank_ref[0] + ps_ref[...]
    wf = wf_ref[0]
    for r in range(TOP_K):
        hit = selr == float(r + 1)
        dest_ref[0, r:r + 1, :] = jnp.sum(jnp.where(hit, destfull, 0.0), axis=0, keepdims=True).astype(jnp.int32)
        wk_ref[0, r:r + 1, :] = jnp.sum(jnp.where(hit, wf, 0.0), axis=0, keepdims=True)


def _dest_t(selr, wf, rank, pstart_f):
    B, E, T = selr.shape
    tm = ROUTE_TOKENS
    blk = pl.BlockSpec((1, E, tm), lambda b, i: (b, 0, i))
    outb = pl.BlockSpec((1, TOP_K, tm), lambda b, i: (b, 0, i))
    return pl.pallas_call(
        _dest_kernel_t,
        grid=(B, T // tm),
        in_specs=[blk, blk, blk, pl.BlockSpec((E, 1), lambda b, i: (0, 0))],
        out_specs=[outb, outb],
        out_shape=[jax.ShapeDtypeStruct((B, TOP_K, T), jnp.int32), jax.ShapeDtypeStruct((B, TOP_K, T), F32)],
        compiler_params=_cparams(("parallel", "parallel")),
        name="route_dest",
    )(selr, wf, rank, pstart_f)


def _row_copy(src, s_row, dst, d_row, n, sem):
    return pltpu.make_async_copy(src.at[pl.ds(s_row, n)], dst.at[pl.ds(d_row, n)], sem)


def _shared_swiglu(h_ref, wsg_ref, wsu_ref, wsd_ref, n):
    lo, hi = _unpack_pair(_load_tiles_as_rows(_flat_tiles(h_ref), n))
    lo = lo.astype(BF16)
    hi = hi.astype(BF16)
    half = lo.shape[-1]
    a = (jnp.dot(lo, wsg_ref[:half, :], preferred_element_type=F32)
         + jnp.dot(hi, wsg_ref[half:, :], preferred_element_type=F32))
    u = (jnp.dot(lo, wsu_ref[:half, :], preferred_element_type=F32)
         + jnp.dot(hi, wsu_ref[half:, :], preferred_element_type=F32))
    hmid = (a * jax.nn.sigmoid(a) * u).astype(BF16)
    return jnp.dot(hmid, wsd_ref[...], preferred_element_type=F32)


def _dispatch_kernel(padlo_ref, padn_ref, dest_ref, h_ref, z_ref, wsg_ref, wsu_ref, wsd_ref,
                     xs_ref, sh_ref, sem, zsem, *, tt, n_exp):
    i = pl.program_id(0)

    def issue(t, _):
        for k in range(TOP_K):
            _row_copy(h_ref, t, xs_ref, dest_ref[0, 0, k * tt + t], 1, sem).start(priority=k % 2)
        return 0

    lax.fori_loop(0, tt, issue, 0, unroll=2)

    def each_pad(fn):
        def per_expert(e, _):
            lo = padlo_ref[e]

            def one(r, _):
                fn(lo + r)
                return 0

            lax.fori_loop(0, padn_ref[e], one, 0)
            return 0

        lax.fori_loop(0, n_exp, per_expert, 0)

    @pl.when(i == 0)
    def _():
        each_pad(lambda r: _row_copy(z_ref, 0, xs_ref, r, 1, zsem).start())

    sh_ref[...] = _shared_swiglu(h_ref, wsg_ref, wsu_ref, wsd_ref, tt)

    for k in range(TOP_K):
        _row_copy(h_ref, 0, xs_ref, 0, tt, sem).wait()

    @pl.when(i == 0)
    def _():
        each_pad(lambda r: _row_copy(z_ref, 0, xs_ref, 0, 1, zsem).wait())


def _dispatch(h2p, dest, pad_lo, pad_n, R, wsg, wsu, wsd):
    N = h2p.shape[0]
    tile = h2p.shape[1:]
    D, F = wsg.shape
    tt = TOK_TILE
    dest3 = dest
    zeros = jnp.zeros((SUBLANES,) + tile, h2p.dtype)
    const = lambda shp: pl.BlockSpec(shp, lambda i, lo, n: (0,) * len(shp))
    grid_spec = pltpu.PrefetchScalarGridSpec(
        num_scalar_prefetch=2,
        grid=(N // tt,),
        in_specs=[pl.BlockSpec((1, 1, tt * TOP_K), lambda i, lo, n: (i, 0, 0), memory_space=pltpu.SMEM),
                  pl.BlockSpec((tt,) + tile, lambda i, lo, n: (i, 0, 0)),
                  const((SUBLANES,) + tile), const((D, F)), const((D, F)), const((F, D))],
        out_specs=[pl.BlockSpec(memory_space=pl.ANY), pl.BlockSpec((tt, D), lambda i, lo, n: (i, 0))],
        scratch_shapes=[pltpu.SemaphoreType.DMA(()), pltpu.SemaphoreType.DMA(())],
    )
    return pl.pallas_call(
        functools.partial(_dispatch_kernel, tt=tt, n_exp=N_EXPERTS),
        grid_spec=grid_spec,
        out_shape=[jax.ShapeDtypeStruct((R,) + tile, h2p.dtype), jax.ShapeDtypeStruct((N, D), F32)],
        compiler_params=_cparams(("arbitrary",)),
        name="moe_dispatch",
    )(pad_lo, pad_n, dest3, h2p, zeros, wsg, wsu, wsd)


def _expert_kernel(te_ref, nu_ref, first_ref, slot_ref, nxt_ref, nv_ref, xs_ref, wg_hbm, wu_hbm, wd_hbm, y_ref,
                   stg, stu, std, wgb, wub, wdb, sem):
    i = pl.program_id(0)
    n_chunks = 2

    def fetch(e, s):
        cps = []
        for m, (src, dst) in enumerate(((wg_hbm, stg), (wu_hbm, stu), (wd_hbm, std))):
            rows = dst.shape[1] // n_chunks
            for c in range(n_chunks):
                cps.append(pltpu.make_async_copy(src.at[e, pl.ds(c * rows, rows)],
                                                 dst.at[s, pl.ds(c * rows, rows)], sem.at[s, m]))
        return cps

    @pl.when(i == 0)
    def _():
        for cp in fetch(te_ref[0], 0):
            cp.start(priority=1)

        @pl.when(nxt_ref[0, 0] >= 0)
        def _():
            for cp in fetch(nxt_ref[0, 0], 1):
                cp.start(priority=1)

    active = i < nu_ref[0]

    def swiglu_tile(wg, wu, wd, m_rows=xs_ref.shape[0]):
        lo, hi = _unpack_pair(_load_tiles_as_rows(_flat_tiles(xs_ref), m_rows))
        lo = lo.astype(BF16)
        hi = hi.astype(BF16)
        half = lo.shape[-1]
        a = (jnp.dot(lo, wg[:half, :], preferred_element_type=F32)
             + jnp.dot(hi, wg[half:, :], preferred_element_type=F32))
        u = (jnp.dot(lo, wu[:half, :], preferred_element_type=F32)
             + jnp.dot(hi, wu[half:, :], preferred_element_type=F32))
        hmid = (a * jax.nn.sigmoid(a) * u).astype(BF16)
        y = jnp.dot(hmid, wd, preferred_element_type=F32)
        _store_rows_as_tiles(_flat_tiles(y_ref), _pack_pair(y[:, :half], y[:, half:]))
        rest = xs_ref.shape[0] - m_rows
        if rest:
            _store_rows_as_tiles(_flat_tiles(y_ref), jnp.zeros((rest, half), U32), base=m_rows)

    is_first = first_ref[i] == 1
    half_rows = xs_ref.shape[0] // 2
    short = nv_ref[i] <= half_rows

    @pl.when(active & is_first)
    def _():
        s = slot_ref[i]
        for cp in fetch(0, s):
            cp.wait()
        wg = stg[s].astype(BF16)
        wu = stu[s].astype(BF16)
        wd = std[s].astype(BF16)
        wgb[...] = wg
        wub[...] = wu
        wdb[...] = wd
        swiglu_tile(wg, wu, wd)

        @pl.when(nxt_ref[1, i] >= 0)
        def _():
            for cp in fetch(nxt_ref[1, i], s):
                cp.start(priority=1)

    @pl.when(active & jnp.logical_not(is_first) & jnp.logical_not(short))
    def _():
        swiglu_tile(wgb[...], wub[...], wdb[...])

    @pl.when(active & jnp.logical_not(is_first) & short)
    def _():
        swiglu_tile(wgb[...], wub[...], wdb[...], m_rows=half_rows)


def _experts(xs, tile_expert, n_used, first, slot, nxt, nvalid, wg, wu, wd):
    R = xs.shape[0]
    tile = xs.shape[1:]
    M = ROW_TILE
    _, D, F = wg.shape
    row = lambda i, te, nu, fi, sl, nx, nv: (jnp.minimum(i, nu[0] - 1), 0, 0)
    grid_spec = pltpu.PrefetchScalarGridSpec(
        num_scalar_prefetch=6,
        grid=(R // M,),
        in_specs=[pl.BlockSpec((M,) + tile, row),
                  pl.BlockSpec(memory_space=pl.ANY),
                  pl.BlockSpec(memory_space=pl.ANY),
                  pl.BlockSpec(memory_space=pl.ANY)],
        out_specs=pl.BlockSpec((M,) + tile, row),
        scratch_shapes=[pltpu.VMEM((2, D, F), F32), pltpu.VMEM((2, D, F), F32), pltpu.VMEM((2, F, D), F32),
                        pltpu.VMEM((D, F), BF16), pltpu.VMEM((D, F), BF16), pltpu.VMEM((F, D), BF16),
                        pltpu.SemaphoreType.DMA((2, 3))],
    )
    return pl.pallas_call(
        _expert_kernel,
        grid_spec=grid_spec,
        out_shape=jax.ShapeDtypeStruct((R,) + tile, U32),
        compiler_params=_cparams(("arbitrary",)),
        name="moe_experts",
    )(tile_expert, n_used, first, slot, nxt, nvalid, xs, wg, wu, wd)


def _final_kernel(dcur_ref, dnxt_ref, sh_ref, x1_ref, gf_ref, g_ref, wk_ref, y_ref,
                  o_ref, ybuf, sem, *, tt, n_tiles):
    i = pl.program_id(0)
    slot = lax.rem(i, 2)

    def gather(d_ref, s):
        def issue(t, _):
            for k in range(TOP_K):
                pltpu.make_async_copy(y_ref.at[pl.ds(d_ref[0, 0, k * tt + t], 1)],
                                      ybuf.at[s, k, pl.ds(t, 1)], sem.at[s]).start(priority=k % 2)
            return 0

        lax.fori_loop(0, tt, issue, 0, unroll=2)

    @pl.when(i == 0)
    def _():
        gather(dcur_ref, 0)

    for p in range(2):
        @pl.when((i + 1 < n_tiles) & (slot == p))
        def _(p=p):
            gather(dnxt_ref, 1 - p)

    half = SUBLANES * LANES

    for k in range(TOP_K):
        pltpu.make_async_copy(y_ref.at[pl.ds(0, tt)], ybuf.at[slot, k], sem.at[slot]).wait()

    wk = wk_ref[...]
    yflat = _flat_tiles(ybuf)

    for p in range(2):
        @pl.when(slot == p)
        def _(p=p):
            r_lo = jnp.zeros((tt, half), F32)
            r_hi = jnp.zeros((tt, half), F32)
            for k in range(TOP_K):
                ylo, yhi = _unpack_pair(_load_tiles_as_rows(yflat, tt, base=(p * TOP_K + k) * tt))
                wcol = wk[:, k:k + 1]
                r_lo = r_lo + wcol * ylo
                r_hi = r_hi + wcol * yhi
            total = sh_ref[...] + jnp.concatenate([r_lo, r_hi], axis=1)
            x2 = x1_ref[...] + gf_ref[0] * total
            ms = jnp.mean(x2 * x2, axis=-1, keepdims=True)
            o_ref[...] = x2 * lax.rsqrt(ms + NORM_EPS) * g_ref[...]


def _final(shared, x1, gate_f, norm_out, dest, wk, y, T):
    N, D = x1.shape
    tile = y.shape[1:]
    tt = TOK_TILE
    per_b = T // tt
    n_tiles = N // tt
    dest3 = dest
    rowb = lambda w: pl.BlockSpec((tt, w), lambda i: (i, 0))
    const = lambda shp: pl.BlockSpec(shp, lambda i: (0,) * len(shp))
    dspec = lambda f: pl.BlockSpec((1, 1, tt * TOP_K), f, memory_space=pltpu.SMEM)
    return pl.pallas_call(
        functools.partial(_final_kernel, tt=tt, n_tiles=n_tiles),
        grid=(n_tiles,),
        in_specs=[dspec(lambda i: (i, 0, 0)),
                  dspec(lambda i: (jnp.minimum(i + 1, n_tiles - 1), 0, 0)),
                  rowb(D), rowb(D),
                  pl.BlockSpec((1, 1, D), lambda i: (i // per_b, 0, 0)),
                  const((1, D)),
                  pl.BlockSpec((tt, TOP_K), lambda i: (i, 0)),
                  pl.BlockSpec(memory_space=pl.ANY)],
        out_specs=rowb(D),
        out_shape=jax.ShapeDtypeStruct((N, D), F32),
        scratch_shapes=[pltpu.VMEM((2, TOP_K, tt) + tile, U32), pltpu.SemaphoreType.DMA((2,))],
        compiler_params=_cparams(("arbitrary",)),
        name="moe_combine_final",
    )(dest3, dest3, shared, x1, gate_f[:, None, :], norm_out[None, :], wk, y)


def kernel(x, c, positions, w_ada, b_ada, norm_mix, norm_ffn, norm_out, w_in, w_out, w_router, router_bias,
           w_gate, w_up, w_down, w_sh_gate, w_sh_up, w_sh_down):
    B, T, D = x.shape
    depth = w_ada.shape[0]
    assert depth == 1, "the final rmsnorm is fused into the layer's last kernel"
    moba_w = MOBA_HEADS * MOBA_HEAD_DIM
    ret_w = RET_HEADS * RET_HEAD_DIM
    N = B * T
    E, M = N_EXPERTS, ROW_TILE
    assert D == 2 * moba_w and moba_w == ret_w, "projection column layout assumes equal head groups"
    for blk in (MOBA_BLOCK, RET_CHUNK, TOK_TILE, TABLE_ROWS, INPROJ_ROWS, OUTPROJ_ROWS, ROUTE_TOKENS):
        assert T % blk == 0, (T, blk)
    R = N * TOP_K + E * M
    n_tiles = R // M
    tabs = _rope_tables(positions)

    for l in range(depth):
        mod = _ada(c, w_ada[l], b_ada[l])
        shift_a, scale_a, gate_a, shift_f, scale_f, gate_f = jnp.split(mod, 6, axis=-1)

        proj = _in_proj(x, norm_mix[l], shift_a, scale_a, w_in[l], tabs, moba_w, ret_w)
        o_a = _moba(proj, B, T, moba_w)
        o_r = _retention(proj, B, T, moba_w, ret_w)
        x1, h2p, logits = _out_proj(o_a, o_r, w_out[l].astype(BF16), x, gate_a, norm_ffn[l],
                                    shift_f, scale_f, w_router[l])

        assert D // 2 == SUBLANES * LANES, "a packed row must fill exactly one (SUBLANES, LANES) tile"
        h2p = h2p.reshape(N, SUBLANES, LANES)
        selr, wf, rank, counts = _route_t(logits, router_bias[l])
        cnt = counts[:, 0].astype(jnp.int32)
        pcnt = (cnt + M - 1) // M * M
        pend = jnp.cumsum(pcnt)
        pstart = pend - pcnt
        tidx = jnp.arange(n_tiles, dtype=jnp.int32)
        end_tile = pend // M
        eids = jnp.arange(E, dtype=jnp.int32)
        owner = lambda v: jnp.minimum(jnp.sum(end_tile[None, :] <= v[:, None], axis=1), E - 1).astype(jnp.int32)
        end_of = lambda e: jnp.sum(jnp.where(e[:, None] == eids[None, :], end_tile[None, :], 0), axis=1)
        tile_expert = owner(tidx)
        n_used = end_tile[-1:].astype(jnp.int32)
        first = (((tidx == 0) | (tile_expert != jnp.roll(tile_expert, 1))) & (tidx < n_used[0])).astype(jnp.int32)
        slot = ((jnp.cumsum(first) - 1) % 2).astype(jnp.int32)
        end1 = end_of(tile_expert)
        exp1 = owner(end1)
        end2 = end_of(exp1)
        exp2 = owner(end2)
        has1 = end1 < n_used[0]
        of_tile = lambda v: jnp.sum(jnp.where(tile_expert[:, None] == eids[None, :], v[None, :], 0), axis=1)
        nvalid = jnp.clip(of_tile(cnt) - (tidx - of_tile(pstart // M)) * M, 0, M).astype(jnp.int32)
        nxt = jnp.stack([jnp.where(has1, exp1, -1),
                         jnp.where(has1 & (end2 < n_used[0]), exp2, -1)]).astype(jnp.int32)
        dest, wk = _dest_t(selr, wf, rank, pstart.astype(F32)[:, None])
        tt = TOK_TILE
        dest = dest.reshape(B, TOP_K, T // tt, tt).transpose(0, 2, 1, 3).reshape(N // tt, 1, TOP_K * tt)
        wk = wk.transpose(0, 2, 1).reshape(N, TOP_K)

        xs, shared = _dispatch(h2p, dest, (pstart + cnt).astype(jnp.int32), (pcnt - cnt).astype(jnp.int32), R,
                               w_sh_gate[l].astype(BF16), w_sh_up[l].astype(BF16), w_sh_down[l].astype(BF16))
        y = _experts(xs, tile_expert, n_used, first, slot, nxt, nvalid, w_gate[l], w_up[l], w_down[l])
        out = _final(shared, x1.reshape(N, D), gate_f, norm_out, dest, wk, y, T)
        x = out.reshape(B, T, D)
    return x
```
